```python
import math
import jax, jax.numpy as jnp
from jax import lax
import numpy as np

D_MODEL = 1024
BATCH = 1
SEQ = 16384
DEPTH = 4
DEC_BATCH = 8
DEC_SEQ = 2048
PAST_LEN = 128

HEAD_DIM = 64
N_RET_HEADS = 8
N_DIFF_HEADS = 8
RET_WIDTH = N_RET_HEADS * HEAD_DIM
DIFF_WIDTH = N_DIFF_HEADS * HEAD_DIM
MIX_WIDTH = RET_WIDTH + DIFF_WIDTH
DIFF_QK_DIM = HEAD_DIM // 2
IN_WIDTH = 4 * RET_WIDTH + 3 * DIFF_WIDTH
D_FF = 2816
CONV_WIDTH = 3
CHUNK = 128
Q_BLOCK = 128
N_BUCKETS = 32
MAX_DISTANCE = 128
ROPE_BASE = 10000.0
LN_EPS = 1e-5
HEAD_NORM_EPS = 1e-6
DEEPNORM_ALPHA = (2 * DEPTH) ** 0.25
DEEPNORM_BETA = (8 * DEPTH) ** -0.25

kernel_name = "hymba_style_retnet_diffattn_encoder"


def layer_norm(x, g, b):
    xf = x.astype(jnp.float32)
    mu = jnp.mean(xf, axis=-1, keepdims=True)
    var = jnp.mean(jnp.square(xf - mu), axis=-1, keepdims=True)
    y = (xf - mu) * lax.rsqrt(var + LN_EPS) * g.astype(jnp.float32) + b.astype(jnp.float32)
    return y.astype(x.dtype)


def head_rms(x):
    xf = x.astype(jnp.float32)
    return xf * lax.rsqrt(jnp.mean(jnp.square(xf), axis=-1, keepdims=True) + HEAD_NORM_EPS)


def to_heads(t, n_heads):
    b, s, _ = t.shape
    return t.reshape(b, s, n_heads, -1).transpose(0, 2, 1, 3)


def rotary(x):
    s, d = x.shape[-2], x.shape[-1]
    inv = 1.0 / (ROPE_BASE ** (jnp.arange(0, d, 2, dtype=jnp.float32) / d))
    ang = jnp.arange(s, dtype=jnp.float32)[:, None] * inv[None, :]
    cos, sin = jnp.cos(ang), jnp.sin(ang)
    x1, x2 = x[..., : d // 2].astype(jnp.float32), x[..., d // 2:].astype(jnp.float32)
    return jnp.concatenate([x1 * cos - x2 * sin, x1 * sin + x2 * cos], axis=-1)


def t5_bucket(rel):
    nb = N_BUCKETS // 2
    ret = jnp.where(rel > 0, nb, 0)
    n = jnp.abs(rel)
    max_exact = nb // 2
    nf = jnp.maximum(n, 1).astype(jnp.float32)
    large = max_exact + (jnp.log(nf / max_exact) / math.log(MAX_DISTANCE / max_exact)
                         * (nb - max_exact)).astype(jnp.int32)
    large = jnp.minimum(large, nb - 1)
    return ret + jnp.where(n < max_exact, n, large)


def relative_bias_vector(rel_bias, s):
    offsets = jnp.arange(-(s - 1), s, dtype=jnp.int32)
    return rel_bias.astype(jnp.float32)[t5_bucket(offsets)]


def bidir_retention(q, k, v, log_g_f, log_g_b):
    b, h, s, d = q.shape
    nc = s // CHUNK
    qc = q.reshape(b, h, nc, CHUNK, d)
    kc = k.reshape(b, h, nc, CHUNK, d)
    vc = v.reshape(b, h, nc, CHUNK, v.shape[-1])
    i = jnp.arange(CHUNK, dtype=jnp.float32)
    diff = i[:, None] - i[None, :]
    lf = log_g_f[:, None, None]
    lb = log_g_b[:, None, None]
    mask = (jnp.where(diff >= 0, jnp.exp(lf * jnp.maximum(diff, 0.0)), 0.0)
            + jnp.where(diff < 0, jnp.exp(lb * jnp.maximum(-diff, 0.0)), 0.0))
    scores = jnp.einsum('bhnid,bhnjd->bhnij', qc, kc) * mask[None, :, None]
    intra = jnp.einsum('bhnij,bhnje->bhnie', scores, vc)
    w_f = jnp.exp(log_g_f[:, None] * (CHUNK - 1 - i)[None, :])
    w_b = jnp.exp(log_g_b[:, None] * i[None, :])
    kv_f = jnp.einsum('bhnjd,hj,bhnje->nbhde', kc, w_f, vc)
    kv_b = jnp.einsum('bhnjd,hj,bhnje->nbhde', kc, w_b, vc)
    dec_f = jnp.exp(log_g_f * CHUNK)[None, :, None, None]
    dec_b = jnp.exp(log_g_b * CHUNK)[None, :, None, None]
    zero = jnp.zeros_like(kv_f[0])

    def step_f(r, kv):
        return dec_f * r + kv, r

    def step_b(r, kv):
        return dec_b * r + kv, r

    _, r_prev = lax.scan(step_f, zero, kv_f)
    _, r_next = lax.scan(step_b, zero, kv_b, reverse=True)
    q_f = jnp.exp(log_g_f[:, None] * (i + 1.0)[None, :])
    q_b = jnp.exp(log_g_b[:, None] * (CHUNK - i)[None, :])
    cross = (jnp.einsum('bhnid,nbhde,hi->bhnie', qc, r_prev, q_f)
             + jnp.einsum('bhnid,nbhde,hi->bhnie', qc, r_next, q_b))
    return (intra + cross).reshape(b, h, s, -1)


def diff_attention(q, k, v, lam, bias_vec):
    b, h, _, s, dq = q.shape
    nb = s // Q_BLOCK
    qb = (q * (dq ** -0.5)).reshape(b, h, 2, nb, Q_BLOCK, dq).transpose(3, 0, 1, 2, 4, 5)
    kpos = jnp.arange(s, dtype=jnp.int32)

    def block(args):
        qblk, start = args
        qpos = start + jnp.arange(Q_BLOCK, dtype=jnp.int32)
        bias = bias_vec[kpos[None, :] - qpos[:, None] + (s - 1)].transpose(2, 0, 1)
        logits = jnp.einsum('bhtqd,bhtkd->bhtqk', qblk, k).astype(jnp.float32) + bias[None, :, None]
        p = jax.nn.softmax(logits, axis=-1)
        w = p[:, :, 0] - lam * p[:, :, 1]
        return jnp.einsum('bhqk,bhkd->bhqd', w, v.astype(jnp.float32))

    starts = jnp.arange(nb, dtype=jnp.int32) * Q_BLOCK
    out = lax.map(block, (qb, starts))
    return out.transpose(1, 2, 0, 3, 4).reshape(b, h, s, -1)


def conv_glu(x, w_up, conv_w, conv_b, w_down):
    h = x @ w_up
    a, val = jnp.split(h, 2, axis=-1)
    pad = CONV_WIDTH // 2
    s = a.shape[1]
    ap = jnp.pad(a, ((0, 0), (pad, pad), (0, 0)))
    conv = conv_b
    for t in range(CONV_WIDTH):
        conv = conv + ap[:, t:t + s] * conv_w[t]
    return (jax.nn.gelu(conv, approximate=False) * val).astype(x.dtype) @ w_down


def encoder_layer(x, bias_vec, lam_init, w_in, decay_logit, lq1, lk1, lq2, lk2, dn_g,
                  w_out, ln_g, ln_b, w_up, conv_w, conv_b, w_down):
    b, s, _ = x.shape
    proj = x @ w_in
    r = RET_WIDTH
    splits = [r, 2 * r, 3 * r, 4 * r, 4 * r + DIFF_WIDTH, 4 * r + 2 * DIFF_WIDTH]
    rq, rk, rv, rg, dq, dk, dv = jnp.split(proj, splits, axis=-1)

    log_g = jax.nn.log_sigmoid(decay_logit.astype(jnp.float32))
    qr = rotary(to_heads(rq, N_RET_HEADS))
    kr = rotary(to_heads(rk, N_RET_HEADS)) * (HEAD_DIM ** -0.5)
    vr = to_heads(rv, N_RET_HEADS).astype(jnp.float32)
    yr = head_rms(bidir_retention(qr, kr, vr, log_g[0], log_g[1]))
    yr = yr.transpose(0, 2, 1, 3).reshape(b, s, RET_WIDTH)
    yr = (jax.nn.silu(rg.astype(jnp.float32)) * yr).astype(x.dtype)

    lam = (jnp.exp(jnp.sum(lq1.astype(jnp.float32) * lk1.astype(jnp.float32)))
           - jnp.exp(jnp.sum(lq2.astype(jnp.float32) * lk2.astype(jnp.float32))) + lam_init)
    qd = dq.reshape(b, s, N_DIFF_HEADS, 2, DIFF_QK_DIM).transpose(0, 2, 3, 1, 4)
    kd = dk.reshape(b, s, N_DIFF_HEADS, 2, DIFF_QK_DIM).transpose(0, 2, 3, 1, 4)
    vd = to_heads(dv, N_DIFF_HEADS)
    yd = diff_attention(qd, kd, vd, lam, bias_vec)
    yd = head_rms(yd) * dn_g.astype(jnp.float32) * (1.0 - lam_init)
    yd = yd.transpose(0, 2, 1, 3).reshape(b, s, DIFF_WIDTH).astype(x.dtype)

    mix = jnp.concatenate([yr, yd], axis=-1) @ w_out
    x = layer_norm(DEEPNORM_ALPHA * x + mix, ln_g[0], ln_b[0])
    x = layer_norm(DEEPNORM_ALPHA * x + conv_glu(x, w_up, conv_w, conv_b, w_down), ln_g[1], ln_b[1])
    return x


def setup_inputs(seed: int = 0) -> dict:
    key = jax.random.key(seed)
    ks = jax.random.split(key, 20)
    f32 = jnp.float32
    x_prompt = jax.random.normal(ks[0], (BATCH, SEQ, D_MODEL), f32)
    x_sample = jax.random.normal(ks[1], (DEC_BATCH, DEC_SEQ, D_MODEL), f32)
    col_scale = jnp.concatenate([
        jnp.ones((2 * RET_WIDTH,), f32), jnp.full((RET_WIDTH,), DEEPNORM_BETA, f32),
        jnp.ones((RET_WIDTH + 2 * DIFF_WIDTH,), f32), jnp.full((DIFF_WIDTH,), DEEPNORM_BETA, f32)])
    w_in = jax.random.normal(ks[2], (DEPTH, D_MODEL, IN_WIDTH), f32) * (D_MODEL ** -0.5) * col_scale
    base = jnp.log(2.0 ** (5.0 + jnp.arange(N_RET_HEADS, dtype=f32)) - 1.0)
    ret_decay_logit = base[None, None, :] + 0.1 * jax.random.normal(ks[3], (DEPTH, 2, N_RET_HEADS), f32)
    rel_bias = 0.5 * jax.random.normal(ks[4], (N_BUCKETS, N_DIFF_HEADS), f32)
    lambda_q1 = 0.1 * jax.random.normal(ks[5], (DEPTH, DIFF_QK_DIM), f32)
    lambda_k1 = 0.1 * jax.random.normal(ks[6], (DEPTH, DIFF_QK_DIM), f32)
    lambda_q2 = 0.1 * jax.random.normal(ks[7], (DEPTH, DIFF_QK_DIM), f32)
    lambda_k2 = 0.1 * jax.random.normal(ks[8], (DEPTH, DIFF_QK_DIM), f32)
    diff_norm_g = 1.0 + 0.02 * jax.random.normal(ks[9], (DEPTH, HEAD_DIM), f32)
    w_out = jax.random.normal(ks[10], (DEPTH, MIX_WIDTH, D_MODEL), f32) * (MIX_WIDTH ** -0.5) * DEEPNORM_BETA
    ln_g = 1.0 + 0.02 * jax.random.normal(ks[11], (DEPTH, 2, D_MODEL), f32)
    ln_b = 0.02 * jax.random.normal(ks[12], (DEPTH, 2, D_MODEL), f32)
    w_up = jax.random.normal(ks[13], (DEPTH, D_MODEL, 2 * D_FF), f32) * (D_MODEL ** -0.5)
    conv_w = jax.random.normal(ks[14], (DEPTH, CONV_WIDTH, D_FF), f32) * (CONV_WIDTH ** -0.5)
    conv_b = 0.01 * jax.random.normal(ks[15], (DEPTH, D_FF), f32)
    w_down = jax.random.normal(ks[16], (DEPTH, D_FF, D_MODEL), f32) * (D_FF ** -0.5) * DEEPNORM_BETA
    return {"x_prompt": x_prompt, "x_sample": x_sample, "w_in": w_in,
            "ret_decay_logit": ret_decay_logit, "rel_bias": rel_bias,
            "lambda_q1": lambda_q1, "lambda_k1": lambda_k1, "lambda_q2": lambda_q2,
            "lambda_k2": lambda_k2, "diff_norm_g": diff_norm_g, "w_out": w_out,
            "ln_g": ln_g, "ln_b": ln_b, "w_up": w_up, "conv_w": conv_w,
            "conv_b": conv_b, "w_down": w_down}


def trunk(x, w_in, ret_decay_logit, rel_bias, lambda_q1, lambda_k1, lambda_q2, lambda_k2,
          diff_norm_g, w_out, ln_g, ln_b, w_up, conv_w, conv_b, w_down):
    bias_vec = relative_bias_vector(rel_bias, x.shape[1])
    for l in range(DEPTH):
        lam_init = 0.8 - 0.6 * math.exp(-0.3 * l)
        x = encoder_layer(x, bias_vec, lam_init, w_in[l], ret_decay_logit[l],
                          lambda_q1[l], lambda_k1[l], lambda_q2[l], lambda_k2[l], diff_norm_g[l],
                          w_out[l], ln_g[l], ln_b[l], w_up[l], conv_w[l], conv_b[l], w_down[l])
    return x


def reference(x_prompt, x_sample, w_in, ret_decay_logit, rel_bias, lambda_q1, lambda_k1,
              lambda_q2, lambda_k2, diff_norm_g, w_out, ln_g, ln_b, w_up, conv_w, conv_b, w_down):
    y_prompt = trunk(x_prompt, w_in, ret_decay_logit, rel_bias, lambda_q1, lambda_k1, lambda_q2,
                     lambda_k2, diff_norm_g, w_out, ln_g, ln_b, w_up, conv_w, conv_b, w_down)
    y_sample = trunk(x_sample, w_in, ret_decay_logit, rel_bias, lambda_q1, lambda_k1, lambda_q2,
                     lambda_k2, diff_norm_g, w_out, ln_g, ln_b, w_up, conv_w, conv_b, w_down)
    return (y_prompt, y_sample)
```

```python
import functools
import math

import jax
import jax.numpy as jnp
from jax import lax
from jax.experimental import pallas as pl
from jax.experimental.pallas import tpu as pltpu

D_MODEL = 1024
HEAD_DIM = 64
N_HEADS = 8
GROUP_WIDTH = N_HEADS * HEAD_DIM
DIFF_QK_DIM = HEAD_DIM // 2
D_FF = 2816
N_BUCKETS = 32
MAX_DISTANCE = 128
ROPE_BASE = 10000.0
LN_EPS = 1e-5
HEAD_NORM_EPS = 1e-6
LANES = 128
FAR_DISTANCE = 91
NEG_BIG = -1e30
VMEM_LIMIT = 56 * 1024 * 1024

F32 = jnp.float32
BF16 = jnp.bfloat16


def _cparams(*sem):
    return pltpu.CompilerParams(dimension_semantics=sem, vmem_limit_bytes=VMEM_LIMIT)


def _proj_kernel(x_ref, w_ref, s_ref, o_ref):
    acc = jnp.dot(x_ref[...], w_ref[...], preferred_element_type=F32)
    o_ref[...] = (acc * s_ref[...]).astype(o_ref.dtype)


def _project(xb, w, scale, col0, n_out, out_dtype, tm, tn):
    t, k = xb.shape
    cb = col0 // tn
    return pl.pallas_call(
        _proj_kernel,
        grid=(t // tm, n_out // tn),
        in_specs=[pl.BlockSpec((tm, k), lambda i, j: (i, 0)),
                  pl.BlockSpec((k, tn), lambda i, j: (0, j + cb)),
                  pl.BlockSpec((1, tn), lambda i, j: (0, j + cb))],
        out_specs=pl.BlockSpec((tm, tn), lambda i, j: (i, j)),
        out_shape=jax.ShapeDtypeStruct((t, n_out), out_dtype),
        compiler_params=_cparams("parallel", "arbitrary"),
        name="in_proj",
    )(xb, w, scale)


def _ret_kernel(lg_ref, q_ref, k_ref, v_ref, g_ref, cos_ref, sin_ref, o_ref,
                rf_ref, rb_ref, rnext_ref, mask_ref, tab_ref, *, nc, chunk):
    p = pl.program_id(1)
    t = pl.program_id(2)
    c = chunk
    lane = lax.broadcasted_iota(jnp.int32, (c, LANES), 1)
    head0 = lane < HEAD_DIM
    low_half = (lane % HEAD_DIM) < (HEAD_DIM // 2)
    r_i = lax.broadcasted_iota(jnp.int32, (LANES, LANES), 0) // HEAD_DIM
    c_i = lax.broadcasted_iota(jnp.int32, (LANES, LANES), 1) // HEAD_DIM
    same_head = r_i == c_i

    @pl.when(t == 0)
    def _init():
        rb_ref[...] = jnp.zeros_like(rb_ref)
        qi = lax.broadcasted_iota(jnp.int32, (c, c), 0)
        ki = lax.broadcasted_iota(jnp.int32, (c, c), 1)
        diff = (qi - ki).astype(F32)
        for hh in range(2):
            lf = lg_ref[0, 2 * p + hh]
            lb = lg_ref[1, 2 * p + hh]
            mask_ref[hh] = jnp.where(diff >= 0, jnp.exp(lf * jnp.maximum(diff, 0.0)),
                                     jnp.exp(lb * jnp.maximum(-diff, 0.0)))
        pos = lax.broadcasted_iota(jnp.int32, (c, LANES), 0).astype(F32)
        lfl = jnp.where(head0, lg_ref[0, 2 * p], lg_ref[0, 2 * p + 1])
        lbl = jnp.where(head0, lg_ref[1, 2 * p], lg_ref[1, 2 * p + 1])
        tab_ref[0] = jnp.exp(lfl * (c - 1 - pos))
        tab_ref[1] = jnp.exp(lbl * pos)
        tab_ref[2] = jnp.exp(lfl * (pos + 1.0))
        tab_ref[3] = jnp.exp(lbl * (c - pos))
        tab_ref[4] = jnp.exp(lfl * c)
        tab_ref[5] = jnp.exp(lbl * c)

    def rotary(x):
        swapped = jnp.where(low_half, pltpu.roll(x, LANES - HEAD_DIM // 2, 1),
                            pltpu.roll(x, HEAD_DIM // 2, 1))
        return x * cos_ref[...] + swapped * sin_ref[...]

    def summary(kw, v):
        kv = lax.dot_general(kw.astype(BF16), v, (((0,), (0,)), ((), ())),
                             preferred_element_type=F32)
        return jnp.where(same_head, kv, 0.0)

    k = rotary(k_ref[...])
    vb = v_ref[...].astype(BF16)

    @pl.when(t < nc)
    def _backward():
        rnext_ref[nc - 1 - t] = rb_ref[...]
        rb_ref[...] = tab_ref[5][:LANES] * rb_ref[...] + summary(k * tab_ref[1], vb)

    @pl.when(t >= nc)
    def _forward():
        @pl.when(t == nc)
        def _():
            rf_ref[...] = jnp.zeros_like(rf_ref)

        q = rotary(q_ref[...])
        qb = q.astype(BF16)
        kb = k.astype(BF16)
        outs = []
        for hh in range(2):
            qm = jnp.where(head0 if hh == 0 else jnp.logical_not(head0), qb, jnp.zeros_like(qb))
            s = lax.dot_general(qm, kb, (((1,), (1,)), ((), ())), preferred_element_type=F32)
            s = s * mask_ref[hh]
            outs.append(jnp.dot(s.astype(BF16), vb, preferred_element_type=F32))
        y = jnp.where(head0, outs[0], outs[1])
        y = y + jnp.dot(qb, rf_ref[...].astype(BF16), preferred_element_type=F32) * tab_ref[2]
        y = y + jnp.dot(qb, rnext_ref[t - nc].astype(BF16), preferred_element_type=F32) * tab_ref[3]

        sq = y * y
        s0 = jnp.sum(jnp.where(head0, sq, 0.0), axis=1, keepdims=True)
        s1 = jnp.sum(jnp.where(head0, 0.0, sq), axis=1, keepdims=True)
        ms = jnp.where(head0, s0, s1) * (1.0 / HEAD_DIM)
        y = y * lax.rsqrt(ms + HEAD_NORM_EPS)
        g = g_ref[...]
        o_ref[...] = (g / (1.0 + jnp.exp(-g)) * y).astype(o_ref.dtype)

        rf_ref[...] = tab_ref[4][:LANES] * rf_ref[...] + summary(k * tab_ref[0], vb)


def _retention(ret, log_g, cos_t, sin_t, row0, batch, seq, chunk):
    nc = seq // chunk
    blk0 = row0 // chunk
    n_pairs = GROUP_WIDTH // LANES

    def kc(t):
        return jnp.where(t < nc, nc - 1 - t, t - nc)

    def qc(t):
        return jnp.maximum(t - nc, 0)

    def rows(b, cc):
        return blk0 + b * nc + cc

    kernel = functools.partial(_ret_kernel, nc=nc, chunk=chunk)
    return pl.pallas_call(
        kernel,
        grid=(batch, n_pairs, 2 * nc),
        in_specs=[pl.BlockSpec(memory_space=pltpu.SMEM),
                  pl.BlockSpec((chunk, LANES), lambda b, p, t: (rows(b, qc(t)), p)),
                  pl.BlockSpec((chunk, LANES), lambda b, p, t: (rows(b, kc(t)), n_pairs + p)),
                  pl.BlockSpec((chunk, LANES), lambda b, p, t: (rows(b, kc(t)), 2 * n_pairs + p)),
                  pl.BlockSpec((chunk, LANES), lambda b, p, t: (rows(b, qc(t)), 3 * n_pairs + p)),
                  pl.BlockSpec((chunk, LANES), lambda b, p, t: (kc(t), 0)),
                  pl.BlockSpec((chunk, LANES), lambda b, p, t: (kc(t), 0))],
        out_specs=pl.BlockSpec((chunk, LANES), lambda b, p, t: (b * nc + qc(t), p)),
        out_shape=jax.ShapeDtypeStruct((batch * seq, GROUP_WIDTH), BF16),
        scratch_shapes=[pltpu.VMEM((LANES, LANES), F32),
                        pltpu.VMEM((LANES, LANES), F32),
                        pltpu.VMEM((nc, LANES, LANES), F32),
                        pltpu.VMEM((2, chunk, chunk), F32),
                        pltpu.VMEM((6, chunk, LANES), F32)],
        compiler_params=_cparams("parallel", "parallel", "arbitrary"),
        name="retention",
    )(log_g, ret, ret, ret, ret, cos_t, sin_t)


def _bias_kernel(rb_ref, bucket_ref, o_ref):
    h = pl.program_id(0)
    bk = bucket_ref[...]
    out = jnp.zeros(bk.shape, F32)
    for n in range(N_BUCKETS):
        out = jnp.where(bk == n, rb_ref[n, h], out)
    o_ref[...] = out


def _bias_tiles(rel_bias, tile):
    a = jnp.arange(tile, dtype=jnp.int32)[:, None]
    b = jnp.arange(tile, dtype=jnp.int32)[None, :]
    rel = jnp.stack([(d * tile + a - b) for d in (-1, 0, 1)])
    nb = N_BUCKETS // 2
    max_exact = nb // 2
    n = jnp.abs(rel)
    nf = jnp.maximum(n, 1).astype(F32)
    large = max_exact + (jnp.log(nf / max_exact) / math.log(MAX_DISTANCE / max_exact)
                         * (nb - max_exact)).astype(jnp.int32)
    large = jnp.minimum(large, nb - 1)
    bucket = jnp.where(rel > 0, nb, 0) + jnp.where(n < max_exact, n, large)
    return pl.pallas_call(
        _bias_kernel,
        grid=(N_HEADS, 3),
        in_specs=[pl.BlockSpec(memory_space=pltpu.SMEM),
                  pl.BlockSpec((None, tile, tile), lambda h, d: (d, 0, 0))],
        out_specs=pl.BlockSpec((None, None, tile, tile), lambda h, d: (h, d, 0, 0)),
        out_shape=jax.ShapeDtypeStruct((N_HEADS, 3, tile, tile), F32),
        compiler_params=_cparams("parallel", "arbitrary"),
        name="t5_bias_tiles",
    )(rel_bias.astype(F32), bucket.astype(jnp.int32))


def _attn_kernel(rb_ref, lamp_ref, dng_ref, q_ref, k_ref, vt_ref, bt_ref, o_ref,
                 m1_ref, l1_ref, a1_ref, m2_ref, l2_ref, a2_ref, *, n_tiles, tile, lam_init):
    h = pl.program_id(1)
    qi = pl.program_id(2)
    hh = h % 2
    n = n_tiles

    q = q_ref[...]
    lane = lax.broadcasted_iota(jnp.int32, q.shape, 1)
    base = hh * HEAD_DIM
    zero = jnp.zeros_like(q)
    qz1 = jnp.where((lane >= base) & (lane < base + DIFF_QK_DIM), q, zero)
    qz2 = jnp.where((lane >= base + DIFF_QK_DIM) & (lane < base + HEAD_DIM), q, zero)

    for m_ref, l_ref, a_ref in ((m1_ref, l1_ref, a1_ref), (m2_ref, l2_ref, a2_ref)):
        m_ref[...] = jnp.full(m_ref.shape, NEG_BIG, F32)
        l_ref[...] = jnp.zeros_like(l_ref)
        a_ref[...] = jnp.zeros_like(a_ref)

    def update(s, vt, m_ref, l_ref, a_ref):
        m_prev = m_ref[...]
        m_new = jnp.maximum(m_prev, jnp.max(s, axis=0, keepdims=True))
        alpha = jnp.exp(m_prev - m_new)
        pr = jnp.exp(s - m_new)
        l_ref[...] = alpha * l_ref[...] + jnp.sum(pr, axis=0, keepdims=True)
        a_ref[...] = alpha * a_ref[...] + jnp.dot(vt, pr.astype(BF16), preferred_element_type=F32)
        m_ref[...] = m_new

    def step(j, bias):
        kt = k_ref[pl.ds(pl.multiple_of(j * tile, tile), tile), :]
        vt = vt_ref[j]
        dims = (((1,), (1,)), ((), ()))
        s1 = lax.dot_general(kt, qz1, dims, preferred_element_type=F32) + bias
        update(s1, vt, m1_ref, l1_ref, a1_ref)
        s2 = lax.dot_general(kt, qz2, dims, preferred_element_type=F32) + bias
        update(s2, vt, m2_ref, l2_ref, a2_ref)

    c_before = rb_ref[N_BUCKETS // 2 - 1, h]
    c_after = rb_ref[N_BUCKETS - 1, h]
    n_before = jnp.maximum(qi - 1, 0)
    first_after = jnp.minimum(qi + 2, n)

    def far_body(i, carry):
        j = jnp.where(i < n_before, i, i - n_before + first_after)
        step(j, jnp.where(i < n_before, c_before, c_after))
        return carry

    lax.fori_loop(0, n_before + (n - first_after), far_body, 0)

    def near_body(d, carry):
        j = qi + d - 1

        @pl.when((j >= 0) & (j < n))
        def _():
            step(j, bt_ref[d])
        return carry

    lax.fori_loop(0, 3, near_body, 0)

    lp = lamp_ref[...]
    lam = (jnp.exp(jnp.sum(lp[0:1] * lp[1:2], axis=1, keepdims=True))
           - jnp.exp(jnp.sum(lp[2:3] * lp[3:4], axis=1, keepdims=True)) + lam_init)
    o = a1_ref[...] / l1_ref[...] - lam * (a2_ref[...] / l2_ref[...])
    ms = jnp.mean(o * o, axis=0, keepdims=True)
    y = o * lax.rsqrt(ms + HEAD_NORM_EPS) * dng_ref[...] * (1.0 - lam_init)
    o_ref[...] = y.astype(o_ref.dtype)


def _diff_attention(dqkv, vt3, bias_t, rel_bias, lamp, dng, row0, batch, seq, tile, lam_init):
    n = seq // tile
    n_pairs = GROUP_WIDTH // LANES
    qblk0 = row0 // tile
    sblk0 = row0 // seq
    kernel = functools.partial(_attn_kernel, n_tiles=n, tile=tile, lam_init=lam_init)
    return pl.pallas_call(
        kernel,
        grid=(batch, N_HEADS, n),
        in_specs=[pl.BlockSpec(memory_space=pltpu.SMEM),
                  pl.BlockSpec((4, DIFF_QK_DIM), lambda b, h, i: (0, 0)),
                  pl.BlockSpec((HEAD_DIM, 1), lambda b, h, i: (0, 0)),
                  pl.BlockSpec((tile, LANES), lambda b, h, i: (qblk0 + b * n + i, h // 2)),
                  pl.BlockSpec((seq, LANES), lambda b, h, i: (sblk0 + b, n_pairs + h // 2)),
                  pl.BlockSpec((n, HEAD_DIM, tile), lambda b, h, i: (sblk0 + b, h, 0)),
                  pl.BlockSpec((None, 3, tile, tile), lambda b, h, i: (h, 0, 0, 0))],
        out_specs=pl.BlockSpec((None, HEAD_DIM, tile), lambda b, h, i: (b * n + i, h, 0)),
        out_shape=jax.ShapeDtypeStruct((batch * n, GROUP_WIDTH, tile), BF16),
        scratch_shapes=[pltpu.VMEM((1, tile), F32), pltpu.VMEM((1, tile), F32),
                        pltpu.VMEM((HEAD_DIM, tile), F32),
                        pltpu.VMEM((1, tile), F32), pltpu.VMEM((1, tile), F32),
                        pltpu.VMEM((HEAD_DIM, tile), F32)],
        compiler_params=_cparams("parallel", "parallel", "arbitrary"),
        name="diff_attention",
    )(rel_bias.astype(F32), lamp, dng, dqkv, dqkv, vt3, bias_t)


def _layer_norm(y, g, b):
    mu = jnp.mean(y, axis=-1, keepdims=True)
    d = y - mu
    var = jnp.mean(d * d, axis=-1, keepdims=True)
    return d * lax.rsqrt(var + LN_EPS) * g + b


def _outproj_kernel(yr_ref, yd_ref, w_ref, x_ref, g_ref, b_ref, o_ref, ob_ref, *, alpha):
    mix = jnp.dot(yr_ref[...], w_ref[:GROUP_WIDTH, :], preferred_element_type=F32)
    mix = mix + jnp.dot(yd_ref[...], w_ref[GROUP_WIDTH:, :], preferred_element_type=F32)
    y = _layer_norm(alpha * x_ref[...] + mix, g_ref[...], b_ref[...])
    o_ref[...] = y
    ob_ref[...] = y.astype(BF16)


def _out_projection(yr, yd, w_out, x, g, b, alpha, tm):
    t = x.shape[0]
    kernel = functools.partial(_outproj_kernel, alpha=alpha)
    row = lambda i: (i, 0)
    fixed = lambda i: (0, 0)
    return pl.pallas_call(
        kernel,
        grid=(t // tm,),
        in_specs=[pl.BlockSpec((tm, GROUP_WIDTH), row),
                  pl.BlockSpec((tm, GROUP_WIDTH), row),
                  pl.BlockSpec((2 * GROUP_WIDTH, D_MODEL), fixed),
                  pl.BlockSpec((tm, D_MODEL), row),
                  pl.BlockSpec((1, D_MODEL), fixed),
                  pl.BlockSpec((1, D_MODEL), fixed)],
        out_specs=[pl.BlockSpec((tm, D_MODEL), row), pl.BlockSpec((tm, D_MODEL), row)],
        out_shape=[jax.ShapeDtypeStruct((t, D_MODEL), F32), jax.ShapeDtypeStruct((t, D_MODEL), BF16)],
        compiler_params=_cparams("parallel"),
        name="out_proj_ln",
    )(yr, yd, w_out, x, g, b)


def _ffn_kernel(x_ref, xb_ref, xp_ref, xn_ref, wa_ref, wv_ref, wd_ref, cp_ref, g_ref, b_ref,
                o_ref, ob_ref, acc_ref, *, alpha, tm, n_chunks, starts, ends):
    i = pl.program_id(0)
    t0 = i * tm
    is_start = functools.reduce(jnp.logical_or, [t0 == s for s in starts])
    is_end = functools.reduce(jnp.logical_or, [t0 + tm == e for e in ends])
    keep_prev = jnp.where(is_start, 0.0, 1.0)
    keep_next = jnp.where(is_end, 0.0, 1.0)

    xb = xb_ref[...]
    halo = jnp.concatenate([xp_ref[...], xn_ref[...]], axis=0)
    hrows = xp_ref.shape[0]
    acc_ref[...] = jnp.zeros_like(acc_ref)

    def chunk_body(c, carry):
        wa = wa_ref[c]
        a = jnp.dot(xb, wa, preferred_element_type=F32)
        val = jnp.dot(xb, wv_ref[c], preferred_element_type=F32)
        ah = jnp.dot(halo, wa, preferred_element_type=F32)
        prev_row = ah[hrows - 1:hrows] * keep_prev
        next_row = ah[hrows:hrows + 1] * keep_next
        row = lax.broadcasted_iota(jnp.int32, a.shape, 0)
        a_m1 = jnp.where(row == 0, prev_row, pltpu.roll(a, 1, 0))
        a_p1 = jnp.where(row == tm - 1, next_row, pltpu.roll(a, tm - 1, 0))
        cp = cp_ref[c]
        conv = cp[3:4] + a_m1 * cp[0:1]
        conv = conv + a * cp[1:2]
        conv = conv + a_p1 * cp[2:3]
        gelu = 0.5 * conv * (1.0 + lax.erf(conv * (1.0 / math.sqrt(2.0))))
        hidden = (gelu * val).astype(BF16)
        acc_ref[...] += jnp.dot(hidden, wd_ref[c], preferred_element_type=F32)
        return carry

    lax.fori_loop(0, n_chunks, chunk_body, 0)
    y = _layer_norm(alpha * x_ref[...] + acc_ref[...], g_ref[...], b_ref[...])
    o_ref[...] = y
    ob_ref[...] = y.astype(BF16)


def _conv_glu(x, xb, wa, wv, wd, cp, g, b, alpha, tm, groups):
    t = x.shape[0]
    n_chunks, _, ck = wa.shape
    hrows = 16
    starts = tuple(r0 + bi * s for (r0, nb, s) in groups for bi in range(nb))
    ends = tuple(r0 + (bi + 1) * s for (r0, nb, s) in groups for bi in range(nb))
    kernel = functools.partial(_ffn_kernel, alpha=alpha, tm=tm, n_chunks=n_chunks,
                               starts=starts, ends=ends)
    row = lambda i: (i, 0)
    fixed2 = lambda i: (0, 0)
    fixed3 = lambda i: (0, 0, 0)
    per = tm // hrows
    last = t // hrows - 1
    return pl.pallas_call(
        kernel,
        grid=(t // tm,),
        in_specs=[pl.BlockSpec((tm, D_MODEL), row),
                  pl.BlockSpec((tm, D_MODEL), row),
                  pl.BlockSpec((hrows, D_MODEL), lambda i: (jnp.maximum(i * per - 1, 0), 0)),
                  pl.BlockSpec((hrows, D_MODEL), lambda i: (jnp.minimum((i + 1) * per, last), 0)),
                  pl.BlockSpec((n_chunks, D_MODEL, ck), fixed3),
                  pl.BlockSpec((n_chunks, D_MODEL, ck), fixed3),
                  pl.BlockSpec((n_chunks, ck, D_MODEL), fixed3),
                  pl.BlockSpec((n_chunks, 8, ck), fixed3),
                  pl.BlockSpec((1, D_MODEL), fixed2),
                  pl.BlockSpec((1, D_MODEL), fixed2)],
        out_specs=[pl.BlockSpec((tm, D_MODEL), row), pl.BlockSpec((tm, D_MODEL), row)],
        out_shape=[jax.ShapeDtypeStruct((t, D_MODEL), F32), jax.ShapeDtypeStruct((t, D_MODEL), BF16)],
        scratch_shapes=[pltpu.VMEM((tm, D_MODEL), F32)],
        compiler_params=_cparams("parallel"),
        name="conv_glu_ln",
    )(x, xb, xb, xb, wa, wv, wd, cp, g, b)


def _rotary_tables(seq):
    d = HEAD_DIM
    inv = 1.0 / (ROPE_BASE ** (jnp.arange(0, d, 2, dtype=F32) / d))
    ang = jnp.arange(seq, dtype=F32)[:, None] * inv[None, :]
    cos, sin = jnp.cos(ang), jnp.sin(ang)
    cos_t = jnp.concatenate([cos, cos, cos, cos], axis=-1)
    sin_t = jnp.concatenate([-sin, sin, -sin, sin], axis=-1)
    return cos_t, sin_t


def _tiles(groups):
    smin = min(s for (_, _, s) in groups)
    attn_tile = min(512, smin // 2)
    chunk = min(256, smin // 2)
    tm = min(512, smin // 2)
    return attn_tile, chunk, tm


def _forward(x, groups, w_in, ret_decay_logit, rel_bias, lambda_q1, lambda_k1, lambda_q2,
             lambda_k2, diff_norm_g, w_out, ln_g, ln_b, w_up, conv_w, conv_b, w_down):
    depth = w_in.shape[0]
    alpha = (2 * depth) ** 0.25
    t = x.shape[0]
    attn_tile, chunk, tm = _tiles(groups)
    smax = max(s for (_, _, s) in groups)
    gw = GROUP_WIDTH
    ck = 256
    n_chunks = D_FF // ck

    cos_t, sin_t = _rotary_tables(smax)
    bias_t = _bias_tiles(rel_bias, attn_tile)
    in_scale = jnp.concatenate([
        jnp.ones((gw,), F32), jnp.full((gw,), HEAD_DIM ** -0.5, F32), jnp.ones((2 * gw,), F32),
        jnp.full((gw,), DIFF_QK_DIM ** -0.5, F32), jnp.ones((2 * gw,), F32)])[None, :]

    xb = x.astype(BF16)
    for l in range(depth):
        lam_init = 0.8 - 0.6 * math.exp(-0.3 * l)
        w_in_b = w_in[l].astype(BF16)
        ret = _project(xb, w_in_b, in_scale, 0, 4 * gw, F32, tm, 512)
        dqkv = _project(xb, w_in_b, in_scale, 4 * gw, 3 * gw, BF16, tm, 512)
        vt3 = dqkv[:, 2 * gw:].reshape(t // attn_tile, attn_tile, gw).transpose(0, 2, 1)

        log_g = jax.nn.log_sigmoid(ret_decay_logit[l].astype(F32))
        lamp = jnp.stack([lambda_q1[l], lambda_k1[l], lambda_q2[l], lambda_k2[l]]).astype(F32)
        dng = diff_norm_g[l].astype(F32)[:, None]

        yr_parts, yd_parts = [], []
        for (row0, batch, seq) in groups:
            yr_parts.append(_retention(ret, log_g, cos_t, sin_t, row0, batch, seq, chunk))
            ydt = _diff_attention(dqkv, vt3, bias_t, rel_bias, lamp, dng, row0, batch, seq,
                                  attn_tile, lam_init)
            yd_parts.append(ydt.transpose(0, 2, 1).reshape(batch * seq, gw))
        yr = jnp.concatenate(yr_parts, axis=0)
        yd = jnp.concatenate(yd_parts, axis=0)

        x, xb = _out_projection(yr, yd, w_out[l].astype(BF16), x, ln_g[l, 0][None, :].astype(F32),
                                ln_b[l, 0][None, :].astype(F32), alpha, tm)

        wa = w_up[l][:, :D_FF].astype(BF16).reshape(D_MODEL, n_chunks, ck).transpose(1, 0, 2)
        wv = w_up[l][:, D_FF:].astype(BF16).reshape(D_MODEL, n_chunks, ck).transpose(1, 0, 2)
        wd = w_down[l].astype(BF16).reshape(n_chunks, ck, D_MODEL)
        cp = jnp.concatenate([conv_w[l].astype(F32), conv_b[l].astype(F32)[None, :],
                              jnp.zeros((4, D_FF), F32)], axis=0)
        cp = cp.reshape(8, n_chunks, ck).transpose(1, 0, 2)
        x, xb = _conv_glu(x, xb, wa, wv, wd, cp, ln_g[l, 1][None, :].astype(F32),
                          ln_b[l, 1][None, :].astype(F32), alpha, tm, groups)
    return x


def kernel(x_prompt, x_sample, w_in, ret_decay_logit, rel_bias, lambda_q1, lambda_k1, lambda_q2,
           lambda_k2, diff_norm_g, w_out, ln_g, ln_b, w_up, conv_w, conv_b, w_down):
    bp, sp, d = x_prompt.shape
    bs, ss, _ = x_sample.shape
    groups = ((0, bp, sp), (bp * sp, bs, ss))
    x = jnp.concatenate([x_prompt.reshape(bp * sp, d), x_sample.reshape(bs * ss, d)], axis=0)
    y = _forward(x.astype(F32), groups, w_in, ret_decay_logit, rel_bias, lambda_q1, lambda_k1,
                 lambda_q2, lambda_k2, diff_norm_g, w_out, ln_g, ln_b, w_up, conv_w, conv_b, w_down)
    y_prompt = y[:bp * sp].reshape(bp, sp, d).astype(x_prompt.dtype)
    y_sample = y[bp * sp:].reshape(bs, ss, d).astype(x_sample.dtype)
    return y_prompt, y_sample
```

```python
import functools
import math

import jax
import jax.numpy as jnp
from jax import lax
from jax.experimental import pallas as pl
from jax.experimental.pallas import tpu as pltpu

D_MODEL = 1024
HEAD_DIM = 64
N_HEADS = 8
GROUP_WIDTH = N_HEADS * HEAD_DIM
DIFF_QK_DIM = HEAD_DIM // 2
D_FF = 2816
N_BUCKETS = 32
MAX_DISTANCE = 128
ROPE_BASE = 10000.0
LN_EPS = 1e-5
HEAD_NORM_EPS = 1e-6
LANES = 128
FAR_DISTANCE = 91
BIAS_REACH = 2
V_ROWS = HEAD_DIM + 16
LOG2E = math.log2(math.e)
NEG_BIG = -1e30
VMEM_LIMIT = 56 * 1024 * 1024

F32 = jnp.float32
BF16 = jnp.bfloat16


def _cparams(*sem):
    return pltpu.CompilerParams(dimension_semantics=sem, vmem_limit_bytes=VMEM_LIMIT)


def _proj_kernel(x_ref, w_ref, s_ref, o_ref):
    acc = jnp.dot(x_ref[...], w_ref[...], preferred_element_type=F32)
    o_ref[...] = (acc * s_ref[...]).astype(o_ref.dtype)


def _project(xb, w, scale, col0, n_out, out_dtype, tm, tn):
    t, k = xb.shape
    cb = col0 // tn
    return pl.pallas_call(
        _proj_kernel,
        grid=(t // tm, n_out // tn),
        in_specs=[pl.BlockSpec((tm, k), lambda i, j: (i, 0)),
                  pl.BlockSpec((k, tn), lambda i, j: (0, j + cb)),
                  pl.BlockSpec((1, tn), lambda i, j: (0, j + cb))],
        out_specs=pl.BlockSpec((tm, tn), lambda i, j: (i, j)),
        out_shape=jax.ShapeDtypeStruct((t, n_out), out_dtype),
        compiler_params=_cparams("parallel", "arbitrary"),
        name="in_proj",
    )(xb, w, scale)


def _ret_kernel(lg_ref, q_ref, k_ref, v_ref, g_ref, cos_ref, sin_ref, o_ref,
                rf_ref, rb_ref, rnext_ref, mask_ref, tab_ref, *, nc, chunk):
    p = pl.program_id(1)
    t = pl.program_id(2)
    c = chunk
    lane = lax.broadcasted_iota(jnp.int32, (c, LANES), 1)
    head0 = lane < HEAD_DIM
    low_half = (lane % HEAD_DIM) < (HEAD_DIM // 2)
    r_i = lax.broadcasted_iota(jnp.int32, (LANES, LANES), 0) // HEAD_DIM
    c_i = lax.broadcasted_iota(jnp.int32, (LANES, LANES), 1) // HEAD_DIM
    same_head = r_i == c_i

    @pl.when(t == 0)
    def _init():
        rb_ref[...] = jnp.zeros_like(rb_ref)
        qi = lax.broadcasted_iota(jnp.int32, (c, c), 0)
        ki = lax.broadcasted_iota(jnp.int32, (c, c), 1)
        diff = (qi - ki).astype(F32)
        for hh in range(2):
            lf = lg_ref[0, 2 * p + hh]
            lb = lg_ref[1, 2 * p + hh]
            mask_ref[hh] = jnp.where(diff >= 0, jnp.exp(lf * jnp.maximum(diff, 0.0)),
                                     jnp.exp(lb * jnp.maximum(-diff, 0.0)))
        pos = lax.broadcasted_iota(jnp.int32, (c, LANES), 0).astype(F32)
        lfl = jnp.where(head0, lg_ref[0, 2 * p], lg_ref[0, 2 * p + 1])
        lbl = jnp.where(head0, lg_ref[1, 2 * p], lg_ref[1, 2 * p + 1])
        tab_ref[0] = jnp.exp(lfl * (c - 1 - pos))
        tab_ref[1] = jnp.exp(lbl * pos)
        tab_ref[2] = jnp.exp(lfl * (pos + 1.0))
        tab_ref[3] = jnp.exp(lbl * (c - pos))
        tab_ref[4] = jnp.exp(lfl * c)
        tab_ref[5] = jnp.exp(lbl * c)

    def rotary(x):
        swapped = jnp.where(low_half, pltpu.roll(x, LANES - HEAD_DIM // 2, 1),
                            pltpu.roll(x, HEAD_DIM // 2, 1))
        return x * cos_ref[...] + swapped * sin_ref[...]

    def summary(kw, v):
        kv = lax.dot_general(kw.astype(BF16), v, (((0,), (0,)), ((), ())),
                             preferred_element_type=F32)
        return jnp.where(same_head, kv, 0.0)

    k = rotary(k_ref[...])
    vb = v_ref[...].astype(BF16)

    @pl.when(t < nc)
    def _backward():
        rnext_ref[nc - 1 - t] = rb_ref[...]
        rb_ref[...] = tab_ref[5][:LANES] * rb_ref[...] + summary(k * tab_ref[1], vb)

    @pl.when(t >= nc)
    def _forward():
        @pl.when(t == nc)
        def _():
            rf_ref[...] = jnp.zeros_like(rf_ref)

        q = rotary(q_ref[...])
        qb = q.astype(BF16)
        kb = k.astype(BF16)
        outs = []
        for hh in range(2):
            qm = jnp.where(head0 if hh == 0 else jnp.logical_not(head0), qb, jnp.zeros_like(qb))
            s = lax.dot_general(qm, kb, (((1,), (1,)), ((), ())), preferred_element_type=F32)
            s = s * mask_ref[hh]
            outs.append(jnp.dot(s.astype(BF16), vb, preferred_element_type=F32))
        y = jnp.where(head0, outs[0], outs[1])
        y = y + jnp.dot(qb, rf_ref[...].astype(BF16), preferred_element_type=F32) * tab_ref[2]
        y = y + jnp.dot(qb, rnext_ref[t - nc].astype(BF16), preferred_element_type=F32) * tab_ref[3]

        sq = y * y
        s0 = jnp.sum(jnp.where(head0, sq, 0.0), axis=1, keepdims=True)
        s1 = jnp.sum(jnp.where(head0, 0.0, sq), axis=1, keepdims=True)
        ms = jnp.where(head0, s0, s1) * (1.0 / HEAD_DIM)
        y = y * lax.rsqrt(ms + HEAD_NORM_EPS)
        g = g_ref[...]
        o_ref[...] = (g / (1.0 + jnp.exp(-g)) * y).astype(o_ref.dtype)

        rf_ref[...] = tab_ref[4][:LANES] * rf_ref[...] + summary(k * tab_ref[0], vb)


def _retention(ret, log_g, cos_t, sin_t, row0, batch, seq, chunk):
    nc = seq // chunk
    blk0 = row0 // chunk
    n_pairs = GROUP_WIDTH // LANES

    def kc(t):
        return jnp.where(t < nc, nc - 1 - t, t - nc)

    def qc(t):
        return jnp.maximum(t - nc, 0)

    def rows(b, cc):
        return blk0 + b * nc + cc

    kernel = functools.partial(_ret_kernel, nc=nc, chunk=chunk)
    return pl.pallas_call(
        kernel,
        grid=(batch, n_pairs, 2 * nc),
        in_specs=[pl.BlockSpec(memory_space=pltpu.SMEM),
                  pl.BlockSpec((chunk, LANES), lambda b, p, t: (rows(b, qc(t)), p)),
                  pl.BlockSpec((chunk, LANES), lambda b, p, t: (rows(b, kc(t)), n_pairs + p)),
                  pl.BlockSpec((chunk, LANES), lambda b, p, t: (rows(b, kc(t)), 2 * n_pairs + p)),
                  pl.BlockSpec((chunk, LANES), lambda b, p, t: (rows(b, qc(t)), 3 * n_pairs + p)),
                  pl.BlockSpec((chunk, LANES), lambda b, p, t: (kc(t), 0)),
                  pl.BlockSpec((chunk, LANES), lambda b, p, t: (kc(t), 0))],
        out_specs=pl.BlockSpec((chunk, LANES), lambda b, p, t: (b * nc + qc(t), p)),
        out_shape=jax.ShapeDtypeStruct((batch * seq, GROUP_WIDTH), BF16),
        scratch_shapes=[pltpu.VMEM((LANES, LANES), F32),
                        pltpu.VMEM((LANES, LANES), F32),
                        pltpu.VMEM((nc, LANES, LANES), F32),
                        pltpu.VMEM((2, chunk, chunk), F32),
                        pltpu.VMEM((6, chunk, LANES), F32)],
        compiler_params=_cparams("parallel", "parallel", "arbitrary"),
        name="retention",
    )(log_g, ret, ret, ret, ret, cos_t, sin_t)


def _bias_kernel(rb_ref, bucket_ref, o_ref):
    h = pl.program_id(0)
    bk = bucket_ref[...]
    out = jnp.zeros(bk.shape, F32)
    for n in range(N_BUCKETS):
        out = jnp.where(bk == n, rb_ref[n, h], out)
    o_ref[...] = out * LOG2E


def _bias_tiles(rel_bias, tile):
    assert tile > FAR_DISTANCE
    a = jnp.arange(tile, dtype=jnp.int32)[:, None]
    b = jnp.arange(tile, dtype=jnp.int32)[None, :]
    rel = jnp.stack([(d * tile + a - b) for d in range(-BIAS_REACH, BIAS_REACH + 1)])
    nb = N_BUCKETS // 2
    max_exact = nb // 2
    n = jnp.abs(rel)
    nf = jnp.maximum(n, 1).astype(F32)
    large = max_exact + (jnp.log(nf / max_exact) / math.log(MAX_DISTANCE / max_exact)
                         * (nb - max_exact)).astype(jnp.int32)
    large = jnp.minimum(large, nb - 1)
    bucket = jnp.where(rel > 0, nb, 0) + jnp.where(n < max_exact, n, large)
    return pl.pallas_call(
        _bias_kernel,
        grid=(N_HEADS, 2 * BIAS_REACH + 1),
        in_specs=[pl.BlockSpec(memory_space=pltpu.SMEM),
                  pl.BlockSpec((None, tile, tile), lambda h, d: (d, 0, 0))],
        out_specs=pl.BlockSpec((None, None, tile, tile), lambda h, d: (h, d, 0, 0)),
        out_shape=jax.ShapeDtypeStruct((N_HEADS, 2 * BIAS_REACH + 1, tile, tile), F32),
        compiler_params=_cparams("parallel", "arbitrary"),
        name="t5_bias_tiles",
    )(rel_bias.astype(F32), bucket.astype(jnp.int32))


def _attn_kernel(rb_ref, lamp_ref, dng_ref, q_ref, k_ref, vt_ref, bt_ref, o_ref,
                 s_ref, mt_ref, m_ref, a_ref, *, n_tiles, tile, lam_init):
    h = pl.program_id(1)
    qi = pl.program_id(2)
    hh = h % 2
    n = n_tiles

    q = q_ref[...]
    lane = lax.broadcasted_iota(jnp.int32, q.shape, 1)
    base = hh * HEAD_DIM
    zero = jnp.zeros_like(q)
    qz = (jnp.where((lane >= base) & (lane < base + DIFF_QK_DIM), q, zero),
          jnp.where((lane >= base + DIFF_QK_DIM) & (lane < base + HEAD_DIM), q, zero))

    m_ref[...] = jnp.full(m_ref.shape, NEG_BIG, F32)
    a_ref[...] = jnp.zeros_like(a_ref)

    def near(j):
        return jnp.abs(j - qi) < BIAS_REACH

    def scores(j, slot, with_bias):
        kt = k_ref[pl.ds(pl.multiple_of(j * tile, tile), tile), :]
        dims = (((1,), (1,)), ((), ()))
        for t in range(2):
            s = lax.dot_general(kt, qz[t], dims, preferred_element_type=F32)
            if with_bias:
                s = s + bt_ref[jnp.clip(j - qi, -BIAS_REACH, BIAS_REACH) + BIAS_REACH]
            s_ref[slot, t] = s
            mt_ref[slot, t] = jnp.max(s, axis=0, keepdims=True)

    def accumulate(j, slot, c):
        vt = vt_ref[j]
        for t in range(2):
            m_prev = m_ref[t]
            m_new = jnp.maximum(m_prev, mt_ref[slot, t] + c)
            alpha = jnp.exp2(m_prev - m_new)
            pr = jnp.exp2((s_ref[slot, t] - (m_new - c)).astype(BF16))
            a_ref[t] = alpha * a_ref[t] + jnp.dot(vt, pr, preferred_element_type=F32)
            m_ref[t] = m_new

    def scored_with_bias(j):
        b = (j - 1) // 2
        return jnp.where(j == 0, near(0), near(2 * b + 1) | near(2 * b + 2))

    def missing_bias(j):
        far = jnp.where(j < qi, rb_ref[N_BUCKETS // 2 - 1, h], rb_ref[N_BUCKETS - 1, h])
        return jnp.where(scored_with_bias(j), 0.0, far)

    lax.cond(near(0), lambda: scores(0, 0, True), lambda: scores(0, 0, False))

    def pair(j, last):
        c0 = missing_bias(j)
        c1 = missing_bias(j + 1)

        def run(with_bias):
            scores(j + 1, 1, with_bias)
            accumulate(j, 0, c0)
            if not last:
                scores(j + 2, 0, with_bias)
            accumulate(j + 1, 1, c1)

        lax.cond(scored_with_bias(j + 1), lambda: run(True), lambda: run(False))

    def pair_body(i, carry):
        pair(2 * i, False)
        return carry

    lax.fori_loop(0, n // 2 - 1, pair_body, 0)
    pair(n - 2, True)

    lp = lamp_ref[...]
    lam = (jnp.exp(jnp.sum(lp[0:1] * lp[1:2], axis=1, keepdims=True))
           - jnp.exp(jnp.sum(lp[2:3] * lp[3:4], axis=1, keepdims=True)) + lam_init)
    a1 = a_ref[0]
    a2 = a_ref[1]
    o = (a1[:HEAD_DIM] / a1[HEAD_DIM:HEAD_DIM + 1]
         - lam * (a2[:HEAD_DIM] / a2[HEAD_DIM:HEAD_DIM + 1]))
    ms = jnp.mean(o * o, axis=0, keepdims=True)
    y = o * lax.rsqrt(ms + HEAD_NORM_EPS) * dng_ref[...] * (1.0 - lam_init)
    o_ref[...] = y.astype(o_ref.dtype)


def _diff_attention(dqkv, vt3, bias_t, rel_log2, lamp, dng, row0, batch, seq, tile, lam_init):
    n = seq // tile
    assert n % 2 == 0
    n_pairs = GROUP_WIDTH // LANES
    qblk0 = row0 // tile
    sblk0 = row0 // seq
    n_bias = 2 * BIAS_REACH + 1
    kernel = functools.partial(_attn_kernel, n_tiles=n, tile=tile, lam_init=lam_init)
    return pl.pallas_call(
        kernel,
        grid=(batch, N_HEADS, n),
        in_specs=[pl.BlockSpec(memory_space=pltpu.SMEM),
                  pl.BlockSpec((4, DIFF_QK_DIM), lambda b, h, i: (0, 0)),
                  pl.BlockSpec((HEAD_DIM, 1), lambda b, h, i: (0, 0)),
                  pl.BlockSpec((tile, LANES), lambda b, h, i: (qblk0 + b * n + i, h // 2)),
                  pl.BlockSpec((seq, LANES), lambda b, h, i: (sblk0 + b, n_pairs + h // 2)),
                  pl.BlockSpec((n, V_ROWS, tile), lambda b, h, i: (sblk0 + b, h, 0)),
                  pl.BlockSpec((None, n_bias, tile, tile), lambda b, h, i: (h, 0, 0, 0))],
        out_specs=pl.BlockSpec((None, HEAD_DIM, tile), lambda b, h, i: (b * n + i, h, 0)),
        out_shape=jax.ShapeDtypeStruct((batch * n, GROUP_WIDTH, tile), BF16),
        scratch_shapes=[pltpu.VMEM((2, 2, tile, tile), F32),
                        pltpu.VMEM((2, 2, 1, tile), F32),
                        pltpu.VMEM((2, 1, tile), F32),
                        pltpu.VMEM((2, V_ROWS, tile), F32)],
        compiler_params=_cparams("parallel", "parallel", "arbitrary"),
        name="diff_attention",
    )(rel_log2, lamp, dng, dqkv, dqkv, vt3, bias_t)


def _layer_norm(y, g, b):
    mu = jnp.mean(y, axis=-1, keepdims=True)
    d = y - mu
    var = jnp.mean(d * d, axis=-1, keepdims=True)
    return d * lax.rsqrt(var + LN_EPS) * g + b


def _outproj_kernel(yr_ref, yd_ref, w_ref, x_ref, g_ref, b_ref, o_ref, ob_ref, *, alpha):
    mix = jnp.dot(yr_ref[...], w_ref[:GROUP_WIDTH, :], preferred_element_type=F32)
    mix = mix + jnp.dot(yd_ref[...], w_ref[GROUP_WIDTH:, :], preferred_element_type=F32)
    y = _layer_norm(alpha * x_ref[...] + mix, g_ref[...], b_ref[...])
    o_ref[...] = y
    ob_ref[...] = y.astype(BF16)


def _out_projection(yr, yd, w_out, x, g, b, alpha, tm):
    t = x.shape[0]
    kernel = functools.partial(_outproj_kernel, alpha=alpha)
    row = lambda i: (i, 0)
    fixed = lambda i: (0, 0)
    return pl.pallas_call(
        kernel,
        grid=(t // tm,),
        in_specs=[pl.BlockSpec((tm, GROUP_WIDTH), row),
                  pl.BlockSpec((tm, GROUP_WIDTH), row),
                  pl.BlockSpec((2 * GROUP_WIDTH, D_MODEL), fixed),
                  pl.BlockSpec((tm, D_MODEL), row),
                  pl.BlockSpec((1, D_MODEL), fixed),
                  pl.BlockSpec((1, D_MODEL), fixed)],
        out_specs=[pl.BlockSpec((tm, D_MODEL), row), pl.BlockSpec((tm, D_MODEL), row)],
        out_shape=[jax.ShapeDtypeStruct((t, D_MODEL), F32), jax.ShapeDtypeStruct((t, D_MODEL), BF16)],
        compiler_params=_cparams("parallel"),
        name="out_proj_ln",
    )(yr, yd, w_out, x, g, b)


def _ffn_kernel(x_ref, xb_ref, xp_ref, xn_ref, wa_ref, wv_ref, wd_ref, cp_ref, g_ref, b_ref,
                o_ref, ob_ref, acc_ref, *, alpha, tm, n_chunks, starts, ends):
    i = pl.program_id(0)
    t0 = i * tm
    is_start = functools.reduce(jnp.logical_or, [t0 == s for s in starts])
    is_end = functools.reduce(jnp.logical_or, [t0 + tm == e for e in ends])
    keep_prev = jnp.where(is_start, 0.0, 1.0)
    keep_next = jnp.where(is_end, 0.0, 1.0)

    xb = xb_ref[...]
    halo = jnp.concatenate([xp_ref[...], xn_ref[...]], axis=0)
    hrows = xp_ref.shape[0]
    acc_ref[...] = jnp.zeros_like(acc_ref)

    def chunk_body(c, carry):
        wa = wa_ref[c]
        a = jnp.dot(xb, wa, preferred_element_type=F32)
        val = jnp.dot(xb, wv_ref[c], preferred_element_type=F32)
        ah = jnp.dot(halo, wa, preferred_element_type=F32)
        prev_row = ah[hrows - 1:hrows] * keep_prev
        next_row = ah[hrows:hrows + 1] * keep_next
        row = lax.broadcasted_iota(jnp.int32, a.shape, 0)
        a_m1 = jnp.where(row == 0, prev_row, pltpu.roll(a, 1, 0))
        a_p1 = jnp.where(row == tm - 1, next_row, pltpu.roll(a, tm - 1, 0))
        cp = cp_ref[c]
        conv = cp[3:4] + a_m1 * cp[0:1]
        conv = conv + a * cp[1:2]
        conv = conv + a_p1 * cp[2:3]
        gelu = 0.5 * conv * (1.0 + lax.erf(conv * (1.0 / math.sqrt(2.0))))
        hidden = (gelu * val).astype(BF16)
        acc_ref[...] += jnp.dot(hidden, wd_ref[c], preferred_element_type=F32)
        return carry

    lax.fori_loop(0, n_chunks, chunk_body, 0)
    y = _layer_norm(alpha * x_ref[...] + acc_ref[...], g_ref[...], b_ref[...])
    o_ref[...] = y
    ob_ref[...] = y.astype(BF16)


def _conv_glu(x, xb, wa, wv, wd, cp, g, b, alpha, tm, groups):
    t = x.shape[0]
    n_chunks, _, ck = wa.shape
    hrows = 16
    starts = tuple(r0 + bi * s for (r0, nb, s) in groups for bi in range(nb))
    ends = tuple(r0 + (bi + 1) * s for (r0, nb, s) in groups for bi in range(nb))
    kernel = functools.partial(_ffn_kernel, alpha=alpha, tm=tm, n_chunks=n_chunks,
                               starts=starts, ends=ends)
    row = lambda i: (i, 0)
    fixed2 = lambda i: (0, 0)
    fixed3 = lambda i: (0, 0, 0)
    per = tm // hrows
    last = t // hrows - 1
    return pl.pallas_call(
        kernel,
        grid=(t // tm,),
        in_specs=[pl.BlockSpec((tm, D_MODEL), row),
                  pl.BlockSpec((tm, D_MODEL), row),
                  pl.BlockSpec((hrows, D_MODEL), lambda i: (jnp.maximum(i * per - 1, 0), 0)),
                  pl.BlockSpec((hrows, D_MODEL), lambda i: (jnp.minimum((i + 1) * per, last), 0)),
                  pl.BlockSpec((n_chunks, D_MODEL, ck), fixed3),
                  pl.BlockSpec((n_chunks, D_MODEL, ck), fixed3),
                  pl.BlockSpec((n_chunks, ck, D_MODEL), fixed3),
                  pl.BlockSpec((n_chunks, 8, ck), fixed3),
                  pl.BlockSpec((1, D_MODEL), fixed2),
                  pl.BlockSpec((1, D_MODEL), fixed2)],
        out_specs=[pl.BlockSpec((tm, D_MODEL), row), pl.BlockSpec((tm, D_MODEL), row)],
        out_shape=[jax.ShapeDtypeStruct((t, D_MODEL), F32), jax.ShapeDtypeStruct((t, D_MODEL), BF16)],
        scratch_shapes=[pltpu.VMEM((tm, D_MODEL), F32)],
        compiler_params=_cparams("parallel"),
        name="conv_glu_ln",
    )(x, xb, xb, xb, wa, wv, wd, cp, g, b)


def _rotary_tables(seq):
    d = HEAD_DIM
    inv = 1.0 / (ROPE_BASE ** (jnp.arange(0, d, 2, dtype=F32) / d))
    ang = jnp.arange(seq, dtype=F32)[:, None] * inv[None, :]
    cos, sin = jnp.cos(ang), jnp.sin(ang)
    cos_t = jnp.concatenate([cos, cos, cos, cos], axis=-1)
    sin_t = jnp.concatenate([-sin, sin, -sin, sin], axis=-1)
    return cos_t, sin_t


def _tiles(groups):
    smin = min(s for (_, _, s) in groups)
    attn_tile = min(512, smin // 2)
    chunk = min(256, smin // 2)
    tm = min(512, smin // 2)
    return attn_tile, chunk, tm


def _forward(x, groups, w_in, ret_decay_logit, rel_bias, lambda_q1, lambda_k1, lambda_q2,
             lambda_k2, diff_norm_g, w_out, ln_g, ln_b, w_up, conv_w, conv_b, w_down):
    depth = w_in.shape[0]
    alpha = (2 * depth) ** 0.25
    t = x.shape[0]
    attn_tile, chunk, tm = _tiles(groups)
    smax = max(s for (_, _, s) in groups)
    gw = GROUP_WIDTH
    ck = 256
    n_chunks = D_FF // ck

    cos_t, sin_t = _rotary_tables(smax)
    bias_t = _bias_tiles(rel_bias, attn_tile)
    rel_log2 = rel_bias.astype(F32) * LOG2E
    in_scale = jnp.concatenate([
        jnp.ones((gw,), F32), jnp.full((gw,), HEAD_DIM ** -0.5, F32), jnp.ones((2 * gw,), F32),
        jnp.full((gw,), DIFF_QK_DIM ** -0.5 * LOG2E, F32), jnp.ones((2 * gw,), F32)])[None, :]
    ones_rows = jnp.concatenate([jnp.ones((1, attn_tile), BF16),
                                 jnp.zeros((V_ROWS - HEAD_DIM - 1, attn_tile), BF16)], axis=0)
    ones_rows = jnp.broadcast_to(ones_rows, (t // attn_tile, N_HEADS) + ones_rows.shape)

    xb = x.astype(BF16)
    for l in range(depth):
        lam_init = 0.8 - 0.6 * math.exp(-0.3 * l)
        w_in_b = w_in[l].astype(BF16)
        ret = _project(xb, w_in_b, in_scale, 0, 4 * gw, F32, tm, 512)
        dqkv = _project(xb, w_in_b, in_scale, 4 * gw, 3 * gw, BF16, tm, 512)
        vt4 = dqkv[:, 2 * gw:].reshape(t // attn_tile, attn_tile, N_HEADS, HEAD_DIM).transpose(0, 2, 3, 1)
        vt3 = jnp.concatenate([vt4, ones_rows], axis=2).reshape(t // attn_tile, N_HEADS * V_ROWS, attn_tile)

        log_g = jax.nn.log_sigmoid(ret_decay_logit[l].astype(F32))
        lamp = jnp.stack([lambda_q1[l], lambda_k1[l], lambda_q2[l], lambda_k2[l]]).astype(F32)
        dng = diff_norm_g[l].astype(F32)[:, None]

        yr_parts, yd_parts = [], []
        for (row0, batch, seq) in groups:
            yr_parts.append(_retention(ret, log_g, cos_t, sin_t, row0, batch, seq, chunk))
            ydt = _diff_attention(dqkv, vt3, bias_t, rel_log2, lamp, dng, row0, batch, seq,
                                  attn_tile, lam_init)
            yd_parts.append(ydt.transpose(0, 2, 1).reshape(batch * seq, gw))
        yr = jnp.concatenate(yr_parts, axis=0)
        yd = jnp.concatenate(yd_parts, axis=0)

        x, xb = _out_projection(yr, yd, w_out[l].astype(BF16), x, ln_g[l, 0][None, :].astype(F32),
                                ln_b[l, 0][None, :].astype(F32), alpha, tm)

        wa = w_up[l][:, :D_FF].astype(BF16).reshape(D_MODEL, n_chunks, ck).transpose(1, 0, 2)
        wv = w_up[l][:, D_FF:].astype(BF16).reshape(D_MODEL, n_chunks, ck).transpose(1, 0, 2)
        wd = w_down[l].astype(BF16).reshape(n_chunks, ck, D_MODEL)
        cp = jnp.concatenate([conv_w[l].astype(F32), conv_b[l].astype(F32)[None, :],
                              jnp.zeros((4, D_FF), F32)], axis=0)
        cp = cp.reshape(8, n_chunks, ck).transpose(1, 0, 2)
        x, xb = _conv_glu(x, xb, wa, wv, wd, cp, ln_g[l, 1][None, :].astype(F32),
                          ln_b[l, 1][None, :].astype(F32), alpha, tm, groups)
    return x


def kernel(x_prompt, x_sample, w_in, ret_decay_logit, rel_bias, lambda_q1, lambda_k1, lambda_q2,
           lambda_k2, diff_norm_g, w_out, ln_g, ln_b, w_up, conv_w, conv_b, w_down):
    bp, sp, d = x_prompt.shape
    bs, ss, _ = x_sample.shape
    groups = ((0, bp, sp), (bp * sp, bs, ss))
    x = jnp.concatenate([x_prompt.reshape(bp * sp, d), x_sample.reshape(bs * ss, d)], axis=0)
    y = _forward(x.astype(F32), groups, w_in, ret_decay_logit, rel_bias, lambda_q1, lambda_k1,
                 lambda_q2, lambda_k2, diff_norm_g, w_out, ln_g, ln_b, w_up, conv_w, conv_b, w_down)
    y_prompt = y[:bp * sp].reshape(bp, sp, d).astype(x_prompt.dtype)
    y_sample = y[bp * sp:].reshape(bs, ss, d).astype(x_sample.dtype)
    return y_prompt, y_sample
```

```python
import functools
import math

import jax
import jax.numpy as jnp
from jax import lax
from jax.experimental import pallas as pl
from jax.experimental.pallas import tpu as pltpu

D_MODEL = 1024
HEAD_DIM = 64
N_HEADS = 8
GROUP_WIDTH = N_HEADS * HEAD_DIM
DIFF_QK_DIM = HEAD_DIM // 2
D_FF = 2816
N_BUCKETS = 32
MAX_DISTANCE = 128
ROPE_BASE = 10000.0
LN_EPS = 1e-5
HEAD_NORM_EPS = 1e-6
LANES = 128
FAR_DISTANCE = 91
BIAS_REACH = 2
V_ROWS = HEAD_DIM + 16
LOG2E = math.log2(math.e)
NEG_BIG = -1e30
VMEM_LIMIT = 56 * 1024 * 1024

F32 = jnp.float32
BF16 = jnp.bfloat16


def _cparams(*sem):
    return pltpu.CompilerParams(dimension_semantics=sem, vmem_limit_bytes=VMEM_LIMIT)


def _proj_kernel(x_ref, w_ref, s_ref, o_ref):
    acc = jnp.dot(x_ref[...], w_ref[...], preferred_element_type=F32)
    o_ref[...] = (acc * s_ref[...]).astype(o_ref.dtype)


def _project(xb, w, scale, col0, n_out, out_dtype, tm, tn):
    t, k = xb.shape
    cb = col0 // tn
    return pl.pallas_call(
        _proj_kernel,
        grid=(t // tm, n_out // tn),
        in_specs=[pl.BlockSpec((tm, k), lambda i, j: (i, 0)),
                  pl.BlockSpec((k, tn), lambda i, j: (0, j + cb)),
                  pl.BlockSpec((1, tn), lambda i, j: (0, j + cb))],
        out_specs=pl.BlockSpec((tm, tn), lambda i, j: (i, j)),
        out_shape=jax.ShapeDtypeStruct((t, n_out), out_dtype),
        compiler_params=_cparams("parallel", "arbitrary"),
        name="in_proj",
    )(xb, w, scale)


def _ret_kernel(lg_ref, q_ref, k_ref, v_ref, g_ref, cos_ref, sin_ref, o_ref,
                rf_ref, rb_ref, rnext_ref, mask_ref, tab_ref, *, nc, chunk):
    p = pl.program_id(1)
    t = pl.program_id(2)
    c = chunk
    lane = lax.broadcasted_iota(jnp.int32, (c, LANES), 1)
    head0 = lane < HEAD_DIM
    low_half = (lane % HEAD_DIM) < (HEAD_DIM // 2)
    r_i = lax.broadcasted_iota(jnp.int32, (LANES, LANES), 0) // HEAD_DIM
    c_i = lax.broadcasted_iota(jnp.int32, (LANES, LANES), 1) // HEAD_DIM
    same_head = r_i == c_i

    @pl.when(t == 0)
    def _init():
        rb_ref[...] = jnp.zeros_like(rb_ref)
        qi = lax.broadcasted_iota(jnp.int32, (c, c), 0)
        ki = lax.broadcasted_iota(jnp.int32, (c, c), 1)
        diff = (qi - ki).astype(F32)
        for hh in range(2):
            lf = lg_ref[0, 2 * p + hh]
            lb = lg_ref[1, 2 * p + hh]
            mask_ref[hh] = jnp.where(diff >= 0, jnp.exp(lf * jnp.maximum(diff, 0.0)),
                                     jnp.exp(lb * jnp.maximum(-diff, 0.0)))
        pos = lax.broadcasted_iota(jnp.int32, (c, LANES), 0).astype(F32)
        lfl = jnp.where(head0, lg_ref[0, 2 * p], lg_ref[0, 2 * p + 1])
        lbl = jnp.where(head0, lg_ref[1, 2 * p], lg_ref[1, 2 * p + 1])
        tab_ref[0] = jnp.exp(lfl * (c - 1 - pos))
        tab_ref[1] = jnp.exp(lbl * pos)
        tab_ref[2] = jnp.exp(lfl * (pos + 1.0))
        tab_ref[3] = jnp.exp(lbl * (c - pos))
        tab_ref[4] = jnp.exp(lfl * c)
        tab_ref[5] = jnp.exp(lbl * c)

    def rotary(x):
        swapped = jnp.where(low_half, pltpu.roll(x, LANES - HEAD_DIM // 2, 1),
                            pltpu.roll(x, HEAD_DIM // 2, 1))
        return x * cos_ref[...] + swapped * sin_ref[...]

    def summary(kw, v):
        kv = lax.dot_general(kw.astype(BF16), v, (((0,), (0,)), ((), ())),
                             preferred_element_type=F32)
        return jnp.where(same_head, kv, 0.0)

    k = rotary(k_ref[...])
    vb = v_ref[...].astype(BF16)

    @pl.when(t < nc)
    def _backward():
        rnext_ref[nc - 1 - t] = rb_ref[...]
        rb_ref[...] = tab_ref[5][:LANES] * rb_ref[...] + summary(k * tab_ref[1], vb)

    @pl.when(t >= nc)
    def _forward():
        @pl.when(t == nc)
        def _():
            rf_ref[...] = jnp.zeros_like(rf_ref)

        q = rotary(q_ref[...])
        qb = q.astype(BF16)
        kb = k.astype(BF16)
        outs = []
        for hh in range(2):
            qm = jnp.where(head0 if hh == 0 else jnp.logical_not(head0), qb, jnp.zeros_like(qb))
            s = lax.dot_general(qm, kb, (((1,), (1,)), ((), ())), preferred_element_type=F32)
            s = s * mask_ref[hh]
            outs.append(jnp.dot(s.astype(BF16), vb, preferred_element_type=F32))
        y = jnp.where(head0, outs[0], outs[1])
        y = y + jnp.dot(qb, rf_ref[...].astype(BF16), preferred_element_type=F32) * tab_ref[2]
        y = y + jnp.dot(qb, rnext_ref[t - nc].astype(BF16), preferred_element_type=F32) * tab_ref[3]

        sq = y * y
        s0 = jnp.sum(jnp.where(head0, sq, 0.0), axis=1, keepdims=True)
        s1 = jnp.sum(jnp.where(head0, 0.0, sq), axis=1, keepdims=True)
        ms = jnp.where(head0, s0, s1) * (1.0 / HEAD_DIM)
        y = y * lax.rsqrt(ms + HEAD_NORM_EPS)
        g = g_ref[...]
        o_ref[...] = (g / (1.0 + jnp.exp(-g)) * y).astype(o_ref.dtype)

        rf_ref[...] = tab_ref[4][:LANES] * rf_ref[...] + summary(k * tab_ref[0], vb)


def _retention(ret, log_g, cos_t, sin_t, row0, batch, seq, chunk):
    nc = seq // chunk
    blk0 = row0 // chunk
    n_pairs = GROUP_WIDTH // LANES

    def kc(t):
        return jnp.where(t < nc, nc - 1 - t, t - nc)

    def qc(t):
        return jnp.maximum(t - nc, 0)

    def rows(b, cc):
        return blk0 + b * nc + cc

    kernel = functools.partial(_ret_kernel, nc=nc, chunk=chunk)
    return pl.pallas_call(
        kernel,
        grid=(batch, n_pairs, 2 * nc),
        in_specs=[pl.BlockSpec(memory_space=pltpu.SMEM),
                  pl.BlockSpec((chunk, LANES), lambda b, p, t: (rows(b, qc(t)), p)),
                  pl.BlockSpec((chunk, LANES), lambda b, p, t: (rows(b, kc(t)), n_pairs + p)),
                  pl.BlockSpec((chunk, LANES), lambda b, p, t: (rows(b, kc(t)), 2 * n_pairs + p)),
                  pl.BlockSpec((chunk, LANES), lambda b, p, t: (rows(b, qc(t)), 3 * n_pairs + p)),
                  pl.BlockSpec((chunk, LANES), lambda b, p, t: (kc(t), 0)),
                  pl.BlockSpec((chunk, LANES), lambda b, p, t: (kc(t), 0))],
        out_specs=pl.BlockSpec((chunk, LANES), lambda b, p, t: (b * nc + qc(t), p)),
        out_shape=jax.ShapeDtypeStruct((batch * seq, GROUP_WIDTH), BF16),
        scratch_shapes=[pltpu.VMEM((LANES, LANES), F32),
                        pltpu.VMEM((LANES, LANES), F32),
                        pltpu.VMEM((nc, LANES, LANES), F32),
                        pltpu.VMEM((2, chunk, chunk), F32),
                        pltpu.VMEM((6, chunk, LANES), F32)],
        compiler_params=_cparams("parallel", "parallel", "arbitrary"),
        name="retention",
    )(log_g, ret, ret, ret, ret, cos_t, sin_t)


def _bias_kernel(rb_ref, bucket_ref, o_ref):
    h = pl.program_id(0)
    bk = bucket_ref[...]
    out = jnp.zeros(bk.shape, F32)
    for n in range(N_BUCKETS):
        out = jnp.where(bk == n, rb_ref[n, h], out)
    o_ref[...] = out * LOG2E


def _bias_tiles(rel_bias, tile):
    assert tile > FAR_DISTANCE
    a = jnp.arange(tile, dtype=jnp.int32)[:, None]
    b = jnp.arange(tile, dtype=jnp.int32)[None, :]
    rel = jnp.stack([(d * tile + a - b) for d in range(-BIAS_REACH, BIAS_REACH + 1)])
    nb = N_BUCKETS // 2
    max_exact = nb // 2
    n = jnp.abs(rel)
    nf = jnp.maximum(n, 1).astype(F32)
    large = max_exact + (jnp.log(nf / max_exact) / math.log(MAX_DISTANCE / max_exact)
                         * (nb - max_exact)).astype(jnp.int32)
    large = jnp.minimum(large, nb - 1)
    bucket = jnp.where(rel > 0, nb, 0) + jnp.where(n < max_exact, n, large)
    return pl.pallas_call(
        _bias_kernel,
        grid=(N_HEADS, 2 * BIAS_REACH + 1),
        in_specs=[pl.BlockSpec(memory_space=pltpu.SMEM),
                  pl.BlockSpec((None, tile, tile), lambda h, d: (d, 0, 0))],
        out_specs=pl.BlockSpec((None, None, tile, tile), lambda h, d: (h, d, 0, 0)),
        out_shape=jax.ShapeDtypeStruct((N_HEADS, 2 * BIAS_REACH + 1, tile, tile), F32),
        compiler_params=_cparams("parallel", "arbitrary"),
        name="t5_bias_tiles",
    )(rel_bias.astype(F32), bucket.astype(jnp.int32))


def _attn_kernel(rb_ref, lamp_ref, dng_ref, q_ref, k_ref, vt_ref, bt_ref, o_ref,
                 s_ref, mt_ref, m_ref, a_ref, *, n_tiles, tile, lam_init):
    h = pl.program_id(1)
    qi = pl.program_id(2)
    hh = h % 2
    n = n_tiles

    q = q_ref[...]
    lane = lax.broadcasted_iota(jnp.int32, q.shape, 1)
    base = hh * HEAD_DIM
    zero = jnp.zeros_like(q)
    qz = (jnp.where((lane >= base) & (lane < base + DIFF_QK_DIM), q, zero),
          jnp.where((lane >= base + DIFF_QK_DIM) & (lane < base + HEAD_DIM), q, zero))

    m_ref[...] = jnp.full(m_ref.shape, NEG_BIG, F32)
    a_ref[...] = jnp.zeros_like(a_ref)

    def near(j):
        return jnp.abs(j - qi) < BIAS_REACH

    def scores(j, slot, with_bias):
        kt = k_ref[pl.ds(pl.multiple_of(j * tile, tile), tile), :]
        dims = (((1,), (1,)), ((), ()))
        for t in range(2):
            s = lax.dot_general(kt, qz[t], dims, preferred_element_type=F32)
            if with_bias:
                s = s + bt_ref[jnp.clip(j - qi, -BIAS_REACH, BIAS_REACH) + BIAS_REACH]
            s_ref[slot, t] = s
            mt_ref[slot, t] = jnp.max(s, axis=0, keepdims=True)

    def accumulate(j, slot, c):
        vt = vt_ref[j]
        for t in range(2):
            m_prev = m_ref[t]
            m_new = jnp.maximum(m_prev, mt_ref[slot, t] + c)
            alpha = jnp.exp2(m_prev - m_new)
            pr = jnp.exp2(s_ref[slot, t] - (m_new - c)).astype(BF16)
            a_ref[t] = alpha * a_ref[t] + jnp.dot(vt, pr, preferred_element_type=F32)
            m_ref[t] = m_new

    def scored_with_bias(j):
        b = (j - 1) // 2
        return jnp.where(j == 0, near(0), near(2 * b + 1) | near(2 * b + 2))

    def missing_bias(j):
        far = jnp.where(j < qi, rb_ref[N_BUCKETS // 2 - 1, h], rb_ref[N_BUCKETS - 1, h])
        return jnp.where(scored_with_bias(j), 0.0, far)

    lax.cond(near(0), lambda: scores(0, 0, True), lambda: scores(0, 0, False))

    def pair(j, last):
        c0 = missing_bias(j)
        c1 = missing_bias(j + 1)

        def run(with_bias):
            scores(j + 1, 1, with_bias)
            accumulate(j, 0, c0)
            if not last:
                scores(j + 2, 0, with_bias)
            accumulate(j + 1, 1, c1)

        lax.cond(scored_with_bias(j + 1), lambda: run(True), lambda: run(False))

    def pair_body(i, carry):
        pair(2 * i, False)
        return carry

    lax.fori_loop(0, n // 2 - 1, pair_body, 0)
    pair(n - 2, True)

    lp = lamp_ref[...]
    lam = (jnp.exp(jnp.sum(lp[0:1] * lp[1:2], axis=1, keepdims=True))
           - jnp.exp(jnp.sum(lp[2:3] * lp[3:4], axis=1, keepdims=True)) + lam_init)
    a1 = a_ref[0]
    a2 = a_ref[1]
    o = (a1[:HEAD_DIM] / a1[HEAD_DIM:HEAD_DIM + 1]
         - lam * (a2[:HEAD_DIM] / a2[HEAD_DIM:HEAD_DIM + 1]))
    ms = jnp.mean(o * o, axis=0, keepdims=True)
    y = o * lax.rsqrt(ms + HEAD_NORM_EPS) * dng_ref[...] * (1.0 - lam_init)
    o_ref[...] = y.astype(o_ref.dtype)


def _diff_attention(dqkv, vt3, bias_t, rel_log2, lamp, dng, row0, batch, seq, tile, lam_init):
    n = seq // tile
    assert n % 2 == 0
    n_pairs = GROUP_WIDTH // LANES
    qblk0 = row0 // tile
    sblk0 = row0 // seq
    n_bias = 2 * BIAS_REACH + 1
    kernel = functools.partial(_attn_kernel, n_tiles=n, tile=tile, lam_init=lam_init)
    return pl.pallas_call(
        kernel,
        grid=(batch, N_HEADS, n),
        in_specs=[pl.BlockSpec(memory_space=pltpu.SMEM),
                  pl.BlockSpec((4, DIFF_QK_DIM), lambda b, h, i: (0, 0)),
                  pl.BlockSpec((HEAD_DIM, 1), lambda b, h, i: (0, 0)),
                  pl.BlockSpec((tile, LANES), lambda b, h, i: (qblk0 + b * n + i, h // 2)),
                  pl.BlockSpec((seq, LANES), lambda b, h, i: (sblk0 + b, n_pairs + h // 2)),
                  pl.BlockSpec((n, V_ROWS, tile), lambda b, h, i: (sblk0 + b, h, 0)),
                  pl.BlockSpec((None, n_bias, tile, tile), lambda b, h, i: (h, 0, 0, 0))],
        out_specs=pl.BlockSpec((None, HEAD_DIM, tile), lambda b, h, i: (b * n + i, h, 0)),
        out_shape=jax.ShapeDtypeStruct((batch * n, GROUP_WIDTH, tile), BF16),
        scratch_shapes=[pltpu.VMEM((2, 2, tile, tile), F32),
                        pltpu.VMEM((2, 2, 1, tile), F32),
                        pltpu.VMEM((2, 1, tile), F32),
                        pltpu.VMEM((2, V_ROWS, tile), F32)],
        compiler_params=_cparams("parallel", "parallel", "arbitrary"),
        name="diff_attention",
    )(rel_log2, lamp, dng, dqkv, dqkv, vt3, bias_t)


def _layer_norm(y, g, b):
    mu = jnp.mean(y, axis=-1, keepdims=True)
    d = y - mu
    var = jnp.mean(d * d, axis=-1, keepdims=True)
    return d * lax.rsqrt(var + LN_EPS) * g + b


def _outproj_kernel(yr_ref, yd_ref, w_ref, x_ref, g_ref, b_ref, o_ref, ob_ref, *, alpha):
    mix = jnp.dot(yr_ref[...], w_ref[:GROUP_WIDTH, :], preferred_element_type=F32)
    mix = mix + jnp.dot(yd_ref[...], w_ref[GROUP_WIDTH:, :], preferred_element_type=F32)
    y = _layer_norm(alpha * x_ref[...] + mix, g_ref[...], b_ref[...])
    o_ref[...] = y
    ob_ref[...] = y.astype(BF16)


def _out_projection(yr, yd, w_out, x, g, b, alpha, tm):
    t = x.shape[0]
    kernel = functools.partial(_outproj_kernel, alpha=alpha)
    row = lambda i: (i, 0)
    fixed = lambda i: (0, 0)
    return pl.pallas_call(
        kernel,
        grid=(t // tm,),
        in_specs=[pl.BlockSpec((tm, GROUP_WIDTH), row),
                  pl.BlockSpec((tm, GROUP_WIDTH), row),
                  pl.BlockSpec((2 * GROUP_WIDTH, D_MODEL), fixed),
                  pl.BlockSpec((tm, D_MODEL), row),
                  pl.BlockSpec((1, D_MODEL), fixed),
                  pl.BlockSpec((1, D_MODEL), fixed)],
        out_specs=[pl.BlockSpec((tm, D_MODEL), row), pl.BlockSpec((tm, D_MODEL), row)],
        out_shape=[jax.ShapeDtypeStruct((t, D_MODEL), F32), jax.ShapeDtypeStruct((t, D_MODEL), BF16)],
        compiler_params=_cparams("parallel"),
        name="out_proj_ln",
    )(yr, yd, w_out, x, g, b)


def _ffn_kernel(x_ref, xb_ref, xp_ref, xn_ref, wa_ref, wv_ref, wd_ref, cp_ref, g_ref, b_ref,
                o_ref, ob_ref, acc_ref, *, alpha, tm, n_chunks, starts, ends):
    i = pl.program_id(0)
    t0 = i * tm
    is_start = functools.reduce(jnp.logical_or, [t0 == s for s in starts])
    is_end = functools.reduce(jnp.logical_or, [t0 + tm == e for e in ends])
    keep_prev = jnp.where(is_start, 0.0, 1.0)
    keep_next = jnp.where(is_end, 0.0, 1.0)

    xb = xb_ref[...]
    halo = jnp.concatenate([xp_ref[...], xn_ref[...]], axis=0)
    hrows = xp_ref.shape[0]
    acc_ref[...] = jnp.zeros_like(acc_ref)

    def chunk_body(c, carry):
        wa = wa_ref[c]
        a = jnp.dot(xb, wa, preferred_element_type=F32)
        val = jnp.dot(xb, wv_ref[c], preferred_element_type=F32)
        ah = jnp.dot(halo, wa, preferred_element_type=F32)
        prev_row = ah[hrows - 1:hrows] * keep_prev
        next_row = ah[hrows:hrows + 1] * keep_next
        row = lax.broadcasted_iota(jnp.int32, a.shape, 0)
        a_m1 = jnp.where(row == 0, prev_row, pltpu.roll(a, 1, 0))
        a_p1 = jnp.where(row == tm - 1, next_row, pltpu.roll(a, tm - 1, 0))
        cp = cp_ref[c]
        conv = cp[3:4] + a_m1 * cp[0:1]
        conv = conv + a * cp[1:2]
        conv = conv + a_p1 * cp[2:3]
        gelu = 0.5 * conv * (1.0 + lax.erf(conv * (1.0 / math.sqrt(2.0))))
        hidden = (gelu * val).astype(BF16)
        acc_ref[...] += jnp.dot(hidden, wd_ref[c], preferred_element_type=F32)
        return carry

    lax.fori_loop(0, n_chunks, chunk_body, 0)
    y = _layer_norm(alpha * x_ref[...] + acc_ref[...], g_ref[...], b_ref[...])
    o_ref[...] = y
    ob_ref[...] = y.astype(BF16)


def _conv_glu(x, xb, wa, wv, wd, cp, g, b, alpha, tm, groups):
    t = x.shape[0]
    n_chunks, _, ck = wa.shape
    hrows = 16
    starts = tuple(r0 + bi * s for (r0, nb, s) in groups for bi in range(nb))
    ends = tuple(r0 + (bi + 1) * s for (r0, nb, s) in groups for bi in range(nb))
    kernel = functools.partial(_ffn_kernel, alpha=alpha, tm=tm, n_chunks=n_chunks,
                               starts=starts, ends=ends)
    row = lambda i: (i, 0)
    fixed2 = lambda i: (0, 0)
    fixed3 = lambda i: (0, 0, 0)
    per = tm // hrows
    last = t // hrows - 1
    return pl.pallas_call(
        kernel,
        grid=(t // tm,),
        in_specs=[pl.BlockSpec((tm, D_MODEL), row),
                  pl.BlockSpec((tm, D_MODEL), row),
                  pl.BlockSpec((hrows, D_MODEL), lambda i: (jnp.maximum(i * per - 1, 0), 0)),
                  pl.BlockSpec((hrows, D_MODEL), lambda i: (jnp.minimum((i + 1) * per, last), 0)),
                  pl.BlockSpec((n_chunks, D_MODEL, ck), fixed3),
                  pl.BlockSpec((n_chunks, D_MODEL, ck), fixed3),
                  pl.BlockSpec((n_chunks, ck, D_MODEL), fixed3),
                  pl.BlockSpec((n_chunks, 8, ck), fixed3),
                  pl.BlockSpec((1, D_MODEL), fixed2),
                  pl.BlockSpec((1, D_MODEL), fixed2)],
        out_specs=[pl.BlockSpec((tm, D_MODEL), row), pl.BlockSpec((tm, D_MODEL), row)],
        out_shape=[jax.ShapeDtypeStruct((t, D_MODEL), F32), jax.ShapeDtypeStruct((t, D_MODEL), BF16)],
        scratch_shapes=[pltpu.VMEM((tm, D_MODEL), F32)],
        compiler_params=_cparams("parallel"),
        name="conv_glu_ln",
    )(x, xb, xb, xb, wa, wv, wd, cp, g, b)


def _rotary_tables(seq):
    d = HEAD_DIM
    inv = 1.0 / (ROPE_BASE ** (jnp.arange(0, d, 2, dtype=F32) / d))
    ang = jnp.arange(seq, dtype=F32)[:, None] * inv[None, :]
    cos, sin = jnp.cos(ang), jnp.sin(ang)
    cos_t = jnp.concatenate([cos, cos, cos, cos], axis=-1)
    sin_t = jnp.concatenate([-sin, sin, -sin, sin], axis=-1)
    return cos_t, sin_t


def _tiles(groups):
    smin = min(s for (_, _, s) in groups)
    attn_tile = min(512, smin // 2)
    chunk = min(256, smin // 2)
    tm = min(512, smin // 2)
    return attn_tile, chunk, tm


def _forward(x, groups, w_in, ret_decay_logit, rel_bias, lambda_q1, lambda_k1, lambda_q2,
             lambda_k2, diff_norm_g, w_out, ln_g, ln_b, w_up, conv_w, conv_b, w_down):
    depth = w_in.shape[0]
    alpha = (2 * depth) ** 0.25
    t = x.shape[0]
    attn_tile, chunk, tm = _tiles(groups)
    smax = max(s for (_, _, s) in groups)
    gw = GROUP_WIDTH
    ck = 256
    n_chunks = D_FF // ck

    cos_t, sin_t = _rotary_tables(smax)
    bias_t = _bias_tiles(rel_bias, attn_tile)
    rel_log2 = rel_bias.astype(F32) * LOG2E
    in_scale = jnp.concatenate([
        jnp.ones((gw,), F32), jnp.full((gw,), HEAD_DIM ** -0.5, F32), jnp.ones((2 * gw,), F32),
        jnp.full((gw,), DIFF_QK_DIM ** -0.5 * LOG2E, F32), jnp.ones((2 * gw,), F32)])[None, :]
    ones_rows = jnp.concatenate([jnp.ones((1, attn_tile), BF16),
                                 jnp.zeros((V_ROWS - HEAD_DIM - 1, attn_tile), BF16)], axis=0)
    ones_rows = jnp.broadcast_to(ones_rows, (t // attn_tile, N_HEADS) + ones_rows.shape)

    xb = x.astype(BF16)
    for l in range(depth):
        lam_init = 0.8 - 0.6 * math.exp(-0.3 * l)
        w_in_b = w_in[l].astype(BF16)
        ret = _project(xb, w_in_b, in_scale, 0, 4 * gw, F32, tm, 512)
        dqkv = _project(xb, w_in_b, in_scale, 4 * gw, 3 * gw, BF16, tm, 512)
        vt4 = dqkv[:, 2 * gw:].reshape(t // attn_tile, attn_tile, N_HEADS, HEAD_DIM).transpose(0, 2, 3, 1)
        vt3 = jnp.concatenate([vt4, ones_rows], axis=2).reshape(t // attn_tile, N_HEADS * V_ROWS, attn_tile)

        log_g = jax.nn.log_sigmoid(ret_decay_logit[l].astype(F32))
        lamp = jnp.stack([lambda_q1[l], lambda_k1[l], lambda_q2[l], lambda_k2[l]]).astype(F32)
        dng = diff_norm_g[l].astype(F32)[:, None]

        yr_parts, yd_parts = [], []
        for (row0, batch, seq) in groups:
            yr_parts.append(_retention(ret, log_g, cos_t, sin_t, row0, batch, seq, chunk))
            ydt = _diff_attention(dqkv, vt3, bias_t, rel_log2, lamp, dng, row0, batch, seq,
                                  attn_tile, lam_init)
            yd_parts.append(ydt.transpose(0, 2, 1).reshape(batch * seq, gw))
        yr = jnp.concatenate(yr_parts, axis=0)
        yd = jnp.concatenate(yd_parts, axis=0)

        x, xb = _out_projection(yr, yd, w_out[l].astype(BF16), x, ln_g[l, 0][None, :].astype(F32),
                                ln_b[l, 0][None, :].astype(F32), alpha, tm)

        wa = w_up[l][:, :D_FF].astype(BF16).reshape(D_MODEL, n_chunks, ck).transpose(1, 0, 2)
        wv = w_up[l][:, D_FF:].astype(BF16).reshape(D_MODEL, n_chunks, ck).transpose(1, 0, 2)
        wd = w_down[l].astype(BF16).reshape(n_chunks, ck, D_MODEL)
        cp = jnp.concatenate([conv_w[l].astype(F32), conv_b[l].astype(F32)[None, :],
                              jnp.zeros((4, D_FF), F32)], axis=0)
        cp = cp.reshape(8, n_chunks, ck).transpose(1, 0, 2)
        x, xb = _conv_glu(x, xb, wa, wv, wd, cp, ln_g[l, 1][None, :].astype(F32),
                          ln_b[l, 1][None, :].astype(F32), alpha, tm, groups)
    return x


def kernel(x_prompt, x_sample, w_in, ret_decay_logit, rel_bias, lambda_q1, lambda_k1, lambda_q2,
           lambda_k2, diff_norm_g, w_out, ln_g, ln_b, w_up, conv_w, conv_b, w_down):
    bp, sp, d = x_prompt.shape
    bs, ss, _ = x_sample.shape
    groups = ((0, bp, sp), (bp * sp, bs, ss))
    x = jnp.concatenate([x_prompt.reshape(bp * sp, d), x_sample.reshape(bs * ss, d)], axis=0)
    y = _forward(x.astype(F32), groups, w_in, ret_decay_logit, rel_bias, lambda_q1, lambda_k1,
                 lambda_q2, lambda_k2, diff_norm_g, w_out, ln_g, ln_b, w_up, conv_w, conv_b, w_down)
    y_prompt = y[:bp * sp].reshape(bp, sp, d).astype(x_prompt.dtype)
    y_sample = y[bp * sp:].reshape(bs, ss, d).astype(x_sample.dtype)
    return y_prompt, y_sample
```

```python
import functools
import math

import jax
import jax.numpy as jnp
from jax import lax
from jax.experimental import pallas as pl
from jax.experimental.pallas import tpu as pltpu

D_MODEL = 1024
HEAD_DIM = 64
N_HEADS = 8
GROUP_WIDTH = N_HEADS * HEAD_DIM
DIFF_QK_DIM = HEAD_DIM // 2
D_FF = 2816
N_BUCKETS = 32
MAX_DISTANCE = 128
ROPE_BASE = 10000.0
LN_EPS = 1e-5
HEAD_NORM_EPS = 1e-6
LANES = 128
FAR_DISTANCE = 91
BIAS_REACH = 2
V_ROWS = HEAD_DIM + 16
LOG2E = math.log2(math.e)
NEG_BIG = -1e30
VMEM_LIMIT = 56 * 1024 * 1024

F32 = jnp.float32
BF16 = jnp.bfloat16


def _cparams(*sem):
    return pltpu.CompilerParams(dimension_semantics=sem, vmem_limit_bytes=VMEM_LIMIT)


def _proj_kernel(x_ref, w_ref, s_ref, cos_ref, sin_ref, ret_ref, gate_ref, dqkv_ref):
    gw = GROUP_WIDTH
    x = x_ref[...]
    tm = x.shape[0]
    lane = lax.broadcasted_iota(jnp.int32, (tm, LANES), 1)
    low_half = (lane % HEAD_DIM) < (HEAD_DIM // 2)
    cos = cos_ref[...]
    sin = sin_ref[...]
    qk = jnp.dot(x, w_ref[:, :2 * gw], preferred_element_type=F32) * s_ref[:, :2 * gw]
    for p in range(2 * gw // LANES):
        sl = slice(p * LANES, (p + 1) * LANES)
        xx = qk[:, sl]
        swapped = jnp.where(low_half, pltpu.roll(xx, LANES - HEAD_DIM // 2, 1),
                            pltpu.roll(xx, HEAD_DIM // 2, 1))
        ret_ref[:, sl] = (xx * cos + swapped * sin).astype(ret_ref.dtype)
    ret_ref[:, 2 * gw:] = jnp.dot(x, w_ref[:, 2 * gw:3 * gw],
                                  preferred_element_type=F32).astype(ret_ref.dtype)
    gate_ref[...] = jnp.dot(x, w_ref[:, 3 * gw:4 * gw], preferred_element_type=F32)
    dqkv_ref[...] = (jnp.dot(x, w_ref[:, 4 * gw:], preferred_element_type=F32)
                     * s_ref[:, 4 * gw:]).astype(dqkv_ref.dtype)


def _position_block(i, tm, groups):
    t0 = i * tm
    blk = t0 // tm
    for (row0, _, seq) in groups:
        blk = jnp.where(t0 >= row0, ((t0 - row0) % seq) // tm, blk)
    return blk


def _project(xb, w, scale, cos_t, sin_t, groups, tm):
    t, k = xb.shape
    gw = GROUP_WIDTH
    row = lambda i: (i, 0)
    fixed = lambda i: (0, 0)
    pos = lambda i: (_position_block(i, tm, groups), 0)
    return pl.pallas_call(
        _proj_kernel,
        grid=(t // tm,),
        in_specs=[pl.BlockSpec((tm, k), row),
                  pl.BlockSpec((k, 7 * gw), fixed),
                  pl.BlockSpec((1, 7 * gw), fixed),
                  pl.BlockSpec((tm, LANES), pos),
                  pl.BlockSpec((tm, LANES), pos)],
        out_specs=[pl.BlockSpec((tm, 3 * gw), row), pl.BlockSpec((tm, gw), row),
                   pl.BlockSpec((tm, 3 * gw), row)],
        out_shape=[jax.ShapeDtypeStruct((t, 3 * gw), BF16), jax.ShapeDtypeStruct((t, gw), F32),
                   jax.ShapeDtypeStruct((t, 3 * gw), BF16)],
        compiler_params=_cparams("parallel"),
        name="in_proj",
    )(xb, w, scale, cos_t, sin_t)


def _ret_kernel(lg_ref, q_ref, k_ref, v_ref, g_ref, o_ref,
                rf_ref, rb_ref, rnext_ref, mask_ref, tab_ref, *, nc, chunk):
    t = pl.program_id(1)
    c = chunk
    n_pairs = GROUP_WIDTH // LANES
    lane = lax.broadcasted_iota(jnp.int32, (c, LANES), 1)
    head0 = lane < HEAD_DIM
    r_i = lax.broadcasted_iota(jnp.int32, (LANES, LANES), 0) // HEAD_DIM
    c_i = lax.broadcasted_iota(jnp.int32, (LANES, LANES), 1) // HEAD_DIM
    same_head = r_i == c_i

    @pl.when(t == 0)
    def _init():
        rb_ref[...] = jnp.zeros_like(rb_ref)
        qi = lax.broadcasted_iota(jnp.int32, (c, c), 0)
        ki = lax.broadcasted_iota(jnp.int32, (c, c), 1)
        diff = (qi - ki).astype(F32)
        pos = lax.broadcasted_iota(jnp.int32, (c, LANES), 0).astype(F32)
        for p in range(n_pairs):
            for hh in range(2):
                lf = lg_ref[0, 2 * p + hh]
                lb = lg_ref[1, 2 * p + hh]
                mask_ref[p, hh * c:(hh + 1) * c, :] = jnp.where(
                    diff >= 0, jnp.exp(lf * jnp.maximum(diff, 0.0)),
                    jnp.exp(lb * jnp.maximum(-diff, 0.0)))
            lfl = jnp.where(head0, lg_ref[0, 2 * p], lg_ref[0, 2 * p + 1])
            lbl = jnp.where(head0, lg_ref[1, 2 * p], lg_ref[1, 2 * p + 1])
            tab_ref[p, 0] = jnp.exp(lfl * (c - 1 - pos))
            tab_ref[p, 1] = jnp.exp(lbl * pos)
            tab_ref[p, 2] = jnp.exp(lfl * (pos + 1.0))
            tab_ref[p, 3] = jnp.exp(lbl * (c - pos))
            tab_ref[p, 4] = jnp.exp(lfl * c)
            tab_ref[p, 5] = jnp.exp(lbl * c)

    def summary(k, w, v):
        kw = (k.astype(F32) * w).astype(BF16)
        kv = lax.dot_general(kw, v, (((0,), (0,)), ((), ())), preferred_element_type=F32)
        return jnp.where(same_head, kv, 0.0)

    @pl.when(t < nc)
    def _backward():
        for p in range(n_pairs):
            sl = slice(p * LANES, (p + 1) * LANES)
            rnext_ref[nc - 1 - t, p] = rb_ref[p].astype(BF16)
            rb_ref[p] = (tab_ref[p, 5][:LANES] * rb_ref[p]
                         + summary(k_ref[:, sl], tab_ref[p, 1], v_ref[:, sl]))

    @pl.when(t >= nc)
    def _forward():
        @pl.when(t == nc)
        def _():
            rf_ref[...] = jnp.zeros_like(rf_ref)

        for p in range(n_pairs):
            sl = slice(p * LANES, (p + 1) * LANES)
            q = q_ref[:, sl]
            k = k_ref[:, sl]
            v = v_ref[:, sl]
            zero = jnp.zeros_like(q)
            q2 = jnp.concatenate([jnp.where(head0, q, zero), jnp.where(head0, zero, q)], axis=0)
            s = lax.dot_general(q2, k, (((1,), (1,)), ((), ())), preferred_element_type=F32)
            o2 = jnp.dot((s * mask_ref[p]).astype(BF16), v, preferred_element_type=F32)
            y = jnp.where(head0, o2[:c], o2[c:])
            y = y + jnp.dot(q, rf_ref[p].astype(BF16), preferred_element_type=F32) * tab_ref[p, 2]
            y = y + jnp.dot(q, rnext_ref[t - nc, p], preferred_element_type=F32) * tab_ref[p, 3]

            sq = y * y
            s0 = jnp.sum(jnp.where(head0, sq, 0.0), axis=1, keepdims=True)
            s1 = jnp.sum(jnp.where(head0, 0.0, sq), axis=1, keepdims=True)
            ms = jnp.where(head0, s0, s1) * (1.0 / HEAD_DIM)
            y = y * lax.rsqrt(ms + HEAD_NORM_EPS)
            g = g_ref[:, sl]
            o_ref[:, sl] = (g / (1.0 + jnp.exp(-g)) * y).astype(o_ref.dtype)

            rf_ref[p] = tab_ref[p, 4][:LANES] * rf_ref[p] + summary(k, tab_ref[p, 0], v)


def _retention(ret, gate, log_g, row0, batch, seq, chunk):
    nc = seq // chunk
    blk0 = row0 // chunk
    n_pairs = GROUP_WIDTH // LANES
    gw = GROUP_WIDTH

    def kc(t):
        return jnp.where(t < nc, nc - 1 - t, t - nc)

    def qc(t):
        return jnp.maximum(t - nc, 0)

    def rows(b, cc):
        return blk0 + b * nc + cc

    kernel = functools.partial(_ret_kernel, nc=nc, chunk=chunk)
    return pl.pallas_call(
        kernel,
        grid=(batch, 2 * nc),
        in_specs=[pl.BlockSpec(memory_space=pltpu.SMEM),
                  pl.BlockSpec((chunk, gw), lambda b, t: (rows(b, qc(t)), 0)),
                  pl.BlockSpec((chunk, gw), lambda b, t: (rows(b, kc(t)), 1)),
                  pl.BlockSpec((chunk, gw), lambda b, t: (rows(b, kc(t)), 2)),
                  pl.BlockSpec((chunk, gw), lambda b, t: (rows(b, qc(t)), 0))],
        out_specs=pl.BlockSpec((chunk, gw), lambda b, t: (b * nc + qc(t), 0)),
        out_shape=jax.ShapeDtypeStruct((batch * seq, gw), BF16),
        scratch_shapes=[pltpu.VMEM((n_pairs, LANES, LANES), F32),
                        pltpu.VMEM((n_pairs, LANES, LANES), F32),
                        pltpu.VMEM((nc, n_pairs, LANES, LANES), BF16),
                        pltpu.VMEM((n_pairs, 2 * chunk, chunk), F32),
                        pltpu.VMEM((n_pairs, 6, chunk, LANES), F32)],
        compiler_params=_cparams("parallel", "arbitrary"),
        name="retention",
    )(log_g, ret, ret, ret, gate)


def _bias_kernel(rb_ref, bucket_ref, o_ref):
    h = pl.program_id(0)
    bk = bucket_ref[...]
    out = jnp.zeros(bk.shape, F32)
    for n in range(N_BUCKETS):
        out = jnp.where(bk == n, rb_ref[n, h], out)
    o_ref[...] = out * LOG2E


def _bias_tiles(rel_bias, tile):
    assert tile > FAR_DISTANCE
    a = jnp.arange(tile, dtype=jnp.int32)[:, None]
    b = jnp.arange(tile, dtype=jnp.int32)[None, :]
    rel = jnp.stack([(d * tile + a - b) for d in range(-BIAS_REACH, BIAS_REACH + 1)])
    nb = N_BUCKETS // 2
    max_exact = nb // 2
    n = jnp.abs(rel)
    nf = jnp.maximum(n, 1).astype(F32)
    large = max_exact + (jnp.log(nf / max_exact) / math.log(MAX_DISTANCE / max_exact)
                         * (nb - max_exact)).astype(jnp.int32)
    large = jnp.minimum(large, nb - 1)
    bucket = jnp.where(rel > 0, nb, 0) + jnp.where(n < max_exact, n, large)
    return pl.pallas_call(
        _bias_kernel,
        grid=(N_HEADS, 2 * BIAS_REACH + 1),
        in_specs=[pl.BlockSpec(memory_space=pltpu.SMEM),
                  pl.BlockSpec((None, tile, tile), lambda h, d: (d, 0, 0))],
        out_specs=pl.BlockSpec((None, None, tile, tile), lambda h, d: (h, d, 0, 0)),
        out_shape=jax.ShapeDtypeStruct((N_HEADS, 2 * BIAS_REACH + 1, tile, tile), F32),
        compiler_params=_cparams("parallel", "arbitrary"),
        name="t5_bias_tiles",
    )(rel_bias.astype(F32), bucket.astype(jnp.int32))


def _attn_kernel(rb_ref, lamp_ref, dng_ref, q_ref, k_ref, vt_ref, bt_ref, o_ref,
                 s_ref, mt_ref, m_ref, a_ref, *, n_tiles, tile, lam_init):
    h = pl.program_id(1)
    qi = pl.program_id(2)
    hh = h % 2
    n = n_tiles

    q = q_ref[...]
    lane = lax.broadcasted_iota(jnp.int32, q.shape, 1)
    base = hh * HEAD_DIM
    zero = jnp.zeros_like(q)
    qz = (jnp.where((lane >= base) & (lane < base + DIFF_QK_DIM), q, zero),
          jnp.where((lane >= base + DIFF_QK_DIM) & (lane < base + HEAD_DIM), q, zero))

    m_ref[...] = jnp.full(m_ref.shape, NEG_BIG, F32)
    a_ref[...] = jnp.zeros_like(a_ref)

    def near(j):
        return jnp.abs(j - qi) < BIAS_REACH

    def scores(j, slot, with_bias):
        kt = k_ref[pl.ds(pl.multiple_of(j * tile, tile), tile), :]
        dims = (((1,), (1,)), ((), ()))
        for t in range(2):
            s = lax.dot_general(kt, qz[t], dims, preferred_element_type=F32)
            if with_bias:
                s = s + bt_ref[jnp.clip(j - qi, -BIAS_REACH, BIAS_REACH) + BIAS_REACH]
            s_ref[slot, t] = s
            mt_ref[slot, t] = jnp.max(s, axis=0, keepdims=True)

    def accumulate(j, slot, c):
        vt = vt_ref[j]
        for t in range(2):
            m_prev = m_ref[t]
            m_new = jnp.maximum(m_prev, mt_ref[slot, t] + c)
            alpha = jnp.exp2(m_prev - m_new)
            pr = jnp.exp2(s_ref[slot, t] - (m_new - c)).astype(BF16)
            a_ref[t] = alpha * a_ref[t] + jnp.dot(vt, pr, preferred_element_type=F32)
            m_ref[t] = m_new

    def scored_with_bias(j):
        b = (j - 1) // 2
        return jnp.where(j == 0, near(0), near(2 * b + 1) | near(2 * b + 2))

    def missing_bias(j):
        far = jnp.where(j < qi, rb_ref[N_BUCKETS // 2 - 1, h], rb_ref[N_BUCKETS - 1, h])
        return jnp.where(scored_with_bias(j), 0.0, far)

    lax.cond(near(0), lambda: scores(0, 0, True), lambda: scores(0, 0, False))

    def pair(j, last):
        c0 = missing_bias(j)
        c1 = missing_bias(j + 1)

        def run(with_bias):
            scores(j + 1, 1, with_bias)
            accumulate(j, 0, c0)
            if not last:
                scores(j + 2, 0, with_bias)
            accumulate(j + 1, 1, c1)

        lax.cond(scored_with_bias(j + 1), lambda: run(True), lambda: run(False))

    def pair_body(i, carry):
        pair(2 * i, False)
        return carry

    lax.fori_loop(0, n // 2 - 1, pair_body, 0)
    pair(n - 2, True)

    lp = lamp_ref[...]
    lam = (jnp.exp(jnp.sum(lp[0:1] * lp[1:2], axis=1, keepdims=True))
           - jnp.exp(jnp.sum(lp[2:3] * lp[3:4], axis=1, keepdims=True)) + lam_init)
    a1 = a_ref[0]
    a2 = a_ref[1]
    o = (a1[:HEAD_DIM] / a1[HEAD_DIM:HEAD_DIM + 1]
         - lam * (a2[:HEAD_DIM] / a2[HEAD_DIM:HEAD_DIM + 1]))
    ms = jnp.mean(o * o, axis=0, keepdims=True)
    y = o * lax.rsqrt(ms + HEAD_NORM_EPS) * dng_ref[...] * (1.0 - lam_init)
    o_ref[...] = y.astype(o_ref.dtype)


def _diff_attention(dqkv, vt3, bias_t, rel_log2, lamp, dng, row0, batch, seq, tile, lam_init):
    n = seq // tile
    assert n % 2 == 0
    n_pairs = GROUP_WIDTH // LANES
    qblk0 = row0 // tile
    sblk0 = row0 // seq
    n_bias = 2 * BIAS_REACH + 1
    kernel = functools.partial(_attn_kernel, n_tiles=n, tile=tile, lam_init=lam_init)
    return pl.pallas_call(
        kernel,
        grid=(batch, N_HEADS, n),
        in_specs=[pl.BlockSpec(memory_space=pltpu.SMEM),
                  pl.BlockSpec((4, DIFF_QK_DIM), lambda b, h, i: (0, 0)),
                  pl.BlockSpec((HEAD_DIM, 1), lambda b, h, i: (0, 0)),
                  pl.BlockSpec((tile, LANES), lambda b, h, i: (qblk0 + b * n + i, h // 2)),
                  pl.BlockSpec((seq, LANES), lambda b, h, i: (sblk0 + b, n_pairs + h // 2)),
                  pl.BlockSpec((n, V_ROWS, tile), lambda b, h, i: (sblk0 + b, h, 0)),
                  pl.BlockSpec((None, n_bias, tile, tile), lambda b, h, i: (h, 0, 0, 0))],
        out_specs=pl.BlockSpec((None, HEAD_DIM, tile), lambda b, h, i: (b * n + i, h, 0)),
        out_shape=jax.ShapeDtypeStruct((batch * n, GROUP_WIDTH, tile), BF16),
        scratch_shapes=[pltpu.VMEM((2, 2, tile, tile), F32),
                        pltpu.VMEM((2, 2, 1, tile), F32),
                        pltpu.VMEM((2, 1, tile), F32),
                        pltpu.VMEM((2, V_ROWS, tile), F32)],
        compiler_params=_cparams("parallel", "parallel", "arbitrary"),
        name="diff_attention",
    )(rel_log2, lamp, dng, dqkv, dqkv, vt3, bias_t)


def _layer_norm(y, g, b):
    mu = jnp.mean(y, axis=-1, keepdims=True)
    d = y - mu
    var = jnp.mean(d * d, axis=-1, keepdims=True)
    return d * lax.rsqrt(var + LN_EPS) * g + b


def _outproj_kernel(yr_ref, yd_ref, w_ref, x_ref, g_ref, b_ref, o_ref, ob_ref, *, alpha):
    mix = jnp.dot(yr_ref[...], w_ref[:GROUP_WIDTH, :], preferred_element_type=F32)
    mix = mix + jnp.dot(yd_ref[...], w_ref[GROUP_WIDTH:, :], preferred_element_type=F32)
    y = _layer_norm(alpha * x_ref[...] + mix, g_ref[...], b_ref[...])
    o_ref[...] = y
    ob_ref[...] = y.astype(BF16)


def _out_projection(yr, yd, w_out, x, g, b, alpha, tm):
    t = x.shape[0]
    kernel = functools.partial(_outproj_kernel, alpha=alpha)
    row = lambda i: (i, 0)
    fixed = lambda i: (0, 0)
    return pl.pallas_call(
        kernel,
        grid=(t // tm,),
        in_specs=[pl.BlockSpec((tm, GROUP_WIDTH), row),
                  pl.BlockSpec((tm, GROUP_WIDTH), row),
                  pl.BlockSpec((2 * GROUP_WIDTH, D_MODEL), fixed),
                  pl.BlockSpec((tm, D_MODEL), row),
                  pl.BlockSpec((1, D_MODEL), fixed),
                  pl.BlockSpec((1, D_MODEL), fixed)],
        out_specs=[pl.BlockSpec((tm, D_MODEL), row), pl.BlockSpec((tm, D_MODEL), row)],
        out_shape=[jax.ShapeDtypeStruct((t, D_MODEL), F32), jax.ShapeDtypeStruct((t, D_MODEL), BF16)],
        compiler_params=_cparams("parallel"),
        name="out_proj_ln",
    )(yr, yd, w_out, x, g, b)


def _ffn_kernel(x_ref, xb_ref, xp_ref, xn_ref, wa_ref, wv_ref, wd_ref, cp_ref, g_ref, b_ref,
                o_ref, ob_ref, acc_ref, *, alpha, tm, n_chunks, starts, ends):
    i = pl.program_id(0)
    t0 = i * tm
    is_start = functools.reduce(jnp.logical_or, [t0 == s for s in starts])
    is_end = functools.reduce(jnp.logical_or, [t0 + tm == e for e in ends])
    keep_prev = jnp.where(is_start, 0.0, 1.0)
    keep_next = jnp.where(is_end, 0.0, 1.0)

    xb = xb_ref[...]
    halo = jnp.concatenate([xp_ref[...], xn_ref[...]], axis=0)
    hrows = xp_ref.shape[0]
    acc_ref[...] = jnp.zeros_like(acc_ref)

    def chunk_body(c, carry):
        wa = wa_ref[c]
        a = jnp.dot(xb, wa, preferred_element_type=F32)
        val = jnp.dot(xb, wv_ref[c], preferred_element_type=F32)
        ah = jnp.dot(halo, wa, preferred_element_type=F32)
        prev_row = ah[hrows - 1:hrows] * keep_prev
        next_row = ah[hrows:hrows + 1] * keep_next
        row = lax.broadcasted_iota(jnp.int32, a.shape, 0)
        a_m1 = jnp.where(row == 0, prev_row, pltpu.roll(a, 1, 0))
        a_p1 = jnp.where(row == tm - 1, next_row, pltpu.roll(a, tm - 1, 0))
        cp = cp_ref[c]
        conv = cp[3:4] + a_m1 * cp[0:1]
        conv = conv + a * cp[1:2]
        conv = conv + a_p1 * cp[2:3]
        gelu = 0.5 * conv * (1.0 + lax.erf(conv * (1.0 / math.sqrt(2.0))))
        hidden = (gelu * val).astype(BF16)
        acc_ref[...] += jnp.dot(hidden, wd_ref[c], preferred_element_type=F32)
        return carry

    lax.fori_loop(0, n_chunks, chunk_body, 0)
    y = _layer_norm(alpha * x_ref[...] + acc_ref[...], g_ref[...], b_ref[...])
    o_ref[...] = y
    ob_ref[...] = y.astype(BF16)


def _conv_glu(x, xb, wa, wv, wd, cp, g, b, alpha, tm, groups):
    t = x.shape[0]
    n_chunks, _, ck = wa.shape
    hrows = 16
    starts = tuple(r0 + bi * s for (r0, nb, s) in groups for bi in range(nb))
    ends = tuple(r0 + (bi + 1) * s for (r0, nb, s) in groups for bi in range(nb))
    kernel = functools.partial(_ffn_kernel, alpha=alpha, tm=tm, n_chunks=n_chunks,
                               starts=starts, ends=ends)
    row = lambda i: (i, 0)
    fixed2 = lambda i: (0, 0)
    fixed3 = lambda i: (0, 0, 0)
    per = tm // hrows
    last = t // hrows - 1
    return pl.pallas_call(
        kernel,
        grid=(t // tm,),
        in_specs=[pl.BlockSpec((tm, D_MODEL), row),
                  pl.BlockSpec((tm, D_MODEL), row),
                  pl.BlockSpec((hrows, D_MODEL), lambda i: (jnp.maximum(i * per - 1, 0), 0)),
                  pl.BlockSpec((hrows, D_MODEL), lambda i: (jnp.minimum((i + 1) * per, last), 0)),
                  pl.BlockSpec((n_chunks, D_MODEL, ck), fixed3),
                  pl.BlockSpec((n_chunks, D_MODEL, ck), fixed3),
                  pl.BlockSpec((n_chunks, ck, D_MODEL), fixed3),
                  pl.BlockSpec((n_chunks, 8, ck), fixed3),
                  pl.BlockSpec((1, D_MODEL), fixed2),
                  pl.BlockSpec((1, D_MODEL), fixed2)],
        out_specs=[pl.BlockSpec((tm, D_MODEL), row), pl.BlockSpec((tm, D_MODEL), row)],
        out_shape=[jax.ShapeDtypeStruct((t, D_MODEL), F32), jax.ShapeDtypeStruct((t, D_MODEL), BF16)],
        scratch_shapes=[pltpu.VMEM((tm, D_MODEL), F32)],
        compiler_params=_cparams("parallel"),
        name="conv_glu_ln",
    )(x, xb, xb, xb, wa, wv, wd, cp, g, b)


def _rotary_tables(seq):
    d = HEAD_DIM
    inv = 1.0 / (ROPE_BASE ** (jnp.arange(0, d, 2, dtype=F32) / d))
    ang = jnp.arange(seq, dtype=F32)[:, None] * inv[None, :]
    cos, sin = jnp.cos(ang), jnp.sin(ang)
    cos_t = jnp.concatenate([cos, cos, cos, cos], axis=-1)
    sin_t = jnp.concatenate([-sin, sin, -sin, sin], axis=-1)
    return cos_t, sin_t


def _tiles(groups):
    smin = min(s for (_, _, s) in groups)
    attn_tile = min(512, smin // 2)
    chunk = min(256, smin // 2)
    tm = min(512, smin // 2)
    return attn_tile, chunk, tm


def _forward(x, groups, w_in, ret_decay_logit, rel_bias, lambda_q1, lambda_k1, lambda_q2,
             lambda_k2, diff_norm_g, w_out, ln_g, ln_b, w_up, conv_w, conv_b, w_down):
    depth = w_in.shape[0]
    alpha = (2 * depth) ** 0.25
    t = x.shape[0]
    attn_tile, chunk, tm = _tiles(groups)
    smax = max(s for (_, _, s) in groups)
    gw = GROUP_WIDTH
    ck = 256
    n_chunks = D_FF // ck

    cos_t, sin_t = _rotary_tables(smax)
    bias_t = _bias_tiles(rel_bias, attn_tile)
    rel_log2 = rel_bias.astype(F32) * LOG2E
    in_scale = jnp.concatenate([
        jnp.ones((gw,), F32), jnp.full((gw,), HEAD_DIM ** -0.5, F32), jnp.ones((2 * gw,), F32),
        jnp.full((gw,), DIFF_QK_DIM ** -0.5 * LOG2E, F32), jnp.ones((2 * gw,), F32)])[None, :]
    ones_rows = jnp.concatenate([jnp.ones((1, attn_tile), BF16),
                                 jnp.zeros((V_ROWS - HEAD_DIM - 1, attn_tile), BF16)], axis=0)
    ones_rows = jnp.broadcast_to(ones_rows, (t // attn_tile, N_HEADS) + ones_rows.shape)

    xb = x.astype(BF16)
    for l in range(depth):
        lam_init = 0.8 - 0.6 * math.exp(-0.3 * l)
        w_in_b = w_in[l].astype(BF16)
        ret, gate, dqkv = _project(xb, w_in_b, in_scale, cos_t, sin_t, groups, tm)
        vt4 = dqkv[:, 2 * gw:].reshape(t // attn_tile, attn_tile, N_HEADS, HEAD_DIM).transpose(0, 2, 3, 1)
        vt3 = jnp.concatenate([vt4, ones_rows], axis=2).reshape(t // attn_tile, N_HEADS * V_ROWS, attn_tile)

        log_g = jax.nn.log_sigmoid(ret_decay_logit[l].astype(F32))
        lamp = jnp.stack([lambda_q1[l], lambda_k1[l], lambda_q2[l], lambda_k2[l]]).astype(F32)
        dng = diff_norm_g[l].astype(F32)[:, None]

        yr_parts, yd_parts = [], []
        for (row0, batch, seq) in groups:
            yr_parts.append(_retention(ret, gate, log_g, row0, batch, seq, chunk))
            ydt = _diff_attention(dqkv, vt3, bias_t, rel_log2, lamp, dng, row0, batch, seq,
                                  attn_tile, lam_init)
            yd_parts.append(ydt.transpose(0, 2, 1).reshape(batch * seq, gw))
        yr = jnp.concatenate(yr_parts, axis=0)
        yd = jnp.concatenate(yd_parts, axis=0)

        x, xb = _out_projection(yr, yd, w_out[l].astype(BF16), x, ln_g[l, 0][None, :].astype(F32),
                                ln_b[l, 0][None, :].astype(F32), alpha, tm)

        wa = w_up[l][:, :D_FF].astype(BF16).reshape(D_MODEL, n_chunks, ck).transpose(1, 0, 2)
        wv = w_up[l][:, D_FF:].astype(BF16).reshape(D_MODEL, n_chunks, ck).transpose(1, 0, 2)
        wd = w_down[l].astype(BF16).reshape(n_chunks, ck, D_MODEL)
        cp = jnp.concatenate([conv_w[l].astype(F32), conv_b[l].astype(F32)[None, :],
                              jnp.zeros((4, D_FF), F32)], axis=0)
        cp = cp.reshape(8, n_chunks, ck).transpose(1, 0, 2)
        x, xb = _conv_glu(x, xb, wa, wv, wd, cp, ln_g[l, 1][None, :].astype(F32),
                          ln_b[l, 1][None, :].astype(F32), alpha, tm, groups)
    return x


def kernel(x_prompt, x_sample, w_in, ret_decay_logit, rel_bias, lambda_q1, lambda_k1, lambda_q2,
           lambda_k2, diff_norm_g, w_out, ln_g, ln_b, w_up, conv_w, conv_b, w_down):
    bp, sp, d = x_prompt.shape
    bs, ss, _ = x_sample.shape
    groups = ((0, bp, sp), (bp * sp, bs, ss))
    x = jnp.concatenate([x_prompt.reshape(bp * sp, d), x_sample.reshape(bs * ss, d)], axis=0)
    y = _forward(x.astype(F32), groups, w_in, ret_decay_logit, rel_bias, lambda_q1, lambda_k1,
                 lambda_q2, lambda_k2, diff_norm_g, w_out, ln_g, ln_b, w_up, conv_w, conv_b, w_down)
    y_prompt = y[:bp * sp].reshape(bp, sp, d).astype(x_prompt.dtype)
    y_sample = y[bp * sp:].reshape(bs, ss, d).astype(x_sample.dtype)
    return y_prompt, y_sample
```

```python
import functools
import math

import jax
import jax.numpy as jnp
from jax import lax
from jax.experimental import pallas as pl
from jax.experimental.pallas import tpu as pltpu

D_MODEL = 1024
HEAD_DIM = 64
N_HEADS = 8
GROUP_WIDTH = N_HEADS * HEAD_DIM
DIFF_QK_DIM = HEAD_DIM // 2
D_FF = 2816
N_BUCKETS = 32
MAX_DISTANCE = 128
ROPE_BASE = 10000.0
LN_EPS = 1e-5
HEAD_NORM_EPS = 1e-6
LANES = 128
FAR_DISTANCE = 91
BIAS_REACH = 2
V_ROWS = HEAD_DIM + 16
LOG2E = math.log2(math.e)
NEG_BIG = -1e30
UNDERFLOW_GUARD = 2.0 ** -90
VMEM_LIMIT = 56 * 1024 * 1024

F32 = jnp.float32
BF16 = jnp.bfloat16


def _cparams(*sem):
    return pltpu.CompilerParams(dimension_semantics=sem, vmem_limit_bytes=VMEM_LIMIT)


def _proj_kernel(x_ref, w_ref, s_ref, cos_ref, sin_ref, ret_ref, gate_ref, dqkv_ref):
    gw = GROUP_WIDTH
    x = x_ref[...]
    tm = x.shape[0]
    lane = lax.broadcasted_iota(jnp.int32, (tm, LANES), 1)
    low_half = (lane % HEAD_DIM) < (HEAD_DIM // 2)
    cos = cos_ref[...]
    sin = sin_ref[...]
    qk = jnp.dot(x, w_ref[:, :2 * gw], preferred_element_type=F32) * s_ref[:, :2 * gw]
    for p in range(2 * gw // LANES):
        sl = slice(p * LANES, (p + 1) * LANES)
        xx = qk[:, sl]
        swapped = jnp.where(low_half, pltpu.roll(xx, LANES - HEAD_DIM // 2, 1),
                            pltpu.roll(xx, HEAD_DIM // 2, 1))
        ret_ref[:, sl] = (xx * cos + swapped * sin).astype(ret_ref.dtype)
    ret_ref[:, 2 * gw:] = jnp.dot(x, w_ref[:, 2 * gw:3 * gw],
                                  preferred_element_type=F32).astype(ret_ref.dtype)
    gate_ref[...] = jnp.dot(x, w_ref[:, 3 * gw:4 * gw], preferred_element_type=F32)
    dqkv_ref[...] = (jnp.dot(x, w_ref[:, 4 * gw:], preferred_element_type=F32)
                     * s_ref[:, 4 * gw:]).astype(dqkv_ref.dtype)


def _position_block(i, tm, groups):
    t0 = i * tm
    blk = t0 // tm
    for (row0, _, seq) in groups:
        blk = jnp.where(t0 >= row0, ((t0 - row0) % seq) // tm, blk)
    return blk


def _project(xb, w, scale, cos_t, sin_t, groups, tm):
    t, k = xb.shape
    gw = GROUP_WIDTH
    row = lambda i: (i, 0)
    fixed = lambda i: (0, 0)
    pos = lambda i: (_position_block(i, tm, groups), 0)
    return pl.pallas_call(
        _proj_kernel,
        grid=(t // tm,),
        in_specs=[pl.BlockSpec((tm, k), row),
                  pl.BlockSpec((k, 7 * gw), fixed),
                  pl.BlockSpec((1, 7 * gw), fixed),
                  pl.BlockSpec((tm, LANES), pos),
                  pl.BlockSpec((tm, LANES), pos)],
        out_specs=[pl.BlockSpec((tm, 3 * gw), row), pl.BlockSpec((tm, gw), row),
                   pl.BlockSpec((tm, 3 * gw), row)],
        out_shape=[jax.ShapeDtypeStruct((t, 3 * gw), BF16), jax.ShapeDtypeStruct((t, gw), F32),
                   jax.ShapeDtypeStruct((t, 3 * gw), BF16)],
        compiler_params=_cparams("parallel"),
        name="in_proj",
    )(xb, w, scale, cos_t, sin_t)


def _ret_kernel(lg_ref, q_ref, k_ref, v_ref, g_ref, o_ref,
                rf_ref, rb_ref, rnext_ref, mask_ref, tab_ref, *, nc, chunk):
    t = pl.program_id(1)
    c = chunk
    n_pairs = GROUP_WIDTH // LANES
    lane = lax.broadcasted_iota(jnp.int32, (c, LANES), 1)
    head0 = lane < HEAD_DIM
    r_i = lax.broadcasted_iota(jnp.int32, (LANES, LANES), 0) // HEAD_DIM
    c_i = lax.broadcasted_iota(jnp.int32, (LANES, LANES), 1) // HEAD_DIM
    same_head = r_i == c_i

    @pl.when(t == 0)
    def _init():
        rb_ref[...] = jnp.zeros_like(rb_ref)
        qi = lax.broadcasted_iota(jnp.int32, (c, c), 0)
        ki = lax.broadcasted_iota(jnp.int32, (c, c), 1)
        diff = (qi - ki).astype(F32)
        pos = lax.broadcasted_iota(jnp.int32, (c, LANES), 0).astype(F32)
        for p in range(n_pairs):
            for hh in range(2):
                lf = lg_ref[0, 2 * p + hh]
                lb = lg_ref[1, 2 * p + hh]
                mask_ref[p, hh * c:(hh + 1) * c, :] = jnp.where(
                    diff >= 0, jnp.exp(lf * jnp.maximum(diff, 0.0)),
                    jnp.exp(lb * jnp.maximum(-diff, 0.0)))
            lfl = jnp.where(head0, lg_ref[0, 2 * p], lg_ref[0, 2 * p + 1])
            lbl = jnp.where(head0, lg_ref[1, 2 * p], lg_ref[1, 2 * p + 1])
            tab_ref[p, 0] = jnp.exp(lfl * (c - 1 - pos))
            tab_ref[p, 1] = jnp.exp(lbl * pos)
            tab_ref[p, 2] = jnp.exp(lfl * (pos + 1.0))
            tab_ref[p, 3] = jnp.exp(lbl * (c - pos))
            tab_ref[p, 4] = jnp.exp(lfl * c)
            tab_ref[p, 5] = jnp.exp(lbl * c)

    def summary(k, w, v):
        kw = (k.astype(F32) * w).astype(BF16)
        kv = lax.dot_general(kw, v, (((0,), (0,)), ((), ())), preferred_element_type=F32)
        return jnp.where(same_head, kv, 0.0)

    @pl.when(t < nc)
    def _backward():
        for p in range(n_pairs):
            sl = slice(p * LANES, (p + 1) * LANES)
            rnext_ref[nc - 1 - t, p] = rb_ref[p].astype(BF16)
            rb_ref[p] = (tab_ref[p, 5][:LANES] * rb_ref[p]
                         + summary(k_ref[:, sl], tab_ref[p, 1], v_ref[:, sl]))

    @pl.when(t >= nc)
    def _forward():
        @pl.when(t == nc)
        def _():
            rf_ref[...] = jnp.zeros_like(rf_ref)

        for p in range(n_pairs):
            sl = slice(p * LANES, (p + 1) * LANES)
            q = q_ref[:, sl]
            k = k_ref[:, sl]
            v = v_ref[:, sl]
            zero = jnp.zeros_like(q)
            q2 = jnp.concatenate([jnp.where(head0, q, zero), jnp.where(head0, zero, q)], axis=0)
            s = lax.dot_general(q2, k, (((1,), (1,)), ((), ())), preferred_element_type=F32)
            o2 = jnp.dot((s * mask_ref[p]).astype(BF16), v, preferred_element_type=F32)
            y = jnp.where(head0, o2[:c], o2[c:])
            y = y + jnp.dot(q, rf_ref[p].astype(BF16), preferred_element_type=F32) * tab_ref[p, 2]
            y = y + jnp.dot(q, rnext_ref[t - nc, p], preferred_element_type=F32) * tab_ref[p, 3]

            sq = y * y
            s0 = jnp.sum(jnp.where(head0, sq, 0.0), axis=1, keepdims=True)
            s1 = jnp.sum(jnp.where(head0, 0.0, sq), axis=1, keepdims=True)
            ms = jnp.where(head0, s0, s1) * (1.0 / HEAD_DIM)
            y = y * lax.rsqrt(ms + HEAD_NORM_EPS)
            g = g_ref[:, sl]
            o_ref[:, sl] = (g / (1.0 + jnp.exp(-g)) * y).astype(o_ref.dtype)

            rf_ref[p] = tab_ref[p, 4][:LANES] * rf_ref[p] + summary(k, tab_ref[p, 0], v)


def _retention(ret, gate, log_g, row0, batch, seq, chunk):
    nc = seq // chunk
    blk0 = row0 // chunk
    n_pairs = GROUP_WIDTH // LANES
    gw = GROUP_WIDTH

    def kc(t):
        return jnp.where(t < nc, nc - 1 - t, t - nc)

    def qc(t):
        return jnp.maximum(t - nc, 0)

    def rows(b, cc):
        return blk0 + b * nc + cc

    kernel = functools.partial(_ret_kernel, nc=nc, chunk=chunk)
    return pl.pallas_call(
        kernel,
        grid=(batch, 2 * nc),
        in_specs=[pl.BlockSpec(memory_space=pltpu.SMEM),
                  pl.BlockSpec((chunk, gw), lambda b, t: (rows(b, qc(t)), 0)),
                  pl.BlockSpec((chunk, gw), lambda b, t: (rows(b, kc(t)), 1)),
                  pl.BlockSpec((chunk, gw), lambda b, t: (rows(b, kc(t)), 2)),
                  pl.BlockSpec((chunk, gw), lambda b, t: (rows(b, qc(t)), 0))],
        out_specs=pl.BlockSpec((chunk, gw), lambda b, t: (b * nc + qc(t), 0)),
        out_shape=jax.ShapeDtypeStruct((batch * seq, gw), BF16),
        scratch_shapes=[pltpu.VMEM((n_pairs, LANES, LANES), F32),
                        pltpu.VMEM((n_pairs, LANES, LANES), F32),
                        pltpu.VMEM((nc, n_pairs, LANES, LANES), BF16),
                        pltpu.VMEM((n_pairs, 2 * chunk, chunk), F32),
                        pltpu.VMEM((n_pairs, 6, chunk, LANES), F32)],
        compiler_params=_cparams("parallel", "arbitrary"),
        name="retention",
    )(log_g, ret, ret, ret, gate)


def _bias_kernel(rb_ref, bucket_ref, o_ref):
    h = pl.program_id(0)
    bk = bucket_ref[...]
    out = jnp.zeros(bk.shape, F32)
    for n in range(N_BUCKETS):
        out = jnp.where(bk == n, rb_ref[n, h], out)
    o_ref[...] = out * LOG2E


def _bias_tiles(rel_bias, tile):
    assert tile > FAR_DISTANCE
    a = jnp.arange(tile, dtype=jnp.int32)[:, None]
    b = jnp.arange(tile, dtype=jnp.int32)[None, :]
    rel = jnp.stack([(d * tile + a - b) for d in range(-BIAS_REACH, BIAS_REACH + 1)])
    nb = N_BUCKETS // 2
    max_exact = nb // 2
    n = jnp.abs(rel)
    nf = jnp.maximum(n, 1).astype(F32)
    large = max_exact + (jnp.log(nf / max_exact) / math.log(MAX_DISTANCE / max_exact)
                         * (nb - max_exact)).astype(jnp.int32)
    large = jnp.minimum(large, nb - 1)
    bucket = jnp.where(rel > 0, nb, 0) + jnp.where(n < max_exact, n, large)
    return pl.pallas_call(
        _bias_kernel,
        grid=(N_HEADS, 2 * BIAS_REACH + 1),
        in_specs=[pl.BlockSpec(memory_space=pltpu.SMEM),
                  pl.BlockSpec((None, tile, tile), lambda h, d: (d, 0, 0))],
        out_specs=pl.BlockSpec((None, None, tile, tile), lambda h, d: (h, d, 0, 0)),
        out_shape=jax.ShapeDtypeStruct((N_HEADS, 2 * BIAS_REACH + 1, tile, tile), F32),
        compiler_params=_cparams("parallel", "arbitrary"),
        name="t5_bias_tiles",
    )(rel_bias.astype(F32), bucket.astype(jnp.int32))


def _attn_kernel(rb_ref, lamp_ref, dng_ref, q_ref, k_ref, vt_ref, bt_ref, o_ref,
                 a_ref, p_ref, qa_ref, kmax_ref, m_ref, *, n_tiles, tile, unroll, lam_init):
    h = pl.program_id(1)
    qi = pl.program_id(2)
    hh = h % 2
    n = n_tiles
    dims_nt = (((1,), (1,)), ((), ()))
    half = DIFF_QK_DIM
    group_of_lane = lax.broadcasted_iota(jnp.int32, (LANES, LANES), 0) // half
    group_sum = (group_of_lane == lax.broadcasted_iota(jnp.int32, (LANES, LANES), 1)).astype(BF16)

    def half_norms(x):
        xf = x.astype(F32)
        return jnp.dot((xf * xf).astype(BF16), group_sum, preferred_element_type=F32)

    @pl.when(qi == 0)
    def _key_norms():
        def body(j, mx):
            kt = k_ref[pl.ds(pl.multiple_of(j * tile, tile), tile), :]
            return jnp.maximum(mx, jnp.max(half_norms(kt), axis=0, keepdims=True))
        kmax_ref[...] = lax.fori_loop(0, n, body, jnp.zeros((1, LANES), F32))

    q = q_ref[...]
    lane = lax.broadcasted_iota(jnp.int32, q.shape, 1)
    lane_row = lax.broadcasted_iota(jnp.int32, (1, LANES), 1)
    zero = jnp.zeros_like(q)
    qn = half_norms(q)
    side_bias = (rb_ref[N_BUCKETS // 2 - 1, h], 0.0, rb_ref[N_BUCKETS - 1, h])
    for t in range(2):
        g = 2 * hh + t
        lo_lane = g * half
        qz = jnp.where((lane >= lo_lane) & (lane < lo_lane + half), q, zero)
        qn2 = jnp.sum(jnp.where(lane == g, qn, 0.0), axis=1, keepdims=True)
        kn2 = jnp.max(jnp.where(lane_row == g, kmax_ref[...], 0.0), axis=1, keepdims=True)
        bound = jnp.sqrt(qn2 * kn2) + rb_ref[N_BUCKETS, h]
        for side in range(3):
            v = side_bias[side] - bound
            hi = v.astype(BF16).astype(F32)
            lo = (v - hi).astype(BF16).astype(F32)
            cols = jnp.where(lane == 0, hi, jnp.where(lane == 1, lo, 0.0)).astype(BF16)
            qa_ref[t, side] = jnp.concatenate([qz, cols], axis=1)

    ones = jnp.ones((tile, LANES), BF16)

    def near(j):
        return jnp.abs(j - qi) < BIAS_REACH

    def probabilities(j, slot, with_bias):
        kt = k_ref[pl.ds(pl.multiple_of(j * tile, tile), tile), :]
        kaug = jnp.concatenate([kt, ones], axis=1)
        side = 1 if with_bias else jnp.where(j < qi, 0, 2)
        for t in range(2):
            s = lax.dot_general(kaug, qa_ref[t, side], dims_nt, preferred_element_type=F32)
            if with_bias:
                s = s + bt_ref[jnp.clip(j - qi, -BIAS_REACH, BIAS_REACH) + BIAS_REACH]
            p_ref[slot, t] = jnp.exp2(s).astype(BF16)

    def accumulate(j, slot):
        vt = vt_ref[j]
        for t in range(2):
            a_ref[t] += jnp.dot(vt, p_ref[slot, t], preferred_element_type=F32)

    a_ref[...] = jnp.zeros_like(a_ref)

    lax.cond(near(0), lambda: probabilities(0, 0, True), lambda: probabilities(0, 0, False))

    def group(b, last):
        j = unroll * b

        def run(with_bias):
            for k in range(unroll):
                if not (last and k == unroll - 1):
                    probabilities(j + k + 1, (k + 1) % 2, with_bias)
                accumulate(j + k, k % 2)

        any_near = functools.reduce(jnp.logical_or, [near(j + k + 1) for k in range(unroll)])
        lax.cond(any_near, lambda: run(True), lambda: run(False))

    def group_body(b, carry):
        group(b, False)
        return carry

    lax.fori_loop(0, n // unroll - 1, group_body, 0)
    group(n // unroll - 1, True)

    sums = jnp.minimum(a_ref[0, HEAD_DIM:HEAD_DIM + 1, :], a_ref[1, HEAD_DIM:HEAD_DIM + 1, :])
    trusted = jnp.min(sums) >= UNDERFLOW_GUARD

    @pl.when(jnp.logical_not(trusted))
    def _exact_running_max():
        m_ref[...] = jnp.full(m_ref.shape, NEG_BIG, F32)
        a_ref[...] = jnp.zeros_like(a_ref)

        def body(j, carry):
            kt = k_ref[pl.ds(pl.multiple_of(j * tile, tile), tile), :]
            vt = vt_ref[j]
            bias = bt_ref[jnp.clip(j - qi, -BIAS_REACH, BIAS_REACH) + BIAS_REACH]
            for t in range(2):
                s = lax.dot_general(kt, qa_ref[t, 1, :, :LANES], dims_nt,
                                    preferred_element_type=F32) + bias
                m_prev = m_ref[t]
                m_new = jnp.maximum(m_prev, jnp.max(s, axis=0, keepdims=True))
                pr = jnp.exp2(s - m_new).astype(BF16)
                a_ref[t] = (jnp.exp2(m_prev - m_new) * a_ref[t]
                            + jnp.dot(vt, pr, preferred_element_type=F32))
                m_ref[t] = m_new
            return carry

        lax.fori_loop(0, n, body, 0)

    lp = lamp_ref[...]
    lam = (jnp.exp(jnp.sum(lp[0:1] * lp[1:2], axis=1, keepdims=True))
           - jnp.exp(jnp.sum(lp[2:3] * lp[3:4], axis=1, keepdims=True)) + lam_init)
    a1 = a_ref[0]
    a2 = a_ref[1]
    o = (a1[:HEAD_DIM] / a1[HEAD_DIM:HEAD_DIM + 1]
         - lam * (a2[:HEAD_DIM] / a2[HEAD_DIM:HEAD_DIM + 1]))
    ms = jnp.mean(o * o, axis=0, keepdims=True)
    y = o * lax.rsqrt(ms + HEAD_NORM_EPS) * dng_ref[...] * (1.0 - lam_init)
    o_ref[...] = y.astype(o_ref.dtype)


def _diff_attention(dqkv, vt3, bias_t, rel_log2, lamp, dng, row0, batch, seq, tile, lam_init):
    n = seq // tile
    unroll = 4 if n % 4 == 0 else 2
    assert n % unroll == 0
    n_pairs = GROUP_WIDTH // LANES
    qblk0 = row0 // tile
    sblk0 = row0 // seq
    n_bias = 2 * BIAS_REACH + 1
    kernel = functools.partial(_attn_kernel, n_tiles=n, tile=tile, unroll=unroll, lam_init=lam_init)
    return pl.pallas_call(
        kernel,
        grid=(batch, N_HEADS, n),
        in_specs=[pl.BlockSpec(memory_space=pltpu.SMEM),
                  pl.BlockSpec((4, DIFF_QK_DIM), lambda b, h, i: (0, 0)),
                  pl.BlockSpec((HEAD_DIM, 1), lambda b, h, i: (0, 0)),
                  pl.BlockSpec((tile, LANES), lambda b, h, i: (qblk0 + b * n + i, h // 2)),
                  pl.BlockSpec((seq, LANES), lambda b, h, i: (sblk0 + b, n_pairs + h // 2)),
                  pl.BlockSpec((n, V_ROWS, tile), lambda b, h, i: (sblk0 + b, h, 0)),
                  pl.BlockSpec((None, n_bias, tile, tile), lambda b, h, i: (h, 0, 0, 0))],
        out_specs=pl.BlockSpec((None, HEAD_DIM, tile), lambda b, h, i: (b * n + i, h, 0)),
        out_shape=jax.ShapeDtypeStruct((batch * n, GROUP_WIDTH, tile), BF16),
        scratch_shapes=[pltpu.VMEM((2, V_ROWS, tile), F32),
                        pltpu.VMEM((2, 2, tile, tile), BF16),
                        pltpu.VMEM((2, 3, tile, 2 * LANES), BF16),
                        pltpu.VMEM((1, LANES), F32),
                        pltpu.VMEM((2, 1, tile), F32)],
        compiler_params=_cparams("parallel", "parallel", "arbitrary"),
        name="diff_attention",
    )(rel_log2, lamp, dng, dqkv, dqkv, vt3, bias_t)


def _layer_norm(y, g, b):
    mu = jnp.mean(y, axis=-1, keepdims=True)
    d = y - mu
    var = jnp.mean(d * d, axis=-1, keepdims=True)
    return d * lax.rsqrt(var + LN_EPS) * g + b


def _outproj_kernel(yr_ref, yd_ref, w_ref, x_ref, g_ref, b_ref, o_ref, ob_ref, *, alpha):
    mix = jnp.dot(yr_ref[...], w_ref[:GROUP_WIDTH, :], preferred_element_type=F32)
    mix = mix + jnp.dot(yd_ref[...], w_ref[GROUP_WIDTH:, :], preferred_element_type=F32)
    y = _layer_norm(alpha * x_ref[...] + mix, g_ref[...], b_ref[...])
    o_ref[...] = y
    ob_ref[...] = y.astype(BF16)


def _out_projection(yr, yd, w_out, x, g, b, alpha, tm):
    t = x.shape[0]
    kernel = functools.partial(_outproj_kernel, alpha=alpha)
    row = lambda i: (i, 0)
    fixed = lambda i: (0, 0)
    return pl.pallas_call(
        kernel,
        grid=(t // tm,),
        in_specs=[pl.BlockSpec((tm, GROUP_WIDTH), row),
                  pl.BlockSpec((tm, GROUP_WIDTH), row),
                  pl.BlockSpec((2 * GROUP_WIDTH, D_MODEL), fixed),
                  pl.BlockSpec((tm, D_MODEL), row),
                  pl.BlockSpec((1, D_MODEL), fixed),
                  pl.BlockSpec((1, D_MODEL), fixed)],
        out_specs=[pl.BlockSpec((tm, D_MODEL), row), pl.BlockSpec((tm, D_MODEL), row)],
        out_shape=[jax.ShapeDtypeStruct((t, D_MODEL), F32), jax.ShapeDtypeStruct((t, D_MODEL), BF16)],
        compiler_params=_cparams("parallel"),
        name="out_proj_ln",
    )(yr, yd, w_out, x, g, b)


def _ffn_kernel(x_ref, xb_ref, xp_ref, xn_ref, wa_ref, wv_ref, wd_ref, cp_ref, g_ref, b_ref,
                o_ref, ob_ref, acc_ref, *, alpha, tm, n_chunks, starts, ends):
    i = pl.program_id(0)
    t0 = i * tm
    is_start = functools.reduce(jnp.logical_or, [t0 == s for s in starts])
    is_end = functools.reduce(jnp.logical_or, [t0 + tm == e for e in ends])
    keep_prev = jnp.where(is_start, 0.0, 1.0)
    keep_next = jnp.where(is_end, 0.0, 1.0)

    xb = xb_ref[...]
    halo = jnp.concatenate([xp_ref[...], xn_ref[...]], axis=0)
    hrows = xp_ref.shape[0]
    acc_ref[...] = jnp.zeros_like(acc_ref)

    def chunk_body(c, carry):
        wa = wa_ref[c]
        a = jnp.dot(xb, wa, preferred_element_type=F32)
        val = jnp.dot(xb, wv_ref[c], preferred_element_type=F32)
        ah = jnp.dot(halo, wa, preferred_element_type=F32)
        prev_row = ah[hrows - 1:hrows] * keep_prev
        next_row = ah[hrows:hrows + 1] * keep_next
        row = lax.broadcasted_iota(jnp.int32, a.shape, 0)
        a_m1 = jnp.where(row == 0, prev_row, pltpu.roll(a, 1, 0))
        a_p1 = jnp.where(row == tm - 1, next_row, pltpu.roll(a, tm - 1, 0))
        cp = cp_ref[c]
        conv = cp[3:4] + a_m1 * cp[0:1]
        conv = conv + a * cp[1:2]
        conv = conv + a_p1 * cp[2:3]
        gelu = 0.5 * conv * (1.0 + lax.erf(conv * (1.0 / math.sqrt(2.0))))
        hidden = (gelu * val).astype(BF16)
        acc_ref[...] += jnp.dot(hidden, wd_ref[c], preferred_element_type=F32)
        return carry

    lax.fori_loop(0, n_chunks, chunk_body, 0)
    y = _layer_norm(alpha * x_ref[...] + acc_ref[...], g_ref[...], b_ref[...])
    o_ref[...] = y
    ob_ref[...] = y.astype(BF16)


def _conv_glu(x, xb, wa, wv, wd, cp, g, b, alpha, tm, groups):
    t = x.shape[0]
    n_chunks, _, ck = wa.shape
    hrows = 16
    starts = tuple(r0 + bi * s for (r0, nb, s) in groups for bi in range(nb))
    ends = tuple(r0 + (bi + 1) * s for (r0, nb, s) in groups for bi in range(nb))
    kernel = functools.partial(_ffn_kernel, alpha=alpha, tm=tm, n_chunks=n_chunks,
                               starts=starts, ends=ends)
    row = lambda i: (i, 0)
    fixed2 = lambda i: (0, 0)
    fixed3 = lambda i: (0, 0, 0)
    per = tm // hrows
    last = t // hrows - 1
    return pl.pallas_call(
        kernel,
        grid=(t // tm,),
        in_specs=[pl.BlockSpec((tm, D_MODEL), row),
                  pl.BlockSpec((tm, D_MODEL), row),
                  pl.BlockSpec((hrows, D_MODEL), lambda i: (jnp.maximum(i * per - 1, 0), 0)),
                  pl.BlockSpec((hrows, D_MODEL), lambda i: (jnp.minimum((i + 1) * per, last), 0)),
                  pl.BlockSpec((n_chunks, D_MODEL, ck), fixed3),
                  pl.BlockSpec((n_chunks, D_MODEL, ck), fixed3),
                  pl.BlockSpec((n_chunks, ck, D_MODEL), fixed3),
                  pl.BlockSpec((n_chunks, 8, ck), fixed3),
                  pl.BlockSpec((1, D_MODEL), fixed2),
                  pl.BlockSpec((1, D_MODEL), fixed2)],
        out_specs=[pl.BlockSpec((tm, D_MODEL), row), pl.BlockSpec((tm, D_MODEL), row)],
        out_shape=[jax.ShapeDtypeStruct((t, D_MODEL), F32), jax.ShapeDtypeStruct((t, D_MODEL), BF16)],
        scratch_shapes=[pltpu.VMEM((tm, D_MODEL), F32)],
        compiler_params=_cparams("parallel"),
        name="conv_glu_ln",
    )(x, xb, xb, xb, wa, wv, wd, cp, g, b)


def _rotary_tables(seq):
    d = HEAD_DIM
    inv = 1.0 / (ROPE_BASE ** (jnp.arange(0, d, 2, dtype=F32) / d))
    ang = jnp.arange(seq, dtype=F32)[:, None] * inv[None, :]
    cos, sin = jnp.cos(ang), jnp.sin(ang)
    cos_t = jnp.concatenate([cos, cos, cos, cos], axis=-1)
    sin_t = jnp.concatenate([-sin, sin, -sin, sin], axis=-1)
    return cos_t, sin_t


def _tiles(groups):
    smin = min(s for (_, _, s) in groups)
    attn_tile = min(512, smin // 2)
    chunk = min(256, smin // 2)
    tm = min(512, smin // 2)
    return attn_tile, chunk, tm


def _forward(x, groups, w_in, ret_decay_logit, rel_bias, lambda_q1, lambda_k1, lambda_q2,
             lambda_k2, diff_norm_g, w_out, ln_g, ln_b, w_up, conv_w, conv_b, w_down):
    depth = w_in.shape[0]
    alpha = (2 * depth) ** 0.25
    t = x.shape[0]
    attn_tile, chunk, tm = _tiles(groups)
    smax = max(s for (_, _, s) in groups)
    gw = GROUP_WIDTH
    ck = 256
    n_chunks = D_FF // ck

    cos_t, sin_t = _rotary_tables(smax)
    bias_t = _bias_tiles(rel_bias, attn_tile)
    rel_log2 = rel_bias.astype(F32) * LOG2E
    rel_log2 = jnp.concatenate([rel_log2, jnp.max(rel_log2, axis=0, keepdims=True)], axis=0)
    in_scale = jnp.concatenate([
        jnp.ones((gw,), F32), jnp.full((gw,), HEAD_DIM ** -0.5, F32), jnp.ones((2 * gw,), F32),
        jnp.full((gw,), DIFF_QK_DIM ** -0.5 * LOG2E, F32), jnp.ones((2 * gw,), F32)])[None, :]
    ones_rows = jnp.concatenate([jnp.ones((1, attn_tile), BF16),
                                 jnp.zeros((V_ROWS - HEAD_DIM - 1, attn_tile), BF16)], axis=0)
    ones_rows = jnp.broadcast_to(ones_rows, (t // attn_tile, N_HEADS) + ones_rows.shape)

    xb = x.astype(BF16)
    for l in range(depth):
        lam_init = 0.8 - 0.6 * math.exp(-0.3 * l)
        w_in_b = w_in[l].astype(BF16)
        ret, gate, dqkv = _project(xb, w_in_b, in_scale, cos_t, sin_t, groups, tm)
        vt4 = dqkv[:, 2 * gw:].reshape(t // attn_tile, attn_tile, N_HEADS, HEAD_DIM).transpose(0, 2, 3, 1)
        vt3 = jnp.concatenate([vt4, ones_rows], axis=2).reshape(t // attn_tile, N_HEADS * V_ROWS, attn_tile)

        log_g = jax.nn.log_sigmoid(ret_decay_logit[l].astype(F32))
        lamp = jnp.stack([lambda_q1[l], lambda_k1[l], lambda_q2[l], lambda_k2[l]]).astype(F32)
        dng = diff_norm_g[l].astype(F32)[:, None]

        yr_parts, yd_parts = [], []
        for (row0, batch, seq) in groups:
            yr_parts.append(_retention(ret, gate, log_g, row0, batch, seq, chunk))
            ydt = _diff_attention(dqkv, vt3, bias_t, rel_log2, lamp, dng, row0, batch, seq,
                                  attn_tile, lam_init)
            yd_parts.append(ydt.transpose(0, 2, 1).reshape(batch * seq, gw))
        yr = jnp.concatenate(yr_parts, axis=0)
        yd = jnp.concatenate(yd_parts, axis=0)

        x, xb = _out_projection(yr, yd, w_out[l].astype(BF16), x, ln_g[l, 0][None, :].astype(F32),
                                ln_b[l, 0][None, :].astype(F32), alpha, tm)

        wa = w_up[l][:, :D_FF].astype(BF16).reshape(D_MODEL, n_chunks, ck).transpose(1, 0, 2)
        wv = w_up[l][:, D_FF:].astype(BF16).reshape(D_MODEL, n_chunks, ck).transpose(1, 0, 2)
        wd = w_down[l].astype(BF16).reshape(n_chunks, ck, D_MODEL)
        cp = jnp.concatenate([conv_w[l].astype(F32), conv_b[l].astype(F32)[None, :],
                              jnp.zeros((4, D_FF), F32)], axis=0)
        cp = cp.reshape(8, n_chunks, ck).transpose(1, 0, 2)
        x, xb = _conv_glu(x, xb, wa, wv, wd, cp, ln_g[l, 1][None, :].astype(F32),
                          ln_b[l, 1][None, :].astype(F32), alpha, tm, groups)
    return x


def kernel(x_prompt, x_sample, w_in, ret_decay_logit, rel_bias, lambda_q1, lambda_k1, lambda_q2,
           lambda_k2, diff_norm_g, w_out, ln_g, ln_b, w_up, conv_w, conv_b, w_down):
    bp, sp, d = x_prompt.shape
    bs, ss, _ = x_sample.shape
    groups = ((0, bp, sp), (bp * sp, bs, ss))
    x = jnp.concatenate([x_prompt.reshape(bp * sp, d), x_sample.reshape(bs * ss, d)], axis=0)
    y = _forward(x.astype(F32), groups, w_in, ret_decay_logit, rel_bias, lambda_q1, lambda_k1,
                 lambda_q2, lambda_k2, diff_norm_g, w_out, ln_g, ln_b, w_up, conv_w, conv_b, w_down)
    y_prompt = y[:bp * sp].reshape(bp, sp, d).astype(x_prompt.dtype)
    y_sample = y[bp * sp:].reshape(bs, ss, d).astype(x_sample.dtype)
    return y_prompt, y_sample
```

```python
import functools
import math

import jax
import jax.numpy as jnp
from jax import lax
from jax.experimental import pallas as pl
from jax.experimental.pallas import tpu as pltpu

D_MODEL = 1024
HEAD_DIM = 64
N_HEADS = 8
GROUP_WIDTH = N_HEADS * HEAD_DIM
DIFF_QK_DIM = HEAD_DIM // 2
D_FF = 2816
N_BUCKETS = 32
MAX_DISTANCE = 128
ROPE_BASE = 10000.0
LN_EPS = 1e-5
HEAD_NORM_EPS = 1e-6
LANES = 128
FAR_DISTANCE = 91
BIAS_REACH = 2
V_ROWS = HEAD_DIM + 16
LOG2E = math.log2(math.e)
NEG_BIG = -1e30
UNDERFLOW_GUARD = 2.0 ** -90
VMEM_LIMIT = 56 * 1024 * 1024

F32 = jnp.float32
BF16 = jnp.bfloat16


def _cparams(*sem):
    return pltpu.CompilerParams(dimension_semantics=sem, vmem_limit_bytes=VMEM_LIMIT)


def _proj_kernel(x_ref, w_ref, s_ref, cos_ref, sin_ref, ret_ref, gate_ref, dqkv_ref):
    gw = GROUP_WIDTH
    x = x_ref[...]
    tm = x.shape[0]
    lane = lax.broadcasted_iota(jnp.int32, (tm, LANES), 1)
    low_half = (lane % HEAD_DIM) < (HEAD_DIM // 2)
    cos = cos_ref[...]
    sin = sin_ref[...]
    qk = jnp.dot(x, w_ref[:, :2 * gw], preferred_element_type=F32) * s_ref[:, :2 * gw]
    for p in range(2 * gw // LANES):
        sl = slice(p * LANES, (p + 1) * LANES)
        xx = qk[:, sl]
        swapped = jnp.where(low_half, pltpu.roll(xx, LANES - HEAD_DIM // 2, 1),
                            pltpu.roll(xx, HEAD_DIM // 2, 1))
        ret_ref[:, sl] = (xx * cos + swapped * sin).astype(ret_ref.dtype)
    ret_ref[:, 2 * gw:] = jnp.dot(x, w_ref[:, 2 * gw:3 * gw],
                                  preferred_element_type=F32).astype(ret_ref.dtype)
    gate_ref[...] = jnp.dot(x, w_ref[:, 3 * gw:4 * gw], preferred_element_type=F32)
    dqkv_ref[...] = (jnp.dot(x, w_ref[:, 4 * gw:], preferred_element_type=F32)
                     * s_ref[:, 4 * gw:]).astype(dqkv_ref.dtype)


def _position_block(i, tm, groups):
    t0 = i * tm
    blk = t0 // tm
    for (row0, _, seq) in groups:
        blk = jnp.where(t0 >= row0, ((t0 - row0) % seq) // tm, blk)
    return blk


def _project(xb, w, scale, cos_t, sin_t, groups, tm):
    t, k = xb.shape
    gw = GROUP_WIDTH
    row = lambda i: (i, 0)
    fixed = lambda i: (0, 0)
    pos = lambda i: (_position_block(i, tm, groups), 0)
    return pl.pallas_call(
        _proj_kernel,
        grid=(t // tm,),
        in_specs=[pl.BlockSpec((tm, k), row),
                  pl.BlockSpec((k, 7 * gw), fixed),
                  pl.BlockSpec((1, 7 * gw), fixed),
                  pl.BlockSpec((tm, LANES), pos),
                  pl.BlockSpec((tm, LANES), pos)],
        out_specs=[pl.BlockSpec((tm, 3 * gw), row), pl.BlockSpec((tm, gw), row),
                   pl.BlockSpec((tm, 3 * gw), row)],
        out_shape=[jax.ShapeDtypeStruct((t, 3 * gw), BF16), jax.ShapeDtypeStruct((t, gw), F32),
                   jax.ShapeDtypeStruct((t, 3 * gw), BF16)],
        compiler_params=_cparams("parallel"),
        name="in_proj",
    )(xb, w, scale, cos_t, sin_t)


def _ret_kernel(lg_ref, q_ref, k_ref, v_ref, g_ref, o_ref,
                rf_ref, rb_ref, rnext_ref, mask_ref, tab_ref, *, nc, chunk):
    t = pl.program_id(1)
    c = chunk
    n_pairs = GROUP_WIDTH // LANES
    lane = lax.broadcasted_iota(jnp.int32, (c, LANES), 1)
    head0 = lane < HEAD_DIM
    r_i = lax.broadcasted_iota(jnp.int32, (LANES, LANES), 0) // HEAD_DIM
    c_i = lax.broadcasted_iota(jnp.int32, (LANES, LANES), 1) // HEAD_DIM
    same_head = r_i == c_i

    @pl.when(t == 0)
    def _init():
        rb_ref[...] = jnp.zeros_like(rb_ref)
        qi = lax.broadcasted_iota(jnp.int32, (c, c), 0)
        ki = lax.broadcasted_iota(jnp.int32, (c, c), 1)
        diff = (qi - ki).astype(F32)
        pos = lax.broadcasted_iota(jnp.int32, (c, LANES), 0).astype(F32)
        for p in range(n_pairs):
            for hh in range(2):
                lf = lg_ref[0, 2 * p + hh]
                lb = lg_ref[1, 2 * p + hh]
                mask_ref[p, hh * c:(hh + 1) * c, :] = jnp.where(
                    diff >= 0, jnp.exp(lf * jnp.maximum(diff, 0.0)),
                    jnp.exp(lb * jnp.maximum(-diff, 0.0)))
            lfl = jnp.where(head0, lg_ref[0, 2 * p], lg_ref[0, 2 * p + 1])
            lbl = jnp.where(head0, lg_ref[1, 2 * p], lg_ref[1, 2 * p + 1])
            tab_ref[p, 0] = jnp.exp(lfl * (c - 1 - pos))
            tab_ref[p, 1] = jnp.exp(lbl * pos)
            tab_ref[p, 2] = jnp.exp(lfl * (pos + 1.0))
            tab_ref[p, 3] = jnp.exp(lbl * (c - pos))
            tab_ref[p, 4] = jnp.exp(lfl * c)
            tab_ref[p, 5] = jnp.exp(lbl * c)

    def summary(k, w, v):
        kw = (k.astype(F32) * w).astype(BF16)
        kv = lax.dot_general(kw, v, (((0,), (0,)), ((), ())), preferred_element_type=F32)
        return jnp.where(same_head, kv, 0.0)

    @pl.when(t < nc)
    def _backward():
        for p in range(n_pairs):
            sl = slice(p * LANES, (p + 1) * LANES)
            rnext_ref[nc - 1 - t, p] = rb_ref[p].astype(BF16)
            rb_ref[p] = (tab_ref[p, 5][:LANES] * rb_ref[p]
                         + summary(k_ref[:, sl], tab_ref[p, 1], v_ref[:, sl]))

    @pl.when(t >= nc)
    def _forward():
        @pl.when(t == nc)
        def _():
            rf_ref[...] = jnp.zeros_like(rf_ref)

        for p in range(n_pairs):
            sl = slice(p * LANES, (p + 1) * LANES)
            q = q_ref[:, sl]
            k = k_ref[:, sl]
            v = v_ref[:, sl]
            zero = jnp.zeros_like(q)
            q2 = jnp.concatenate([jnp.where(head0, q, zero), jnp.where(head0, zero, q)], axis=0)
            s = lax.dot_general(q2, k, (((1,), (1,)), ((), ())), preferred_element_type=F32)
            o2 = jnp.dot((s * mask_ref[p]).astype(BF16), v, preferred_element_type=F32)
            y = jnp.where(head0, o2[:c], o2[c:])
            y = y + jnp.dot(q, rf_ref[p].astype(BF16), preferred_element_type=F32) * tab_ref[p, 2]
            y = y + jnp.dot(q, rnext_ref[t - nc, p], preferred_element_type=F32) * tab_ref[p, 3]

            sq = y * y
            s0 = jnp.sum(jnp.where(head0, sq, 0.0), axis=1, keepdims=True)
            s1 = jnp.sum(jnp.where(head0, 0.0, sq), axis=1, keepdims=True)
            ms = jnp.where(head0, s0, s1) * (1.0 / HEAD_DIM)
            y = y * lax.rsqrt(ms + HEAD_NORM_EPS)
            g = g_ref[:, sl]
            o_ref[:, sl] = (g / (1.0 + jnp.exp(-g)) * y).astype(o_ref.dtype)

            rf_ref[p] = tab_ref[p, 4][:LANES] * rf_ref[p] + summary(k, tab_ref[p, 0], v)


def _retention(ret, gate, log_g, row0, batch, seq, chunk):
    nc = seq // chunk
    blk0 = row0 // chunk
    n_pairs = GROUP_WIDTH // LANES
    gw = GROUP_WIDTH

    def kc(t):
        return jnp.where(t < nc, nc - 1 - t, t - nc)

    def qc(t):
        return jnp.maximum(t - nc, 0)

    def rows(b, cc):
        return blk0 + b * nc + cc

    kernel = functools.partial(_ret_kernel, nc=nc, chunk=chunk)
    return pl.pallas_call(
        kernel,
        grid=(batch, 2 * nc),
        in_specs=[pl.BlockSpec(memory_space=pltpu.SMEM),
                  pl.BlockSpec((chunk, gw), lambda b, t: (rows(b, qc(t)), 0)),
                  pl.BlockSpec((chunk, gw), lambda b, t: (rows(b, kc(t)), 1)),
                  pl.BlockSpec((chunk, gw), lambda b, t: (rows(b, kc(t)), 2)),
                  pl.BlockSpec((chunk, gw), lambda b, t: (rows(b, qc(t)), 0))],
        out_specs=pl.BlockSpec((chunk, gw), lambda b, t: (b * nc + qc(t), 0)),
        out_shape=jax.ShapeDtypeStruct((batch * seq, gw), BF16),
        scratch_shapes=[pltpu.VMEM((n_pairs, LANES, LANES), F32),
                        pltpu.VMEM((n_pairs, LANES, LANES), F32),
                        pltpu.VMEM((nc, n_pairs, LANES, LANES), BF16),
                        pltpu.VMEM((n_pairs, 2 * chunk, chunk), F32),
                        pltpu.VMEM((n_pairs, 6, chunk, LANES), F32)],
        compiler_params=_cparams("parallel", "arbitrary"),
        name="retention",
    )(log_g, ret, ret, ret, gate)


def _bias_kernel(rb_ref, bucket_ref, o_ref):
    h = pl.program_id(0)
    bk = bucket_ref[...]
    out = jnp.zeros(bk.shape, F32)
    for n in range(N_BUCKETS):
        out = jnp.where(bk == n, rb_ref[n, h], out)
    o_ref[...] = out * LOG2E


def _bias_tiles(rel_bias, tile):
    assert tile > FAR_DISTANCE
    a = jnp.arange(tile, dtype=jnp.int32)[:, None]
    b = jnp.arange(tile, dtype=jnp.int32)[None, :]
    rel = jnp.stack([(d * tile + a - b) for d in range(-BIAS_REACH, BIAS_REACH + 1)])
    nb = N_BUCKETS // 2
    max_exact = nb // 2
    n = jnp.abs(rel)
    nf = jnp.maximum(n, 1).astype(F32)
    large = max_exact + (jnp.log(nf / max_exact) / math.log(MAX_DISTANCE / max_exact)
                         * (nb - max_exact)).astype(jnp.int32)
    large = jnp.minimum(large, nb - 1)
    bucket = jnp.where(rel > 0, nb, 0) + jnp.where(n < max_exact, n, large)
    return pl.pallas_call(
        _bias_kernel,
        grid=(N_HEADS, 2 * BIAS_REACH + 1),
        in_specs=[pl.BlockSpec(memory_space=pltpu.SMEM),
                  pl.BlockSpec((None, tile, tile), lambda h, d: (d, 0, 0))],
        out_specs=pl.BlockSpec((None, None, tile, tile), lambda h, d: (h, d, 0, 0)),
        out_shape=jax.ShapeDtypeStruct((N_HEADS, 2 * BIAS_REACH + 1, tile, tile), F32),
        compiler_params=_cparams("parallel", "arbitrary"),
        name="t5_bias_tiles",
    )(rel_bias.astype(F32), bucket.astype(jnp.int32))


def _attn_kernel(rb_ref, lamp_ref, dng_ref, q_ref, k_ref, vt_ref, bt_ref, o_ref,
                 a_ref, p_ref, qa_ref, kmax_ref, m_ref, *, n_tiles, tile, unroll, lam_init):
    h = pl.program_id(1)
    qi = pl.program_id(2)
    hh = h % 2
    n = n_tiles
    dims_nt = (((1,), (1,)), ((), ()))
    half = DIFF_QK_DIM
    group_of_lane = lax.broadcasted_iota(jnp.int32, (LANES, LANES), 0) // half
    group_sum = (group_of_lane == lax.broadcasted_iota(jnp.int32, (LANES, LANES), 1)).astype(BF16)

    def half_norms(x):
        xf = x.astype(F32)
        return jnp.dot((xf * xf).astype(BF16), group_sum, preferred_element_type=F32)

    @pl.when(qi == 0)
    def _key_norms():
        def body(j, mx):
            kt = k_ref[pl.ds(pl.multiple_of(j * tile, tile), tile), :]
            return jnp.maximum(mx, jnp.max(half_norms(kt), axis=0, keepdims=True))
        kmax_ref[...] = lax.fori_loop(0, n, body, jnp.zeros((1, LANES), F32))

    q = q_ref[...]
    lane = lax.broadcasted_iota(jnp.int32, q.shape, 1)
    lane_row = lax.broadcasted_iota(jnp.int32, (1, LANES), 1)
    zero = jnp.zeros_like(q)
    qn = half_norms(q)
    side_bias = (rb_ref[N_BUCKETS // 2 - 1, h], 0.0, rb_ref[N_BUCKETS - 1, h])
    for t in range(2):
        g = 2 * hh + t
        lo_lane = g * half
        qz = jnp.where((lane >= lo_lane) & (lane < lo_lane + half), q, zero)
        qn2 = jnp.sum(jnp.where(lane == g, qn, 0.0), axis=1, keepdims=True)
        kn2 = jnp.max(jnp.where(lane_row == g, kmax_ref[...], 0.0), axis=1, keepdims=True)
        bound = jnp.sqrt(qn2 * kn2) + rb_ref[N_BUCKETS, h]
        for side in range(3):
            v = side_bias[side] - bound
            hi = v.astype(BF16).astype(F32)
            lo = (v - hi).astype(BF16).astype(F32)
            cols = jnp.where(lane == 0, hi, jnp.where(lane == 1, lo, 0.0))
            qa_ref[t, side] = jnp.concatenate([qz.astype(F32), cols], axis=1).T.astype(BF16)

    ones = jnp.ones((tile, LANES), BF16)

    def near(j):
        return jnp.abs(j - qi) < BIAS_REACH

    def probabilities(j, slot, with_bias):
        kt = k_ref[pl.ds(pl.multiple_of(j * tile, tile), tile), :]
        kaug = jnp.concatenate([kt, ones], axis=1)
        side = 1 if with_bias else jnp.where(j < qi, 0, 2)
        for t in range(2):
            s = jnp.dot(kaug, qa_ref[t, side], preferred_element_type=F32)
            if with_bias:
                s = s + bt_ref[jnp.clip(j - qi, -BIAS_REACH, BIAS_REACH) + BIAS_REACH]
            p_ref[slot, t] = jnp.exp2(s).astype(BF16)

    def accumulate(j, slot):
        vt = vt_ref[j]
        for t in range(2):
            a_ref[t] += jnp.dot(vt, p_ref[slot, t], preferred_element_type=F32)

    a_ref[...] = jnp.zeros_like(a_ref)

    lax.cond(near(0), lambda: probabilities(0, 0, True), lambda: probabilities(0, 0, False))

    def group(b, last):
        j = unroll * b

        def run(with_bias):
            for k in range(unroll):
                if not (last and k == unroll - 1):
                    probabilities(j + k + 1, (k + 1) % 2, with_bias)
                accumulate(j + k, k % 2)

        any_near = functools.reduce(jnp.logical_or, [near(j + k + 1) for k in range(unroll)])
        lax.cond(any_near, lambda: run(True), lambda: run(False))

    def group_body(b, carry):
        group(b, False)
        return carry

    lax.fori_loop(0, n // unroll - 1, group_body, 0)
    group(n // unroll - 1, True)

    sums = jnp.minimum(a_ref[0, HEAD_DIM:HEAD_DIM + 1, :], a_ref[1, HEAD_DIM:HEAD_DIM + 1, :])
    trusted = jnp.min(sums) >= UNDERFLOW_GUARD

    @pl.when(jnp.logical_not(trusted))
    def _exact_running_max():
        m_ref[...] = jnp.full(m_ref.shape, NEG_BIG, F32)
        a_ref[...] = jnp.zeros_like(a_ref)

        def body(j, carry):
            kt = k_ref[pl.ds(pl.multiple_of(j * tile, tile), tile), :]
            vt = vt_ref[j]
            bias = bt_ref[jnp.clip(j - qi, -BIAS_REACH, BIAS_REACH) + BIAS_REACH]
            for t in range(2):
                s = jnp.dot(kt, qa_ref[t, 1, :LANES, :], preferred_element_type=F32) + bias
                m_prev = m_ref[t]
                m_new = jnp.maximum(m_prev, jnp.max(s, axis=0, keepdims=True))
                pr = jnp.exp2(s - m_new).astype(BF16)
                a_ref[t] = (jnp.exp2(m_prev - m_new) * a_ref[t]
                            + jnp.dot(vt, pr, preferred_element_type=F32))
                m_ref[t] = m_new
            return carry

        lax.fori_loop(0, n, body, 0)

    lp = lamp_ref[...]
    lam = (jnp.exp(jnp.sum(lp[0:1] * lp[1:2], axis=1, keepdims=True))
           - jnp.exp(jnp.sum(lp[2:3] * lp[3:4], axis=1, keepdims=True)) + lam_init)
    a1 = a_ref[0]
    a2 = a_ref[1]
    o = (a1[:HEAD_DIM] / a1[HEAD_DIM:HEAD_DIM + 1]
         - lam * (a2[:HEAD_DIM] / a2[HEAD_DIM:HEAD_DIM + 1]))
    ms = jnp.mean(o * o, axis=0, keepdims=True)
    y = o * lax.rsqrt(ms + HEAD_NORM_EPS) * dng_ref[...] * (1.0 - lam_init)
    o_ref[...] = y.astype(o_ref.dtype)


def _diff_attention(dqkv, vt3, bias_t, rel_log2, lamp, dng, row0, batch, seq, tile, lam_init):
    n = seq // tile
    unroll = 4 if n % 4 == 0 else 2
    assert n % unroll == 0
    n_pairs = GROUP_WIDTH // LANES
    qblk0 = row0 // tile
    sblk0 = row0 // seq
    n_bias = 2 * BIAS_REACH + 1
    kernel = functools.partial(_attn_kernel, n_tiles=n, tile=tile, unroll=unroll, lam_init=lam_init)
    return pl.pallas_call(
        kernel,
        grid=(batch, N_HEADS, n),
        in_specs=[pl.BlockSpec(memory_space=pltpu.SMEM),
                  pl.BlockSpec((4, DIFF_QK_DIM), lambda b, h, i: (0, 0)),
                  pl.BlockSpec((HEAD_DIM, 1), lambda b, h, i: (0, 0)),
                  pl.BlockSpec((tile, LANES), lambda b, h, i: (qblk0 + b * n + i, h // 2)),
                  pl.BlockSpec((seq, LANES), lambda b, h, i: (sblk0 + b, n_pairs + h // 2)),
                  pl.BlockSpec((n, V_ROWS, tile), lambda b, h, i: (sblk0 + b, h, 0)),
                  pl.BlockSpec((None, n_bias, tile, tile), lambda b, h, i: (h, 0, 0, 0))],
        out_specs=pl.BlockSpec((None, HEAD_DIM, tile), lambda b, h, i: (b * n + i, h, 0)),
        out_shape=jax.ShapeDtypeStruct((batch * n, GROUP_WIDTH, tile), BF16),
        scratch_shapes=[pltpu.VMEM((2, V_ROWS, tile), F32),
                        pltpu.VMEM((2, 2, tile, tile), BF16),
                        pltpu.VMEM((2, 3, 2 * LANES, tile), BF16),
                        pltpu.VMEM((1, LANES), F32),
                        pltpu.VMEM((2, 1, tile), F32)],
        compiler_params=_cparams("parallel", "parallel", "arbitrary"),
        name="diff_attention",
    )(rel_log2, lamp, dng, dqkv, dqkv, vt3, bias_t)


def _layer_norm(y, g, b):
    mu = jnp.mean(y, axis=-1, keepdims=True)
    d = y - mu
    var = jnp.mean(d * d, axis=-1, keepdims=True)
    return d * lax.rsqrt(var + LN_EPS) * g + b


def _outproj_kernel(yr_ref, yd_ref, w_ref, x_ref, g_ref, b_ref, o_ref, ob_ref, *, alpha):
    mix = jnp.dot(yr_ref[...], w_ref[:GROUP_WIDTH, :], preferred_element_type=F32)
    mix = mix + jnp.dot(yd_ref[...], w_ref[GROUP_WIDTH:, :], preferred_element_type=F32)
    y = _layer_norm(alpha * x_ref[...] + mix, g_ref[...], b_ref[...])
    o_ref[...] = y
    ob_ref[...] = y.astype(BF16)


def _out_projection(yr, yd, w_out, x, g, b, alpha, tm):
    t = x.shape[0]
    kernel = functools.partial(_outproj_kernel, alpha=alpha)
    row = lambda i: (i, 0)
    fixed = lambda i: (0, 0)
    return pl.pallas_call(
        kernel,
        grid=(t // tm,),
        in_specs=[pl.BlockSpec((tm, GROUP_WIDTH), row),
                  pl.BlockSpec((tm, GROUP_WIDTH), row),
                  pl.BlockSpec((2 * GROUP_WIDTH, D_MODEL), fixed),
                  pl.BlockSpec((tm, D_MODEL), row),
                  pl.BlockSpec((1, D_MODEL), fixed),
                  pl.BlockSpec((1, D_MODEL), fixed)],
        out_specs=[pl.BlockSpec((tm, D_MODEL), row), pl.BlockSpec((tm, D_MODEL), row)],
        out_shape=[jax.ShapeDtypeStruct((t, D_MODEL), F32), jax.ShapeDtypeStruct((t, D_MODEL), BF16)],
        compiler_params=_cparams("parallel"),
        name="out_proj_ln",
    )(yr, yd, w_out, x, g, b)


def _ffn_kernel(x_ref, xb_ref, xp_ref, xn_ref, wa_ref, wv_ref, wd_ref, cp_ref, g_ref, b_ref,
                o_ref, ob_ref, acc_ref, *, alpha, tm, n_chunks, starts, ends):
    i = pl.program_id(0)
    t0 = i * tm
    is_start = functools.reduce(jnp.logical_or, [t0 == s for s in starts])
    is_end = functools.reduce(jnp.logical_or, [t0 + tm == e for e in ends])
    keep_prev = jnp.where(is_start, 0.0, 1.0)
    keep_next = jnp.where(is_end, 0.0, 1.0)

    xb = xb_ref[...]
    halo = jnp.concatenate([xp_ref[...], xn_ref[...]], axis=0)
    hrows = xp_ref.shape[0]
    acc_ref[...] = jnp.zeros_like(acc_ref)

    def chunk_body(c, carry):
        wa = wa_ref[c]
        a = jnp.dot(xb, wa, preferred_element_type=F32)
        val = jnp.dot(xb, wv_ref[c], preferred_element_type=F32)
        ah = jnp.dot(halo, wa, preferred_element_type=F32)
        prev_row = ah[hrows - 1:hrows] * keep_prev
        next_row = ah[hrows:hrows + 1] * keep_next
        row = lax.broadcasted_iota(jnp.int32, a.shape, 0)
        a_m1 = jnp.where(row == 0, prev_row, pltpu.roll(a, 1, 0))
        a_p1 = jnp.where(row == tm - 1, next_row, pltpu.roll(a, tm - 1, 0))
        cp = cp_ref[c]
        conv = cp[3:4] + a_m1 * cp[0:1]
        conv = conv + a * cp[1:2]
        conv = conv + a_p1 * cp[2:3]
        gelu = 0.5 * conv * (1.0 + lax.erf(conv * (1.0 / math.sqrt(2.0))))
        hidden = (gelu * val).astype(BF16)
        acc_ref[...] += jnp.dot(hidden, wd_ref[c], preferred_element_type=F32)
        return carry

    lax.fori_loop(0, n_chunks, chunk_body, 0)
    y = _layer_norm(alpha * x_ref[...] + acc_ref[...], g_ref[...], b_ref[...])
    o_ref[...] = y
    ob_ref[...] = y.astype(BF16)


def _conv_glu(x, xb, wa, wv, wd, cp, g, b, alpha, tm, groups):
    t = x.shape[0]
    n_chunks, _, ck = wa.shape
    hrows = 16
    starts = tuple(r0 + bi * s for (r0, nb, s) in groups for bi in range(nb))
    ends = tuple(r0 + (bi + 1) * s for (r0, nb, s) in groups for bi in range(nb))
    kernel = functools.partial(_ffn_kernel, alpha=alpha, tm=tm, n_chunks=n_chunks,
                               starts=starts, ends=ends)
    row = lambda i: (i, 0)
    fixed2 = lambda i: (0, 0)
    fixed3 = lambda i: (0, 0, 0)
    per = tm // hrows
    last = t // hrows - 1
    return pl.pallas_call(
        kernel,
        grid=(t // tm,),
        in_specs=[pl.BlockSpec((tm, D_MODEL), row),
                  pl.BlockSpec((tm, D_MODEL), row),
                  pl.BlockSpec((hrows, D_MODEL), lambda i: (jnp.maximum(i * per - 1, 0), 0)),
                  pl.BlockSpec((hrows, D_MODEL), lambda i: (jnp.minimum((i + 1) * per, last), 0)),
                  pl.BlockSpec((n_chunks, D_MODEL, ck), fixed3),
                  pl.BlockSpec((n_chunks, D_MODEL, ck), fixed3),
                  pl.BlockSpec((n_chunks, ck, D_MODEL), fixed3),
                  pl.BlockSpec((n_chunks, 8, ck), fixed3),
                  pl.BlockSpec((1, D_MODEL), fixed2),
                  pl.BlockSpec((1, D_MODEL), fixed2)],
        out_specs=[pl.BlockSpec((tm, D_MODEL), row), pl.BlockSpec((tm, D_MODEL), row)],
        out_shape=[jax.ShapeDtypeStruct((t, D_MODEL), F32), jax.ShapeDtypeStruct((t, D_MODEL), BF16)],
        scratch_shapes=[pltpu.VMEM((tm, D_MODEL), F32)],
        compiler_params=_cparams("parallel"),
        name="conv_glu_ln",
    )(x, xb, xb, xb, wa, wv, wd, cp, g, b)


def _rotary_tables(seq):
    d = HEAD_DIM
    inv = 1.0 / (ROPE_BASE ** (jnp.arange(0, d, 2, dtype=F32) / d))
    ang = jnp.arange(seq, dtype=F32)[:, None] * inv[None, :]
    cos, sin = jnp.cos(ang), jnp.sin(ang)
    cos_t = jnp.concatenate([cos, cos, cos, cos], axis=-1)
    sin_t = jnp.concatenate([-sin, sin, -sin, sin], axis=-1)
    return cos_t, sin_t


def _tiles(groups):
    smin = min(s for (_, _, s) in groups)
    attn_tile = min(512, smin // 2)
    chunk = min(256, smin // 2)
    tm = min(512, smin // 2)
    return attn_tile, chunk, tm


def _forward(x, groups, w_in, ret_decay_logit, rel_bias, lambda_q1, lambda_k1, lambda_q2,
             lambda_k2, diff_norm_g, w_out, ln_g, ln_b, w_up, conv_w, conv_b, w_down):
    depth = w_in.shape[0]
    alpha = (2 * depth) ** 0.25
    t = x.shape[0]
    attn_tile, chunk, tm = _tiles(groups)
    smax = max(s for (_, _, s) in groups)
    gw = GROUP_WIDTH
    ck = 256
    n_chunks = D_FF // ck

    cos_t, sin_t = _rotary_tables(smax)
    bias_t = _bias_tiles(rel_bias, attn_tile)
    rel_log2 = rel_bias.astype(F32) * LOG2E
    rel_log2 = jnp.concatenate([rel_log2, jnp.max(rel_log2, axis=0, keepdims=True)], axis=0)
    in_scale = jnp.concatenate([
        jnp.ones((gw,), F32), jnp.full((gw,), HEAD_DIM ** -0.5, F32), jnp.ones((2 * gw,), F32),
        jnp.full((gw,), DIFF_QK_DIM ** -0.5 * LOG2E, F32), jnp.ones((2 * gw,), F32)])[None, :]
    ones_rows = jnp.concatenate([jnp.ones((1, attn_tile), BF16),
                                 jnp.zeros((V_ROWS - HEAD_DIM - 1, attn_tile), BF16)], axis=0)
    ones_rows = jnp.broadcast_to(ones_rows, (t // attn_tile, N_HEADS) + ones_rows.shape)

    xb = x.astype(BF16)
    for l in range(depth):
        lam_init = 0.8 - 0.6 * math.exp(-0.3 * l)
        w_in_b = w_in[l].astype(BF16)
        ret, gate, dqkv = _project(xb, w_in_b, in_scale, cos_t, sin_t, groups, tm)
        vt4 = dqkv[:, 2 * gw:].reshape(t // attn_tile, attn_tile, N_HEADS, HEAD_DIM).transpose(0, 2, 3, 1)
        vt3 = jnp.concatenate([vt4, ones_rows], axis=2).reshape(t // attn_tile, N_HEADS * V_ROWS, attn_tile)

        log_g = jax.nn.log_sigmoid(ret_decay_logit[l].astype(F32))
        lamp = jnp.stack([lambda_q1[l], lambda_k1[l], lambda_q2[l], lambda_k2[l]]).astype(F32)
        dng = diff_norm_g[l].astype(F32)[:, None]

        yr_parts, yd_parts = [], []
        for (row0, batch, seq) in groups:
            yr_parts.append(_retention(ret, gate, log_g, row0, batch, seq, chunk))
            ydt = _diff_attention(dqkv, vt3, bias_t, rel_log2, lamp, dng, row0, batch, seq,
                                  attn_tile, lam_init)
            yd_parts.append(ydt.transpose(0, 2, 1).reshape(batch * seq, gw))
        yr = jnp.concatenate(yr_parts, axis=0)
        yd = jnp.concatenate(yd_parts, axis=0)

        x, xb = _out_projection(yr, yd, w_out[l].astype(BF16), x, ln_g[l, 0][None, :].astype(F32),
                                ln_b[l, 0][None, :].astype(F32), alpha, tm)

        wa = w_up[l][:, :D_FF].astype(BF16).reshape(D_MODEL, n_chunks, ck).transpose(1, 0, 2)
        wv = w_up[l][:, D_FF:].astype(BF16).reshape(D_MODEL, n_chunks, ck).transpose(1, 0, 2)
        wd = w_down[l].astype(BF16).reshape(n_chunks, ck, D_MODEL)
        cp = jnp.concatenate([conv_w[l].astype(F32), conv_b[l].astype(F32)[None, :],
                              jnp.zeros((4, D_FF), F32)], axis=0)
        cp = cp.reshape(8, n_chunks, ck).transpose(1, 0, 2)
        x, xb = _conv_glu(x, xb, wa, wv, wd, cp, ln_g[l, 1][None, :].astype(F32),
                          ln_b[l, 1][None, :].astype(F32), alpha, tm, groups)
    return x


def kernel(x_prompt, x_sample, w_in, ret_decay_logit, rel_bias, lambda_q1, lambda_k1, lambda_q2,
           lambda_k2, diff_norm_g, w_out, ln_g, ln_b, w_up, conv_w, conv_b, w_down):
    bp, sp, d = x_prompt.shape
    bs, ss, _ = x_sample.shape
    groups = ((0, bp, sp), (bp * sp, bs, ss))
    x = jnp.concatenate([x_prompt.reshape(bp * sp, d), x_sample.reshape(bs * ss, d)], axis=0)
    y = _forward(x.astype(F32), groups, w_in, ret_decay_logit, rel_bias, lambda_q1, lambda_k1,
                 lambda_q2, lambda_k2, diff_norm_g, w_out, ln_g, ln_b, w_up, conv_w, conv_b, w_down)
    y_prompt = y[:bp * sp].reshape(bp, sp, d).astype(x_prompt.dtype)
    y_sample = y[bp * sp:].reshape(bs, ss, d).astype(x_sample.dtype)
    return y_prompt, y_sample
```

```python
import functools
import math

import jax
import jax.numpy as jnp
from jax import lax
from jax.experimental import pallas as pl
from jax.experimental.pallas import tpu as pltpu

D_MODEL = 1024
HEAD_DIM = 64
N_HEADS = 8
GROUP_WIDTH = N_HEADS * HEAD_DIM
DIFF_QK_DIM = HEAD_DIM // 2
D_FF = 2816
N_BUCKETS = 32
MAX_DISTANCE = 128
ROPE_BASE = 10000.0
LN_EPS = 1e-5
HEAD_NORM_EPS = 1e-6
LANES = 128
FAR_DISTANCE = 91
BIAS_REACH = 2
V_ROWS = HEAD_DIM + 16
LOG2E = math.log2(math.e)
NEG_BIG = -1e30
UNDERFLOW_GUARD = 2.0 ** -90
VMEM_LIMIT = 56 * 1024 * 1024

F32 = jnp.float32
BF16 = jnp.bfloat16


def _cparams(*sem):
    return pltpu.CompilerParams(dimension_semantics=sem, vmem_limit_bytes=VMEM_LIMIT)


def _proj_kernel(x_ref, w_ref, wvt_ref, s_ref, cos_ref, sin_ref, ret_ref, gate_ref, dqk_ref, vt_ref):
    gw = GROUP_WIDTH
    x = x_ref[...]
    tm = x.shape[0]
    lane = lax.broadcasted_iota(jnp.int32, (tm, LANES), 1)
    low_half = (lane % HEAD_DIM) < (HEAD_DIM // 2)
    cos = cos_ref[...]
    sin = sin_ref[...]
    qk = jnp.dot(x, w_ref[:, :2 * gw], preferred_element_type=F32) * s_ref[:, :2 * gw]
    for p in range(2 * gw // LANES):
        sl = slice(p * LANES, (p + 1) * LANES)
        xx = qk[:, sl]
        swapped = jnp.where(low_half, pltpu.roll(xx, LANES - HEAD_DIM // 2, 1),
                            pltpu.roll(xx, HEAD_DIM // 2, 1))
        ret_ref[:, sl] = (xx * cos + swapped * sin).astype(ret_ref.dtype)
    ret_ref[:, 2 * gw:] = jnp.dot(x, w_ref[:, 2 * gw:3 * gw],
                                  preferred_element_type=F32).astype(ret_ref.dtype)
    gate_ref[...] = jnp.dot(x, w_ref[:, 3 * gw:4 * gw], preferred_element_type=F32)
    dqk_ref[...] = (jnp.dot(x, w_ref[:, 4 * gw:], preferred_element_type=F32)
                    * s_ref[:, 4 * gw:]).astype(dqk_ref.dtype)
    vt = lax.dot_general(wvt_ref[...], x, (((1,), (1,)), ((), ())), preferred_element_type=F32)
    pad_rows = V_ROWS - HEAD_DIM
    pad = (lax.broadcasted_iota(jnp.int32, (pad_rows, tm), 0) == 0).astype(vt_ref.dtype)
    for h in range(N_HEADS):
        vt_ref[h * V_ROWS:h * V_ROWS + HEAD_DIM, :] = (
            vt[h * HEAD_DIM:(h + 1) * HEAD_DIM].astype(vt_ref.dtype))
        vt_ref[h * V_ROWS + HEAD_DIM:(h + 1) * V_ROWS, :] = pad


def _position_block(i, tm, groups):
    t0 = i * tm
    blk = t0 // tm
    for (row0, _, seq) in groups:
        blk = jnp.where(t0 >= row0, ((t0 - row0) % seq) // tm, blk)
    return blk


def _project(xb, w, w_vt, scale, cos_t, sin_t, groups, tm):
    t, k = xb.shape
    gw = GROUP_WIDTH
    row = lambda i: (i, 0)
    fixed = lambda i: (0, 0)
    pos = lambda i: (_position_block(i, tm, groups), 0)
    return pl.pallas_call(
        _proj_kernel,
        grid=(t // tm,),
        in_specs=[pl.BlockSpec((tm, k), row),
                  pl.BlockSpec((k, 6 * gw), fixed, pipeline_mode=pl.Buffered(1)),
                  pl.BlockSpec((gw, k), fixed, pipeline_mode=pl.Buffered(1)),
                  pl.BlockSpec((1, 6 * gw), fixed),
                  pl.BlockSpec((tm, LANES), pos),
                  pl.BlockSpec((tm, LANES), pos)],
        out_specs=[pl.BlockSpec((tm, 3 * gw), row), pl.BlockSpec((tm, gw), row),
                   pl.BlockSpec((tm, 2 * gw), row),
                   pl.BlockSpec((None, N_HEADS * V_ROWS, tm), lambda i: (i, 0, 0))],
        out_shape=[jax.ShapeDtypeStruct((t, 3 * gw), BF16), jax.ShapeDtypeStruct((t, gw), F32),
                   jax.ShapeDtypeStruct((t, 2 * gw), BF16),
                   jax.ShapeDtypeStruct((t // tm, N_HEADS * V_ROWS, tm), BF16)],
        compiler_params=_cparams("parallel"),
        name="in_proj",
    )(xb, w, w_vt, scale, cos_t, sin_t)


def _ret_kernel(lg_ref, q_ref, k_ref, v_ref, g_ref, _, o_ref,
                rf_ref, rb_ref, rnext_ref, mask_ref, tab_ref, *, nc, chunk):
    t = pl.program_id(1)
    c = chunk
    n_pairs = GROUP_WIDTH // LANES
    lane = lax.broadcasted_iota(jnp.int32, (c, LANES), 1)
    head0 = lane < HEAD_DIM
    r_i = lax.broadcasted_iota(jnp.int32, (LANES, LANES), 0) // HEAD_DIM
    c_i = lax.broadcasted_iota(jnp.int32, (LANES, LANES), 1) // HEAD_DIM
    same_head = r_i == c_i

    @pl.when(t == 0)
    def _init():
        rb_ref[...] = jnp.zeros_like(rb_ref)
        qi = lax.broadcasted_iota(jnp.int32, (c, c), 0)
        ki = lax.broadcasted_iota(jnp.int32, (c, c), 1)
        diff = (qi - ki).astype(F32)
        pos = lax.broadcasted_iota(jnp.int32, (c, LANES), 0).astype(F32)
        for p in range(n_pairs):
            for hh in range(2):
                lf = lg_ref[0, 2 * p + hh]
                lb = lg_ref[1, 2 * p + hh]
                mask_ref[p, hh * c:(hh + 1) * c, :] = jnp.where(
                    diff >= 0, jnp.exp(lf * jnp.maximum(diff, 0.0)),
                    jnp.exp(lb * jnp.maximum(-diff, 0.0)))
            lfl = jnp.where(head0, lg_ref[0, 2 * p], lg_ref[0, 2 * p + 1])
            lbl = jnp.where(head0, lg_ref[1, 2 * p], lg_ref[1, 2 * p + 1])
            tab_ref[p, 0] = jnp.exp(lfl * (c - 1 - pos))
            tab_ref[p, 1] = jnp.exp(lbl * pos)
            tab_ref[p, 2] = jnp.exp(lfl * (pos + 1.0))
            tab_ref[p, 3] = jnp.exp(lbl * (c - pos))
            tab_ref[p, 4] = jnp.exp(lfl * c)
            tab_ref[p, 5] = jnp.exp(lbl * c)

    def summary(k, w, v):
        kw = (k.astype(F32) * w).astype(BF16)
        kv = lax.dot_general(kw, v, (((0,), (0,)), ((), ())), preferred_element_type=F32)
        return jnp.where(same_head, kv, 0.0)

    @pl.when(t < nc)
    def _backward():
        for p in range(n_pairs):
            sl = slice(p * LANES, (p + 1) * LANES)
            rnext_ref[nc - 1 - t, p] = rb_ref[p].astype(BF16)
            rb_ref[p] = (tab_ref[p, 5][:LANES] * rb_ref[p]
                         + summary(k_ref[:, sl], tab_ref[p, 1], v_ref[:, sl]))

    @pl.when(t >= nc)
    def _forward():
        @pl.when(t == nc)
        def _():
            rf_ref[...] = jnp.zeros_like(rf_ref)

        for p in range(n_pairs):
            sl = slice(p * LANES, (p + 1) * LANES)
            q = q_ref[:, sl]
            k = k_ref[:, sl]
            v = v_ref[:, sl]
            zero = jnp.zeros_like(q)
            q2 = jnp.concatenate([jnp.where(head0, q, zero), jnp.where(head0, zero, q)], axis=0)
            s = lax.dot_general(q2, k, (((1,), (1,)), ((), ())), preferred_element_type=F32)
            o2 = jnp.dot((s * mask_ref[p]).astype(BF16), v, preferred_element_type=F32)
            y = jnp.where(head0, o2[:c], o2[c:])
            y = y + jnp.dot(q, rf_ref[p].astype(BF16), preferred_element_type=F32) * tab_ref[p, 2]
            y = y + jnp.dot(q, rnext_ref[t - nc, p], preferred_element_type=F32) * tab_ref[p, 3]

            sq = y * y
            s0 = jnp.sum(jnp.where(head0, sq, 0.0), axis=1, keepdims=True)
            s1 = jnp.sum(jnp.where(head0, 0.0, sq), axis=1, keepdims=True)
            ms = jnp.where(head0, s0, s1) * (1.0 / HEAD_DIM)
            y = y * lax.rsqrt(ms + HEAD_NORM_EPS)
            g = g_ref[:, sl]
            o_ref[:, sl] = (g / (1.0 + jnp.exp(-g)) * y).astype(o_ref.dtype)

            rf_ref[p] = tab_ref[p, 4][:LANES] * rf_ref[p] + summary(k, tab_ref[p, 0], v)


def _retention(ret, gate, log_g, out, row0, batch, seq, chunk):
    nc = seq // chunk
    blk0 = row0 // chunk
    n_pairs = GROUP_WIDTH // LANES
    gw = GROUP_WIDTH

    def kc(t):
        return jnp.where(t < nc, nc - 1 - t, t - nc)

    def qc(t):
        return jnp.maximum(t - nc, 0)

    def rows(b, cc):
        return blk0 + b * nc + cc

    kernel = functools.partial(_ret_kernel, nc=nc, chunk=chunk)
    return pl.pallas_call(
        kernel,
        grid=(batch, 2 * nc),
        in_specs=[pl.BlockSpec(memory_space=pltpu.SMEM),
                  pl.BlockSpec((chunk, gw), lambda b, t: (rows(b, qc(t)), 0)),
                  pl.BlockSpec((chunk, gw), lambda b, t: (rows(b, kc(t)), 1)),
                  pl.BlockSpec((chunk, gw), lambda b, t: (rows(b, kc(t)), 2)),
                  pl.BlockSpec((chunk, gw), lambda b, t: (rows(b, qc(t)), 0)),
                  pl.BlockSpec(memory_space=pl.ANY)],
        out_specs=pl.BlockSpec((chunk, gw), lambda b, t: (rows(b, qc(t)), 0)),
        out_shape=jax.ShapeDtypeStruct(out.shape, out.dtype),
        input_output_aliases={5: 0},
        scratch_shapes=[pltpu.VMEM((n_pairs, LANES, LANES), F32),
                        pltpu.VMEM((n_pairs, LANES, LANES), F32),
                        pltpu.VMEM((nc, n_pairs, LANES, LANES), BF16),
                        pltpu.VMEM((n_pairs, 2 * chunk, chunk), F32),
                        pltpu.VMEM((n_pairs, 6, chunk, LANES), F32)],
        compiler_params=_cparams("parallel", "arbitrary"),
        name="retention",
    )(log_g, ret, ret, ret, gate, out)


def _bias_kernel(rb_ref, bucket_ref, o_ref):
    h = pl.program_id(0)
    bk = bucket_ref[...]
    out = jnp.zeros(bk.shape, F32)
    for n in range(N_BUCKETS):
        out = jnp.where(bk == n, rb_ref[n, h], out)
    o_ref[...] = out * LOG2E


def _bias_tiles(rel_bias, tile):
    assert tile > FAR_DISTANCE
    a = jnp.arange(tile, dtype=jnp.int32)[:, None]
    b = jnp.arange(tile, dtype=jnp.int32)[None, :]
    rel = jnp.stack([(d * tile + a - b) for d in range(-BIAS_REACH, BIAS_REACH + 1)])
    nb = N_BUCKETS // 2
    max_exact = nb // 2
    n = jnp.abs(rel)
    nf = jnp.maximum(n, 1).astype(F32)
    large = max_exact + (jnp.log(nf / max_exact) / math.log(MAX_DISTANCE / max_exact)
                         * (nb - max_exact)).astype(jnp.int32)
    large = jnp.minimum(large, nb - 1)
    bucket = jnp.where(rel > 0, nb, 0) + jnp.where(n < max_exact, n, large)
    return pl.pallas_call(
        _bias_kernel,
        grid=(N_HEADS, 2 * BIAS_REACH + 1),
        in_specs=[pl.BlockSpec(memory_space=pltpu.SMEM),
                  pl.BlockSpec((None, tile, tile), lambda h, d: (d, 0, 0))],
        out_specs=pl.BlockSpec((None, None, tile, tile), lambda h, d: (h, d, 0, 0)),
        out_shape=jax.ShapeDtypeStruct((N_HEADS, 2 * BIAS_REACH + 1, tile, tile), F32),
        compiler_params=_cparams("parallel", "arbitrary"),
        name="t5_bias_tiles",
    )(rel_bias.astype(F32), bucket.astype(jnp.int32))


def _attn_kernel(rb_ref, lamp_ref, dng_ref, q_ref, k_ref, vt_ref, bt_ref, _, o_ref,
                 a_ref, p_ref, qa_ref, kmax_ref, m_ref, *, n_tiles, tile, unroll, lam_init):
    h = pl.program_id(1)
    qi = pl.program_id(2)
    hh = h % 2
    n = n_tiles
    dims_nt = (((1,), (1,)), ((), ()))
    half = DIFF_QK_DIM
    group_of_lane = lax.broadcasted_iota(jnp.int32, (LANES, LANES), 0) // half
    group_sum = (group_of_lane == lax.broadcasted_iota(jnp.int32, (LANES, LANES), 1)).astype(BF16)

    def half_norms(x):
        xf = x.astype(F32)
        return jnp.dot((xf * xf).astype(BF16), group_sum, preferred_element_type=F32)

    @pl.when(qi == 0)
    def _key_norms():
        def body(j, mx):
            kt = k_ref[pl.ds(pl.multiple_of(j * tile, tile), tile), :]
            return jnp.maximum(mx, jnp.max(half_norms(kt), axis=0, keepdims=True))
        kmax_ref[...] = lax.fori_loop(0, n, body, jnp.zeros((1, LANES), F32))

    q = q_ref[...]
    lane = lax.broadcasted_iota(jnp.int32, q.shape, 1)
    lane_row = lax.broadcasted_iota(jnp.int32, (1, LANES), 1)
    zero = jnp.zeros_like(q)
    qn = half_norms(q)
    side_bias = (rb_ref[N_BUCKETS // 2 - 1, h], 0.0, rb_ref[N_BUCKETS - 1, h])
    for t in range(2):
        g = 2 * hh + t
        lo_lane = g * half
        qz = jnp.where((lane >= lo_lane) & (lane < lo_lane + half), q, zero)
        qn2 = jnp.sum(jnp.where(lane == g, qn, 0.0), axis=1, keepdims=True)
        kn2 = jnp.max(jnp.where(lane_row == g, kmax_ref[...], 0.0), axis=1, keepdims=True)
        bound = jnp.sqrt(qn2 * kn2) + rb_ref[N_BUCKETS, h]
        for side in range(3):
            v = side_bias[side] - bound
            hi = v.astype(BF16).astype(F32)
            lo = (v - hi).astype(BF16).astype(F32)
            cols = jnp.where(lane == 0, hi, jnp.where(lane == 1, lo, 0.0))
            qa_ref[t, side] = jnp.concatenate([qz.astype(F32), cols], axis=1).T.astype(BF16)

    ones = jnp.ones((tile, LANES), BF16)

    def near(j):
        return jnp.abs(j - qi) < BIAS_REACH

    def probabilities(j, slot, with_bias):
        kt = k_ref[pl.ds(pl.multiple_of(j * tile, tile), tile), :]
        kaug = jnp.concatenate([kt, ones], axis=1)
        side = 1 if with_bias else jnp.where(j < qi, 0, 2)
        for t in range(2):
            s = jnp.dot(kaug, qa_ref[t, side], preferred_element_type=F32)
            if with_bias:
                s = s + bt_ref[jnp.clip(j - qi, -BIAS_REACH, BIAS_REACH) + BIAS_REACH]
            p_ref[slot, t] = jnp.exp2(s).astype(BF16)

    def accumulate(j, slot):
        vt = vt_ref[j]
        for t in range(2):
            a_ref[t] += jnp.dot(vt, p_ref[slot, t], preferred_element_type=F32)

    a_ref[...] = jnp.zeros_like(a_ref)

    lax.cond(near(0), lambda: probabilities(0, 0, True), lambda: probabilities(0, 0, False))

    def group(b, last):
        j = unroll * b

        def run(with_bias):
            for k in range(unroll):
                if not (last and k == unroll - 1):
                    probabilities(j + k + 1, (k + 1) % 2, with_bias)
                accumulate(j + k, k % 2)

        any_near = functools.reduce(jnp.logical_or, [near(j + k + 1) for k in range(unroll)])
        lax.cond(any_near, lambda: run(True), lambda: run(False))

    def group_body(b, carry):
        group(b, False)
        return carry

    lax.fori_loop(0, n // unroll - 1, group_body, 0)
    group(n // unroll - 1, True)

    sums = jnp.minimum(a_ref[0, HEAD_DIM:HEAD_DIM + 1, :], a_ref[1, HEAD_DIM:HEAD_DIM + 1, :])
    trusted = jnp.min(sums) >= UNDERFLOW_GUARD

    @pl.when(jnp.logical_not(trusted))
    def _exact_running_max():
        m_ref[...] = jnp.full(m_ref.shape, NEG_BIG, F32)
        a_ref[...] = jnp.zeros_like(a_ref)

        def body(j, carry):
            kt = k_ref[pl.ds(pl.multiple_of(j * tile, tile), tile), :]
            vt = vt_ref[j]
            bias = bt_ref[jnp.clip(j - qi, -BIAS_REACH, BIAS_REACH) + BIAS_REACH]
            for t in range(2):
                s = jnp.dot(kt, qa_ref[t, 1, :LANES, :], preferred_element_type=F32) + bias
                m_prev = m_ref[t]
                m_new = jnp.maximum(m_prev, jnp.max(s, axis=0, keepdims=True))
                pr = jnp.exp2(s - m_new).astype(BF16)
                a_ref[t] = (jnp.exp2(m_prev - m_new) * a_ref[t]
                            + jnp.dot(vt, pr, preferred_element_type=F32))
                m_ref[t] = m_new
            return carry

        lax.fori_loop(0, n, body, 0)

    lp = lamp_ref[...]
    lam = (jnp.exp(jnp.sum(lp[0:1] * lp[1:2], axis=1, keepdims=True))
           - jnp.exp(jnp.sum(lp[2:3] * lp[3:4], axis=1, keepdims=True)) + lam_init)
    a1 = a_ref[0]
    a2 = a_ref[1]
    o = (a1[:HEAD_DIM] / a1[HEAD_DIM:HEAD_DIM + 1]
         - lam * (a2[:HEAD_DIM] / a2[HEAD_DIM:HEAD_DIM + 1]))
    ms = jnp.mean(o * o, axis=0, keepdims=True)
    y = o * lax.rsqrt(ms + HEAD_NORM_EPS) * dng_ref[...] * (1.0 - lam_init)
    o_ref[...] = y.astype(o_ref.dtype)


def _diff_attention(dqk, vt3, bias_t, rel_log2, lamp, dng, out, row0, batch, seq, tile, lam_init):
    n = seq // tile
    unroll = 4 if n % 4 == 0 else 2
    assert n % unroll == 0
    n_pairs = GROUP_WIDTH // LANES
    qblk0 = row0 // tile
    sblk0 = row0 // seq
    n_bias = 2 * BIAS_REACH + 1
    kernel = functools.partial(_attn_kernel, n_tiles=n, tile=tile, unroll=unroll, lam_init=lam_init)
    return pl.pallas_call(
        kernel,
        grid=(batch, N_HEADS, n),
        in_specs=[pl.BlockSpec(memory_space=pltpu.SMEM),
                  pl.BlockSpec((4, DIFF_QK_DIM), lambda b, h, i: (0, 0)),
                  pl.BlockSpec((HEAD_DIM, 1), lambda b, h, i: (0, 0)),
                  pl.BlockSpec((tile, LANES), lambda b, h, i: (qblk0 + b * n + i, h // 2)),
                  pl.BlockSpec((seq, LANES), lambda b, h, i: (sblk0 + b, n_pairs + h // 2)),
                  pl.BlockSpec((n, V_ROWS, tile), lambda b, h, i: (sblk0 + b, h, 0)),
                  pl.BlockSpec((None, n_bias, tile, tile), lambda b, h, i: (h, 0, 0, 0)),
                  pl.BlockSpec(memory_space=pl.ANY)],
        out_specs=pl.BlockSpec((None, HEAD_DIM, tile), lambda b, h, i: (qblk0 + b * n + i, h, 0)),
        out_shape=jax.ShapeDtypeStruct(out.shape, out.dtype),
        input_output_aliases={7: 0},
        scratch_shapes=[pltpu.VMEM((2, V_ROWS, tile), F32),
                        pltpu.VMEM((2, 2, tile, tile), BF16),
                        pltpu.VMEM((2, 3, 2 * LANES, tile), BF16),
                        pltpu.VMEM((1, LANES), F32),
                        pltpu.VMEM((2, 1, tile), F32)],
        compiler_params=_cparams("parallel", "parallel", "arbitrary"),
        name="diff_attention",
    )(rel_log2, lamp, dng, dqk, dqk, vt3, bias_t, out)


def _layer_norm(y, g, b):
    mu = jnp.mean(y, axis=-1, keepdims=True)
    d = y - mu
    var = jnp.mean(d * d, axis=-1, keepdims=True)
    return d * lax.rsqrt(var + LN_EPS) * g + b


def _outproj_kernel(yr_ref, yd_ref, w_ref, x_ref, g_ref, b_ref, o_ref, ob_ref, *, alpha):
    mix = jnp.dot(yr_ref[...], w_ref[:GROUP_WIDTH, :], preferred_element_type=F32)
    mix = mix + lax.dot_general(yd_ref[...], w_ref[GROUP_WIDTH:, :], (((0,), (0,)), ((), ())),
                                preferred_element_type=F32)
    y = _layer_norm(alpha * x_ref[...] + mix, g_ref[...], b_ref[...])
    o_ref[...] = y
    ob_ref[...] = y.astype(BF16)


def _out_projection(yr, ydt, w_out, x, g, b, alpha, tm):
    t = x.shape[0]
    assert ydt.shape == (t // tm, GROUP_WIDTH, tm)
    kernel = functools.partial(_outproj_kernel, alpha=alpha)
    row = lambda i: (i, 0)
    fixed = lambda i: (0, 0)
    return pl.pallas_call(
        kernel,
        grid=(t // tm,),
        in_specs=[pl.BlockSpec((tm, GROUP_WIDTH), row),
                  pl.BlockSpec((None, GROUP_WIDTH, tm), lambda i: (i, 0, 0)),
                  pl.BlockSpec((2 * GROUP_WIDTH, D_MODEL), fixed),
                  pl.BlockSpec((tm, D_MODEL), row),
                  pl.BlockSpec((1, D_MODEL), fixed),
                  pl.BlockSpec((1, D_MODEL), fixed)],
        out_specs=[pl.BlockSpec((tm, D_MODEL), row), pl.BlockSpec((tm, D_MODEL), row)],
        out_shape=[jax.ShapeDtypeStruct((t, D_MODEL), F32), jax.ShapeDtypeStruct((t, D_MODEL), BF16)],
        compiler_params=_cparams("parallel"),
        name="out_proj_ln",
    )(yr, ydt, w_out, x, g, b)


def _ffn_kernel(x_ref, xb_ref, xp_ref, xn_ref, wa_ref, wv_ref, wd_ref, cp_ref, g_ref, b_ref,
                o_ref, ob_ref, acc_ref, *, alpha, tm, n_chunks, starts, ends):
    i = pl.program_id(0)
    t0 = i * tm
    is_start = functools.reduce(jnp.logical_or, [t0 == s for s in starts])
    is_end = functools.reduce(jnp.logical_or, [t0 + tm == e for e in ends])
    keep_prev = jnp.where(is_start, 0.0, 1.0)
    keep_next = jnp.where(is_end, 0.0, 1.0)

    xb = xb_ref[...]
    halo = jnp.concatenate([xp_ref[...], xn_ref[...]], axis=0)
    hrows = xp_ref.shape[0]
    acc_ref[...] = jnp.zeros_like(acc_ref)

    def chunk_body(c, carry):
        wa = wa_ref[c]
        a = jnp.dot(xb, wa, preferred_element_type=F32)
        val = jnp.dot(xb, wv_ref[c], preferred_element_type=F32)
        ah = jnp.dot(halo, wa, preferred_element_type=F32)
        prev_row = ah[hrows - 1:hrows] * keep_prev
        next_row = ah[hrows:hrows + 1] * keep_next
        row = lax.broadcasted_iota(jnp.int32, a.shape, 0)
        a_m1 = jnp.where(row == 0, prev_row, pltpu.roll(a, 1, 0))
        a_p1 = jnp.where(row == tm - 1, next_row, pltpu.roll(a, tm - 1, 0))
        cp = cp_ref[c]
        conv = cp[3:4] + a_m1 * cp[0:1]
        conv = conv + a * cp[1:2]
        conv = conv + a_p1 * cp[2:3]
        gelu = 0.5 * conv * (1.0 + lax.erf(conv * (1.0 / math.sqrt(2.0))))
        hidden = (gelu * val).astype(BF16)
        acc_ref[...] += jnp.dot(hidden, wd_ref[c], preferred_element_type=F32)
        return carry

    for c in range(n_chunks):
        chunk_body(c, 0)
    y = _layer_norm(alpha * x_ref[...] + acc_ref[...], g_ref[...], b_ref[...])
    o_ref[...] = y
    ob_ref[...] = y.astype(BF16)


def _conv_glu(x, xb, wa, wv, wd, cp, g, b, alpha, tm, groups):
    t = x.shape[0]
    n_chunks, _, ck = wa.shape
    hrows = 16
    starts = tuple(r0 + bi * s for (r0, nb, s) in groups for bi in range(nb))
    ends = tuple(r0 + (bi + 1) * s for (r0, nb, s) in groups for bi in range(nb))
    kernel = functools.partial(_ffn_kernel, alpha=alpha, tm=tm, n_chunks=n_chunks,
                               starts=starts, ends=ends)
    row = lambda i: (i, 0)
    fixed2 = lambda i: (0, 0)
    fixed3 = lambda i: (0, 0, 0)
    per = tm // hrows
    last = t // hrows - 1
    return pl.pallas_call(
        kernel,
        grid=(t // tm,),
        in_specs=[pl.BlockSpec((tm, D_MODEL), row),
                  pl.BlockSpec((tm, D_MODEL), row),
                  pl.BlockSpec((hrows, D_MODEL), lambda i: (jnp.maximum(i * per - 1, 0), 0)),
                  pl.BlockSpec((hrows, D_MODEL), lambda i: (jnp.minimum((i + 1) * per, last), 0)),
                  pl.BlockSpec((n_chunks, D_MODEL, ck), fixed3, pipeline_mode=pl.Buffered(1)),
                  pl.BlockSpec((n_chunks, D_MODEL, ck), fixed3, pipeline_mode=pl.Buffered(1)),
                  pl.BlockSpec((n_chunks, ck, D_MODEL), fixed3, pipeline_mode=pl.Buffered(1)),
                  pl.BlockSpec((n_chunks, 8, ck), fixed3),
                  pl.BlockSpec((1, D_MODEL), fixed2),
                  pl.BlockSpec((1, D_MODEL), fixed2)],
        out_specs=[pl.BlockSpec((tm, D_MODEL), row), pl.BlockSpec((tm, D_MODEL), row)],
        out_shape=[jax.ShapeDtypeStruct((t, D_MODEL), F32), jax.ShapeDtypeStruct((t, D_MODEL), BF16)],
        scratch_shapes=[pltpu.VMEM((tm, D_MODEL), F32)],
        compiler_params=_cparams("parallel"),
        name="conv_glu_ln",
    )(x, xb, xb, xb, wa, wv, wd, cp, g, b)


def _rotary_tables(seq):
    d = HEAD_DIM
    inv = 1.0 / (ROPE_BASE ** (jnp.arange(0, d, 2, dtype=F32) / d))
    ang = jnp.arange(seq, dtype=F32)[:, None] * inv[None, :]
    cos, sin = jnp.cos(ang), jnp.sin(ang)
    cos_t = jnp.concatenate([cos, cos, cos, cos], axis=-1)
    sin_t = jnp.concatenate([-sin, sin, -sin, sin], axis=-1)
    return cos_t, sin_t


def _tiles(groups):
    smin = min(s for (_, _, s) in groups)
    attn_tile = min(512, smin // 2)
    chunk = min(256, smin // 2)
    tm = min(512, smin // 2)
    ffn_tm = min(1024, smin // 2)
    return attn_tile, chunk, tm, ffn_tm


def _forward(x, groups, w_in, ret_decay_logit, rel_bias, lambda_q1, lambda_k1, lambda_q2,
             lambda_k2, diff_norm_g, w_out, ln_g, ln_b, w_up, conv_w, conv_b, w_down):
    depth = w_in.shape[0]
    alpha = (2 * depth) ** 0.25
    t = x.shape[0]
    attn_tile, chunk, tm, ffn_tm = _tiles(groups)
    smax = max(s for (_, _, s) in groups)
    gw = GROUP_WIDTH
    ck = 256
    n_chunks = D_FF // ck

    cos_t, sin_t = _rotary_tables(smax)
    bias_t = _bias_tiles(rel_bias, attn_tile)
    rel_log2 = rel_bias.astype(F32) * LOG2E
    rel_log2 = jnp.concatenate([rel_log2, jnp.max(rel_log2, axis=0, keepdims=True)], axis=0)
    in_scale = jnp.concatenate([
        jnp.ones((gw,), F32), jnp.full((gw,), HEAD_DIM ** -0.5, F32), jnp.ones((2 * gw,), F32),
        jnp.full((gw,), DIFF_QK_DIM ** -0.5 * LOG2E, F32), jnp.ones((gw,), F32)])[None, :]
    assert tm == attn_tile

    xb = x.astype(BF16)
    for l in range(depth):
        lam_init = 0.8 - 0.6 * math.exp(-0.3 * l)
        w_in_b = w_in[l].astype(BF16)
        ret, gate, dqk, vt3 = _project(xb, w_in_b[:, :6 * gw], w_in_b[:, 6 * gw:].T, in_scale,
                                       cos_t, sin_t, groups, tm)

        log_g = jax.nn.log_sigmoid(ret_decay_logit[l].astype(F32))
        lamp = jnp.stack([lambda_q1[l], lambda_k1[l], lambda_q2[l], lambda_k2[l]]).astype(F32)
        dng = diff_norm_g[l].astype(F32)[:, None]

        yr = jnp.zeros((t, gw), BF16)
        ydt = jnp.zeros((t // attn_tile, gw, attn_tile), BF16)
        for (row0, batch, seq) in groups:
            yr = _retention(ret, gate, log_g, yr, row0, batch, seq, chunk)
            ydt = _diff_attention(dqk, vt3, bias_t, rel_log2, lamp, dng, ydt, row0, batch, seq,
                                  attn_tile, lam_init)

        x, xb = _out_projection(yr, ydt, w_out[l].astype(BF16), x, ln_g[l, 0][None, :].astype(F32),
                                ln_b[l, 0][None, :].astype(F32), alpha, tm)

        wa = w_up[l][:, :D_FF].astype(BF16).reshape(D_MODEL, n_chunks, ck).transpose(1, 0, 2)
        wv = w_up[l][:, D_FF:].astype(BF16).reshape(D_MODEL, n_chunks, ck).transpose(1, 0, 2)
        wd = w_down[l].astype(BF16).reshape(n_chunks, ck, D_MODEL)
        cp = jnp.concatenate([conv_w[l].astype(F32), conv_b[l].astype(F32)[None, :],
                              jnp.zeros((4, D_FF), F32)], axis=0)
        cp = cp.reshape(8, n_chunks, ck).transpose(1, 0, 2)
        x, xb = _conv_glu(x, xb, wa, wv, wd, cp, ln_g[l, 1][None, :].astype(F32),
                          ln_b[l, 1][None, :].astype(F32), alpha, ffn_tm, groups)
    return x


def kernel(x_prompt, x_sample, w_in, ret_decay_logit, rel_bias, lambda_q1, lambda_k1, lambda_q2,
           lambda_k2, diff_norm_g, w_out, ln_g, ln_b, w_up, conv_w, conv_b, w_down):
    bp, sp, d = x_prompt.shape
    bs, ss, _ = x_sample.shape
    groups = ((0, bp, sp), (bp * sp, bs, ss))
    x = jnp.concatenate([x_prompt.reshape(bp * sp, d), x_sample.reshape(bs * ss, d)], axis=0)
    y = _forward(x.astype(F32), groups, w_in, ret_decay_logit, rel_bias, lambda_q1, lambda_k1,
                 lambda_q2, lambda_k2, diff_norm_g, w_out, ln_g, ln_b, w_up, conv_w, conv_b, w_down)
    y_prompt = y[:bp * sp].reshape(bp, sp, d).astype(x_prompt.dtype)
    y_sample = y[bp * sp:].reshape(bs, ss, d).astype(x_sample.dtype)
    return y_prompt, y_sample
```

```python
import functools
import math

import jax
import jax.numpy as jnp
from jax import lax
from jax.experimental import pallas as pl
from jax.experimental.pallas import tpu as pltpu

D_MODEL = 1024
HEAD_DIM = 64
N_HEADS = 8
GROUP_WIDTH = N_HEADS * HEAD_DIM
DIFF_QK_DIM = HEAD_DIM // 2
D_FF = 2816
N_BUCKETS = 32
MAX_DISTANCE = 128
ROPE_BASE = 10000.0
LN_EPS = 1e-5
HEAD_NORM_EPS = 1e-6
LANES = 128
FAR_DISTANCE = 91
BIAS_REACH = 2
V_ROWS = HEAD_DIM + 16
LOG2E = math.log2(math.e)
NEG_BIG = -1e30
UNDERFLOW_GUARD = 2.0 ** -90
VMEM_LIMIT = 56 * 1024 * 1024

F32 = jnp.float32
BF16 = jnp.bfloat16


def _cparams(*sem):
    return pltpu.CompilerParams(dimension_semantics=sem, vmem_limit_bytes=VMEM_LIMIT)


def _proj_kernel(x_ref, w_ref, wvt_ref, s_ref, cos_ref, sin_ref, ret_ref, gate_ref, dqk_ref, vt_ref):
    gw = GROUP_WIDTH
    x = x_ref[...]
    tm = x.shape[0]
    lane = lax.broadcasted_iota(jnp.int32, (tm, LANES), 1)
    low_half = (lane % HEAD_DIM) < (HEAD_DIM // 2)
    cos = cos_ref[...]
    sin = sin_ref[...]
    qk = jnp.dot(x, w_ref[:, :2 * gw], preferred_element_type=F32) * s_ref[:, :2 * gw]
    for p in range(2 * gw // LANES):
        sl = slice(p * LANES, (p + 1) * LANES)
        xx = qk[:, sl]
        swapped = jnp.where(low_half, pltpu.roll(xx, LANES - HEAD_DIM // 2, 1),
                            pltpu.roll(xx, HEAD_DIM // 2, 1))
        ret_ref[:, sl] = (xx * cos + swapped * sin).astype(ret_ref.dtype)
    ret_ref[:, 2 * gw:] = jnp.dot(x, w_ref[:, 2 * gw:3 * gw],
                                  preferred_element_type=F32).astype(ret_ref.dtype)
    gate_ref[...] = jnp.dot(x, w_ref[:, 3 * gw:4 * gw], preferred_element_type=F32)
    dqk_ref[...] = (jnp.dot(x, w_ref[:, 4 * gw:], preferred_element_type=F32)
                    * s_ref[:, 4 * gw:]).astype(dqk_ref.dtype)
    vt = lax.dot_general(wvt_ref[...], x, (((1,), (1,)), ((), ())), preferred_element_type=F32)
    pad_rows = V_ROWS - HEAD_DIM
    pad = (lax.broadcasted_iota(jnp.int32, (pad_rows, tm), 0) == 0).astype(vt_ref.dtype)
    for h in range(N_HEADS):
        vt_ref[h * V_ROWS:h * V_ROWS + HEAD_DIM, :] = (
            vt[h * HEAD_DIM:(h + 1) * HEAD_DIM].astype(vt_ref.dtype))
        vt_ref[h * V_ROWS + HEAD_DIM:(h + 1) * V_ROWS, :] = pad


def _position_block(i, tm, groups):
    t0 = i * tm
    blk = t0 // tm
    for (row0, _, seq) in groups:
        blk = jnp.where(t0 >= row0, ((t0 - row0) % seq) // tm, blk)
    return blk


def _project(xb, w, w_vt, scale, cos_t, sin_t, groups, tm):
    t, k = xb.shape
    gw = GROUP_WIDTH
    row = lambda i: (i, 0)
    fixed = lambda i: (0, 0)
    pos = lambda i: (_position_block(i, tm, groups), 0)
    return pl.pallas_call(
        _proj_kernel,
        grid=(t // tm,),
        in_specs=[pl.BlockSpec((tm, k), row),
                  pl.BlockSpec((k, 6 * gw), fixed, pipeline_mode=pl.Buffered(1)),
                  pl.BlockSpec((gw, k), fixed, pipeline_mode=pl.Buffered(1)),
                  pl.BlockSpec((1, 6 * gw), fixed),
                  pl.BlockSpec((tm, LANES), pos),
                  pl.BlockSpec((tm, LANES), pos)],
        out_specs=[pl.BlockSpec((tm, 3 * gw), row), pl.BlockSpec((tm, gw), row),
                   pl.BlockSpec((tm, 2 * gw), row),
                   pl.BlockSpec((None, N_HEADS * V_ROWS, tm), lambda i: (i, 0, 0))],
        out_shape=[jax.ShapeDtypeStruct((t, 3 * gw), BF16), jax.ShapeDtypeStruct((t, gw), F32),
                   jax.ShapeDtypeStruct((t, 2 * gw), BF16),
                   jax.ShapeDtypeStruct((t // tm, N_HEADS * V_ROWS, tm), BF16)],
        compiler_params=_cparams("parallel"),
        name="in_proj",
    )(xb, w, w_vt, scale, cos_t, sin_t)


def _ret_kernel(lg_ref, q_ref, k_ref, v_ref, g_ref, _, o_ref,
                rf_ref, rb_ref, rnext_ref, mask_ref, tab_ref, *, nc, chunk):
    t = pl.program_id(1)
    c = chunk
    n_pairs = GROUP_WIDTH // LANES
    lane = lax.broadcasted_iota(jnp.int32, (c, LANES), 1)
    head0 = lane < HEAD_DIM
    r_i = lax.broadcasted_iota(jnp.int32, (LANES, LANES), 0) // HEAD_DIM
    c_i = lax.broadcasted_iota(jnp.int32, (LANES, LANES), 1) // HEAD_DIM
    same_head = r_i == c_i

    @pl.when(t == 0)
    def _init():
        rb_ref[...] = jnp.zeros_like(rb_ref)
        qi = lax.broadcasted_iota(jnp.int32, (c, c), 0)
        ki = lax.broadcasted_iota(jnp.int32, (c, c), 1)
        diff = (qi - ki).astype(F32)
        pos = lax.broadcasted_iota(jnp.int32, (c, LANES), 0).astype(F32)
        for p in range(n_pairs):
            for hh in range(2):
                lf = lg_ref[0, 2 * p + hh]
                lb = lg_ref[1, 2 * p + hh]
                mask_ref[p, hh * c:(hh + 1) * c, :] = jnp.where(
                    diff >= 0, jnp.exp(lf * jnp.maximum(diff, 0.0)),
                    jnp.exp(lb * jnp.maximum(-diff, 0.0)))
            lfl = jnp.where(head0, lg_ref[0, 2 * p], lg_ref[0, 2 * p + 1])
            lbl = jnp.where(head0, lg_ref[1, 2 * p], lg_ref[1, 2 * p + 1])
            tab_ref[p, 0] = jnp.exp(lfl * (c - 1 - pos))
            tab_ref[p, 1] = jnp.exp(lbl * pos)
            tab_ref[p, 2] = jnp.exp(lfl * (pos + 1.0))
            tab_ref[p, 3] = jnp.exp(lbl * (c - pos))
            tab_ref[p, 4] = jnp.exp(lfl * c)
            tab_ref[p, 5] = jnp.exp(lbl * c)

    def summary(k, w, v):
        kw = (k.astype(F32) * w).astype(BF16)
        kv = lax.dot_general(kw, v, (((0,), (0,)), ((), ())), preferred_element_type=F32)
        return jnp.where(same_head, kv, 0.0)

    @pl.when(t < nc)
    def _backward():
        for p in range(n_pairs):
            sl = slice(p * LANES, (p + 1) * LANES)
            rnext_ref[nc - 1 - t, p] = rb_ref[p].astype(BF16)
            rb_ref[p] = (tab_ref[p, 5][:LANES] * rb_ref[p]
                         + summary(k_ref[:, sl], tab_ref[p, 1], v_ref[:, sl]))

    @pl.when(t >= nc)
    def _forward():
        @pl.when(t == nc)
        def _():
            rf_ref[...] = jnp.zeros_like(rf_ref)

        for p in range(n_pairs):
            sl = slice(p * LANES, (p + 1) * LANES)
            q = q_ref[:, sl]
            k = k_ref[:, sl]
            v = v_ref[:, sl]
            zero = jnp.zeros_like(q)
            q2 = jnp.concatenate([jnp.where(head0, q, zero), jnp.where(head0, zero, q)], axis=0)
            s = lax.dot_general(q2, k, (((1,), (1,)), ((), ())), preferred_element_type=F32)
            o2 = jnp.dot((s * mask_ref[p]).astype(BF16), v, preferred_element_type=F32)
            y = jnp.where(head0, o2[:c], o2[c:])
            y = y + jnp.dot(q, rf_ref[p].astype(BF16), preferred_element_type=F32) * tab_ref[p, 2]
            y = y + jnp.dot(q, rnext_ref[t - nc, p], preferred_element_type=F32) * tab_ref[p, 3]

            sq = y * y
            s0 = jnp.sum(jnp.where(head0, sq, 0.0), axis=1, keepdims=True)
            s1 = jnp.sum(jnp.where(head0, 0.0, sq), axis=1, keepdims=True)
            ms = jnp.where(head0, s0, s1) * (1.0 / HEAD_DIM)
            y = y * lax.rsqrt(ms + HEAD_NORM_EPS)
            g = g_ref[:, sl]
            o_ref[:, sl] = (g / (1.0 + jnp.exp(-g)) * y).astype(o_ref.dtype)

            rf_ref[p] = tab_ref[p, 4][:LANES] * rf_ref[p] + summary(k, tab_ref[p, 0], v)


def _retention(ret, gate, log_g, out, row0, batch, seq, chunk):
    nc = seq // chunk
    blk0 = row0 // chunk
    n_pairs = GROUP_WIDTH // LANES
    gw = GROUP_WIDTH

    def kc(t):
        return jnp.where(t < nc, nc - 1 - t, t - nc)

    def qc(t):
        return jnp.maximum(t - nc, 0)

    def rows(b, cc):
        return blk0 + b * nc + cc

    kernel = functools.partial(_ret_kernel, nc=nc, chunk=chunk)
    return pl.pallas_call(
        kernel,
        grid=(batch, 2 * nc),
        in_specs=[pl.BlockSpec(memory_space=pltpu.SMEM),
                  pl.BlockSpec((chunk, gw), lambda b, t: (rows(b, qc(t)), 0)),
                  pl.BlockSpec((chunk, gw), lambda b, t: (rows(b, kc(t)), 1)),
                  pl.BlockSpec((chunk, gw), lambda b, t: (rows(b, kc(t)), 2)),
                  pl.BlockSpec((chunk, gw), lambda b, t: (rows(b, qc(t)), 0)),
                  pl.BlockSpec(memory_space=pl.ANY)],
        out_specs=pl.BlockSpec((chunk, gw), lambda b, t: (rows(b, qc(t)), 0)),
        out_shape=jax.ShapeDtypeStruct(out.shape, out.dtype),
        input_output_aliases={5: 0},
        scratch_shapes=[pltpu.VMEM((n_pairs, LANES, LANES), F32),
                        pltpu.VMEM((n_pairs, LANES, LANES), F32),
                        pltpu.VMEM((nc, n_pairs, LANES, LANES), BF16),
                        pltpu.VMEM((n_pairs, 2 * chunk, chunk), F32),
                        pltpu.VMEM((n_pairs, 6, chunk, LANES), F32)],
        compiler_params=_cparams("parallel", "arbitrary"),
        name="retention",
    )(log_g, ret, ret, ret, gate, out)


def _bias_kernel(rb_ref, bucket_ref, o_ref):
    h = pl.program_id(0)
    bk = bucket_ref[...]
    out = jnp.zeros(bk.shape, F32)
    for n in range(N_BUCKETS):
        out = jnp.where(bk == n, rb_ref[n, h], out)
    o_ref[...] = out * LOG2E


def _bias_tiles(rel_bias, tile):
    assert tile > FAR_DISTANCE
    a = jnp.arange(tile, dtype=jnp.int32)[:, None]
    b = jnp.arange(tile, dtype=jnp.int32)[None, :]
    rel = jnp.stack([(d * tile + a - b) for d in range(-BIAS_REACH, BIAS_REACH + 1)])
    nb = N_BUCKETS // 2
    max_exact = nb // 2
    n = jnp.abs(rel)
    nf = jnp.maximum(n, 1).astype(F32)
    large = max_exact + (jnp.log(nf / max_exact) / math.log(MAX_DISTANCE / max_exact)
                         * (nb - max_exact)).astype(jnp.int32)
    large = jnp.minimum(large, nb - 1)
    bucket = jnp.where(rel > 0, nb, 0) + jnp.where(n < max_exact, n, large)
    return pl.pallas_call(
        _bias_kernel,
        grid=(N_HEADS, 2 * BIAS_REACH + 1),
        in_specs=[pl.BlockSpec(memory_space=pltpu.SMEM),
                  pl.BlockSpec((None, tile, tile), lambda h, d: (d, 0, 0))],
        out_specs=pl.BlockSpec((None, None, tile, tile), lambda h, d: (h, d, 0, 0)),
        out_shape=jax.ShapeDtypeStruct((N_HEADS, 2 * BIAS_REACH + 1, tile, tile), F32),
        compiler_params=_cparams("parallel", "arbitrary"),
        name="t5_bias_tiles",
    )(rel_bias.astype(F32), bucket.astype(jnp.int32))


def _attn_kernel(rb_ref, lamp_ref, dng_ref, q_ref, k_ref, vt_ref, bt_ref, _, o_ref,
                 a_ref, p_ref, qa_ref, kmax_ref, m_ref, *, n_tiles, tile, unroll, lam_init):
    h = pl.program_id(1)
    qi = pl.program_id(2)
    hh = h % 2
    n = n_tiles
    half = DIFF_QK_DIM
    group_of_lane = lax.broadcasted_iota(jnp.int32, (LANES, LANES), 0) // half
    group_sum = (group_of_lane == lax.broadcasted_iota(jnp.int32, (LANES, LANES), 1)).astype(BF16)

    def half_norms(x):
        xf = x.astype(F32)
        return jnp.dot((xf * xf).astype(BF16), group_sum, preferred_element_type=F32)

    @pl.when(qi == 0)
    def _key_norms():
        def body(j, mx):
            kt = k_ref[pl.ds(pl.multiple_of(j * tile, tile), tile), :]
            return jnp.maximum(mx, jnp.max(half_norms(kt), axis=0, keepdims=True))
        kmax_ref[...] = lax.fori_loop(0, n, body, jnp.zeros((1, LANES), F32))

    qf = q_ref[...].astype(F32)
    lane = lax.broadcasted_iota(jnp.int32, qf.shape, 1)
    lane_row = lax.broadcasted_iota(jnp.int32, (1, LANES), 1)
    row = lax.broadcasted_iota(jnp.int32, (LANES, tile), 0)
    group_sum_t = (lax.broadcasted_iota(jnp.int32, (LANES, LANES), 0)
                   == lax.broadcasted_iota(jnp.int32, (LANES, LANES), 1) // half).astype(BF16)
    qn_rows = lax.dot_general(group_sum_t, (qf * qf).astype(BF16), (((1,), (1,)), ((), ())),
                              preferred_element_type=F32)
    side_bias = (rb_ref[N_BUCKETS // 2 - 1, h], 0.0, rb_ref[N_BUCKETS - 1, h])
    for t in range(2):
        g = 2 * hh + t
        lo_lane = g * half
        qzt = jnp.where((lane >= lo_lane) & (lane < lo_lane + half), qf, 0.0).T.astype(BF16)
        qn2 = jnp.sum(jnp.where(row == g, qn_rows, 0.0), axis=0, keepdims=True)
        kn2 = jnp.max(jnp.where(lane_row == g, kmax_ref[...], 0.0), axis=1, keepdims=True)
        bound = jnp.sqrt(qn2 * kn2) + rb_ref[N_BUCKETS, h]
        for side in range(3):
            v = side_bias[side] - bound
            hi = v.astype(BF16).astype(F32)
            lo = (v - hi).astype(BF16).astype(F32)
            qa_ref[t, side, :LANES, :] = qzt
            qa_ref[t, side, LANES:, :] = jnp.where(row == 0, hi, jnp.where(row == 1, lo, 0.0)).astype(BF16)

    ones = jnp.ones((tile, LANES), BF16)

    def near(j):
        return jnp.abs(j - qi) < BIAS_REACH

    def probabilities(j, slot, with_bias):
        kt = k_ref[pl.ds(pl.multiple_of(j * tile, tile), tile), :]
        kaug = jnp.concatenate([kt, ones], axis=1)
        side = 1 if with_bias else jnp.where(j < qi, 0, 2)
        for t in range(2):
            s = jnp.dot(kaug, qa_ref[t, side], preferred_element_type=F32)
            if with_bias:
                s = s + bt_ref[jnp.clip(j - qi, -BIAS_REACH, BIAS_REACH) + BIAS_REACH]
            p_ref[slot, t] = jnp.exp2(s).astype(BF16)

    def accumulate(j, slot):
        vt = vt_ref[j]
        for t in range(2):
            a_ref[t] += jnp.dot(vt, p_ref[slot, t], preferred_element_type=F32)

    a_ref[...] = jnp.zeros_like(a_ref)

    lax.cond(near(0), lambda: probabilities(0, 0, True), lambda: probabilities(0, 0, False))

    def group(b, last):
        j = unroll * b

        def run(with_bias):
            for k in range(unroll):
                if not (last and k == unroll - 1):
                    probabilities(j + k + 1, (k + 1) % 2, with_bias)
                accumulate(j + k, k % 2)

        any_near = functools.reduce(jnp.logical_or, [near(j + k + 1) for k in range(unroll)])
        lax.cond(any_near, lambda: run(True), lambda: run(False))

    def group_body(b, carry):
        group(b, False)
        return carry

    lax.fori_loop(0, n // unroll - 1, group_body, 0)
    group(n // unroll - 1, True)

    sums = jnp.minimum(a_ref[0, HEAD_DIM:HEAD_DIM + 1, :], a_ref[1, HEAD_DIM:HEAD_DIM + 1, :])
    trusted = jnp.min(sums) >= UNDERFLOW_GUARD

    @pl.when(jnp.logical_not(trusted))
    def _exact_running_max():
        m_ref[...] = jnp.full(m_ref.shape, NEG_BIG, F32)
        a_ref[...] = jnp.zeros_like(a_ref)

        def body(j, carry):
            kt = k_ref[pl.ds(pl.multiple_of(j * tile, tile), tile), :]
            vt = vt_ref[j]
            bias = bt_ref[jnp.clip(j - qi, -BIAS_REACH, BIAS_REACH) + BIAS_REACH]
            for t in range(2):
                s = jnp.dot(kt, qa_ref[t, 1, :LANES, :], preferred_element_type=F32) + bias
                m_prev = m_ref[t]
                m_new = jnp.maximum(m_prev, jnp.max(s, axis=0, keepdims=True))
                pr = jnp.exp2(s - m_new).astype(BF16)
                a_ref[t] = (jnp.exp2(m_prev - m_new) * a_ref[t]
                            + jnp.dot(vt, pr, preferred_element_type=F32))
                m_ref[t] = m_new
            return carry

        lax.fori_loop(0, n, body, 0)

    lp = lamp_ref[...]
    lam = (jnp.exp(jnp.sum(lp[0:1] * lp[1:2], axis=1, keepdims=True))
           - jnp.exp(jnp.sum(lp[2:3] * lp[3:4], axis=1, keepdims=True)) + lam_init)
    a1 = a_ref[0]
    a2 = a_ref[1]
    o = (a1[:HEAD_DIM] / a1[HEAD_DIM:HEAD_DIM + 1]
         - lam * (a2[:HEAD_DIM] / a2[HEAD_DIM:HEAD_DIM + 1]))
    ms = jnp.mean(o * o, axis=0, keepdims=True)
    y = o * lax.rsqrt(ms + HEAD_NORM_EPS) * dng_ref[...] * (1.0 - lam_init)
    o_ref[...] = y.astype(o_ref.dtype)


def _diff_attention(dqk, vt3, bias_t, rel_log2, lamp, dng, out, row0, batch, seq, tile, lam_init):
    n = seq // tile
    unroll = max(u for u in (2, 4, 8) if n % u == 0)
    n_pairs = GROUP_WIDTH // LANES
    qblk0 = row0 // tile
    sblk0 = row0 // seq
    n_bias = 2 * BIAS_REACH + 1
    kernel = functools.partial(_attn_kernel, n_tiles=n, tile=tile, unroll=unroll, lam_init=lam_init)
    return pl.pallas_call(
        kernel,
        grid=(batch, N_HEADS, n),
        in_specs=[pl.BlockSpec(memory_space=pltpu.SMEM),
                  pl.BlockSpec((4, DIFF_QK_DIM), lambda b, h, i: (0, 0)),
                  pl.BlockSpec((HEAD_DIM, 1), lambda b, h, i: (0, 0)),
                  pl.BlockSpec((tile, LANES), lambda b, h, i: (qblk0 + b * n + i, h // 2)),
                  pl.BlockSpec((seq, LANES), lambda b, h, i: (sblk0 + b, n_pairs + h // 2)),
                  pl.BlockSpec((n, V_ROWS, tile), lambda b, h, i: (sblk0 + b, h, 0)),
                  pl.BlockSpec((None, n_bias, tile, tile), lambda b, h, i: (h, 0, 0, 0)),
                  pl.BlockSpec(memory_space=pl.ANY)],
        out_specs=pl.BlockSpec((None, HEAD_DIM, tile), lambda b, h, i: (qblk0 + b * n + i, h, 0)),
        out_shape=jax.ShapeDtypeStruct(out.shape, out.dtype),
        input_output_aliases={7: 0},
        scratch_shapes=[pltpu.VMEM((2, V_ROWS, tile), F32),
                        pltpu.VMEM((2, 2, tile, tile), BF16),
                        pltpu.VMEM((2, 3, 2 * LANES, tile), BF16),
                        pltpu.VMEM((1, LANES), F32),
                        pltpu.VMEM((2, 1, tile), F32)],
        compiler_params=_cparams("parallel", "parallel", "arbitrary"),
        name="diff_attention",
    )(rel_log2, lamp, dng, dqk, dqk, vt3, bias_t, out)


def _layer_norm(y, g, b):
    mu = jnp.mean(y, axis=-1, keepdims=True)
    d = y - mu
    var = jnp.mean(d * d, axis=-1, keepdims=True)
    return d * lax.rsqrt(var + LN_EPS) * g + b


def _outproj_kernel(yr_ref, yd_ref, w_ref, x_ref, g_ref, b_ref, o_ref, ob_ref, *, alpha):
    mix = jnp.dot(yr_ref[...], w_ref[:GROUP_WIDTH, :], preferred_element_type=F32)
    mix = mix + lax.dot_general(yd_ref[...], w_ref[GROUP_WIDTH:, :], (((0,), (0,)), ((), ())),
                                preferred_element_type=F32)
    y = _layer_norm(alpha * x_ref[...] + mix, g_ref[...], b_ref[...])
    o_ref[...] = y
    ob_ref[...] = y.astype(BF16)


def _out_projection(yr, ydt, w_out, x, g, b, alpha, tm):
    t = x.shape[0]
    assert ydt.shape == (t // tm, GROUP_WIDTH, tm)
    kernel = functools.partial(_outproj_kernel, alpha=alpha)
    row = lambda i: (i, 0)
    fixed = lambda i: (0, 0)
    return pl.pallas_call(
        kernel,
        grid=(t // tm,),
        in_specs=[pl.BlockSpec((tm, GROUP_WIDTH), row),
                  pl.BlockSpec((None, GROUP_WIDTH, tm), lambda i: (i, 0, 0)),
                  pl.BlockSpec((2 * GROUP_WIDTH, D_MODEL), fixed),
                  pl.BlockSpec((tm, D_MODEL), row),
                  pl.BlockSpec((1, D_MODEL), fixed),
                  pl.BlockSpec((1, D_MODEL), fixed)],
        out_specs=[pl.BlockSpec((tm, D_MODEL), row), pl.BlockSpec((tm, D_MODEL), row)],
        out_shape=[jax.ShapeDtypeStruct((t, D_MODEL), F32), jax.ShapeDtypeStruct((t, D_MODEL), BF16)],
        compiler_params=_cparams("parallel"),
        name="out_proj_ln",
    )(yr, ydt, w_out, x, g, b)


def _ffn_kernel(x_ref, xb_ref, xp_ref, xn_ref, wa_ref, wv_ref, wd_ref, cp_ref, g_ref, b_ref,
                o_ref, ob_ref, acc_ref, *, alpha, tm, n_chunks, starts, ends):
    i = pl.program_id(0)
    t0 = i * tm
    is_start = functools.reduce(jnp.logical_or, [t0 == s for s in starts])
    is_end = functools.reduce(jnp.logical_or, [t0 + tm == e for e in ends])
    keep_prev = jnp.where(is_start, 0.0, 1.0)
    keep_next = jnp.where(is_end, 0.0, 1.0)

    xb = xb_ref[...]
    halo = jnp.concatenate([xp_ref[...], xn_ref[...]], axis=0)
    hrows = xp_ref.shape[0]
    acc_ref[...] = jnp.zeros_like(acc_ref)

    def chunk_body(c, carry):
        wa = wa_ref[c]
        a = jnp.dot(xb, wa, preferred_element_type=F32)
        val = jnp.dot(xb, wv_ref[c], preferred_element_type=F32)
        ah = jnp.dot(halo, wa, preferred_element_type=F32)
        prev_row = ah[hrows - 1:hrows] * keep_prev
        next_row = ah[hrows:hrows + 1] * keep_next
        row = lax.broadcasted_iota(jnp.int32, a.shape, 0)
        a_m1 = jnp.where(row == 0, prev_row, pltpu.roll(a, 1, 0))
        a_p1 = jnp.where(row == tm - 1, next_row, pltpu.roll(a, tm - 1, 0))
        cp = cp_ref[c]
        conv = cp[3:4] + a_m1 * cp[0:1]
        conv = conv + a * cp[1:2]
        conv = conv + a_p1 * cp[2:3]
        gelu = 0.5 * conv * (1.0 + lax.erf(conv * (1.0 / math.sqrt(2.0))))
        hidden = (gelu * val).astype(BF16)
        acc_ref[...] += jnp.dot(hidden, wd_ref[c], preferred_element_type=F32)
        return carry

    for c in range(n_chunks):
        chunk_body(c, 0)
    y = _layer_norm(alpha * x_ref[...] + acc_ref[...], g_ref[...], b_ref[...])
    o_ref[...] = y
    ob_ref[...] = y.astype(BF16)


def _conv_glu(x, xb, wa, wv, wd, cp, g, b, alpha, tm, groups):
    t = x.shape[0]
    n_chunks, _, ck = wa.shape
    hrows = 16
    starts = tuple(r0 + bi * s for (r0, nb, s) in groups for bi in range(nb))
    ends = tuple(r0 + (bi + 1) * s for (r0, nb, s) in groups for bi in range(nb))
    kernel = functools.partial(_ffn_kernel, alpha=alpha, tm=tm, n_chunks=n_chunks,
                               starts=starts, ends=ends)
    row = lambda i: (i, 0)
    fixed2 = lambda i: (0, 0)
    fixed3 = lambda i: (0, 0, 0)
    per = tm // hrows
    last = t // hrows - 1
    return pl.pallas_call(
        kernel,
        grid=(t // tm,),
        in_specs=[pl.BlockSpec((tm, D_MODEL), row),
                  pl.BlockSpec((tm, D_MODEL), row),
                  pl.BlockSpec((hrows, D_MODEL), lambda i: (jnp.maximum(i * per - 1, 0), 0)),
                  pl.BlockSpec((hrows, D_MODEL), lambda i: (jnp.minimum((i + 1) * per, last), 0)),
                  pl.BlockSpec((n_chunks, D_MODEL, ck), fixed3, pipeline_mode=pl.Buffered(1)),
                  pl.BlockSpec((n_chunks, D_MODEL, ck), fixed3, pipeline_mode=pl.Buffered(1)),
                  pl.BlockSpec((n_chunks, ck, D_MODEL), fixed3, pipeline_mode=pl.Buffered(1)),
                  pl.BlockSpec((n_chunks, 8, ck), fixed3),
                  pl.BlockSpec((1, D_MODEL), fixed2),
                  pl.BlockSpec((1, D_MODEL), fixed2)],
        out_specs=[pl.BlockSpec((tm, D_MODEL), row), pl.BlockSpec((tm, D_MODEL), row)],
        out_shape=[jax.ShapeDtypeStruct((t, D_MODEL), F32), jax.ShapeDtypeStruct((t, D_MODEL), BF16)],
        scratch_shapes=[pltpu.VMEM((tm, D_MODEL), F32)],
        compiler_params=_cparams("parallel"),
        name="conv_glu_ln",
    )(x, xb, xb, xb, wa, wv, wd, cp, g, b)


def _rotary_tables(seq):
    d = HEAD_DIM
    inv = 1.0 / (ROPE_BASE ** (jnp.arange(0, d, 2, dtype=F32) / d))
    ang = jnp.arange(seq, dtype=F32)[:, None] * inv[None, :]
    cos, sin = jnp.cos(ang), jnp.sin(ang)
    cos_t = jnp.concatenate([cos, cos, cos, cos], axis=-1)
    sin_t = jnp.concatenate([-sin, sin, -sin, sin], axis=-1)
    return cos_t, sin_t


def _tiles(groups):
    smin = min(s for (_, _, s) in groups)
    attn_tile = min(512, smin // 2)
    chunk = min(256, smin // 2)
    tm = min(512, smin // 2)
    ffn_tm = min(1024, smin // 2)
    return attn_tile, chunk, tm, ffn_tm


def _forward(x, groups, w_in, ret_decay_logit, rel_bias, lambda_q1, lambda_k1, lambda_q2,
             lambda_k2, diff_norm_g, w_out, ln_g, ln_b, w_up, conv_w, conv_b, w_down):
    depth = w_in.shape[0]
    alpha = (2 * depth) ** 0.25
    t = x.shape[0]
    attn_tile, chunk, tm, ffn_tm = _tiles(groups)
    smax = max(s for (_, _, s) in groups)
    gw = GROUP_WIDTH
    ck = 256
    n_chunks = D_FF // ck

    cos_t, sin_t = _rotary_tables(smax)
    bias_t = _bias_tiles(rel_bias, attn_tile)
    rel_log2 = rel_bias.astype(F32) * LOG2E
    rel_log2 = jnp.concatenate([rel_log2, jnp.max(rel_log2, axis=0, keepdims=True)], axis=0)
    in_scale = jnp.concatenate([
        jnp.ones((gw,), F32), jnp.full((gw,), HEAD_DIM ** -0.5, F32), jnp.ones((2 * gw,), F32),
        jnp.full((gw,), DIFF_QK_DIM ** -0.5 * LOG2E, F32), jnp.ones((gw,), F32)])[None, :]
    assert tm == attn_tile

    xb = x.astype(BF16)
    for l in range(depth):
        lam_init = 0.8 - 0.6 * math.exp(-0.3 * l)
        w_in_b = w_in[l].astype(BF16)
        ret, gate, dqk, vt3 = _project(xb, w_in_b[:, :6 * gw], w_in_b[:, 6 * gw:].T, in_scale,
                                       cos_t, sin_t, groups, tm)

        log_g = jax.nn.log_sigmoid(ret_decay_logit[l].astype(F32))
        lamp = jnp.stack([lambda_q1[l], lambda_k1[l], lambda_q2[l], lambda_k2[l]]).astype(F32)
        dng = diff_norm_g[l].astype(F32)[:, None]

        yr = jnp.zeros((t, gw), BF16)
        ydt = jnp.zeros((t // attn_tile, gw, attn_tile), BF16)
        for (row0, batch, seq) in groups:
            yr = _retention(ret, gate, log_g, yr, row0, batch, seq, chunk)
            ydt = _diff_attention(dqk, vt3, bias_t, rel_log2, lamp, dng, ydt, row0, batch, seq,
                                  attn_tile, lam_init)

        x, xb = _out_projection(yr, ydt, w_out[l].astype(BF16), x, ln_g[l, 0][None, :].astype(F32),
                                ln_b[l, 0][None, :].astype(F32), alpha, tm)

        wa = w_up[l][:, :D_FF].astype(BF16).reshape(D_MODEL, n_chunks, ck).transpose(1, 0, 2)
        wv = w_up[l][:, D_FF:].astype(BF16).reshape(D_MODEL, n_chunks, ck).transpose(1, 0, 2)
        wd = w_down[l].astype(BF16).reshape(n_chunks, ck, D_MODEL)
        cp = jnp.concatenate([conv_w[l].astype(F32), conv_b[l].astype(F32)[None, :],
                              jnp.zeros((4, D_FF), F32)], axis=0)
        cp = cp.reshape(8, n_chunks, ck).transpose(1, 0, 2)
        x, xb = _conv_glu(x, xb, wa, wv, wd, cp, ln_g[l, 1][None, :].astype(F32),
                          ln_b[l, 1][None, :].astype(F32), alpha, ffn_tm, groups)
    return x


def kernel(x_prompt, x_sample, w_in, ret_decay_logit, rel_bias, lambda_q1, lambda_k1, lambda_q2,
           lambda_k2, diff_norm_g, w_out, ln_g, ln_b, w_up, conv_w, conv_b, w_down):
    bp, sp, d = x_prompt.shape
    bs, ss, _ = x_sample.shape
    groups = ((0, bp, sp), (bp * sp, bs, ss))
    x = jnp.concatenate([x_prompt.reshape(bp * sp, d), x_sample.reshape(bs * ss, d)], axis=0)
    y = _forward(x.astype(F32), groups, w_in, ret_decay_logit, rel_bias, lambda_q1, lambda_k1,
                 lambda_q2, lambda_k2, diff_norm_g, w_out, ln_g, ln_b, w_up, conv_w, conv_b, w_down)
    y_prompt = y[:bp * sp].reshape(bp, sp, d).astype(x_prompt.dtype)
    y_sample = y[bp * sp:].reshape(bs, ss, d).astype(x_sample.dtype)
    return y_prompt, y_sample
```

```python
import functools
import math

import jax
import jax.numpy as jnp
from jax import lax
from jax.experimental import pallas as pl
from jax.experimental.pallas import tpu as pltpu

D_MODEL = 1024
HEAD_DIM = 64
N_HEADS = 8
GROUP_WIDTH = N_HEADS * HEAD_DIM
DIFF_QK_DIM = HEAD_DIM // 2
D_FF = 2816
N_BUCKETS = 32
MAX_DISTANCE = 128
ROPE_BASE = 10000.0
LN_EPS = 1e-5
HEAD_NORM_EPS = 1e-6
LANES = 128
FAR_DISTANCE = 91
BIAS_REACH = 2
V_ROWS = HEAD_DIM + 16
LOG2E = math.log2(math.e)
NEG_BIG = -1e30
UNDERFLOW_GUARD = 2.0 ** -90
VMEM_LIMIT = 56 * 1024 * 1024

F32 = jnp.float32
BF16 = jnp.bfloat16


def _cparams(*sem):
    return pltpu.CompilerParams(dimension_semantics=sem, vmem_limit_bytes=VMEM_LIMIT)


def _proj_kernel(x_ref, w_ref, wvt_ref, s_ref, cos_ref, sin_ref, ret_ref, gate_ref, dqk_ref, vt_ref):
    gw = GROUP_WIDTH
    x = x_ref[...]
    tm = x.shape[0]
    lane = lax.broadcasted_iota(jnp.int32, (tm, LANES), 1)
    low_half = (lane % HEAD_DIM) < (HEAD_DIM // 2)
    cos = cos_ref[...]
    sin = sin_ref[...]
    qk = jnp.dot(x, w_ref[:, :2 * gw], preferred_element_type=F32) * s_ref[:, :2 * gw]
    for p in range(2 * gw // LANES):
        sl = slice(p * LANES, (p + 1) * LANES)
        xx = qk[:, sl]
        swapped = jnp.where(low_half, pltpu.roll(xx, LANES - HEAD_DIM // 2, 1),
                            pltpu.roll(xx, HEAD_DIM // 2, 1))
        ret_ref[:, sl] = (xx * cos + swapped * sin).astype(ret_ref.dtype)
    ret_ref[:, 2 * gw:] = jnp.dot(x, w_ref[:, 2 * gw:3 * gw],
                                  preferred_element_type=F32).astype(ret_ref.dtype)
    gate_ref[...] = jnp.dot(x, w_ref[:, 3 * gw:4 * gw], preferred_element_type=F32)
    dqk_ref[...] = (jnp.dot(x, w_ref[:, 4 * gw:], preferred_element_type=F32)
                    * s_ref[:, 4 * gw:]).astype(dqk_ref.dtype)
    vt = lax.dot_general(wvt_ref[...], x, (((1,), (1,)), ((), ())), preferred_element_type=F32)
    pad_rows = V_ROWS - HEAD_DIM
    pad = (lax.broadcasted_iota(jnp.int32, (pad_rows, tm), 0) == 0).astype(vt_ref.dtype)
    for h in range(N_HEADS):
        vt_ref[h * V_ROWS:h * V_ROWS + HEAD_DIM, :] = (
            vt[h * HEAD_DIM:(h + 1) * HEAD_DIM].astype(vt_ref.dtype))
        vt_ref[h * V_ROWS + HEAD_DIM:(h + 1) * V_ROWS, :] = pad


def _position_block(i, tm, groups):
    t0 = i * tm
    blk = t0 // tm
    for (row0, _, seq) in groups:
        blk = jnp.where(t0 >= row0, ((t0 - row0) % seq) // tm, blk)
    return blk


def _project(xb, w, w_vt, scale, cos_t, sin_t, groups, tm):
    t, k = xb.shape
    gw = GROUP_WIDTH
    row = lambda i: (i, 0)
    fixed = lambda i: (0, 0)
    pos = lambda i: (_position_block(i, tm, groups), 0)
    return pl.pallas_call(
        _proj_kernel,
        grid=(t // tm,),
        in_specs=[pl.BlockSpec((tm, k), row),
                  pl.BlockSpec((k, 6 * gw), fixed, pipeline_mode=pl.Buffered(1)),
                  pl.BlockSpec((gw, k), fixed, pipeline_mode=pl.Buffered(1)),
                  pl.BlockSpec((1, 6 * gw), fixed),
                  pl.BlockSpec((tm, LANES), pos),
                  pl.BlockSpec((tm, LANES), pos)],
        out_specs=[pl.BlockSpec((tm, 3 * gw), row), pl.BlockSpec((tm, gw), row),
                   pl.BlockSpec((tm, 2 * gw), row),
                   pl.BlockSpec((None, N_HEADS * V_ROWS, tm), lambda i: (i, 0, 0))],
        out_shape=[jax.ShapeDtypeStruct((t, 3 * gw), BF16), jax.ShapeDtypeStruct((t, gw), F32),
                   jax.ShapeDtypeStruct((t, 2 * gw), BF16),
                   jax.ShapeDtypeStruct((t // tm, N_HEADS * V_ROWS, tm), BF16)],
        compiler_params=_cparams("parallel"),
        name="in_proj",
    )(xb, w, w_vt, scale, cos_t, sin_t)


def _ret_kernel(lg_ref, q_ref, k_ref, v_ref, g_ref, _, o_ref,
                rf_ref, rb_ref, rnext_ref, mask_ref, tab_ref, *, nc, chunk):
    t = pl.program_id(1)
    c = chunk
    n_pairs = GROUP_WIDTH // LANES
    lane = lax.broadcasted_iota(jnp.int32, (c, LANES), 1)
    head0 = lane < HEAD_DIM
    r_i = lax.broadcasted_iota(jnp.int32, (LANES, LANES), 0) // HEAD_DIM
    c_i = lax.broadcasted_iota(jnp.int32, (LANES, LANES), 1) // HEAD_DIM
    same_head = r_i == c_i

    @pl.when(t == 0)
    def _init():
        rb_ref[...] = jnp.zeros_like(rb_ref)
        qi = lax.broadcasted_iota(jnp.int32, (c, c), 0)
        ki = lax.broadcasted_iota(jnp.int32, (c, c), 1)
        diff = (qi - ki).astype(F32)
        pos = lax.broadcasted_iota(jnp.int32, (c, LANES), 0).astype(F32)
        for p in range(n_pairs):
            for hh in range(2):
                lf = lg_ref[0, 2 * p + hh]
                lb = lg_ref[1, 2 * p + hh]
                mask_ref[p, hh * c:(hh + 1) * c, :] = jnp.where(
                    diff >= 0, jnp.exp(lf * jnp.maximum(diff, 0.0)),
                    jnp.exp(lb * jnp.maximum(-diff, 0.0)))
            lfl = jnp.where(head0, lg_ref[0, 2 * p], lg_ref[0, 2 * p + 1])
            lbl = jnp.where(head0, lg_ref[1, 2 * p], lg_ref[1, 2 * p + 1])
            tab_ref[p, 0] = jnp.exp(lfl * (c - 1 - pos))
            tab_ref[p, 1] = jnp.exp(lbl * pos)
            tab_ref[p, 2] = jnp.exp(lfl * (pos + 1.0))
            tab_ref[p, 3] = jnp.exp(lbl * (c - pos))
            tab_ref[p, 4] = jnp.exp(lfl * c)
            tab_ref[p, 5] = jnp.exp(lbl * c)

    def summary(k, w, v):
        kw = (k.astype(F32) * w).astype(BF16)
        kv = lax.dot_general(kw, v, (((0,), (0,)), ((), ())), preferred_element_type=F32)
        return jnp.where(same_head, kv, 0.0)

    @pl.when(t < nc)
    def _backward():
        for p in range(n_pairs):
            sl = slice(p * LANES, (p + 1) * LANES)
            rnext_ref[nc - 1 - t, p] = rb_ref[p].astype(BF16)
            rb_ref[p] = (tab_ref[p, 5][:LANES] * rb_ref[p]
                         + summary(k_ref[:, sl], tab_ref[p, 1], v_ref[:, sl]))

    @pl.when(t >= nc)
    def _forward():
        @pl.when(t == nc)
        def _():
            rf_ref[...] = jnp.zeros_like(rf_ref)

        for p in range(n_pairs):
            sl = slice(p * LANES, (p + 1) * LANES)
            q = q_ref[:, sl]
            k = k_ref[:, sl]
            v = v_ref[:, sl]
            zero = jnp.zeros_like(q)
            q2 = jnp.concatenate([jnp.where(head0, q, zero), jnp.where(head0, zero, q)], axis=0)
            s = lax.dot_general(q2, k, (((1,), (1,)), ((), ())), preferred_element_type=F32)
            o2 = jnp.dot((s * mask_ref[p]).astype(BF16), v, preferred_element_type=F32)
            y = jnp.where(head0, o2[:c], o2[c:])
            y = y + jnp.dot(q, rf_ref[p].astype(BF16), preferred_element_type=F32) * tab_ref[p, 2]
            y = y + jnp.dot(q, rnext_ref[t - nc, p], preferred_element_type=F32) * tab_ref[p, 3]

            sq = y * y
            s0 = jnp.sum(jnp.where(head0, sq, 0.0), axis=1, keepdims=True)
            s1 = jnp.sum(jnp.where(head0, 0.0, sq), axis=1, keepdims=True)
            ms = jnp.where(head0, s0, s1) * (1.0 / HEAD_DIM)
            y = y * lax.rsqrt(ms + HEAD_NORM_EPS)
            g = g_ref[:, sl]
            o_ref[:, sl] = (g / (1.0 + jnp.exp(-g)) * y).astype(o_ref.dtype)

            rf_ref[p] = tab_ref[p, 4][:LANES] * rf_ref[p] + summary(k, tab_ref[p, 0], v)


def _retention(ret, gate, log_g, out, row0, batch, seq, chunk):
    nc = seq // chunk
    blk0 = row0 // chunk
    n_pairs = GROUP_WIDTH // LANES
    gw = GROUP_WIDTH

    def kc(t):
        return jnp.where(t < nc, nc - 1 - t, t - nc)

    def qc(t):
        return jnp.maximum(t - nc, 0)

    def rows(b, cc):
        return blk0 + b * nc + cc

    kernel = functools.partial(_ret_kernel, nc=nc, chunk=chunk)
    return pl.pallas_call(
        kernel,
        grid=(batch, 2 * nc),
        in_specs=[pl.BlockSpec(memory_space=pltpu.SMEM),
                  pl.BlockSpec((chunk, gw), lambda b, t: (rows(b, qc(t)), 0)),
                  pl.BlockSpec((chunk, gw), lambda b, t: (rows(b, kc(t)), 1)),
                  pl.BlockSpec((chunk, gw), lambda b, t: (rows(b, kc(t)), 2)),
                  pl.BlockSpec((chunk, gw), lambda b, t: (rows(b, qc(t)), 0)),
                  pl.BlockSpec(memory_space=pl.ANY)],
        out_specs=pl.BlockSpec((chunk, gw), lambda b, t: (rows(b, qc(t)), 0)),
        out_shape=jax.ShapeDtypeStruct(out.shape, out.dtype),
        input_output_aliases={5: 0},
        scratch_shapes=[pltpu.VMEM((n_pairs, LANES, LANES), F32),
                        pltpu.VMEM((n_pairs, LANES, LANES), F32),
                        pltpu.VMEM((nc, n_pairs, LANES, LANES), BF16),
                        pltpu.VMEM((n_pairs, 2 * chunk, chunk), F32),
                        pltpu.VMEM((n_pairs, 6, chunk, LANES), F32)],
        compiler_params=_cparams("parallel", "arbitrary"),
        name="retention",
    )(log_g, ret, ret, ret, gate, out)


def _bias_kernel(rb_ref, bucket_ref, o_ref):
    h = pl.program_id(0)
    bk = bucket_ref[...]
    out = jnp.zeros(bk.shape, F32)
    for n in range(N_BUCKETS):
        out = jnp.where(bk == n, rb_ref[n, h], out)
    o_ref[...] = out * LOG2E


def _bias_tiles(rel_bias, tile):
    assert tile > FAR_DISTANCE
    a = jnp.arange(tile, dtype=jnp.int32)[:, None]
    b = jnp.arange(tile, dtype=jnp.int32)[None, :]
    rel = jnp.stack([(d * tile + a - b) for d in range(-BIAS_REACH, BIAS_REACH + 1)])
    nb = N_BUCKETS // 2
    max_exact = nb // 2
    n = jnp.abs(rel)
    nf = jnp.maximum(n, 1).astype(F32)
    large = max_exact + (jnp.log(nf / max_exact) / math.log(MAX_DISTANCE / max_exact)
                         * (nb - max_exact)).astype(jnp.int32)
    large = jnp.minimum(large, nb - 1)
    bucket = jnp.where(rel > 0, nb, 0) + jnp.where(n < max_exact, n, large)
    return pl.pallas_call(
        _bias_kernel,
        grid=(N_HEADS, 2 * BIAS_REACH + 1),
        in_specs=[pl.BlockSpec(memory_space=pltpu.SMEM),
                  pl.BlockSpec((None, tile, tile), lambda h, d: (d, 0, 0))],
        out_specs=pl.BlockSpec((None, None, tile, tile), lambda h, d: (h, d, 0, 0)),
        out_shape=jax.ShapeDtypeStruct((N_HEADS, 2 * BIAS_REACH + 1, tile, tile), F32),
        compiler_params=_cparams("parallel", "arbitrary"),
        name="t5_bias_tiles",
    )(rel_bias.astype(F32), bucket.astype(jnp.int32))


def _attn_kernel(rb_ref, lamp_ref, dng_ref, q_ref, k_ref, vt_ref, bt_ref, _, o_ref,
                 a_ref, p_ref, qa_ref, kmax_ref, m_ref, *, n_tiles, tile, macro, unroll, lam_init):
    h = pl.program_id(1)
    qi = pl.program_id(2)
    hh = h % 2
    n = n_tiles
    half = DIFF_QK_DIM
    group_of_lane = lax.broadcasted_iota(jnp.int32, (LANES, LANES), 0) // half
    group_sum = (group_of_lane == lax.broadcasted_iota(jnp.int32, (LANES, LANES), 1)).astype(BF16)

    def half_norms(x):
        xf = x.astype(F32)
        return jnp.dot((xf * xf).astype(BF16), group_sum, preferred_element_type=F32)

    @pl.when(qi == 0)
    def _key_norms():
        def body(j, mx):
            kt = k_ref[pl.ds(pl.multiple_of(j * tile, tile), tile), :]
            return jnp.maximum(mx, jnp.max(half_norms(kt), axis=0, keepdims=True))
        kmax_ref[...] = lax.fori_loop(0, n, body, jnp.zeros((1, LANES), F32))

    qf = q_ref[...].astype(F32)
    lane = lax.broadcasted_iota(jnp.int32, qf.shape, 1)
    lane_row = lax.broadcasted_iota(jnp.int32, (1, LANES), 1)
    row = lax.broadcasted_iota(jnp.int32, (LANES, tile), 0)
    group_sum_t = (lax.broadcasted_iota(jnp.int32, (LANES, LANES), 0)
                   == lax.broadcasted_iota(jnp.int32, (LANES, LANES), 1) // half).astype(BF16)
    qn_rows = lax.dot_general(group_sum_t, (qf * qf).astype(BF16), (((1,), (1,)), ((), ())),
                              preferred_element_type=F32)
    side_bias = (rb_ref[N_BUCKETS // 2 - 1, h], 0.0, rb_ref[N_BUCKETS - 1, h])
    for t in range(2):
        g = 2 * hh + t
        lo_lane = g * half
        qzt = jnp.where((lane >= lo_lane) & (lane < lo_lane + half), qf, 0.0).T.astype(BF16)
        qn2 = jnp.sum(jnp.where(row == g, qn_rows, 0.0), axis=0, keepdims=True)
        kn2 = jnp.max(jnp.where(lane_row == g, kmax_ref[...], 0.0), axis=1, keepdims=True)
        bound = jnp.sqrt(qn2 * kn2) + rb_ref[N_BUCKETS, h]
        for side in range(3):
            v = side_bias[side] - bound
            hi = v.astype(BF16).astype(F32)
            lo = (v - hi).astype(BF16).astype(F32)
            qa_ref[t, side, :LANES, :] = qzt
            qa_ref[t, side, LANES:, :] = jnp.where(row == 0, hi, jnp.where(row == 1, lo, 0.0)).astype(BF16)

    ones = jnp.ones((macro * tile, LANES), BF16)

    def near_any(m):
        j0 = m * macro
        return (qi >= j0 - (BIAS_REACH - 1)) & (qi <= j0 + macro - 1 + (BIAS_REACH - 1))

    def probabilities(m, slot, with_bias):
        j0 = m * macro
        kt = k_ref[pl.ds(pl.multiple_of(j0 * tile, macro * tile), macro * tile), :]
        kaug = jnp.concatenate([kt, ones], axis=1)
        side = 1 if with_bias else jnp.where(j0 < qi, 0, 2)
        for t in range(2):
            s = jnp.dot(kaug, qa_ref[t, side], preferred_element_type=F32)
            for r in range(macro):
                sr = s[r * tile:(r + 1) * tile]
                if with_bias:
                    sr = sr + bt_ref[jnp.clip(j0 + r - qi, -BIAS_REACH, BIAS_REACH) + BIAS_REACH]
                p_ref[slot, t, r * tile:(r + 1) * tile, :] = jnp.exp2(sr).astype(BF16)

    def accumulate(m, slot):
        vt = jnp.concatenate([vt_ref[m * macro + r] for r in range(macro)], axis=1)
        for t in range(2):
            a_ref[t] += jnp.dot(vt, p_ref[slot, t], preferred_element_type=F32)

    a_ref[...] = jnp.zeros_like(a_ref)

    lax.cond(near_any(0), lambda: probabilities(0, 0, True), lambda: probabilities(0, 0, False))

    def group(b, last):
        m0 = unroll * b

        def run(with_bias):
            for k in range(unroll):
                if not (last and k == unroll - 1):
                    probabilities(m0 + k + 1, (k + 1) % 2, with_bias)
                accumulate(m0 + k, k % 2)

        any_near = functools.reduce(jnp.logical_or, [near_any(m0 + k + 1) for k in range(unroll)])
        lax.cond(any_near, lambda: run(True), lambda: run(False))

    def group_body(b, carry):
        group(b, False)
        return carry

    n_groups = n // (macro * unroll)
    lax.fori_loop(0, n_groups - 1, group_body, 0)
    group(n_groups - 1, True)

    sums = jnp.minimum(a_ref[0, HEAD_DIM:HEAD_DIM + 1, :], a_ref[1, HEAD_DIM:HEAD_DIM + 1, :])
    trusted = jnp.min(sums) >= UNDERFLOW_GUARD

    @pl.when(jnp.logical_not(trusted))
    def _exact_running_max():
        m_ref[...] = jnp.full(m_ref.shape, NEG_BIG, F32)
        a_ref[...] = jnp.zeros_like(a_ref)

        def body(j, carry):
            kt = k_ref[pl.ds(pl.multiple_of(j * tile, tile), tile), :]
            vt = vt_ref[j]
            bias = bt_ref[jnp.clip(j - qi, -BIAS_REACH, BIAS_REACH) + BIAS_REACH]
            for t in range(2):
                s = jnp.dot(kt, qa_ref[t, 1, :LANES, :], preferred_element_type=F32) + bias
                m_prev = m_ref[t]
                m_new = jnp.maximum(m_prev, jnp.max(s, axis=0, keepdims=True))
                pr = jnp.exp2(s - m_new).astype(BF16)
                a_ref[t] = (jnp.exp2(m_prev - m_new) * a_ref[t]
                            + jnp.dot(vt, pr, preferred_element_type=F32))
                m_ref[t] = m_new
            return carry

        lax.fori_loop(0, n, body, 0)

    lp = lamp_ref[...]
    lam = (jnp.exp(jnp.sum(lp[0:1] * lp[1:2], axis=1, keepdims=True))
           - jnp.exp(jnp.sum(lp[2:3] * lp[3:4], axis=1, keepdims=True)) + lam_init)
    a1 = a_ref[0]
    a2 = a_ref[1]
    o = (a1[:HEAD_DIM] / a1[HEAD_DIM:HEAD_DIM + 1]
         - lam * (a2[:HEAD_DIM] / a2[HEAD_DIM:HEAD_DIM + 1]))
    ms = jnp.mean(o * o, axis=0, keepdims=True)
    y = o * lax.rsqrt(ms + HEAD_NORM_EPS) * dng_ref[...] * (1.0 - lam_init)
    o_ref[...] = y.astype(o_ref.dtype)


def _diff_attention(dqk, vt3, bias_t, rel_log2, lamp, dng, out, row0, batch, seq, tile, lam_init):
    n = seq // tile
    macro = max(g for g in (1, 2, 4) if n % g == 0)
    n_macro = n // macro
    unroll = 1 if n_macro == 1 else 2
    assert n_macro % unroll == 0
    n_pairs = GROUP_WIDTH // LANES
    qblk0 = row0 // tile
    sblk0 = row0 // seq
    n_bias = 2 * BIAS_REACH + 1
    kernel = functools.partial(_attn_kernel, n_tiles=n, tile=tile, macro=macro, unroll=unroll,
                               lam_init=lam_init)
    return pl.pallas_call(
        kernel,
        grid=(batch, N_HEADS, n),
        in_specs=[pl.BlockSpec(memory_space=pltpu.SMEM),
                  pl.BlockSpec((4, DIFF_QK_DIM), lambda b, h, i: (0, 0)),
                  pl.BlockSpec((HEAD_DIM, 1), lambda b, h, i: (0, 0)),
                  pl.BlockSpec((tile, LANES), lambda b, h, i: (qblk0 + b * n + i, h // 2)),
                  pl.BlockSpec((seq, LANES), lambda b, h, i: (sblk0 + b, n_pairs + h // 2)),
                  pl.BlockSpec((n, V_ROWS, tile), lambda b, h, i: (sblk0 + b, h, 0)),
                  pl.BlockSpec((None, n_bias, tile, tile), lambda b, h, i: (h, 0, 0, 0)),
                  pl.BlockSpec(memory_space=pl.ANY)],
        out_specs=pl.BlockSpec((None, HEAD_DIM, tile), lambda b, h, i: (qblk0 + b * n + i, h, 0)),
        out_shape=jax.ShapeDtypeStruct(out.shape, out.dtype),
        input_output_aliases={7: 0},
        scratch_shapes=[pltpu.VMEM((2, V_ROWS, tile), F32),
                        pltpu.VMEM((2, 2, macro * tile, tile), BF16),
                        pltpu.VMEM((2, 3, 2 * LANES, tile), BF16),
                        pltpu.VMEM((1, LANES), F32),
                        pltpu.VMEM((2, 1, tile), F32)],
        compiler_params=_cparams("parallel", "parallel", "arbitrary"),
        name="diff_attention",
    )(rel_log2, lamp, dng, dqk, dqk, vt3, bias_t, out)


def _layer_norm(y, g, b):
    mu = jnp.mean(y, axis=-1, keepdims=True)
    d = y - mu
    var = jnp.mean(d * d, axis=-1, keepdims=True)
    return d * lax.rsqrt(var + LN_EPS) * g + b


def _outproj_kernel(yr_ref, yd_ref, w_ref, x_ref, g_ref, b_ref, o_ref, ob_ref, *, alpha):
    mix = jnp.dot(yr_ref[...], w_ref[:GROUP_WIDTH, :], preferred_element_type=F32)
    mix = mix + lax.dot_general(yd_ref[...], w_ref[GROUP_WIDTH:, :], (((0,), (0,)), ((), ())),
                                preferred_element_type=F32)
    y = _layer_norm(alpha * x_ref[...] + mix, g_ref[...], b_ref[...])
    o_ref[...] = y
    ob_ref[...] = y.astype(BF16)


def _out_projection(yr, ydt, w_out, x, g, b, alpha, tm):
    t = x.shape[0]
    assert ydt.shape == (t // tm, GROUP_WIDTH, tm)
    kernel = functools.partial(_outproj_kernel, alpha=alpha)
    row = lambda i: (i, 0)
    fixed = lambda i: (0, 0)
    return pl.pallas_call(
        kernel,
        grid=(t // tm,),
        in_specs=[pl.BlockSpec((tm, GROUP_WIDTH), row),
                  pl.BlockSpec((None, GROUP_WIDTH, tm), lambda i: (i, 0, 0)),
                  pl.BlockSpec((2 * GROUP_WIDTH, D_MODEL), fixed),
                  pl.BlockSpec((tm, D_MODEL), row),
                  pl.BlockSpec((1, D_MODEL), fixed),
                  pl.BlockSpec((1, D_MODEL), fixed)],
        out_specs=[pl.BlockSpec((tm, D_MODEL), row), pl.BlockSpec((tm, D_MODEL), row)],
        out_shape=[jax.ShapeDtypeStruct((t, D_MODEL), F32), jax.ShapeDtypeStruct((t, D_MODEL), BF16)],
        compiler_params=_cparams("parallel"),
        name="out_proj_ln",
    )(yr, ydt, w_out, x, g, b)


def _ffn_kernel(x_ref, xb_ref, xp_ref, xn_ref, wa_ref, wv_ref, wd_ref, cp_ref, g_ref, b_ref,
                o_ref, ob_ref, acc_ref, *, alpha, tm, n_chunks, starts, ends):
    i = pl.program_id(0)
    t0 = i * tm
    is_start = functools.reduce(jnp.logical_or, [t0 == s for s in starts])
    is_end = functools.reduce(jnp.logical_or, [t0 + tm == e for e in ends])
    keep_prev = jnp.where(is_start, 0.0, 1.0)
    keep_next = jnp.where(is_end, 0.0, 1.0)

    xb = xb_ref[...]
    halo = jnp.concatenate([xp_ref[...], xn_ref[...]], axis=0)
    hrows = xp_ref.shape[0]
    acc_ref[...] = jnp.zeros_like(acc_ref)

    def chunk_body(c, carry):
        wa = wa_ref[c]
        a = jnp.dot(xb, wa, preferred_element_type=F32)
        val = jnp.dot(xb, wv_ref[c], preferred_element_type=F32)
        ah = jnp.dot(halo, wa, preferred_element_type=F32)
        prev_row = ah[hrows - 1:hrows] * keep_prev
        next_row = ah[hrows:hrows + 1] * keep_next
        row = lax.broadcasted_iota(jnp.int32, a.shape, 0)
        a_m1 = jnp.where(row == 0, prev_row, pltpu.roll(a, 1, 0))
        a_p1 = jnp.where(row == tm - 1, next_row, pltpu.roll(a, tm - 1, 0))
        cp = cp_ref[c]
        conv = cp[3:4] + a_m1 * cp[0:1]
        conv = conv + a * cp[1:2]
        conv = conv + a_p1 * cp[2:3]
        gelu = 0.5 * conv * (1.0 + lax.erf(conv * (1.0 / math.sqrt(2.0))))
        hidden = (gelu * val).astype(BF16)
        acc_ref[...] += jnp.dot(hidden, wd_ref[c], preferred_element_type=F32)
        return carry

    for c in range(n_chunks):
        chunk_body(c, 0)
    y = _layer_norm(alpha * x_ref[...] + acc_ref[...], g_ref[...], b_ref[...])
    o_ref[...] = y
    ob_ref[...] = y.astype(BF16)


def _conv_glu(x, xb, wa, wv, wd, cp, g, b, alpha, tm, groups):
    t = x.shape[0]
    n_chunks, _, ck = wa.shape
    hrows = 16
    starts = tuple(r0 + bi * s for (r0, nb, s) in groups for bi in range(nb))
    ends = tuple(r0 + (bi + 1) * s for (r0, nb, s) in groups for bi in range(nb))
    kernel = functools.partial(_ffn_kernel, alpha=alpha, tm=tm, n_chunks=n_chunks,
                               starts=starts, ends=ends)
    row = lambda i: (i, 0)
    fixed2 = lambda i: (0, 0)
    fixed3 = lambda i: (0, 0, 0)
    per = tm // hrows
    last = t // hrows - 1
    return pl.pallas_call(
        kernel,
        grid=(t // tm,),
        in_specs=[pl.BlockSpec((tm, D_MODEL), row),
                  pl.BlockSpec((tm, D_MODEL), row),
                  pl.BlockSpec((hrows, D_MODEL), lambda i: (jnp.maximum(i * per - 1, 0), 0)),
                  pl.BlockSpec((hrows, D_MODEL), lambda i: (jnp.minimum((i + 1) * per, last), 0)),
                  pl.BlockSpec((n_chunks, D_MODEL, ck), fixed3, pipeline_mode=pl.Buffered(1)),
                  pl.BlockSpec((n_chunks, D_MODEL, ck), fixed3, pipeline_mode=pl.Buffered(1)),
                  pl.BlockSpec((n_chunks, ck, D_MODEL), fixed3, pipeline_mode=pl.Buffered(1)),
                  pl.BlockSpec((n_chunks, 8, ck), fixed3),
                  pl.BlockSpec((1, D_MODEL), fixed2),
                  pl.BlockSpec((1, D_MODEL), fixed2)],
        out_specs=[pl.BlockSpec((tm, D_MODEL), row), pl.BlockSpec((tm, D_MODEL), row)],
        out_shape=[jax.ShapeDtypeStruct((t, D_MODEL), F32), jax.ShapeDtypeStruct((t, D_MODEL), BF16)],
        scratch_shapes=[pltpu.VMEM((tm, D_MODEL), F32)],
        compiler_params=_cparams("parallel"),
        name="conv_glu_ln",
    )(x, xb, xb, xb, wa, wv, wd, cp, g, b)


def _rotary_tables(seq):
    d = HEAD_DIM
    inv = 1.0 / (ROPE_BASE ** (jnp.arange(0, d, 2, dtype=F32) / d))
    ang = jnp.arange(seq, dtype=F32)[:, None] * inv[None, :]
    cos, sin = jnp.cos(ang), jnp.sin(ang)
    cos_t = jnp.concatenate([cos, cos, cos, cos], axis=-1)
    sin_t = jnp.concatenate([-sin, sin, -sin, sin], axis=-1)
    return cos_t, sin_t


def _tiles(groups):
    smin = min(s for (_, _, s) in groups)
    attn_tile = min(512, smin // 2)
    chunk = min(256, smin // 2)
    tm = min(512, smin // 2)
    ffn_tm = min(1024, smin // 2)
    return attn_tile, chunk, tm, ffn_tm


def _forward(x, groups, w_in, ret_decay_logit, rel_bias, lambda_q1, lambda_k1, lambda_q2,
             lambda_k2, diff_norm_g, w_out, ln_g, ln_b, w_up, conv_w, conv_b, w_down):
    depth = w_in.shape[0]
    alpha = (2 * depth) ** 0.25
    t = x.shape[0]
    attn_tile, chunk, tm, ffn_tm = _tiles(groups)
    smax = max(s for (_, _, s) in groups)
    gw = GROUP_WIDTH
    ck = 256
    n_chunks = D_FF // ck

    cos_t, sin_t = _rotary_tables(smax)
    bias_t = _bias_tiles(rel_bias, attn_tile)
    rel_log2 = rel_bias.astype(F32) * LOG2E
    rel_log2 = jnp.concatenate([rel_log2, jnp.max(rel_log2, axis=0, keepdims=True)], axis=0)
    in_scale = jnp.concatenate([
        jnp.ones((gw,), F32), jnp.full((gw,), HEAD_DIM ** -0.5, F32), jnp.ones((2 * gw,), F32),
        jnp.full((gw,), DIFF_QK_DIM ** -0.5 * LOG2E, F32), jnp.ones((gw,), F32)])[None, :]
    assert tm == attn_tile

    xb = x.astype(BF16)
    for l in range(depth):
        lam_init = 0.8 - 0.6 * math.exp(-0.3 * l)
        w_in_b = w_in[l].astype(BF16)
        ret, gate, dqk, vt3 = _project(xb, w_in_b[:, :6 * gw], w_in_b[:, 6 * gw:].T, in_scale,
                                       cos_t, sin_t, groups, tm)

        log_g = jax.nn.log_sigmoid(ret_decay_logit[l].astype(F32))
        lamp = jnp.stack([lambda_q1[l], lambda_k1[l], lambda_q2[l], lambda_k2[l]]).astype(F32)
        dng = diff_norm_g[l].astype(F32)[:, None]

        yr = jnp.zeros((t, gw), BF16)
        ydt = jnp.zeros((t // attn_tile, gw, attn_tile), BF16)
        for (row0, batch, seq) in groups:
            yr = _retention(ret, gate, log_g, yr, row0, batch, seq, chunk)
            ydt = _diff_attention(dqk, vt3, bias_t, rel_log2, lamp, dng, ydt, row0, batch, seq,
                                  attn_tile, lam_init)

        x, xb = _out_projection(yr, ydt, w_out[l].astype(BF16), x, ln_g[l, 0][None, :].astype(F32),
                                ln_b[l, 0][None, :].astype(F32), alpha, tm)

        wa = w_up[l][:, :D_FF].astype(BF16).reshape(D_MODEL, n_chunks, ck).transpose(1, 0, 2)
        wv = w_up[l][:, D_FF:].astype(BF16).reshape(D_MODEL, n_chunks, ck).transpose(1, 0, 2)
        wd = w_down[l].astype(BF16).reshape(n_chunks, ck, D_MODEL)
        cp = jnp.concatenate([conv_w[l].astype(F32), conv_b[l].astype(F32)[None, :],
                              jnp.zeros((4, D_FF), F32)], axis=0)
        cp = cp.reshape(8, n_chunks, ck).transpose(1, 0, 2)
        x, xb = _conv_glu(x, xb, wa, wv, wd, cp, ln_g[l, 1][None, :].astype(F32),
                          ln_b[l, 1][None, :].astype(F32), alpha, ffn_tm, groups)
    return x


def kernel(x_prompt, x_sample, w_in, ret_decay_logit, rel_bias, lambda_q1, lambda_k1, lambda_q2,
           lambda_k2, diff_norm_g, w_out, ln_g, ln_b, w_up, conv_w, conv_b, w_down):
    bp, sp, d = x_prompt.shape
    bs, ss, _ = x_sample.shape
    groups = ((0, bp, sp), (bp * sp, bs, ss))
    x = jnp.concatenate([x_prompt.reshape(bp * sp, d), x_sample.reshape(bs * ss, d)], axis=0)
    y = _forward(x.astype(F32), groups, w_in, ret_decay_logit, rel_bias, lambda_q1, lambda_k1,
                 lambda_q2, lambda_k2, diff_norm_g, w_out, ln_g, ln_b, w_up, conv_w, conv_b, w_down)
    y_prompt = y[:bp * sp].reshape(bp, sp, d).astype(x_prompt.dtype)
    y_sample = y[bp * sp:].reshape(bs, ss, d).astype(x_sample.dtype)
    return y_prompt, y_sample
```

```python
import functools
import math

import jax
import jax.numpy as jnp
from jax import lax
from jax.experimental import pallas as pl
from jax.experimental.pallas import tpu as pltpu

D_MODEL = 1024
HEAD_DIM = 64
N_HEADS = 8
GROUP_WIDTH = N_HEADS * HEAD_DIM
DIFF_QK_DIM = HEAD_DIM // 2
D_FF = 2816
N_BUCKETS = 32
MAX_DISTANCE = 128
ROPE_BASE = 10000.0
LN_EPS = 1e-5
HEAD_NORM_EPS = 1e-6
LANES = 128
FAR_DISTANCE = 91
BIAS_REACH = 2
LOG2E = math.log2(math.e)
NEG_BIG = -1e30
UNDERFLOW_GUARD = 2.0 ** -90
VMEM_LIMIT = 56 * 1024 * 1024

F32 = jnp.float32
BF16 = jnp.bfloat16


def _cparams(*sem):
    return pltpu.CompilerParams(dimension_semantics=sem, vmem_limit_bytes=VMEM_LIMIT)


def _proj_kernel(x_ref, w_ref, wvt_ref, s_ref, cos_ref, sin_ref, ret_ref, gate_ref, dqk_ref, vt_ref):
    gw = GROUP_WIDTH
    x = x_ref[...]
    tm = x.shape[0]
    lane = lax.broadcasted_iota(jnp.int32, (tm, LANES), 1)
    low_half = (lane % HEAD_DIM) < (HEAD_DIM // 2)
    cos = cos_ref[...]
    sin = sin_ref[...]
    qk = jnp.dot(x, w_ref[:, :2 * gw], preferred_element_type=F32) * s_ref[:, :2 * gw]
    for p in range(2 * gw // LANES):
        sl = slice(p * LANES, (p + 1) * LANES)
        xx = qk[:, sl]
        swapped = jnp.where(low_half, pltpu.roll(xx, LANES - HEAD_DIM // 2, 1),
                            pltpu.roll(xx, HEAD_DIM // 2, 1))
        ret_ref[:, sl] = (xx * cos + swapped * sin).astype(ret_ref.dtype)
    ret_ref[:, 2 * gw:] = jnp.dot(x, w_ref[:, 2 * gw:3 * gw],
                                  preferred_element_type=F32).astype(ret_ref.dtype)
    gate_ref[...] = jnp.dot(x, w_ref[:, 3 * gw:4 * gw], preferred_element_type=F32)
    dqk_ref[...] = (jnp.dot(x, w_ref[:, 4 * gw:], preferred_element_type=F32)
                    * s_ref[:, 4 * gw:]).astype(dqk_ref.dtype)
    vt_ref[...] = lax.dot_general(wvt_ref[...], x, (((1,), (1,)), ((), ())),
                                  preferred_element_type=F32).astype(vt_ref.dtype)


def _position_block(i, tm, groups):
    t0 = i * tm
    blk = t0 // tm
    for (row0, _, seq) in groups:
        blk = jnp.where(t0 >= row0, ((t0 - row0) % seq) // tm, blk)
    return blk


def _project(xb, w, w_vt, scale, cos_t, sin_t, groups, tm):
    t, k = xb.shape
    gw = GROUP_WIDTH
    row = lambda i: (i, 0)
    fixed = lambda i: (0, 0)
    pos = lambda i: (_position_block(i, tm, groups), 0)
    return pl.pallas_call(
        _proj_kernel,
        grid=(t // tm,),
        in_specs=[pl.BlockSpec((tm, k), row),
                  pl.BlockSpec((k, 6 * gw), fixed, pipeline_mode=pl.Buffered(1)),
                  pl.BlockSpec((gw, k), fixed, pipeline_mode=pl.Buffered(1)),
                  pl.BlockSpec((1, 6 * gw), fixed),
                  pl.BlockSpec((tm, LANES), pos),
                  pl.BlockSpec((tm, LANES), pos)],
        out_specs=[pl.BlockSpec((tm, 3 * gw), row), pl.BlockSpec((tm, gw), row),
                   pl.BlockSpec((tm, 2 * gw), row),
                   pl.BlockSpec((None, gw, tm), lambda i: (i, 0, 0))],
        out_shape=[jax.ShapeDtypeStruct((t, 3 * gw), BF16), jax.ShapeDtypeStruct((t, gw), F32),
                   jax.ShapeDtypeStruct((t, 2 * gw), BF16),
                   jax.ShapeDtypeStruct((t // tm, gw, tm), BF16)],
        compiler_params=_cparams("parallel"),
        name="in_proj",
    )(xb, w, w_vt, scale, cos_t, sin_t)


def _ret_kernel(lg_ref, q_ref, k_ref, v_ref, g_ref, _, o_ref,
                rf_ref, rb_ref, rnext_ref, mask_ref, tab_ref, *, nc, chunk):
    t = pl.program_id(1)
    c = chunk
    n_pairs = GROUP_WIDTH // LANES
    lane = lax.broadcasted_iota(jnp.int32, (c, LANES), 1)
    head0 = lane < HEAD_DIM
    r_i = lax.broadcasted_iota(jnp.int32, (LANES, LANES), 0) // HEAD_DIM
    c_i = lax.broadcasted_iota(jnp.int32, (LANES, LANES), 1) // HEAD_DIM
    same_head = r_i == c_i

    @pl.when(t == 0)
    def _init():
        rb_ref[...] = jnp.zeros_like(rb_ref)
        qi = lax.broadcasted_iota(jnp.int32, (c, c), 0)
        ki = lax.broadcasted_iota(jnp.int32, (c, c), 1)
        diff = (qi - ki).astype(F32)
        pos = lax.broadcasted_iota(jnp.int32, (c, LANES), 0).astype(F32)
        for p in range(n_pairs):
            for hh in range(2):
                lf = lg_ref[0, 2 * p + hh]
                lb = lg_ref[1, 2 * p + hh]
                mask_ref[p, hh * c:(hh + 1) * c, :] = jnp.where(
                    diff >= 0, jnp.exp(lf * jnp.maximum(diff, 0.0)),
                    jnp.exp(lb * jnp.maximum(-diff, 0.0)))
            lfl = jnp.where(head0, lg_ref[0, 2 * p], lg_ref[0, 2 * p + 1])
            lbl = jnp.where(head0, lg_ref[1, 2 * p], lg_ref[1, 2 * p + 1])
            tab_ref[p, 0] = jnp.exp(lfl * (c - 1 - pos))
            tab_ref[p, 1] = jnp.exp(lbl * pos)
            tab_ref[p, 2] = jnp.exp(lfl * (pos + 1.0))
            tab_ref[p, 3] = jnp.exp(lbl * (c - pos))
            tab_ref[p, 4] = jnp.exp(lfl * c)
            tab_ref[p, 5] = jnp.exp(lbl * c)

    def summary(k, w, v):
        kw = (k.astype(F32) * w).astype(BF16)
        kv = lax.dot_general(kw, v, (((0,), (0,)), ((), ())), preferred_element_type=F32)
        return jnp.where(same_head, kv, 0.0)

    @pl.when(t < nc)
    def _backward():
        for p in range(n_pairs):
            sl = slice(p * LANES, (p + 1) * LANES)
            rnext_ref[nc - 1 - t, p] = rb_ref[p].astype(BF16)
            rb_ref[p] = (tab_ref[p, 5][:LANES] * rb_ref[p]
                         + summary(k_ref[:, sl], tab_ref[p, 1], v_ref[:, sl]))

    @pl.when(t >= nc)
    def _forward():
        @pl.when(t == nc)
        def _():
            rf_ref[...] = jnp.zeros_like(rf_ref)

        for p in range(n_pairs):
            sl = slice(p * LANES, (p + 1) * LANES)
            q = q_ref[:, sl]
            k = k_ref[:, sl]
            v = v_ref[:, sl]
            zero = jnp.zeros_like(q)
            q2 = jnp.concatenate([jnp.where(head0, q, zero), jnp.where(head0, zero, q)], axis=0)
            s = lax.dot_general(q2, k, (((1,), (1,)), ((), ())), preferred_element_type=F32)
            o2 = jnp.dot((s * mask_ref[p]).astype(BF16), v, preferred_element_type=F32)
            y = jnp.where(head0, o2[:c], o2[c:])
            y = y + jnp.dot(q, rf_ref[p].astype(BF16), preferred_element_type=F32) * tab_ref[p, 2]
            y = y + jnp.dot(q, rnext_ref[t - nc, p], preferred_element_type=F32) * tab_ref[p, 3]

            sq = y * y
            s0 = jnp.sum(jnp.where(head0, sq, 0.0), axis=1, keepdims=True)
            s1 = jnp.sum(jnp.where(head0, 0.0, sq), axis=1, keepdims=True)
            ms = jnp.where(head0, s0, s1) * (1.0 / HEAD_DIM)
            y = y * lax.rsqrt(ms + HEAD_NORM_EPS)
            g = g_ref[:, sl]
            o_ref[:, sl] = (g / (1.0 + jnp.exp(-g)) * y).astype(o_ref.dtype)

            rf_ref[p] = tab_ref[p, 4][:LANES] * rf_ref[p] + summary(k, tab_ref[p, 0], v)


def _retention(ret, gate, log_g, out, row0, batch, seq, chunk):
    nc = seq // chunk
    blk0 = row0 // chunk
    n_pairs = GROUP_WIDTH // LANES
    gw = GROUP_WIDTH

    def kc(t):
        return jnp.where(t < nc, nc - 1 - t, t - nc)

    def qc(t):
        return jnp.maximum(t - nc, 0)

    def rows(b, cc):
        return blk0 + b * nc + cc

    kernel = functools.partial(_ret_kernel, nc=nc, chunk=chunk)
    return pl.pallas_call(
        kernel,
        grid=(batch, 2 * nc),
        in_specs=[pl.BlockSpec(memory_space=pltpu.SMEM),
                  pl.BlockSpec((chunk, gw), lambda b, t: (rows(b, qc(t)), 0)),
                  pl.BlockSpec((chunk, gw), lambda b, t: (rows(b, kc(t)), 1)),
                  pl.BlockSpec((chunk, gw), lambda b, t: (rows(b, kc(t)), 2)),
                  pl.BlockSpec((chunk, gw), lambda b, t: (rows(b, qc(t)), 0)),
                  pl.BlockSpec(memory_space=pl.ANY)],
        out_specs=pl.BlockSpec((chunk, gw), lambda b, t: (rows(b, qc(t)), 0)),
        out_shape=jax.ShapeDtypeStruct(out.shape, out.dtype),
        input_output_aliases={5: 0},
        scratch_shapes=[pltpu.VMEM((n_pairs, LANES, LANES), F32),
                        pltpu.VMEM((n_pairs, LANES, LANES), F32),
                        pltpu.VMEM((nc, n_pairs, LANES, LANES), BF16),
                        pltpu.VMEM((n_pairs, 2 * chunk, chunk), F32),
                        pltpu.VMEM((n_pairs, 6, chunk, LANES), F32)],
        compiler_params=_cparams("parallel", "arbitrary"),
        name="retention",
    )(log_g, ret, ret, ret, gate, out)


def _bias_kernel(rb_ref, bucket_ref, o_ref):
    h = pl.program_id(0)
    bk = bucket_ref[...]
    out = jnp.zeros(bk.shape, F32)
    for n in range(N_BUCKETS):
        out = jnp.where(bk == n, rb_ref[n, h], out)
    o_ref[...] = out * LOG2E


def _bias_tiles(rel_bias, tile):
    assert tile > FAR_DISTANCE
    a = jnp.arange(tile, dtype=jnp.int32)[:, None]
    b = jnp.arange(tile, dtype=jnp.int32)[None, :]
    rel = jnp.stack([(d * tile + a - b) for d in range(-BIAS_REACH, BIAS_REACH + 1)])
    nb = N_BUCKETS // 2
    max_exact = nb // 2
    n = jnp.abs(rel)
    nf = jnp.maximum(n, 1).astype(F32)
    large = max_exact + (jnp.log(nf / max_exact) / math.log(MAX_DISTANCE / max_exact)
                         * (nb - max_exact)).astype(jnp.int32)
    large = jnp.minimum(large, nb - 1)
    bucket = jnp.where(rel > 0, nb, 0) + jnp.where(n < max_exact, n, large)
    return pl.pallas_call(
        _bias_kernel,
        grid=(N_HEADS, 2 * BIAS_REACH + 1),
        in_specs=[pl.BlockSpec(memory_space=pltpu.SMEM),
                  pl.BlockSpec((None, tile, tile), lambda h, d: (d, 0, 0))],
        out_specs=pl.BlockSpec((None, None, tile, tile), lambda h, d: (h, d, 0, 0)),
        out_shape=jax.ShapeDtypeStruct((N_HEADS, 2 * BIAS_REACH + 1, tile, tile), F32),
        compiler_params=_cparams("parallel", "arbitrary"),
        name="t5_bias_tiles",
    )(rel_bias.astype(F32), bucket.astype(jnp.int32))


def _attn_kernel(rb_ref, lamp_ref, dng_ref, q_ref, k_ref, vt_ref, bt_ref, _, o_ref,
                 a_ref, l_ref, p_ref, qa_ref, kmax_ref, m_ref, *, n_tiles, tile, macro, unroll,
                 lam_init):
    h = pl.program_id(1)
    qi = pl.program_id(2)
    hh = h % 2
    n = n_tiles
    half = DIFF_QK_DIM
    group_of_lane = lax.broadcasted_iota(jnp.int32, (LANES, LANES), 0) // half
    group_sum = (group_of_lane == lax.broadcasted_iota(jnp.int32, (LANES, LANES), 1)).astype(BF16)

    def half_norms(x):
        xf = x.astype(F32)
        return jnp.dot((xf * xf).astype(BF16), group_sum, preferred_element_type=F32)

    @pl.when(qi == 0)
    def _key_norms():
        def body(j, mx):
            kt = k_ref[pl.ds(pl.multiple_of(j * tile, tile), tile), :]
            return jnp.maximum(mx, jnp.max(half_norms(kt), axis=0, keepdims=True))
        kmax_ref[...] = lax.fori_loop(0, n, body, jnp.zeros((1, LANES), F32))

    qf = q_ref[...].astype(F32)
    lane = lax.broadcasted_iota(jnp.int32, qf.shape, 1)
    lane_row = lax.broadcasted_iota(jnp.int32, (1, LANES), 1)
    row = lax.broadcasted_iota(jnp.int32, (LANES, tile), 0)
    group_sum_t = (lax.broadcasted_iota(jnp.int32, (LANES, LANES), 0)
                   == lax.broadcasted_iota(jnp.int32, (LANES, LANES), 1) // half).astype(BF16)
    qn_rows = lax.dot_general(group_sum_t, (qf * qf).astype(BF16), (((1,), (1,)), ((), ())),
                              preferred_element_type=F32)
    side_bias = (rb_ref[N_BUCKETS // 2 - 1, h], 0.0, rb_ref[N_BUCKETS - 1, h])
    for t in range(2):
        g = 2 * hh + t
        lo_lane = g * half
        qzt = jnp.where((lane >= lo_lane) & (lane < lo_lane + half), qf, 0.0).T.astype(BF16)
        qn2 = jnp.sum(jnp.where(row == g, qn_rows, 0.0), axis=0, keepdims=True)
        kn2 = jnp.max(jnp.where(lane_row == g, kmax_ref[...], 0.0), axis=1, keepdims=True)
        bound = jnp.sqrt(qn2 * kn2) + rb_ref[N_BUCKETS, h]
        for side in range(3):
            v = side_bias[side] - bound
            hi = v.astype(BF16).astype(F32)
            lo = (v - hi).astype(BF16).astype(F32)
            qa_ref[t, side, :LANES, :] = qzt
            qa_ref[t, side, LANES:, :] = jnp.where(row == 0, hi, jnp.where(row == 1, lo, 0.0)).astype(BF16)

    ones = jnp.ones((macro * tile, LANES), BF16)

    def near_any(m):
        j0 = m * macro
        return (qi >= j0 - (BIAS_REACH - 1)) & (qi <= j0 + macro - 1 + (BIAS_REACH - 1))

    def probabilities(m, slot, with_bias):
        j0 = m * macro
        kt = k_ref[pl.ds(pl.multiple_of(j0 * tile, macro * tile), macro * tile), :]
        kaug = jnp.concatenate([kt, ones], axis=1)
        side = 1 if with_bias else jnp.where(j0 < qi, 0, 2)
        for t in range(2):
            s = jnp.dot(kaug, qa_ref[t, side], preferred_element_type=F32)
            for r in range(macro):
                sr = s[r * tile:(r + 1) * tile]
                if with_bias:
                    sr = sr + bt_ref[jnp.clip(j0 + r - qi, -BIAS_REACH, BIAS_REACH) + BIAS_REACH]
                e = jnp.exp2(sr)
                l_ref[t] += jnp.sum(e, axis=0, keepdims=True)
                p_ref[slot, t, r * tile:(r + 1) * tile, :] = e.astype(BF16)

    def accumulate(m, slot):
        vt = jnp.concatenate([vt_ref[m * macro + r] for r in range(macro)], axis=1)
        for t in range(2):
            a_ref[t] += jnp.dot(vt, p_ref[slot, t], preferred_element_type=F32)

    a_ref[...] = jnp.zeros_like(a_ref)
    l_ref[...] = jnp.zeros_like(l_ref)

    def group(b, last):
        m0 = unroll * b

        def run(with_bias):
            for k in range(unroll):
                if not (last and k == unroll - 1):
                    probabilities(m0 + k + 1, (k + 1) % 2, with_bias)
                accumulate(m0 + k, k % 2)

        any_near = functools.reduce(jnp.logical_or, [near_any(m0 + k + 1) for k in range(unroll)])
        lax.cond(any_near, lambda: run(True), lambda: run(False))

    def group_body(b, carry):
        group(b, False)
        return carry

    if n == macro:
        probabilities(0, 0, True)
        accumulate(0, 0)
    else:
        lax.cond(near_any(0), lambda: probabilities(0, 0, True),
                 lambda: probabilities(0, 0, False))
        n_groups = n // (macro * unroll)
        lax.fori_loop(0, n_groups - 1, group_body, 0)
        group(n_groups - 1, True)

    trusted = jnp.min(jnp.minimum(l_ref[0], l_ref[1])) >= UNDERFLOW_GUARD

    @pl.when(jnp.logical_not(trusted))
    def _exact_running_max():
        m_ref[...] = jnp.full(m_ref.shape, NEG_BIG, F32)
        a_ref[...] = jnp.zeros_like(a_ref)
        l_ref[...] = jnp.zeros_like(l_ref)

        def body(j, carry):
            kt = k_ref[pl.ds(pl.multiple_of(j * tile, tile), tile), :]
            vt = vt_ref[j]
            bias = bt_ref[jnp.clip(j - qi, -BIAS_REACH, BIAS_REACH) + BIAS_REACH]
            for t in range(2):
                s = jnp.dot(kt, qa_ref[t, 1, :LANES, :], preferred_element_type=F32) + bias
                m_prev = m_ref[t]
                m_new = jnp.maximum(m_prev, jnp.max(s, axis=0, keepdims=True))
                alpha = jnp.exp2(m_prev - m_new)
                e = jnp.exp2(s - m_new)
                l_ref[t] = alpha * l_ref[t] + jnp.sum(e, axis=0, keepdims=True)
                a_ref[t] = alpha * a_ref[t] + jnp.dot(vt, e.astype(BF16),
                                                      preferred_element_type=F32)
                m_ref[t] = m_new
            return carry

        lax.fori_loop(0, n, body, 0)

    lp = lamp_ref[...]
    lam = (jnp.exp(jnp.sum(lp[0:1] * lp[1:2], axis=1, keepdims=True))
           - jnp.exp(jnp.sum(lp[2:3] * lp[3:4], axis=1, keepdims=True)) + lam_init)
    o = a_ref[0] / l_ref[0] - lam * (a_ref[1] / l_ref[1])
    ms = jnp.mean(o * o, axis=0, keepdims=True)
    y = o * lax.rsqrt(ms + HEAD_NORM_EPS) * dng_ref[...] * (1.0 - lam_init)
    o_ref[...] = y.astype(o_ref.dtype)


def _diff_attention(dqk, vt3, bias_t, rel_log2, lamp, dng, out, row0, batch, seq, tile, lam_init):
    n = seq // tile
    macro = max(g for g in (1, 2, 4) if n % g == 0)
    n_macro = n // macro
    unroll = 1 if n_macro == 1 else 2
    assert n_macro % unroll == 0
    n_pairs = GROUP_WIDTH // LANES
    qblk0 = row0 // tile
    sblk0 = row0 // seq
    n_bias = 2 * BIAS_REACH + 1
    kernel = functools.partial(_attn_kernel, n_tiles=n, tile=tile, macro=macro, unroll=unroll,
                               lam_init=lam_init)
    return pl.pallas_call(
        kernel,
        grid=(batch, N_HEADS, n),
        in_specs=[pl.BlockSpec(memory_space=pltpu.SMEM),
                  pl.BlockSpec((4, DIFF_QK_DIM), lambda b, h, i: (0, 0)),
                  pl.BlockSpec((HEAD_DIM, 1), lambda b, h, i: (0, 0)),
                  pl.BlockSpec((tile, LANES), lambda b, h, i: (qblk0 + b * n + i, h // 2)),
                  pl.BlockSpec((seq, LANES), lambda b, h, i: (sblk0 + b, n_pairs + h // 2)),
                  pl.BlockSpec((n, HEAD_DIM, tile), lambda b, h, i: (sblk0 + b, h, 0)),
                  pl.BlockSpec((None, n_bias, tile, tile), lambda b, h, i: (h, 0, 0, 0)),
                  pl.BlockSpec(memory_space=pl.ANY)],
        out_specs=pl.BlockSpec((None, HEAD_DIM, tile), lambda b, h, i: (qblk0 + b * n + i, h, 0)),
        out_shape=jax.ShapeDtypeStruct(out.shape, out.dtype),
        input_output_aliases={7: 0},
        scratch_shapes=[pltpu.VMEM((2, HEAD_DIM, tile), F32),
                        pltpu.VMEM((2, 1, tile), F32),
                        pltpu.VMEM((2, 2, macro * tile, tile), BF16),
                        pltpu.VMEM((2, 3, 2 * LANES, tile), BF16),
                        pltpu.VMEM((1, LANES), F32),
                        pltpu.VMEM((2, 1, tile), F32)],
        compiler_params=_cparams("parallel", "parallel", "arbitrary"),
        name="diff_attention",
    )(rel_log2, lamp, dng, dqk, dqk, vt3, bias_t, out)


def _layer_norm(y, g, b):
    mu = jnp.mean(y, axis=-1, keepdims=True)
    d = y - mu
    var = jnp.mean(d * d, axis=-1, keepdims=True)
    return d * lax.rsqrt(var + LN_EPS) * g + b


def _outproj_kernel(yr_ref, yd_ref, w_ref, x_ref, g_ref, b_ref, o_ref, ob_ref, *, alpha):
    mix = jnp.dot(yr_ref[...], w_ref[:GROUP_WIDTH, :], preferred_element_type=F32)
    mix = mix + lax.dot_general(yd_ref[...], w_ref[GROUP_WIDTH:, :], (((0,), (0,)), ((), ())),
                                preferred_element_type=F32)
    y = _layer_norm(alpha * x_ref[...] + mix, g_ref[...], b_ref[...])
    o_ref[...] = y
    ob_ref[...] = y.astype(BF16)


def _out_projection(yr, ydt, w_out, x, g, b, alpha, tm):
    t = x.shape[0]
    assert ydt.shape == (t // tm, GROUP_WIDTH, tm)
    kernel = functools.partial(_outproj_kernel, alpha=alpha)
    row = lambda i: (i, 0)
    fixed = lambda i: (0, 0)
    return pl.pallas_call(
        kernel,
        grid=(t // tm,),
        in_specs=[pl.BlockSpec((tm, GROUP_WIDTH), row),
                  pl.BlockSpec((None, GROUP_WIDTH, tm), lambda i: (i, 0, 0)),
                  pl.BlockSpec((2 * GROUP_WIDTH, D_MODEL), fixed),
                  pl.BlockSpec((tm, D_MODEL), row),
                  pl.BlockSpec((1, D_MODEL), fixed),
                  pl.BlockSpec((1, D_MODEL), fixed)],
        out_specs=[pl.BlockSpec((tm, D_MODEL), row), pl.BlockSpec((tm, D_MODEL), row)],
        out_shape=[jax.ShapeDtypeStruct((t, D_MODEL), F32), jax.ShapeDtypeStruct((t, D_MODEL), BF16)],
        compiler_params=_cparams("parallel"),
        name="out_proj_ln",
    )(yr, ydt, w_out, x, g, b)


def _ffn_kernel(x_ref, xb_ref, xp_ref, xn_ref, wa_ref, wv_ref, wd_ref, cp_ref, g_ref, b_ref,
                o_ref, ob_ref, acc_ref, *, alpha, tm, n_chunks, starts, ends):
    i = pl.program_id(0)
    t0 = i * tm
    is_start = functools.reduce(jnp.logical_or, [t0 == s for s in starts])
    is_end = functools.reduce(jnp.logical_or, [t0 + tm == e for e in ends])
    keep_prev = jnp.where(is_start, 0.0, 1.0)
    keep_next = jnp.where(is_end, 0.0, 1.0)

    xb = xb_ref[...]
    halo = jnp.concatenate([xp_ref[...], xn_ref[...]], axis=0)
    hrows = xp_ref.shape[0]
    acc_ref[...] = jnp.zeros_like(acc_ref)

    def chunk_body(c, carry):
        wa = wa_ref[c]
        a = jnp.dot(xb, wa, preferred_element_type=F32)
        val = jnp.dot(xb, wv_ref[c], preferred_element_type=F32)
        ah = jnp.dot(halo, wa, preferred_element_type=F32)
        prev_row = ah[hrows - 1:hrows] * keep_prev
        next_row = ah[hrows:hrows + 1] * keep_next
        row = lax.broadcasted_iota(jnp.int32, a.shape, 0)
        a_m1 = jnp.where(row == 0, prev_row, pltpu.roll(a, 1, 0))
        a_p1 = jnp.where(row == tm - 1, next_row, pltpu.roll(a, tm - 1, 0))
        cp = cp_ref[c]
        conv = cp[3:4] + a_m1 * cp[0:1]
        conv = conv + a * cp[1:2]
        conv = conv + a_p1 * cp[2:3]
        gelu = 0.5 * conv * (1.0 + lax.erf(conv * (1.0 / math.sqrt(2.0))))
        hidden = (gelu * val).astype(BF16)
        acc_ref[...] += jnp.dot(hidden, wd_ref[c], preferred_element_type=F32)
        return carry

    for c in range(n_chunks):
        chunk_body(c, 0)
    y = _layer_norm(alpha * x_ref[...] + acc_ref[...], g_ref[...], b_ref[...])
    o_ref[...] = y
    ob_ref[...] = y.astype(BF16)


def _conv_glu(x, xb, wa, wv, wd, cp, g, b, alpha, tm, groups):
    t = x.shape[0]
    n_chunks, _, ck = wa.shape
    hrows = 16
    starts = tuple(r0 + bi * s for (r0, nb, s) in groups for bi in range(nb))
    ends = tuple(r0 + (bi + 1) * s for (r0, nb, s) in groups for bi in range(nb))
    kernel = functools.partial(_ffn_kernel, alpha=alpha, tm=tm, n_chunks=n_chunks,
                               starts=starts, ends=ends)
    row = lambda i: (i, 0)
    fixed2 = lambda i: (0, 0)
    fixed3 = lambda i: (0, 0, 0)
    per = tm // hrows
    last = t // hrows - 1
    return pl.pallas_call(
        kernel,
        grid=(t // tm,),
        in_specs=[pl.BlockSpec((tm, D_MODEL), row),
                  pl.BlockSpec((tm, D_MODEL), row),
                  pl.BlockSpec((hrows, D_MODEL), lambda i: (jnp.maximum(i * per - 1, 0), 0)),
                  pl.BlockSpec((hrows, D_MODEL), lambda i: (jnp.minimum((i + 1) * per, last), 0)),
                  pl.BlockSpec((n_chunks, D_MODEL, ck), fixed3, pipeline_mode=pl.Buffered(1)),
                  pl.BlockSpec((n_chunks, D_MODEL, ck), fixed3, pipeline_mode=pl.Buffered(1)),
                  pl.BlockSpec((n_chunks, ck, D_MODEL), fixed3, pipeline_mode=pl.Buffered(1)),
                  pl.BlockSpec((n_chunks, 8, ck), fixed3),
                  pl.BlockSpec((1, D_MODEL), fixed2),
                  pl.BlockSpec((1, D_MODEL), fixed2)],
        out_specs=[pl.BlockSpec((tm, D_MODEL), row), pl.BlockSpec((tm, D_MODEL), row)],
        out_shape=[jax.ShapeDtypeStruct((t, D_MODEL), F32), jax.ShapeDtypeStruct((t, D_MODEL), BF16)],
        scratch_shapes=[pltpu.VMEM((tm, D_MODEL), F32)],
        compiler_params=_cparams("parallel"),
        name="conv_glu_ln",
    )(x, xb, xb, xb, wa, wv, wd, cp, g, b)


def _rotary_tables(seq):
    d = HEAD_DIM
    inv = 1.0 / (ROPE_BASE ** (jnp.arange(0, d, 2, dtype=F32) / d))
    ang = jnp.arange(seq, dtype=F32)[:, None] * inv[None, :]
    cos, sin = jnp.cos(ang), jnp.sin(ang)
    cos_t = jnp.concatenate([cos, cos, cos, cos], axis=-1)
    sin_t = jnp.concatenate([-sin, sin, -sin, sin], axis=-1)
    return cos_t, sin_t


def _tiles(groups):
    smin = min(s for (_, _, s) in groups)
    attn_tile = min(512, smin // 2)
    chunk = min(256, smin // 2)
    tm = min(512, smin // 2)
    ffn_tm = min(1024, smin // 2)
    return attn_tile, chunk, tm, ffn_tm


def _forward(x, groups, w_in, ret_decay_logit, rel_bias, lambda_q1, lambda_k1, lambda_q2,
             lambda_k2, diff_norm_g, w_out, ln_g, ln_b, w_up, conv_w, conv_b, w_down):
    depth = w_in.shape[0]
    alpha = (2 * depth) ** 0.25
    t = x.shape[0]
    attn_tile, chunk, tm, ffn_tm = _tiles(groups)
    smax = max(s for (_, _, s) in groups)
    gw = GROUP_WIDTH
    ck = 256
    n_chunks = D_FF // ck

    cos_t, sin_t = _rotary_tables(smax)
    bias_t = _bias_tiles(rel_bias, attn_tile)
    rel_log2 = rel_bias.astype(F32) * LOG2E
    rel_log2 = jnp.concatenate([rel_log2, jnp.max(rel_log2, axis=0, keepdims=True)], axis=0)
    in_scale = jnp.concatenate([
        jnp.ones((gw,), F32), jnp.full((gw,), HEAD_DIM ** -0.5, F32), jnp.ones((2 * gw,), F32),
        jnp.full((gw,), DIFF_QK_DIM ** -0.5 * LOG2E, F32), jnp.ones((gw,), F32)])[None, :]
    assert tm == attn_tile

    xb = x.astype(BF16)
    for l in range(depth):
        lam_init = 0.8 - 0.6 * math.exp(-0.3 * l)
        w_in_b = w_in[l].astype(BF16)
        ret, gate, dqk, vt3 = _project(xb, w_in_b[:, :6 * gw], w_in_b[:, 6 * gw:].T, in_scale,
                                       cos_t, sin_t, groups, tm)

        log_g = jax.nn.log_sigmoid(ret_decay_logit[l].astype(F32))
        lamp = jnp.stack([lambda_q1[l], lambda_k1[l], lambda_q2[l], lambda_k2[l]]).astype(F32)
        dng = diff_norm_g[l].astype(F32)[:, None]

        yr = jnp.zeros((t, gw), BF16)
        ydt = jnp.zeros((t // attn_tile, gw, attn_tile), BF16)
        for (row0, batch, seq) in groups:
            yr = _retention(ret, gate, log_g, yr, row0, batch, seq, chunk)
            ydt = _diff_attention(dqk, vt3, bias_t, rel_log2, lamp, dng, ydt, row0, batch, seq,
                                  attn_tile, lam_init)

        x, xb = _out_projection(yr, ydt, w_out[l].astype(BF16), x, ln_g[l, 0][None, :].astype(F32),
                                ln_b[l, 0][None, :].astype(F32), alpha, tm)

        wa = w_up[l][:, :D_FF].astype(BF16).reshape(D_MODEL, n_chunks, ck).transpose(1, 0, 2)
        wv = w_up[l][:, D_FF:].astype(BF16).reshape(D_MODEL, n_chunks, ck).transpose(1, 0, 2)
        wd = w_down[l].astype(BF16).reshape(n_chunks, ck, D_MODEL)
        cp = jnp.concatenate([conv_w[l].astype(F32), conv_b[l].astype(F32)[None, :],
                              jnp.zeros((4, D_FF), F32)], axis=0)
        cp = cp.reshape(8, n_chunks, ck).transpose(1, 0, 2)
        x, xb = _conv_glu(x, xb, wa, wv, wd, cp, ln_g[l, 1][None, :].astype(F32),
                          ln_b[l, 1][None, :].astype(F32), alpha, ffn_tm, groups)
    return x


def kernel(x_prompt, x_sample, w_in, ret_decay_logit, rel_bias, lambda_q1, lambda_k1, lambda_q2,
           lambda_k2, diff_norm_g, w_out, ln_g, ln_b, w_up, conv_w, conv_b, w_down):
    bp, sp, d = x_prompt.shape
    bs, ss, _ = x_sample.shape
    groups = ((0, bp, sp), (bp * sp, bs, ss))
    x = jnp.concatenate([x_prompt.reshape(bp * sp, d), x_sample.reshape(bs * ss, d)], axis=0)
    y = _forward(x.astype(F32), groups, w_in, ret_decay_logit, rel_bias, lambda_q1, lambda_k1,
                 lambda_q2, lambda_k2, diff_norm_g, w_out, ln_g, ln_b, w_up, conv_w, conv_b, w_down)
    y_prompt = y[:bp * sp].reshape(bp, sp, d).astype(x_prompt.dtype)
    y_sample = y[bp * sp:].reshape(bs, ss, d).astype(x_sample.dtype)
    return y_prompt, y_sample
```

```python
import functools
import math

import jax
import jax.numpy as jnp
from jax import lax
from jax.experimental import pallas as pl
from jax.experimental.pallas import tpu as pltpu

D_MODEL = 1024
HEAD_DIM = 64
N_HEADS = 8
GROUP_WIDTH = N_HEADS * HEAD_DIM
DIFF_QK_DIM = HEAD_DIM // 2
D_FF = 2816
N_BUCKETS = 32
MAX_DISTANCE = 128
ROPE_BASE = 10000.0
LN_EPS = 1e-5
HEAD_NORM_EPS = 1e-6
LANES = 128
FAR_DISTANCE = 91
BIAS_REACH = 2
LOG2E = math.log2(math.e)
NEG_BIG = -1e30
UNDERFLOW_GUARD = 2.0 ** -90
F8 = jnp.float8_e4m3fn
FP8_Q_SCALE = 16.0
FP8_K_SCALE = 8.0
FP8_RANGE = 384.0
VMEM_LIMIT = 56 * 1024 * 1024

F32 = jnp.float32
BF16 = jnp.bfloat16


def _cparams(*sem):
    return pltpu.CompilerParams(dimension_semantics=sem, vmem_limit_bytes=VMEM_LIMIT)


def _proj_kernel(x_ref, w_ref, wvt_ref, s_ref, cos_ref, sin_ref, ret_ref, gate_ref, dqk_ref, vt_ref):
    gw = GROUP_WIDTH
    x = x_ref[...]
    tm = x.shape[0]
    lane = lax.broadcasted_iota(jnp.int32, (tm, LANES), 1)
    low_half = (lane % HEAD_DIM) < (HEAD_DIM // 2)
    cos = cos_ref[...]
    sin = sin_ref[...]
    qk = jnp.dot(x, w_ref[:, :2 * gw], preferred_element_type=F32) * s_ref[:, :2 * gw]
    for p in range(2 * gw // LANES):
        sl = slice(p * LANES, (p + 1) * LANES)
        xx = qk[:, sl]
        swapped = jnp.where(low_half, pltpu.roll(xx, LANES - HEAD_DIM // 2, 1),
                            pltpu.roll(xx, HEAD_DIM // 2, 1))
        ret_ref[:, sl] = (xx * cos + swapped * sin).astype(ret_ref.dtype)
    ret_ref[:, 2 * gw:] = jnp.dot(x, w_ref[:, 2 * gw:3 * gw],
                                  preferred_element_type=F32).astype(ret_ref.dtype)
    gate_ref[...] = jnp.dot(x, w_ref[:, 3 * gw:4 * gw], preferred_element_type=F32)
    dqk_ref[...] = (jnp.dot(x, w_ref[:, 4 * gw:], preferred_element_type=F32)
                    * s_ref[:, 4 * gw:]).astype(dqk_ref.dtype)
    vt_ref[...] = lax.dot_general(wvt_ref[...], x, (((1,), (1,)), ((), ())),
                                  preferred_element_type=F32).astype(vt_ref.dtype)


def _position_block(i, tm, groups):
    t0 = i * tm
    blk = t0 // tm
    for (row0, _, seq) in groups:
        blk = jnp.where(t0 >= row0, ((t0 - row0) % seq) // tm, blk)
    return blk


def _project(xb, w, w_vt, scale, cos_t, sin_t, groups, tm):
    t, k = xb.shape
    gw = GROUP_WIDTH
    row = lambda i: (i, 0)
    fixed = lambda i: (0, 0)
    pos = lambda i: (_position_block(i, tm, groups), 0)
    return pl.pallas_call(
        _proj_kernel,
        grid=(t // tm,),
        in_specs=[pl.BlockSpec((tm, k), row),
                  pl.BlockSpec((k, 6 * gw), fixed, pipeline_mode=pl.Buffered(1)),
                  pl.BlockSpec((gw, k), fixed, pipeline_mode=pl.Buffered(1)),
                  pl.BlockSpec((1, 6 * gw), fixed),
                  pl.BlockSpec((tm, LANES), pos),
                  pl.BlockSpec((tm, LANES), pos)],
        out_specs=[pl.BlockSpec((tm, 3 * gw), row), pl.BlockSpec((tm, gw), row),
                   pl.BlockSpec((tm, 2 * gw), row),
                   pl.BlockSpec((None, gw, tm), lambda i: (i, 0, 0))],
        out_shape=[jax.ShapeDtypeStruct((t, 3 * gw), BF16), jax.ShapeDtypeStruct((t, gw), F32),
                   jax.ShapeDtypeStruct((t, 2 * gw), BF16),
                   jax.ShapeDtypeStruct((t // tm, gw, tm), BF16)],
        compiler_params=_cparams("parallel"),
        name="in_proj",
    )(xb, w, w_vt, scale, cos_t, sin_t)


def _ret_kernel(lg_ref, q_ref, k_ref, v_ref, g_ref, _, o_ref,
                rf_ref, rb_ref, rnext_ref, mask_ref, tab_ref, *, nc, chunk):
    t = pl.program_id(1)
    c = chunk
    n_pairs = GROUP_WIDTH // LANES
    lane = lax.broadcasted_iota(jnp.int32, (c, LANES), 1)
    head0 = lane < HEAD_DIM
    r_i = lax.broadcasted_iota(jnp.int32, (LANES, LANES), 0) // HEAD_DIM
    c_i = lax.broadcasted_iota(jnp.int32, (LANES, LANES), 1) // HEAD_DIM
    same_head = r_i == c_i

    @pl.when(t == 0)
    def _init():
        rb_ref[...] = jnp.zeros_like(rb_ref)
        qi = lax.broadcasted_iota(jnp.int32, (c, c), 0)
        ki = lax.broadcasted_iota(jnp.int32, (c, c), 1)
        diff = (qi - ki).astype(F32)
        pos = lax.broadcasted_iota(jnp.int32, (c, LANES), 0).astype(F32)
        for p in range(n_pairs):
            for hh in range(2):
                lf = lg_ref[0, 2 * p + hh]
                lb = lg_ref[1, 2 * p + hh]
                mask_ref[p, hh * c:(hh + 1) * c, :] = jnp.where(
                    diff >= 0, jnp.exp(lf * jnp.maximum(diff, 0.0)),
                    jnp.exp(lb * jnp.maximum(-diff, 0.0)))
            lfl = jnp.where(head0, lg_ref[0, 2 * p], lg_ref[0, 2 * p + 1])
            lbl = jnp.where(head0, lg_ref[1, 2 * p], lg_ref[1, 2 * p + 1])
            tab_ref[p, 0] = jnp.exp(lfl * (c - 1 - pos))
            tab_ref[p, 1] = jnp.exp(lbl * pos)
            tab_ref[p, 2] = jnp.exp(lfl * (pos + 1.0))
            tab_ref[p, 3] = jnp.exp(lbl * (c - pos))
            tab_ref[p, 4] = jnp.exp(lfl * c)
            tab_ref[p, 5] = jnp.exp(lbl * c)

    def summary(k, w, v):
        kw = (k.astype(F32) * w).astype(BF16)
        kv = lax.dot_general(kw, v, (((0,), (0,)), ((), ())), preferred_element_type=F32)
        return jnp.where(same_head, kv, 0.0)

    @pl.when(t < nc)
    def _backward():
        for p in range(n_pairs):
            sl = slice(p * LANES, (p + 1) * LANES)
            rnext_ref[nc - 1 - t, p] = rb_ref[p].astype(BF16)
            rb_ref[p] = (tab_ref[p, 5][:LANES] * rb_ref[p]
                         + summary(k_ref[:, sl], tab_ref[p, 1], v_ref[:, sl]))

    @pl.when(t >= nc)
    def _forward():
        @pl.when(t == nc)
        def _():
            rf_ref[...] = jnp.zeros_like(rf_ref)

        for p in range(n_pairs):
            sl = slice(p * LANES, (p + 1) * LANES)
            q = q_ref[:, sl]
            k = k_ref[:, sl]
            v = v_ref[:, sl]
            zero = jnp.zeros_like(q)
            q2 = jnp.concatenate([jnp.where(head0, q, zero), jnp.where(head0, zero, q)], axis=0)
            s = lax.dot_general(q2, k, (((1,), (1,)), ((), ())), preferred_element_type=F32)
            o2 = jnp.dot((s * mask_ref[p]).astype(BF16), v, preferred_element_type=F32)
            y = jnp.where(head0, o2[:c], o2[c:])
            y = y + jnp.dot(q, rf_ref[p].astype(BF16), preferred_element_type=F32) * tab_ref[p, 2]
            y = y + jnp.dot(q, rnext_ref[t - nc, p], preferred_element_type=F32) * tab_ref[p, 3]

            sq = y * y
            s0 = jnp.sum(jnp.where(head0, sq, 0.0), axis=1, keepdims=True)
            s1 = jnp.sum(jnp.where(head0, 0.0, sq), axis=1, keepdims=True)
            ms = jnp.where(head0, s0, s1) * (1.0 / HEAD_DIM)
            y = y * lax.rsqrt(ms + HEAD_NORM_EPS)
            g = g_ref[:, sl]
            o_ref[:, sl] = (g / (1.0 + jnp.exp(-g)) * y).astype(o_ref.dtype)

            rf_ref[p] = tab_ref[p, 4][:LANES] * rf_ref[p] + summary(k, tab_ref[p, 0], v)


def _retention(ret, gate, log_g, out, row0, batch, seq, chunk):
    nc = seq // chunk
    blk0 = row0 // chunk
    n_pairs = GROUP_WIDTH // LANES
    gw = GROUP_WIDTH

    def kc(t):
        return jnp.where(t < nc, nc - 1 - t, t - nc)

    def qc(t):
        return jnp.maximum(t - nc, 0)

    def rows(b, cc):
        return blk0 + b * nc + cc

    kernel = functools.partial(_ret_kernel, nc=nc, chunk=chunk)
    return pl.pallas_call(
        kernel,
        grid=(batch, 2 * nc),
        in_specs=[pl.BlockSpec(memory_space=pltpu.SMEM),
                  pl.BlockSpec((chunk, gw), lambda b, t: (rows(b, qc(t)), 0)),
                  pl.BlockSpec((chunk, gw), lambda b, t: (rows(b, kc(t)), 1)),
                  pl.BlockSpec((chunk, gw), lambda b, t: (rows(b, kc(t)), 2)),
                  pl.BlockSpec((chunk, gw), lambda b, t: (rows(b, qc(t)), 0)),
                  pl.BlockSpec(memory_space=pl.ANY)],
        out_specs=pl.BlockSpec((chunk, gw), lambda b, t: (rows(b, qc(t)), 0)),
        out_shape=jax.ShapeDtypeStruct(out.shape, out.dtype),
        input_output_aliases={5: 0},
        scratch_shapes=[pltpu.VMEM((n_pairs, LANES, LANES), F32),
                        pltpu.VMEM((n_pairs, LANES, LANES), F32),
                        pltpu.VMEM((nc, n_pairs, LANES, LANES), BF16),
                        pltpu.VMEM((n_pairs, 2 * chunk, chunk), F32),
                        pltpu.VMEM((n_pairs, 6, chunk, LANES), F32)],
        compiler_params=_cparams("parallel", "arbitrary"),
        name="retention",
    )(log_g, ret, ret, ret, gate, out)


def _bias_kernel(rb_ref, bucket_ref, o_ref):
    h = pl.program_id(0)
    bk = bucket_ref[...]
    out = jnp.zeros(bk.shape, F32)
    for n in range(N_BUCKETS):
        out = jnp.where(bk == n, rb_ref[n, h], out)
    o_ref[...] = out * LOG2E


def _bias_tiles(rel_bias, tile):
    assert tile > FAR_DISTANCE
    a = jnp.arange(tile, dtype=jnp.int32)[:, None]
    b = jnp.arange(tile, dtype=jnp.int32)[None, :]
    rel = jnp.stack([(d * tile + a - b) for d in range(-BIAS_REACH, BIAS_REACH + 1)])
    nb = N_BUCKETS // 2
    max_exact = nb // 2
    n = jnp.abs(rel)
    nf = jnp.maximum(n, 1).astype(F32)
    large = max_exact + (jnp.log(nf / max_exact) / math.log(MAX_DISTANCE / max_exact)
                         * (nb - max_exact)).astype(jnp.int32)
    large = jnp.minimum(large, nb - 1)
    bucket = jnp.where(rel > 0, nb, 0) + jnp.where(n < max_exact, n, large)
    return pl.pallas_call(
        _bias_kernel,
        grid=(N_HEADS, 2 * BIAS_REACH + 1),
        in_specs=[pl.BlockSpec(memory_space=pltpu.SMEM),
                  pl.BlockSpec((None, tile, tile), lambda h, d: (d, 0, 0))],
        out_specs=pl.BlockSpec((None, None, tile, tile), lambda h, d: (h, d, 0, 0)),
        out_shape=jax.ShapeDtypeStruct((N_HEADS, 2 * BIAS_REACH + 1, tile, tile), F32),
        compiler_params=_cparams("parallel", "arbitrary"),
        name="t5_bias_tiles",
    )(rel_bias.astype(F32), bucket.astype(jnp.int32))


def _attn_kernel(rb_ref, lamp_ref, dng_ref, q_ref, k_ref, vt_ref, bt_ref, _, o_ref,
                 a_ref, l_ref, p_ref, ka_ref, qa_ref, qb_ref, off_ref, kmax_ref, m_ref, *,
                 n_tiles, tile, macro, unroll):
    h = pl.program_id(1)
    qi = pl.program_id(2)
    hh = h % 2
    n = n_tiles
    half = DIFF_QK_DIM
    group_of_lane = lax.broadcasted_iota(jnp.int32, (LANES, LANES), 0) // half
    group_sum = (group_of_lane == lax.broadcasted_iota(jnp.int32, (LANES, LANES), 1)).astype(BF16)

    def half_norms(x):
        xf = x.astype(F32)
        return jnp.dot((xf * xf).astype(BF16), group_sum, preferred_element_type=F32)

    def this_head(x):
        lane = lax.broadcasted_iota(jnp.int32, x.shape, 1)
        shifted = jnp.where(hh == 0, x, pltpu.roll(x, HEAD_DIM, 1))
        return jnp.where(lane < HEAD_DIM, shifted, 0.0)

    def hi_lo(x):
        hi = x.astype(F8).astype(F32)
        return hi, (x - hi).astype(F8).astype(F32)

    @pl.when(qi == 0)
    def _keys():
        def body(j, mx):
            rows = pl.ds(pl.multiple_of(j * tile, tile), tile)
            kt = k_ref[rows, :]
            hi, lo = hi_lo(this_head(kt.astype(F32)) * FP8_K_SCALE)
            ka_ref[rows, :] = jnp.concatenate([hi + pltpu.roll(lo, HEAD_DIM, 1), hi], axis=1).astype(F8)
            return jnp.maximum(mx, jnp.max(half_norms(kt), axis=0, keepdims=True))
        kmax_ref[...] = lax.fori_loop(0, n, body, jnp.zeros((1, LANES), F32))

    qf = q_ref[...].astype(F32)
    lane_row = lax.broadcasted_iota(jnp.int32, (1, LANES), 1)
    row = lax.broadcasted_iota(jnp.int32, (LANES, tile), 0)
    group_sum_t = (lax.broadcasted_iota(jnp.int32, (LANES, LANES), 0)
                   == lax.broadcasted_iota(jnp.int32, (LANES, LANES), 1) // half).astype(BF16)
    qn_rows = lax.dot_general(group_sum_t, (qf * qf).astype(BF16), (((1,), (1,)), ((), ())),
                              preferred_element_type=F32)
    q_t = this_head(qf).T
    side_bias = (rb_ref[N_BUCKETS // 2 - 1, h], 0.0, rb_ref[N_BUCKETS - 1, h])
    in_range = True
    for t in range(2):
        g = 2 * hh + t
        qz = jnp.where((row >= t * half) & (row < (t + 1) * half), q_t, 0.0)
        hi, lo = hi_lo(qz * FP8_Q_SCALE)
        qa_ref[t] = jnp.concatenate([hi[:HEAD_DIM], hi[:HEAD_DIM], lo[:HEAD_DIM],
                                     jnp.zeros((HEAD_DIM, tile), F32)], axis=0).astype(F8)
        qb_ref[t] = jnp.where(hh == 0, qz, pltpu.roll(qz, HEAD_DIM, 0)).astype(BF16)
        qn2 = jnp.sum(jnp.where(row == g, qn_rows, 0.0), axis=0, keepdims=True)
        kn2 = jnp.max(jnp.where(lane_row == g, kmax_ref[...], 0.0), axis=1, keepdims=True)
        bound = jnp.sqrt(qn2 * kn2) + rb_ref[N_BUCKETS, h]
        for side in range(3):
            off_ref[t, side] = bound - side_bias[side]
        in_range = (in_range & (jnp.max(qn2) * FP8_Q_SCALE ** 2 <= FP8_RANGE ** 2)
                    & (jnp.max(kn2) * FP8_K_SCALE ** 2 <= FP8_RANGE ** 2))

    inv_scale = 1.0 / (FP8_Q_SCALE * FP8_K_SCALE)

    def near_any(m):
        j0 = m * macro
        return (qi >= j0 - (BIAS_REACH - 1)) & (qi <= j0 + macro - 1 + (BIAS_REACH - 1))

    def probabilities(m, slot, with_bias):
        j0 = m * macro
        ka = ka_ref[pl.ds(pl.multiple_of(j0 * tile, macro * tile), macro * tile), :]
        side = 1 if with_bias else jnp.where(j0 < qi, 0, 2)
        for t in range(2):
            s = jnp.dot(ka, qa_ref[t], preferred_element_type=F32)
            off = off_ref[t, side]
            for r in range(macro):
                sr = s[r * tile:(r + 1) * tile] * inv_scale - off
                if with_bias:
                    sr = sr + bt_ref[jnp.clip(j0 + r - qi, -BIAS_REACH, BIAS_REACH) + BIAS_REACH]
                e = jnp.exp2(sr)
                l_ref[t] += jnp.sum(e, axis=0, keepdims=True)
                p_ref[slot, t, r * tile:(r + 1) * tile, :] = e.astype(BF16)

    def accumulate(m, slot):
        vt = jnp.concatenate([vt_ref[m * macro + r] for r in range(macro)], axis=1)
        for t in range(2):
            a_ref[t] += jnp.dot(vt, p_ref[slot, t], preferred_element_type=F32)

    a_ref[...] = jnp.zeros_like(a_ref)
    l_ref[...] = jnp.zeros_like(l_ref)

    def group(b, last):
        m0 = unroll * b

        def run(with_bias):
            for k in range(unroll):
                if not (last and k == unroll - 1):
                    probabilities(m0 + k + 1, (k + 1) % 2, with_bias)
                accumulate(m0 + k, k % 2)

        any_near = functools.reduce(jnp.logical_or, [near_any(m0 + k + 1) for k in range(unroll)])
        lax.cond(any_near, lambda: run(True), lambda: run(False))

    def group_body(b, carry):
        group(b, False)
        return carry

    if n == macro:
        probabilities(0, 0, True)
        accumulate(0, 0)
    else:
        lax.cond(near_any(0), lambda: probabilities(0, 0, True),
                 lambda: probabilities(0, 0, False))
        n_groups = n // (macro * unroll)
        lax.fori_loop(0, n_groups - 1, group_body, 0)
        group(n_groups - 1, True)

    trusted = in_range & (jnp.min(jnp.minimum(l_ref[0], l_ref[1])) >= UNDERFLOW_GUARD)

    @pl.when(jnp.logical_not(trusted))
    def _exact_running_max():
        m_ref[...] = jnp.full(m_ref.shape, NEG_BIG, F32)
        a_ref[...] = jnp.zeros_like(a_ref)
        l_ref[...] = jnp.zeros_like(l_ref)

        def body(j, carry):
            kt = k_ref[pl.ds(pl.multiple_of(j * tile, tile), tile), :]
            vt = vt_ref[j]
            bias = bt_ref[jnp.clip(j - qi, -BIAS_REACH, BIAS_REACH) + BIAS_REACH]
            for t in range(2):
                s = jnp.dot(kt, qb_ref[t], preferred_element_type=F32) + bias
                m_prev = m_ref[t]
                m_new = jnp.maximum(m_prev, jnp.max(s, axis=0, keepdims=True))
                alpha = jnp.exp2(m_prev - m_new)
                e = jnp.exp2(s - m_new)
                l_ref[t] = alpha * l_ref[t] + jnp.sum(e, axis=0, keepdims=True)
                a_ref[t] = alpha * a_ref[t] + jnp.dot(vt, e.astype(BF16),
                                                      preferred_element_type=F32)
                m_ref[t] = m_new
            return carry

        lax.fori_loop(0, n, body, 0)

    lp = lamp_ref[...]
    lam_init = lp[4:5, 0:1]
    lam = (jnp.exp(jnp.sum(lp[0:1] * lp[1:2], axis=1, keepdims=True))
           - jnp.exp(jnp.sum(lp[2:3] * lp[3:4], axis=1, keepdims=True)) + lam_init)
    o = a_ref[0] / l_ref[0] - lam * (a_ref[1] / l_ref[1])
    ms = jnp.mean(o * o, axis=0, keepdims=True)
    y = o * lax.rsqrt(ms + HEAD_NORM_EPS) * dng_ref[...] * (1.0 - lam_init)
    o_ref[...] = y.astype(o_ref.dtype)


def _diff_attention(dqk, vt3, bias_t, rel_log2, lamp, dng, out, row0, batch, seq, tile):
    n = seq // tile
    macro = max(g for g in (1, 2, 4) if n % g == 0)
    n_macro = n // macro
    unroll = 1 if n_macro == 1 else 2
    assert n_macro % unroll == 0
    n_pairs = GROUP_WIDTH // LANES
    qblk0 = row0 // tile
    sblk0 = row0 // seq
    n_bias = 2 * BIAS_REACH + 1
    kernel = functools.partial(_attn_kernel, n_tiles=n, tile=tile, macro=macro, unroll=unroll)
    return pl.pallas_call(
        kernel,
        grid=(batch, N_HEADS, n),
        in_specs=[pl.BlockSpec(memory_space=pltpu.SMEM),
                  pl.BlockSpec((5, DIFF_QK_DIM), lambda b, h, i: (0, 0)),
                  pl.BlockSpec((HEAD_DIM, 1), lambda b, h, i: (0, 0)),
                  pl.BlockSpec((tile, LANES), lambda b, h, i: (qblk0 + b * n + i, h // 2)),
                  pl.BlockSpec((seq, LANES), lambda b, h, i: (sblk0 + b, n_pairs + h // 2)),
                  pl.BlockSpec((n, HEAD_DIM, tile), lambda b, h, i: (sblk0 + b, h, 0)),
                  pl.BlockSpec((None, n_bias, tile, tile), lambda b, h, i: (h, 0, 0, 0)),
                  pl.BlockSpec(memory_space=pl.ANY)],
        out_specs=pl.BlockSpec((None, HEAD_DIM, tile), lambda b, h, i: (qblk0 + b * n + i, h, 0)),
        out_shape=jax.ShapeDtypeStruct(out.shape, out.dtype),
        input_output_aliases={7: 0},
        scratch_shapes=[pltpu.VMEM((2, HEAD_DIM, tile), F32),
                        pltpu.VMEM((2, 1, tile), F32),
                        pltpu.VMEM((2, 2, macro * tile, tile), BF16),
                        pltpu.VMEM((seq, 2 * LANES), F8),
                        pltpu.VMEM((2, 2 * LANES, tile), F8),
                        pltpu.VMEM((2, LANES, tile), BF16),
                        pltpu.VMEM((2, 3, 1, tile), F32),
                        pltpu.VMEM((1, LANES), F32),
                        pltpu.VMEM((2, 1, tile), F32)],
        compiler_params=_cparams("parallel", "parallel", "arbitrary"),
        name="diff_attention",
    )(rel_log2, lamp, dng, dqk, dqk, vt3, bias_t, out)


def _layer_norm(y, g, b):
    mu = jnp.mean(y, axis=-1, keepdims=True)
    d = y - mu
    var = jnp.mean(d * d, axis=-1, keepdims=True)
    return d * lax.rsqrt(var + LN_EPS) * g + b


def _outproj_kernel(yr_ref, yd_ref, w_ref, x_ref, g_ref, b_ref, o_ref, ob_ref, *, alpha):
    mix = jnp.dot(yr_ref[...], w_ref[:GROUP_WIDTH, :], preferred_element_type=F32)
    mix = mix + lax.dot_general(yd_ref[...], w_ref[GROUP_WIDTH:, :], (((0,), (0,)), ((), ())),
                                preferred_element_type=F32)
    y = _layer_norm(alpha * x_ref[...] + mix, g_ref[...], b_ref[...])
    o_ref[...] = y
    ob_ref[...] = y.astype(BF16)


def _out_projection(yr, ydt, w_out, x, g, b, alpha, tm):
    t = x.shape[0]
    assert ydt.shape == (t // tm, GROUP_WIDTH, tm)
    kernel = functools.partial(_outproj_kernel, alpha=alpha)
    row = lambda i: (i, 0)
    fixed = lambda i: (0, 0)
    return pl.pallas_call(
        kernel,
        grid=(t // tm,),
        in_specs=[pl.BlockSpec((tm, GROUP_WIDTH), row),
                  pl.BlockSpec((None, GROUP_WIDTH, tm), lambda i: (i, 0, 0)),
                  pl.BlockSpec((2 * GROUP_WIDTH, D_MODEL), fixed),
                  pl.BlockSpec((tm, D_MODEL), row),
                  pl.BlockSpec((1, D_MODEL), fixed),
                  pl.BlockSpec((1, D_MODEL), fixed)],
        out_specs=[pl.BlockSpec((tm, D_MODEL), row), pl.BlockSpec((tm, D_MODEL), row)],
        out_shape=[jax.ShapeDtypeStruct((t, D_MODEL), F32), jax.ShapeDtypeStruct((t, D_MODEL), BF16)],
        compiler_params=_cparams("parallel"),
        name="out_proj_ln",
    )(yr, ydt, w_out, x, g, b)


def _ffn_kernel(x_ref, xb_ref, xp_ref, xn_ref, wa_ref, wv_ref, wd_ref, cp_ref, g_ref, b_ref,
                o_ref, ob_ref, acc_ref, *, alpha, tm, n_chunks, starts, ends):
    i = pl.program_id(0)
    t0 = i * tm
    is_start = functools.reduce(jnp.logical_or, [t0 == s for s in starts])
    is_end = functools.reduce(jnp.logical_or, [t0 + tm == e for e in ends])
    keep_prev = jnp.where(is_start, 0.0, 1.0)
    keep_next = jnp.where(is_end, 0.0, 1.0)

    xb = xb_ref[...]
    hrows = xp_ref.shape[0]
    x_ext = jnp.concatenate([xb, xp_ref[...], xn_ref[...]], axis=0)
    acc_ref[...] = jnp.zeros_like(acc_ref)

    def chunk_body(c, carry):
        a_ext = jnp.dot(x_ext, wa_ref[c], preferred_element_type=F32)
        a = a_ext[:tm]
        ah = a_ext[tm:]
        val = jnp.dot(xb, wv_ref[c], preferred_element_type=F32)
        prev_row = ah[hrows - 1:hrows] * keep_prev
        next_row = ah[hrows:hrows + 1] * keep_next
        row = lax.broadcasted_iota(jnp.int32, a.shape, 0)
        a_m1 = jnp.where(row == 0, prev_row, pltpu.roll(a, 1, 0))
        a_p1 = jnp.where(row == tm - 1, next_row, pltpu.roll(a, tm - 1, 0))
        cp = cp_ref[c]
        conv = cp[3:4] + a_m1 * cp[0:1]
        conv = conv + a * cp[1:2]
        conv = conv + a_p1 * cp[2:3]
        gelu = 0.5 * conv * (1.0 + lax.erf(conv * (1.0 / math.sqrt(2.0))))
        hidden = (gelu * val).astype(BF16)
        acc_ref[...] += jnp.dot(hidden, wd_ref[c], preferred_element_type=F32)
        return carry

    for c in range(n_chunks):
        chunk_body(c, 0)
    y = _layer_norm(alpha * x_ref[...] + acc_ref[...], g_ref[...], b_ref[...])
    o_ref[...] = y
    ob_ref[...] = y.astype(BF16)


def _conv_glu(x, xb, wa, wv, wd, cp, g, b, alpha, tm, groups):
    t = x.shape[0]
    n_chunks, _, ck = wa.shape
    hrows = 16
    starts = tuple(r0 + bi * s for (r0, nb, s) in groups for bi in range(nb))
    ends = tuple(r0 + (bi + 1) * s for (r0, nb, s) in groups for bi in range(nb))
    kernel = functools.partial(_ffn_kernel, alpha=alpha, tm=tm, n_chunks=n_chunks,
                               starts=starts, ends=ends)
    row = lambda i: (i, 0)
    fixed2 = lambda i: (0, 0)
    fixed3 = lambda i: (0, 0, 0)
    per = tm // hrows
    last = t // hrows - 1
    return pl.pallas_call(
        kernel,
        grid=(t // tm,),
        in_specs=[pl.BlockSpec((tm, D_MODEL), row),
                  pl.BlockSpec((tm, D_MODEL), row),
                  pl.BlockSpec((hrows, D_MODEL), lambda i: (jnp.maximum(i * per - 1, 0), 0)),
                  pl.BlockSpec((hrows, D_MODEL), lambda i: (jnp.minimum((i + 1) * per, last), 0)),
                  pl.BlockSpec((n_chunks, D_MODEL, ck), fixed3, pipeline_mode=pl.Buffered(1)),
                  pl.BlockSpec((n_chunks, D_MODEL, ck), fixed3, pipeline_mode=pl.Buffered(1)),
                  pl.BlockSpec((n_chunks, ck, D_MODEL), fixed3, pipeline_mode=pl.Buffered(1)),
                  pl.BlockSpec((n_chunks, 8, ck), fixed3),
                  pl.BlockSpec((1, D_MODEL), fixed2),
                  pl.BlockSpec((1, D_MODEL), fixed2)],
        out_specs=[pl.BlockSpec((tm, D_MODEL), row), pl.BlockSpec((tm, D_MODEL), row)],
        out_shape=[jax.ShapeDtypeStruct((t, D_MODEL), F32), jax.ShapeDtypeStruct((t, D_MODEL), BF16)],
        scratch_shapes=[pltpu.VMEM((tm, D_MODEL), F32)],
        compiler_params=_cparams("parallel"),
        name="conv_glu_ln",
    )(x, xb, xb, xb, wa, wv, wd, cp, g, b)


def _rotary_tables(seq):
    d = HEAD_DIM
    inv = 1.0 / (ROPE_BASE ** (jnp.arange(0, d, 2, dtype=F32) / d))
    ang = jnp.arange(seq, dtype=F32)[:, None] * inv[None, :]
    cos, sin = jnp.cos(ang), jnp.sin(ang)
    cos_t = jnp.concatenate([cos, cos, cos, cos], axis=-1)
    sin_t = jnp.concatenate([-sin, sin, -sin, sin], axis=-1)
    return cos_t, sin_t


def _tiles(groups):
    smin = min(s for (_, _, s) in groups)
    attn_tile = min(512, smin // 2)
    chunk = min(256, smin // 2)
    tm = min(512, smin // 2)
    ffn_tm = min(1024, smin // 2)
    return attn_tile, chunk, tm, ffn_tm


def _forward(x, groups, w_in, ret_decay_logit, rel_bias, lambda_q1, lambda_k1, lambda_q2,
             lambda_k2, diff_norm_g, w_out, ln_g, ln_b, w_up, conv_w, conv_b, w_down):
    depth = w_in.shape[0]
    alpha = (2 * depth) ** 0.25
    t = x.shape[0]
    attn_tile, chunk, tm, ffn_tm = _tiles(groups)
    smax = max(s for (_, _, s) in groups)
    gw = GROUP_WIDTH
    ck = 256
    n_chunks = D_FF // ck

    cos_t, sin_t = _rotary_tables(smax)
    bias_t = _bias_tiles(rel_bias, attn_tile)
    rel_log2 = rel_bias.astype(F32) * LOG2E
    rel_log2 = jnp.concatenate([rel_log2, jnp.max(rel_log2, axis=0, keepdims=True)], axis=0)
    in_scale = jnp.concatenate([
        jnp.ones((gw,), F32), jnp.full((gw,), HEAD_DIM ** -0.5, F32), jnp.ones((2 * gw,), F32),
        jnp.full((gw,), DIFF_QK_DIM ** -0.5 * LOG2E, F32), jnp.ones((gw,), F32)])[None, :]
    assert tm == attn_tile

    xb = x.astype(BF16)
    for l in range(depth):
        lam_init = 0.8 - 0.6 * math.exp(-0.3 * l)
        w_in_b = w_in[l].astype(BF16)
        ret, gate, dqk, vt3 = _project(xb, w_in_b[:, :6 * gw], w_in_b[:, 6 * gw:].T, in_scale,
                                       cos_t, sin_t, groups, tm)

        log_g = jax.nn.log_sigmoid(ret_decay_logit[l].astype(F32))
        lamp = jnp.stack([lambda_q1[l], lambda_k1[l], lambda_q2[l], lambda_k2[l],
                          jnp.full((DIFF_QK_DIM,), lam_init)]).astype(F32)
        dng = diff_norm_g[l].astype(F32)[:, None]

        yr = jnp.zeros((t, gw), BF16)
        ydt = jnp.zeros((t // attn_tile, gw, attn_tile), BF16)
        for (row0, batch, seq) in groups:
            yr = _retention(ret, gate, log_g, yr, row0, batch, seq, chunk)
            ydt = _diff_attention(dqk, vt3, bias_t, rel_log2, lamp, dng, ydt, row0, batch, seq,
                                  attn_tile)

        x, xb = _out_projection(yr, ydt, w_out[l].astype(BF16), x, ln_g[l, 0][None, :].astype(F32),
                                ln_b[l, 0][None, :].astype(F32), alpha, tm)

        wa = w_up[l][:, :D_FF].astype(BF16).reshape(D_MODEL, n_chunks, ck).transpose(1, 0, 2)
        wv = w_up[l][:, D_FF:].astype(BF16).reshape(D_MODEL, n_chunks, ck).transpose(1, 0, 2)
        wd = w_down[l].astype(BF16).reshape(n_chunks, ck, D_MODEL)
        cp = jnp.concatenate([conv_w[l].astype(F32), conv_b[l].astype(F32)[None, :],
                              jnp.zeros((4, D_FF), F32)], axis=0)
        cp = cp.reshape(8, n_chunks, ck).transpose(1, 0, 2)
        x, xb = _conv_glu(x, xb, wa, wv, wd, cp, ln_g[l, 1][None, :].astype(F32),
                          ln_b[l, 1][None, :].astype(F32), alpha, ffn_tm, groups)
    return x


def kernel(x_prompt, x_sample, w_in, ret_decay_logit, rel_bias, lambda_q1, lambda_k1, lambda_q2,
           lambda_k2, diff_norm_g, w_out, ln_g, ln_b, w_up, conv_w, conv_b, w_down):
    bp, sp, d = x_prompt.shape
    bs, ss, _ = x_sample.shape
    groups = ((0, bp, sp), (bp * sp, bs, ss))
    x = jnp.concatenate([x_prompt.reshape(bp * sp, d), x_sample.reshape(bs * ss, d)], axis=0)
    y = _forward(x.astype(F32), groups, w_in, ret_decay_logit, rel_bias, lambda_q1, lambda_k1,
                 lambda_q2, lambda_k2, diff_norm_g, w_out, ln_g, ln_b, w_up, conv_w, conv_b, w_down)
    y_prompt = y[:bp * sp].reshape(bp, sp, d).astype(x_prompt.dtype)
    y_sample = y[bp * sp:].reshape(bs, ss, d).astype(x_sample.dtype)
    return y_prompt, y_sample
```

```python
import functools
import math

import jax
import jax.numpy as jnp
from jax import lax
from jax.experimental import pallas as pl
from jax.experimental.pallas import tpu as pltpu

D_MODEL = 1024
HEAD_DIM = 64
N_HEADS = 8
GROUP_WIDTH = N_HEADS * HEAD_DIM
DIFF_QK_DIM = HEAD_DIM // 2
D_FF = 2816
N_BUCKETS = 32
MAX_DISTANCE = 128
ROPE_BASE = 10000.0
LN_EPS = 1e-5
HEAD_NORM_EPS = 1e-6
LANES = 128
FAR_DISTANCE = 91
BIAS_REACH = 2
LOG2E = math.log2(math.e)
NEG_BIG = -1e30
UNDERFLOW_GUARD = 2.0 ** -90
VMEM_LIMIT = 56 * 1024 * 1024

F32 = jnp.float32
BF16 = jnp.bfloat16


def _cparams(*sem):
    return pltpu.CompilerParams(dimension_semantics=sem, vmem_limit_bytes=VMEM_LIMIT)


def _proj_kernel(x_ref, w_ref, wvt_ref, s_ref, cos_ref, sin_ref, ret_ref, gate_ref, dqk_ref, vt_ref):
    gw = GROUP_WIDTH
    x = x_ref[...]
    tm = x.shape[0]
    lane = lax.broadcasted_iota(jnp.int32, (tm, LANES), 1)
    low_half = (lane % HEAD_DIM) < (HEAD_DIM // 2)
    cos = cos_ref[...]
    sin = sin_ref[...]
    qk = jnp.dot(x, w_ref[:, :2 * gw], preferred_element_type=F32) * s_ref[:, :2 * gw]
    for p in range(2 * gw // LANES):
        sl = slice(p * LANES, (p + 1) * LANES)
        xx = qk[:, sl]
        swapped = jnp.where(low_half, pltpu.roll(xx, LANES - HEAD_DIM // 2, 1),
                            pltpu.roll(xx, HEAD_DIM // 2, 1))
        ret_ref[:, sl] = (xx * cos + swapped * sin).astype(ret_ref.dtype)
    ret_ref[:, 2 * gw:] = jnp.dot(x, w_ref[:, 2 * gw:3 * gw],
                                  preferred_element_type=F32).astype(ret_ref.dtype)
    gate_ref[...] = jnp.dot(x, w_ref[:, 3 * gw:4 * gw], preferred_element_type=F32)
    dqk_ref[...] = (jnp.dot(x, w_ref[:, 4 * gw:], preferred_element_type=F32)
                    * s_ref[:, 4 * gw:]).astype(dqk_ref.dtype)
    vt_ref[...] = lax.dot_general(wvt_ref[...], x, (((1,), (1,)), ((), ())),
                                  preferred_element_type=F32).astype(vt_ref.dtype)


def _position_block(i, tm, groups):
    t0 = i * tm
    blk = t0 // tm
    for (row0, _, seq) in groups:
        blk = jnp.where(t0 >= row0, ((t0 - row0) % seq) // tm, blk)
    return blk


def _project(xb, w, w_vt, scale, cos_t, sin_t, groups, tm):
    t, k = xb.shape
    gw = GROUP_WIDTH
    row = lambda i: (i, 0)
    fixed = lambda i: (0, 0)
    pos = lambda i: (_position_block(i, tm, groups), 0)
    return pl.pallas_call(
        _proj_kernel,
        grid=(t // tm,),
        in_specs=[pl.BlockSpec((tm, k), row),
                  pl.BlockSpec((k, 6 * gw), fixed, pipeline_mode=pl.Buffered(1)),
                  pl.BlockSpec((gw, k), fixed, pipeline_mode=pl.Buffered(1)),
                  pl.BlockSpec((1, 6 * gw), fixed),
                  pl.BlockSpec((tm, LANES), pos),
                  pl.BlockSpec((tm, LANES), pos)],
        out_specs=[pl.BlockSpec((tm, 3 * gw), row), pl.BlockSpec((tm, gw), row),
                   pl.BlockSpec((tm, 2 * gw), row),
                   pl.BlockSpec((None, gw, tm), lambda i: (i, 0, 0))],
        out_shape=[jax.ShapeDtypeStruct((t, 3 * gw), BF16), jax.ShapeDtypeStruct((t, gw), F32),
                   jax.ShapeDtypeStruct((t, 2 * gw), BF16),
                   jax.ShapeDtypeStruct((t // tm, gw, tm), BF16)],
        compiler_params=_cparams("parallel"),
        name="in_proj",
    )(xb, w, w_vt, scale, cos_t, sin_t)


def _ret_kernel(lg_ref, q_ref, k_ref, v_ref, g_ref, _, o_ref,
                rf_ref, rb_ref, rnext_ref, mask_ref, tab_ref, *, nc, chunk):
    t = pl.program_id(1)
    c = chunk
    n_pairs = GROUP_WIDTH // LANES
    lane = lax.broadcasted_iota(jnp.int32, (c, LANES), 1)
    head0 = lane < HEAD_DIM
    r_i = lax.broadcasted_iota(jnp.int32, (LANES, LANES), 0) // HEAD_DIM
    c_i = lax.broadcasted_iota(jnp.int32, (LANES, LANES), 1) // HEAD_DIM
    same_head = r_i == c_i

    @pl.when(t == 0)
    def _init():
        rb_ref[...] = jnp.zeros_like(rb_ref)
        qi = lax.broadcasted_iota(jnp.int32, (c, c), 0)
        ki = lax.broadcasted_iota(jnp.int32, (c, c), 1)
        diff = (qi - ki).astype(F32)
        pos = lax.broadcasted_iota(jnp.int32, (c, LANES), 0).astype(F32)
        for p in range(n_pairs):
            for hh in range(2):
                lf = lg_ref[0, 2 * p + hh]
                lb = lg_ref[1, 2 * p + hh]
                mask_ref[p, hh * c:(hh + 1) * c, :] = jnp.where(
                    diff >= 0, jnp.exp(lf * jnp.maximum(diff, 0.0)),
                    jnp.exp(lb * jnp.maximum(-diff, 0.0)))
            lfl = jnp.where(head0, lg_ref[0, 2 * p], lg_ref[0, 2 * p + 1])
            lbl = jnp.where(head0, lg_ref[1, 2 * p], lg_ref[1, 2 * p + 1])
            tab_ref[p, 0] = jnp.exp(lfl * (c - 1 - pos))
            tab_ref[p, 1] = jnp.exp(lbl * pos)
            tab_ref[p, 2] = jnp.exp(lfl * (pos + 1.0))
            tab_ref[p, 3] = jnp.exp(lbl * (c - pos))
            tab_ref[p, 4] = jnp.exp(lfl * c)
            tab_ref[p, 5] = jnp.exp(lbl * c)

    def summary(k, w, v):
        kw = (k.astype(F32) * w).astype(BF16)
        kv = lax.dot_general(kw, v, (((0,), (0,)), ((), ())), preferred_element_type=F32)
        return jnp.where(same_head, kv, 0.0)

    @pl.when(t < nc)
    def _backward():
        for p in range(n_pairs):
            sl = slice(p * LANES, (p + 1) * LANES)
            rnext_ref[nc - 1 - t, p] = rb_ref[p].astype(BF16)
            rb_ref[p] = (tab_ref[p, 5][:LANES] * rb_ref[p]
                         + summary(k_ref[:, sl], tab_ref[p, 1], v_ref[:, sl]))

    @pl.when(t >= nc)
    def _forward():
        @pl.when(t == nc)
        def _():
            rf_ref[...] = jnp.zeros_like(rf_ref)

        for p in range(n_pairs):
            sl = slice(p * LANES, (p + 1) * LANES)
            q = q_ref[:, sl]
            k = k_ref[:, sl]
            v = v_ref[:, sl]
            zero = jnp.zeros_like(q)
            q2 = jnp.concatenate([jnp.where(head0, q, zero), jnp.where(head0, zero, q)], axis=0)
            s = lax.dot_general(q2, k, (((1,), (1,)), ((), ())), preferred_element_type=F32)
            o2 = jnp.dot((s * mask_ref[p]).astype(BF16), v, preferred_element_type=F32)
            y = jnp.where(head0, o2[:c], o2[c:])
            y = y + jnp.dot(q, rf_ref[p].astype(BF16), preferred_element_type=F32) * tab_ref[p, 2]
            y = y + jnp.dot(q, rnext_ref[t - nc, p], preferred_element_type=F32) * tab_ref[p, 3]

            sq = y * y
            s0 = jnp.sum(jnp.where(head0, sq, 0.0), axis=1, keepdims=True)
            s1 = jnp.sum(jnp.where(head0, 0.0, sq), axis=1, keepdims=True)
            ms = jnp.where(head0, s0, s1) * (1.0 / HEAD_DIM)
            y = y * lax.rsqrt(ms + HEAD_NORM_EPS)
            g = g_ref[:, sl]
            o_ref[:, sl] = (g / (1.0 + jnp.exp(-g)) * y).astype(o_ref.dtype)

            rf_ref[p] = tab_ref[p, 4][:LANES] * rf_ref[p] + summary(k, tab_ref[p, 0], v)


def _retention(ret, gate, log_g, out, row0, batch, seq, chunk):
    nc = seq // chunk
    blk0 = row0 // chunk
    n_pairs = GROUP_WIDTH // LANES
    gw = GROUP_WIDTH

    def kc(t):
        return jnp.where(t < nc, nc - 1 - t, t - nc)

    def qc(t):
        return jnp.maximum(t - nc, 0)

    def rows(b, cc):
        return blk0 + b * nc + cc

    kernel = functools.partial(_ret_kernel, nc=nc, chunk=chunk)
    return pl.pallas_call(
        kernel,
        grid=(batch, 2 * nc),
        in_specs=[pl.BlockSpec(memory_space=pltpu.SMEM),
                  pl.BlockSpec((chunk, gw), lambda b, t: (rows(b, qc(t)), 0)),
                  pl.BlockSpec((chunk, gw), lambda b, t: (rows(b, kc(t)), 1)),
                  pl.BlockSpec((chunk, gw), lambda b, t: (rows(b, kc(t)), 2)),
                  pl.BlockSpec((chunk, gw), lambda b, t: (rows(b, qc(t)), 0)),
                  pl.BlockSpec(memory_space=pl.ANY)],
        out_specs=pl.BlockSpec((chunk, gw), lambda b, t: (rows(b, qc(t)), 0)),
        out_shape=jax.ShapeDtypeStruct(out.shape, out.dtype),
        input_output_aliases={5: 0},
        scratch_shapes=[pltpu.VMEM((n_pairs, LANES, LANES), F32),
                        pltpu.VMEM((n_pairs, LANES, LANES), F32),
                        pltpu.VMEM((nc, n_pairs, LANES, LANES), BF16),
                        pltpu.VMEM((n_pairs, 2 * chunk, chunk), F32),
                        pltpu.VMEM((n_pairs, 6, chunk, LANES), F32)],
        compiler_params=_cparams("parallel", "arbitrary"),
        name="retention",
    )(log_g, ret, ret, ret, gate, out)


def _bias_kernel(rb_ref, bucket_ref, o_ref):
    h = pl.program_id(0)
    bk = bucket_ref[...]
    out = jnp.zeros(bk.shape, F32)
    for n in range(N_BUCKETS):
        out = jnp.where(bk == n, rb_ref[n, h], out)
    o_ref[...] = out * LOG2E


def _bias_tiles(rel_bias, tile):
    assert tile > FAR_DISTANCE
    a = jnp.arange(tile, dtype=jnp.int32)[:, None]
    b = jnp.arange(tile, dtype=jnp.int32)[None, :]
    rel = jnp.stack([(d * tile + a - b) for d in range(-BIAS_REACH, BIAS_REACH + 1)])
    nb = N_BUCKETS // 2
    max_exact = nb // 2
    n = jnp.abs(rel)
    nf = jnp.maximum(n, 1).astype(F32)
    large = max_exact + (jnp.log(nf / max_exact) / math.log(MAX_DISTANCE / max_exact)
                         * (nb - max_exact)).astype(jnp.int32)
    large = jnp.minimum(large, nb - 1)
    bucket = jnp.where(rel > 0, nb, 0) + jnp.where(n < max_exact, n, large)
    return pl.pallas_call(
        _bias_kernel,
        grid=(N_HEADS, 2 * BIAS_REACH + 1),
        in_specs=[pl.BlockSpec(memory_space=pltpu.SMEM),
                  pl.BlockSpec((None, tile, tile), lambda h, d: (d, 0, 0))],
        out_specs=pl.BlockSpec((None, None, tile, tile), lambda h, d: (h, d, 0, 0)),
        out_shape=jax.ShapeDtypeStruct((N_HEADS, 2 * BIAS_REACH + 1, tile, tile), F32),
        compiler_params=_cparams("parallel", "arbitrary"),
        name="t5_bias_tiles",
    )(rel_bias.astype(F32), bucket.astype(jnp.int32))


def _attn_kernel(rb_ref, lamp_ref, dng_ref, q_ref, k_ref, vt_ref, bt_ref, _, o_ref,
                 a_ref, l_ref, p_ref, qa_ref, kmax_ref, m_ref, *, n_tiles, tile, macro, unroll):
    h = pl.program_id(1)
    qi = pl.program_id(2)
    hh = h % 2
    n = n_tiles
    half = DIFF_QK_DIM
    group_of_lane = lax.broadcasted_iota(jnp.int32, (LANES, LANES), 0) // half
    group_sum = (group_of_lane == lax.broadcasted_iota(jnp.int32, (LANES, LANES), 1)).astype(BF16)

    def half_norms(x):
        xf = x.astype(F32)
        return jnp.dot((xf * xf).astype(BF16), group_sum, preferred_element_type=F32)

    @pl.when(qi == 0)
    def _key_norms():
        def body(j, mx):
            kt = k_ref[pl.ds(pl.multiple_of(j * tile, tile), tile), :]
            return jnp.maximum(mx, jnp.max(half_norms(kt), axis=0, keepdims=True))
        kmax_ref[...] = lax.fori_loop(0, n, body, jnp.zeros((1, LANES), F32))

    qf = q_ref[...].astype(F32)
    lane = lax.broadcasted_iota(jnp.int32, qf.shape, 1)
    lane_row = lax.broadcasted_iota(jnp.int32, (1, LANES), 1)
    row = lax.broadcasted_iota(jnp.int32, (LANES, tile), 0)
    group_sum_t = (lax.broadcasted_iota(jnp.int32, (LANES, LANES), 0)
                   == lax.broadcasted_iota(jnp.int32, (LANES, LANES), 1) // half).astype(BF16)
    qn_rows = lax.dot_general(group_sum_t, (qf * qf).astype(BF16), (((1,), (1,)), ((), ())),
                              preferred_element_type=F32)
    side_bias = (rb_ref[N_BUCKETS // 2 - 1, h], 0.0, rb_ref[N_BUCKETS - 1, h])
    sides = (1,) if n == macro else (0, 1, 2)
    for t in range(2):
        g = 2 * hh + t
        lo_lane = g * half
        qzt = jnp.where((lane >= lo_lane) & (lane < lo_lane + half), qf, 0.0).T.astype(BF16)
        qn2 = jnp.sum(jnp.where(row == g, qn_rows, 0.0), axis=0, keepdims=True)
        kn2 = jnp.max(jnp.where(lane_row == g, kmax_ref[...], 0.0), axis=1, keepdims=True)
        bound = jnp.sqrt(qn2 * kn2) + rb_ref[N_BUCKETS, h]
        for side in sides:
            v = side_bias[side] - bound
            hi = v.astype(BF16).astype(F32)
            lo = (v - hi).astype(BF16).astype(F32)
            qa_ref[t, side, :LANES, :] = qzt
            qa_ref[t, side, LANES:, :] = jnp.where(row == 0, hi, jnp.where(row == 1, lo, 0.0)).astype(BF16)

    ones = jnp.ones((macro * tile, LANES), BF16)

    def near_any(m):
        j0 = m * macro
        return (qi >= j0 - (BIAS_REACH - 1)) & (qi <= j0 + macro - 1 + (BIAS_REACH - 1))

    def probabilities(m, slot, with_bias):
        j0 = m * macro
        kt = k_ref[pl.ds(pl.multiple_of(j0 * tile, macro * tile), macro * tile), :]
        kaug = jnp.concatenate([kt, ones], axis=1)
        side = 1 if with_bias else jnp.where(j0 < qi, 0, 2)
        for t in range(2):
            s = jnp.dot(kaug, qa_ref[t, side], preferred_element_type=F32)
            for r in range(macro):
                sr = s[r * tile:(r + 1) * tile]
                if with_bias:
                    sr = sr + bt_ref[jnp.clip(j0 + r - qi, -BIAS_REACH, BIAS_REACH) + BIAS_REACH]
                e = jnp.exp2(sr)
                l_ref[t] += jnp.sum(e, axis=0, keepdims=True)
                p_ref[slot, t, r * tile:(r + 1) * tile, :] = e.astype(BF16)

    def accumulate(m, slot):
        vt = jnp.concatenate([vt_ref[m * macro + r] for r in range(macro)], axis=1)
        for t in range(2):
            a_ref[t] += jnp.dot(vt, p_ref[slot, t], preferred_element_type=F32)

    a_ref[...] = jnp.zeros_like(a_ref)
    l_ref[...] = jnp.zeros_like(l_ref)

    def group(b, last):
        m0 = unroll * b

        def run(with_bias):
            for k in range(unroll):
                if not (last and k == unroll - 1):
                    probabilities(m0 + k + 1, (k + 1) % 2, with_bias)
                accumulate(m0 + k, k % 2)

        any_near = functools.reduce(jnp.logical_or, [near_any(m0 + k + 1) for k in range(unroll)])
        lax.cond(any_near, lambda: run(True), lambda: run(False))

    def group_body(b, carry):
        group(b, False)
        return carry

    if n == macro:
        probabilities(0, 0, True)
        accumulate(0, 0)
    else:
        lax.cond(near_any(0), lambda: probabilities(0, 0, True),
                 lambda: probabilities(0, 0, False))
        n_groups = n // (macro * unroll)
        lax.fori_loop(0, n_groups - 1, group_body, 0)
        group(n_groups - 1, True)

    trusted = jnp.min(jnp.minimum(l_ref[0], l_ref[1])) >= UNDERFLOW_GUARD

    @pl.when(jnp.logical_not(trusted))
    def _exact_running_max():
        m_ref[...] = jnp.full(m_ref.shape, NEG_BIG, F32)
        a_ref[...] = jnp.zeros_like(a_ref)
        l_ref[...] = jnp.zeros_like(l_ref)

        def body(j, carry):
            kt = k_ref[pl.ds(pl.multiple_of(j * tile, tile), tile), :]
            vt = vt_ref[j]
            bias = bt_ref[jnp.clip(j - qi, -BIAS_REACH, BIAS_REACH) + BIAS_REACH]
            for t in range(2):
                s = jnp.dot(kt, qa_ref[t, 1, :LANES, :], preferred_element_type=F32) + bias
                m_prev = m_ref[t]
                m_new = jnp.maximum(m_prev, jnp.max(s, axis=0, keepdims=True))
                alpha = jnp.exp2(m_prev - m_new)
                e = jnp.exp2(s - m_new)
                l_ref[t] = alpha * l_ref[t] + jnp.sum(e, axis=0, keepdims=True)
                a_ref[t] = alpha * a_ref[t] + jnp.dot(vt, e.astype(BF16),
                                                      preferred_element_type=F32)
                m_ref[t] = m_new
            return carry

        lax.fori_loop(0, n, body, 0)

    lp = lamp_ref[...]
    lam_init = lp[4:5, 0:1]
    lam = (jnp.exp(jnp.sum(lp[0:1] * lp[1:2], axis=1, keepdims=True))
           - jnp.exp(jnp.sum(lp[2:3] * lp[3:4], axis=1, keepdims=True)) + lam_init)
    o = a_ref[0] / l_ref[0] - lam * (a_ref[1] / l_ref[1])
    ms = jnp.mean(o * o, axis=0, keepdims=True)
    y = o * lax.rsqrt(ms + HEAD_NORM_EPS) * dng_ref[...] * (1.0 - lam_init)
    o_ref[...] = y.astype(o_ref.dtype)


def _diff_attention(dqk, vt3, bias_t, rel_log2, lamp, dng, out, row0, batch, seq, tile):
    n = seq // tile
    macro = max(g for g in (1, 2, 4) if n % g == 0)
    n_macro = n // macro
    unroll = 1 if n_macro == 1 else 2
    assert n_macro % unroll == 0
    n_pairs = GROUP_WIDTH // LANES
    qblk0 = row0 // tile
    sblk0 = row0 // seq
    n_bias = 2 * BIAS_REACH + 1
    kernel = functools.partial(_attn_kernel, n_tiles=n, tile=tile, macro=macro, unroll=unroll)
    return pl.pallas_call(
        kernel,
        grid=(batch, N_HEADS, n),
        in_specs=[pl.BlockSpec(memory_space=pltpu.SMEM),
                  pl.BlockSpec((5, DIFF_QK_DIM), lambda b, h, i: (0, 0)),
                  pl.BlockSpec((HEAD_DIM, 1), lambda b, h, i: (0, 0)),
                  pl.BlockSpec((tile, LANES), lambda b, h, i: (qblk0 + b * n + i, h // 2)),
                  pl.BlockSpec((seq, LANES), lambda b, h, i: (sblk0 + b, n_pairs + h // 2)),
                  pl.BlockSpec((n, HEAD_DIM, tile), lambda b, h, i: (sblk0 + b, h, 0)),
                  pl.BlockSpec((None, n_bias, tile, tile), lambda b, h, i: (h, 0, 0, 0)),
                  pl.BlockSpec(memory_space=pl.ANY)],
        out_specs=pl.BlockSpec((None, HEAD_DIM, tile), lambda b, h, i: (qblk0 + b * n + i, h, 0)),
        out_shape=jax.ShapeDtypeStruct(out.shape, out.dtype),
        input_output_aliases={7: 0},
        scratch_shapes=[pltpu.VMEM((2, HEAD_DIM, tile), F32),
                        pltpu.VMEM((2, 1, tile), F32),
                        pltpu.VMEM((2, 2, macro * tile, tile), BF16),
                        pltpu.VMEM((2, 3, 2 * LANES, tile), BF16),
                        pltpu.VMEM((1, LANES), F32),
                        pltpu.VMEM((2, 1, tile), F32)],
        compiler_params=_cparams("parallel", "parallel", "arbitrary"),
        name="diff_attention",
    )(rel_log2, lamp, dng, dqk, dqk, vt3, bias_t, out)


def _layer_norm(y, g, b):
    mu = jnp.mean(y, axis=-1, keepdims=True)
    d = y - mu
    var = jnp.mean(d * d, axis=-1, keepdims=True)
    return d * lax.rsqrt(var + LN_EPS) * g + b


def _outproj_kernel(yr_ref, yd_ref, w_ref, x_ref, g_ref, b_ref, o_ref, ob_ref, *, alpha):
    mix = jnp.dot(yr_ref[...], w_ref[:GROUP_WIDTH, :], preferred_element_type=F32)
    mix = mix + lax.dot_general(yd_ref[...], w_ref[GROUP_WIDTH:, :], (((0,), (0,)), ((), ())),
                                preferred_element_type=F32)
    y = _layer_norm(alpha * x_ref[...] + mix, g_ref[...], b_ref[...])
    o_ref[...] = y
    ob_ref[...] = y.astype(BF16)


def _out_projection(yr, ydt, w_out, x, g, b, alpha, tm):
    t = x.shape[0]
    assert ydt.shape == (t // tm, GROUP_WIDTH, tm)
    kernel = functools.partial(_outproj_kernel, alpha=alpha)
    row = lambda i: (i, 0)
    fixed = lambda i: (0, 0)
    return pl.pallas_call(
        kernel,
        grid=(t // tm,),
        in_specs=[pl.BlockSpec((tm, GROUP_WIDTH), row),
                  pl.BlockSpec((None, GROUP_WIDTH, tm), lambda i: (i, 0, 0)),
                  pl.BlockSpec((2 * GROUP_WIDTH, D_MODEL), fixed),
                  pl.BlockSpec((tm, D_MODEL), row),
                  pl.BlockSpec((1, D_MODEL), fixed),
                  pl.BlockSpec((1, D_MODEL), fixed)],
        out_specs=[pl.BlockSpec((tm, D_MODEL), row), pl.BlockSpec((tm, D_MODEL), row)],
        out_shape=[jax.ShapeDtypeStruct((t, D_MODEL), F32), jax.ShapeDtypeStruct((t, D_MODEL), BF16)],
        compiler_params=_cparams("parallel"),
        name="out_proj_ln",
    )(yr, ydt, w_out, x, g, b)


def _ffn_kernel(x_ref, xb_ref, xp_ref, xn_ref, wa_ref, wv_ref, wd_ref, cp_ref, g_ref, b_ref,
                o_ref, ob_ref, acc_ref, *, alpha, tm, n_chunks, starts, ends):
    i = pl.program_id(0)
    t0 = i * tm
    is_start = functools.reduce(jnp.logical_or, [t0 == s for s in starts])
    is_end = functools.reduce(jnp.logical_or, [t0 + tm == e for e in ends])
    keep_prev = jnp.where(is_start, 0.0, 1.0)
    keep_next = jnp.where(is_end, 0.0, 1.0)

    xb = xb_ref[...]
    halo = jnp.concatenate([xp_ref[...], xn_ref[...]], axis=0)
    hrows = xp_ref.shape[0]
    acc_ref[...] = jnp.zeros_like(acc_ref)

    def chunk_body(c, carry):
        wa = wa_ref[c]
        a = jnp.dot(xb, wa, preferred_element_type=F32)
        val = jnp.dot(xb, wv_ref[c], preferred_element_type=F32)
        ah = jnp.dot(halo, wa, preferred_element_type=F32)
        prev_row = ah[hrows - 1:hrows] * keep_prev
        next_row = ah[hrows:hrows + 1] * keep_next
        row = lax.broadcasted_iota(jnp.int32, a.shape, 0)
        a_m1 = jnp.where(row == 0, prev_row, pltpu.roll(a, 1, 0))
        a_p1 = jnp.where(row == tm - 1, next_row, pltpu.roll(a, tm - 1, 0))
        cp = cp_ref[c]
        conv = cp[3:4] + a_m1 * cp[0:1]
        conv = conv + a * cp[1:2]
        conv = conv + a_p1 * cp[2:3]
        gelu = 0.5 * conv * (1.0 + lax.erf(conv * (1.0 / math.sqrt(2.0))))
        hidden = (gelu * val).astype(BF16)
        acc_ref[...] += jnp.dot(hidden, wd_ref[c], preferred_element_type=F32)
        return carry

    for c in range(n_chunks):
        chunk_body(c, 0)
    y = _layer_norm(alpha * x_ref[...] + acc_ref[...], g_ref[...], b_ref[...])
    o_ref[...] = y
    ob_ref[...] = y.astype(BF16)


def _conv_glu(x, xb, wa, wv, wd, cp, g, b, alpha, tm, groups):
    t = x.shape[0]
    n_chunks, _, ck = wa.shape
    hrows = 16
    starts = tuple(r0 + bi * s for (r0, nb, s) in groups for bi in range(nb))
    ends = tuple(r0 + (bi + 1) * s for (r0, nb, s) in groups for bi in range(nb))
    kernel = functools.partial(_ffn_kernel, alpha=alpha, tm=tm, n_chunks=n_chunks,
                               starts=starts, ends=ends)
    row = lambda i: (i, 0)
    fixed2 = lambda i: (0, 0)
    fixed3 = lambda i: (0, 0, 0)
    per = tm // hrows
    last = t // hrows - 1
    return pl.pallas_call(
        kernel,
        grid=(t // tm,),
        in_specs=[pl.BlockSpec((tm, D_MODEL), row),
                  pl.BlockSpec((tm, D_MODEL), row),
                  pl.BlockSpec((hrows, D_MODEL), lambda i: (jnp.maximum(i * per - 1, 0), 0)),
                  pl.BlockSpec((hrows, D_MODEL), lambda i: (jnp.minimum((i + 1) * per, last), 0)),
                  pl.BlockSpec((n_chunks, D_MODEL, ck), fixed3, pipeline_mode=pl.Buffered(1)),
                  pl.BlockSpec((n_chunks, D_MODEL, ck), fixed3, pipeline_mode=pl.Buffered(1)),
                  pl.BlockSpec((n_chunks, ck, D_MODEL), fixed3, pipeline_mode=pl.Buffered(1)),
                  pl.BlockSpec((n_chunks, 8, ck), fixed3),
                  pl.BlockSpec((1, D_MODEL), fixed2),
                  pl.BlockSpec((1, D_MODEL), fixed2)],
        out_specs=[pl.BlockSpec((tm, D_MODEL), row), pl.BlockSpec((tm, D_MODEL), row)],
        out_shape=[jax.ShapeDtypeStruct((t, D_MODEL), F32), jax.ShapeDtypeStruct((t, D_MODEL), BF16)],
        scratch_shapes=[pltpu.VMEM((tm, D_MODEL), F32)],
        compiler_params=_cparams("parallel"),
        name="conv_glu_ln",
    )(x, xb, xb, xb, wa, wv, wd, cp, g, b)


def _rotary_tables(seq):
    d = HEAD_DIM
    inv = 1.0 / (ROPE_BASE ** (jnp.arange(0, d, 2, dtype=F32) / d))
    ang = jnp.arange(seq, dtype=F32)[:, None] * inv[None, :]
    cos, sin = jnp.cos(ang), jnp.sin(ang)
    cos_t = jnp.concatenate([cos, cos, cos, cos], axis=-1)
    sin_t = jnp.concatenate([-sin, sin, -sin, sin], axis=-1)
    return cos_t, sin_t


def _tiles(groups):
    smin = min(s for (_, _, s) in groups)
    attn_tile = min(512, smin // 2)
    chunk = min(256, smin // 2)
    tm = min(512, smin // 2)
    ffn_tm = min(1024, smin // 2)
    return attn_tile, chunk, tm, ffn_tm


def _forward(x, groups, w_in, ret_decay_logit, rel_bias, lambda_q1, lambda_k1, lambda_q2,
             lambda_k2, diff_norm_g, w_out, ln_g, ln_b, w_up, conv_w, conv_b, w_down):
    depth = w_in.shape[0]
    alpha = (2 * depth) ** 0.25
    t = x.shape[0]
    attn_tile, chunk, tm, ffn_tm = _tiles(groups)
    smax = max(s for (_, _, s) in groups)
    gw = GROUP_WIDTH
    ck = 256
    n_chunks = D_FF // ck

    cos_t, sin_t = _rotary_tables(smax)
    bias_t = _bias_tiles(rel_bias, attn_tile)
    rel_log2 = rel_bias.astype(F32) * LOG2E
    rel_log2 = jnp.concatenate([rel_log2, jnp.max(rel_log2, axis=0, keepdims=True)], axis=0)
    in_scale = jnp.concatenate([
        jnp.ones((gw,), F32), jnp.full((gw,), HEAD_DIM ** -0.5, F32), jnp.ones((2 * gw,), F32),
        jnp.full((gw,), DIFF_QK_DIM ** -0.5 * LOG2E, F32), jnp.ones((gw,), F32)])[None, :]
    assert tm == attn_tile

    xb = x.astype(BF16)
    for l in range(depth):
        lam_init = 0.8 - 0.6 * math.exp(-0.3 * l)
        w_in_b = w_in[l].astype(BF16)
        ret, gate, dqk, vt3 = _project(xb, w_in_b[:, :6 * gw], w_in_b[:, 6 * gw:].T, in_scale,
                                       cos_t, sin_t, groups, tm)

        log_g = jax.nn.log_sigmoid(ret_decay_logit[l].astype(F32))
        lamp = jnp.stack([lambda_q1[l], lambda_k1[l], lambda_q2[l], lambda_k2[l],
                          jnp.full((DIFF_QK_DIM,), lam_init)]).astype(F32)
        dng = diff_norm_g[l].astype(F32)[:, None]

        yr = jnp.zeros((t, gw), BF16)
        ydt = jnp.zeros((t // attn_tile, gw, attn_tile), BF16)
        for (row0, batch, seq) in groups:
            yr = _retention(ret, gate, log_g, yr, row0, batch, seq, chunk)
            ydt = _diff_attention(dqk, vt3, bias_t, rel_log2, lamp, dng, ydt, row0, batch, seq,
                                  attn_tile)

        x, xb = _out_projection(yr, ydt, w_out[l].astype(BF16), x, ln_g[l, 0][None, :].astype(F32),
                                ln_b[l, 0][None, :].astype(F32), alpha, tm)

        wa = w_up[l][:, :D_FF].astype(BF16).reshape(D_MODEL, n_chunks, ck).transpose(1, 0, 2)
        wv = w_up[l][:, D_FF:].astype(BF16).reshape(D_MODEL, n_chunks, ck).transpose(1, 0, 2)
        wd = w_down[l].astype(BF16).reshape(n_chunks, ck, D_MODEL)
        cp = jnp.concatenate([conv_w[l].astype(F32), conv_b[l].astype(F32)[None, :],
                              jnp.zeros((4, D_FF), F32)], axis=0)
        cp = cp.reshape(8, n_chunks, ck).transpose(1, 0, 2)
        x, xb = _conv_glu(x, xb, wa, wv, wd, cp, ln_g[l, 1][None, :].astype(F32),
                          ln_b[l, 1][None, :].astype(F32), alpha, ffn_tm, groups)
    return x


def kernel(x_prompt, x_sample, w_in, ret_decay_logit, rel_bias, lambda_q1, lambda_k1, lambda_q2,
           lambda_k2, diff_norm_g, w_out, ln_g, ln_b, w_up, conv_w, conv_b, w_down):
    bp, sp, d = x_prompt.shape
    bs, ss, _ = x_sample.shape
    groups = ((0, bp, sp), (bp * sp, bs, ss))
    x = jnp.concatenate([x_prompt.reshape(bp * sp, d), x_sample.reshape(bs * ss, d)], axis=0)
    y = _forward(x.astype(F32), groups, w_in, ret_decay_logit, rel_bias, lambda_q1, lambda_k1,
                 lambda_q2, lambda_k2, diff_norm_g, w_out, ln_g, ln_b, w_up, conv_w, conv_b, w_down)
    y_prompt = y[:bp * sp].reshape(bp, sp, d).astype(x_prompt.dtype)
    y_sample = y[bp * sp:].reshape(bs, ss, d).astype(x_sample.dtype)
    return y_prompt, y_sample
```

```python
import functools
import math

import jax
import jax.numpy as jnp
from jax import lax
from jax.experimental import pallas as pl
from jax.experimental.pallas import tpu as pltpu

D_MODEL = 1024
HEAD_DIM = 64
N_HEADS = 8
GROUP_WIDTH = N_HEADS * HEAD_DIM
DIFF_QK_DIM = HEAD_DIM // 2
D_FF = 2816
N_BUCKETS = 32
MAX_DISTANCE = 128
ROPE_BASE = 10000.0
LN_EPS = 1e-5
HEAD_NORM_EPS = 1e-6
LANES = 128
FAR_DISTANCE = 91
BIAS_REACH = 2
MAX_STRAIGHT_MACRO_TILES = 8
LOG2E = math.log2(math.e)
NEG_BIG = -1e30
UNDERFLOW_GUARD = 2.0 ** -90
VMEM_LIMIT = 56 * 1024 * 1024

F32 = jnp.float32
BF16 = jnp.bfloat16


def _cparams(*sem):
    return pltpu.CompilerParams(dimension_semantics=sem, vmem_limit_bytes=VMEM_LIMIT)


def _proj_kernel(x_ref, w_ref, wvt_ref, s_ref, cos_ref, sin_ref, ret_ref, gate_ref, dqk_ref, vt_ref):
    gw = GROUP_WIDTH
    x = x_ref[...]
    tm = x.shape[0]
    lane = lax.broadcasted_iota(jnp.int32, (tm, LANES), 1)
    low_half = (lane % HEAD_DIM) < (HEAD_DIM // 2)
    cos = cos_ref[...]
    sin = sin_ref[...]
    qk = jnp.dot(x, w_ref[:, :2 * gw], preferred_element_type=F32) * s_ref[:, :2 * gw]
    for p in range(2 * gw // LANES):
        sl = slice(p * LANES, (p + 1) * LANES)
        xx = qk[:, sl]
        swapped = jnp.where(low_half, pltpu.roll(xx, LANES - HEAD_DIM // 2, 1),
                            pltpu.roll(xx, HEAD_DIM // 2, 1))
        ret_ref[:, sl] = (xx * cos + swapped * sin).astype(ret_ref.dtype)
    ret_ref[:, 2 * gw:] = jnp.dot(x, w_ref[:, 2 * gw:3 * gw],
                                  preferred_element_type=F32).astype(ret_ref.dtype)
    gate_ref[...] = jnp.dot(x, w_ref[:, 3 * gw:4 * gw], preferred_element_type=F32)
    dqk_ref[...] = (jnp.dot(x, w_ref[:, 4 * gw:], preferred_element_type=F32)
                    * s_ref[:, 4 * gw:]).astype(dqk_ref.dtype)
    vt_ref[...] = lax.dot_general(wvt_ref[...], x, (((1,), (1,)), ((), ())),
                                  preferred_element_type=F32).astype(vt_ref.dtype)


def _position_block(i, tm, groups):
    t0 = i * tm
    blk = t0 // tm
    for (row0, _, seq) in groups:
        blk = jnp.where(t0 >= row0, ((t0 - row0) % seq) // tm, blk)
    return blk


def _project(xb, w, w_vt, scale, cos_t, sin_t, groups, tm):
    t, k = xb.shape
    gw = GROUP_WIDTH
    row = lambda i: (i, 0)
    fixed = lambda i: (0, 0)
    pos = lambda i: (_position_block(i, tm, groups), 0)
    return pl.pallas_call(
        _proj_kernel,
        grid=(t // tm,),
        in_specs=[pl.BlockSpec((tm, k), row),
                  pl.BlockSpec((k, 6 * gw), fixed, pipeline_mode=pl.Buffered(1)),
                  pl.BlockSpec((gw, k), fixed, pipeline_mode=pl.Buffered(1)),
                  pl.BlockSpec((1, 6 * gw), fixed),
                  pl.BlockSpec((tm, LANES), pos),
                  pl.BlockSpec((tm, LANES), pos)],
        out_specs=[pl.BlockSpec((tm, 3 * gw), row), pl.BlockSpec((tm, gw), row),
                   pl.BlockSpec((tm, 2 * gw), row),
                   pl.BlockSpec((None, gw, tm), lambda i: (i, 0, 0))],
        out_shape=[jax.ShapeDtypeStruct((t, 3 * gw), BF16), jax.ShapeDtypeStruct((t, gw), F32),
                   jax.ShapeDtypeStruct((t, 2 * gw), BF16),
                   jax.ShapeDtypeStruct((t // tm, gw, tm), BF16)],
        compiler_params=_cparams("parallel"),
        name="in_proj",
    )(xb, w, w_vt, scale, cos_t, sin_t)


def _ret_kernel(lg_ref, q_ref, k_ref, v_ref, g_ref, _, o_ref,
                rf_ref, rb_ref, rnext_ref, mask_ref, tab_ref, *, nc, chunk):
    t = pl.program_id(1)
    c = chunk
    n_pairs = GROUP_WIDTH // LANES
    lane = lax.broadcasted_iota(jnp.int32, (c, LANES), 1)
    head0 = lane < HEAD_DIM
    r_i = lax.broadcasted_iota(jnp.int32, (LANES, LANES), 0) // HEAD_DIM
    c_i = lax.broadcasted_iota(jnp.int32, (LANES, LANES), 1) // HEAD_DIM
    same_head = r_i == c_i

    @pl.when(t == 0)
    def _init():
        rb_ref[...] = jnp.zeros_like(rb_ref)
        qi = lax.broadcasted_iota(jnp.int32, (c, c), 0)
        ki = lax.broadcasted_iota(jnp.int32, (c, c), 1)
        diff = (qi - ki).astype(F32)
        pos = lax.broadcasted_iota(jnp.int32, (c, LANES), 0).astype(F32)
        for p in range(n_pairs):
            for hh in range(2):
                lf = lg_ref[0, 2 * p + hh]
                lb = lg_ref[1, 2 * p + hh]
                mask_ref[p, hh * c:(hh + 1) * c, :] = jnp.where(
                    diff >= 0, jnp.exp(lf * jnp.maximum(diff, 0.0)),
                    jnp.exp(lb * jnp.maximum(-diff, 0.0)))
            lfl = jnp.where(head0, lg_ref[0, 2 * p], lg_ref[0, 2 * p + 1])
            lbl = jnp.where(head0, lg_ref[1, 2 * p], lg_ref[1, 2 * p + 1])
            tab_ref[p, 0] = jnp.exp(lfl * (c - 1 - pos))
            tab_ref[p, 1] = jnp.exp(lbl * pos)
            tab_ref[p, 2] = jnp.exp(lfl * (pos + 1.0))
            tab_ref[p, 3] = jnp.exp(lbl * (c - pos))
            tab_ref[p, 4] = jnp.exp(lfl * c)
            tab_ref[p, 5] = jnp.exp(lbl * c)

    def summary(k, w, v):
        kw = (k.astype(F32) * w).astype(BF16)
        kv = lax.dot_general(kw, v, (((0,), (0,)), ((), ())), preferred_element_type=F32)
        return jnp.where(same_head, kv, 0.0)

    @pl.when(t < nc)
    def _backward():
        for p in range(n_pairs):
            sl = slice(p * LANES, (p + 1) * LANES)
            rnext_ref[nc - 1 - t, p] = rb_ref[p].astype(BF16)
            rb_ref[p] = (tab_ref[p, 5][:LANES] * rb_ref[p]
                         + summary(k_ref[:, sl], tab_ref[p, 1], v_ref[:, sl]))

    @pl.when(t >= nc)
    def _forward():
        @pl.when(t == nc)
        def _():
            rf_ref[...] = jnp.zeros_like(rf_ref)

        for p in range(n_pairs):
            sl = slice(p * LANES, (p + 1) * LANES)
            q = q_ref[:, sl]
            k = k_ref[:, sl]
            v = v_ref[:, sl]
            zero = jnp.zeros_like(q)
            q2 = jnp.concatenate([jnp.where(head0, q, zero), jnp.where(head0, zero, q)], axis=0)
            s = lax.dot_general(q2, k, (((1,), (1,)), ((), ())), preferred_element_type=F32)
            o2 = jnp.dot((s * mask_ref[p]).astype(BF16), v, preferred_element_type=F32)
            y = jnp.where(head0, o2[:c], o2[c:])
            y = y + jnp.dot(q, rf_ref[p].astype(BF16), preferred_element_type=F32) * tab_ref[p, 2]
            y = y + jnp.dot(q, rnext_ref[t - nc, p], preferred_element_type=F32) * tab_ref[p, 3]

            sq = y * y
            s0 = jnp.sum(jnp.where(head0, sq, 0.0), axis=1, keepdims=True)
            s1 = jnp.sum(jnp.where(head0, 0.0, sq), axis=1, keepdims=True)
            ms = jnp.where(head0, s0, s1) * (1.0 / HEAD_DIM)
            y = y * lax.rsqrt(ms + HEAD_NORM_EPS)
            g = g_ref[:, sl]
            o_ref[:, sl] = (g / (1.0 + jnp.exp(-g)) * y).astype(o_ref.dtype)

            rf_ref[p] = tab_ref[p, 4][:LANES] * rf_ref[p] + summary(k, tab_ref[p, 0], v)


def _retention(ret, gate, log_g, out, row0, batch, seq, chunk):
    nc = seq // chunk
    blk0 = row0 // chunk
    n_pairs = GROUP_WIDTH // LANES
    gw = GROUP_WIDTH

    def kc(t):
        return jnp.where(t < nc, nc - 1 - t, t - nc)

    def qc(t):
        return jnp.maximum(t - nc, 0)

    def rows(b, cc):
        return blk0 + b * nc + cc

    kernel = functools.partial(_ret_kernel, nc=nc, chunk=chunk)
    return pl.pallas_call(
        kernel,
        grid=(batch, 2 * nc),
        in_specs=[pl.BlockSpec(memory_space=pltpu.SMEM),
                  pl.BlockSpec((chunk, gw), lambda b, t: (rows(b, qc(t)), 0)),
                  pl.BlockSpec((chunk, gw), lambda b, t: (rows(b, kc(t)), 1)),
                  pl.BlockSpec((chunk, gw), lambda b, t: (rows(b, kc(t)), 2)),
                  pl.BlockSpec((chunk, gw), lambda b, t: (rows(b, qc(t)), 0)),
                  pl.BlockSpec(memory_space=pl.ANY)],
        out_specs=pl.BlockSpec((chunk, gw), lambda b, t: (rows(b, qc(t)), 0)),
        out_shape=jax.ShapeDtypeStruct(out.shape, out.dtype),
        input_output_aliases={5: 0},
        scratch_shapes=[pltpu.VMEM((n_pairs, LANES, LANES), F32),
                        pltpu.VMEM((n_pairs, LANES, LANES), F32),
                        pltpu.VMEM((nc, n_pairs, LANES, LANES), BF16),
                        pltpu.VMEM((n_pairs, 2 * chunk, chunk), F32),
                        pltpu.VMEM((n_pairs, 6, chunk, LANES), F32)],
        compiler_params=_cparams("parallel", "arbitrary"),
        name="retention",
    )(log_g, ret, ret, ret, gate, out)


def _bias_kernel(rb_ref, bucket_ref, o_ref):
    h = pl.program_id(0)
    bk = bucket_ref[...]
    out = jnp.zeros(bk.shape, F32)
    for n in range(N_BUCKETS):
        out = jnp.where(bk == n, rb_ref[n, h], out)
    o_ref[...] = out * LOG2E


def _bias_tiles(rel_bias, tile):
    assert tile > FAR_DISTANCE
    a = jnp.arange(tile, dtype=jnp.int32)[:, None]
    b = jnp.arange(tile, dtype=jnp.int32)[None, :]
    rel = jnp.stack([(d * tile + a - b) for d in range(-BIAS_REACH, BIAS_REACH + 1)])
    nb = N_BUCKETS // 2
    max_exact = nb // 2
    n = jnp.abs(rel)
    nf = jnp.maximum(n, 1).astype(F32)
    large = max_exact + (jnp.log(nf / max_exact) / math.log(MAX_DISTANCE / max_exact)
                         * (nb - max_exact)).astype(jnp.int32)
    large = jnp.minimum(large, nb - 1)
    bucket = jnp.where(rel > 0, nb, 0) + jnp.where(n < max_exact, n, large)
    return pl.pallas_call(
        _bias_kernel,
        grid=(N_HEADS, 2 * BIAS_REACH + 1),
        in_specs=[pl.BlockSpec(memory_space=pltpu.SMEM),
                  pl.BlockSpec((None, tile, tile), lambda h, d: (d, 0, 0))],
        out_specs=pl.BlockSpec((None, None, tile, tile), lambda h, d: (h, d, 0, 0)),
        out_shape=jax.ShapeDtypeStruct((N_HEADS, 2 * BIAS_REACH + 1, tile, tile), F32),
        compiler_params=_cparams("parallel", "arbitrary"),
        name="t5_bias_tiles",
    )(rel_bias.astype(F32), bucket.astype(jnp.int32))


def _attn_kernel(rb_ref, lamp_ref, dng_ref, q_ref, k_ref, vt_ref, bt_ref, _, o_ref,
                 a_ref, l_ref, p_ref, qa_ref, kmax_ref, m_ref, *, n_tiles, tile, macro, unroll):
    h = pl.program_id(1)
    qi = pl.program_id(2)
    hh = h % 2
    n = n_tiles
    half = DIFF_QK_DIM
    group_of_lane = lax.broadcasted_iota(jnp.int32, (LANES, LANES), 0) // half
    group_sum = (group_of_lane == lax.broadcasted_iota(jnp.int32, (LANES, LANES), 1)).astype(BF16)

    def half_norms(x):
        xf = x.astype(F32)
        return jnp.dot((xf * xf).astype(BF16), group_sum, preferred_element_type=F32)

    @pl.when(qi == 0)
    def _key_norms():
        def body(j, mx):
            kt = k_ref[pl.ds(pl.multiple_of(j * tile, tile), tile), :]
            return jnp.maximum(mx, jnp.max(half_norms(kt), axis=0, keepdims=True))
        kmax_ref[...] = lax.fori_loop(0, n, body, jnp.zeros((1, LANES), F32))

    qf = q_ref[...].astype(F32)
    lane = lax.broadcasted_iota(jnp.int32, qf.shape, 1)
    lane_row = lax.broadcasted_iota(jnp.int32, (1, LANES), 1)
    row = lax.broadcasted_iota(jnp.int32, (LANES, tile), 0)
    group_sum_t = (lax.broadcasted_iota(jnp.int32, (LANES, LANES), 0)
                   == lax.broadcasted_iota(jnp.int32, (LANES, LANES), 1) // half).astype(BF16)
    qn_rows = lax.dot_general(group_sum_t, (qf * qf).astype(BF16), (((1,), (1,)), ((), ())),
                              preferred_element_type=F32)
    side_bias = (rb_ref[N_BUCKETS // 2 - 1, h], 0.0, rb_ref[N_BUCKETS - 1, h])
    straight = n // macro <= MAX_STRAIGHT_MACRO_TILES
    sides = (1,) if straight else (0, 1, 2)
    for t in range(2):
        g = 2 * hh + t
        lo_lane = g * half
        qzt = jnp.where((lane >= lo_lane) & (lane < lo_lane + half), qf, 0.0).T.astype(BF16)
        qn2 = jnp.sum(jnp.where(row == g, qn_rows, 0.0), axis=0, keepdims=True)
        kn2 = jnp.max(jnp.where(lane_row == g, kmax_ref[...], 0.0), axis=1, keepdims=True)
        bound = jnp.sqrt(qn2 * kn2) + rb_ref[N_BUCKETS, h]
        for side in sides:
            v = side_bias[side] - bound
            hi = v.astype(BF16).astype(F32)
            lo = (v - hi).astype(BF16).astype(F32)
            qa_ref[t, side, :LANES, :] = qzt
            qa_ref[t, side, LANES:, :] = jnp.where(row == 0, hi, jnp.where(row == 1, lo, 0.0)).astype(BF16)

    ones = jnp.ones((macro * tile, LANES), BF16)

    def near_any(m):
        j0 = m * macro
        return (qi >= j0 - (BIAS_REACH - 1)) & (qi <= j0 + macro - 1 + (BIAS_REACH - 1))

    def probabilities(m, slot, with_bias):
        j0 = m * macro
        kt = k_ref[pl.ds(pl.multiple_of(j0 * tile, macro * tile), macro * tile), :]
        kaug = jnp.concatenate([kt, ones], axis=1)
        side = 1 if with_bias else jnp.where(j0 < qi, 0, 2)
        for t in range(2):
            s = jnp.dot(kaug, qa_ref[t, side], preferred_element_type=F32)
            for r in range(macro):
                sr = s[r * tile:(r + 1) * tile]
                if with_bias:
                    sr = sr + bt_ref[jnp.clip(j0 + r - qi, -BIAS_REACH, BIAS_REACH) + BIAS_REACH]
                e = jnp.exp2(sr)
                l_ref[t] += jnp.sum(e, axis=0, keepdims=True)
                p_ref[slot, t, r * tile:(r + 1) * tile, :] = e.astype(BF16)

    def accumulate(m, slot):
        vt = jnp.concatenate([vt_ref[m * macro + r] for r in range(macro)], axis=1)
        for t in range(2):
            a_ref[t] += jnp.dot(vt, p_ref[slot, t], preferred_element_type=F32)

    a_ref[...] = jnp.zeros_like(a_ref)
    l_ref[...] = jnp.zeros_like(l_ref)

    def group(b, last):
        m0 = unroll * b

        def run(with_bias):
            for k in range(unroll):
                if not (last and k == unroll - 1):
                    probabilities(m0 + k + 1, (k + 1) % 2, with_bias)
                accumulate(m0 + k, k % 2)

        any_near = functools.reduce(jnp.logical_or, [near_any(m0 + k + 1) for k in range(unroll)])
        lax.cond(any_near, lambda: run(True), lambda: run(False))

    def group_body(b, carry):
        group(b, False)
        return carry

    if straight:
        n_macro = n // macro
        probabilities(0, 0, True)
        for m in range(n_macro):
            if m + 1 < n_macro:
                probabilities(m + 1, (m + 1) % 2, True)
            accumulate(m, m % 2)
    else:
        lax.cond(near_any(0), lambda: probabilities(0, 0, True),
                 lambda: probabilities(0, 0, False))
        n_groups = n // (macro * unroll)
        lax.fori_loop(0, n_groups - 1, group_body, 0)
        group(n_groups - 1, True)

    trusted = jnp.min(jnp.minimum(l_ref[0], l_ref[1])) >= UNDERFLOW_GUARD

    @pl.when(jnp.logical_not(trusted))
    def _exact_running_max():
        m_ref[...] = jnp.full(m_ref.shape, NEG_BIG, F32)
        a_ref[...] = jnp.zeros_like(a_ref)
        l_ref[...] = jnp.zeros_like(l_ref)

        def body(j, carry):
            kt = k_ref[pl.ds(pl.multiple_of(j * tile, tile), tile), :]
            vt = vt_ref[j]
            bias = bt_ref[jnp.clip(j - qi, -BIAS_REACH, BIAS_REACH) + BIAS_REACH]
            for t in range(2):
                s = jnp.dot(kt, qa_ref[t, 1, :LANES, :], preferred_element_type=F32) + bias
                m_prev = m_ref[t]
                m_new = jnp.maximum(m_prev, jnp.max(s, axis=0, keepdims=True))
                alpha = jnp.exp2(m_prev - m_new)
                e = jnp.exp2(s - m_new)
                l_ref[t] = alpha * l_ref[t] + jnp.sum(e, axis=0, keepdims=True)
                a_ref[t] = alpha * a_ref[t] + jnp.dot(vt, e.astype(BF16),
                                                      preferred_element_type=F32)
                m_ref[t] = m_new
            return carry

        lax.fori_loop(0, n, body, 0)

    lp = lamp_ref[...]
    lam_init = lp[4:5, 0:1]
    lam = (jnp.exp(jnp.sum(lp[0:1] * lp[1:2], axis=1, keepdims=True))
           - jnp.exp(jnp.sum(lp[2:3] * lp[3:4], axis=1, keepdims=True)) + lam_init)
    o = a_ref[0] / l_ref[0] - lam * (a_ref[1] / l_ref[1])
    ms = jnp.mean(o * o, axis=0, keepdims=True)
    y = o * lax.rsqrt(ms + HEAD_NORM_EPS) * dng_ref[...] * (1.0 - lam_init)
    o_ref[...] = y.astype(o_ref.dtype)


def _diff_attention(dqk, vt3, bias_t, rel_log2, lamp, dng, out, row0, batch, seq, tile):
    n = seq // tile
    macro = max(g for g in (1, 2, 4) if n % g == 0)
    n_macro = n // macro
    unroll = 1 if n_macro == 1 else 2
    assert n_macro % unroll == 0
    n_pairs = GROUP_WIDTH // LANES
    qblk0 = row0 // tile
    sblk0 = row0 // seq
    n_bias = 2 * BIAS_REACH + 1
    kernel = functools.partial(_attn_kernel, n_tiles=n, tile=tile, macro=macro, unroll=unroll)
    return pl.pallas_call(
        kernel,
        grid=(batch, N_HEADS, n),
        in_specs=[pl.BlockSpec(memory_space=pltpu.SMEM),
                  pl.BlockSpec((5, DIFF_QK_DIM), lambda b, h, i: (0, 0)),
                  pl.BlockSpec((HEAD_DIM, 1), lambda b, h, i: (0, 0)),
                  pl.BlockSpec((tile, LANES), lambda b, h, i: (qblk0 + b * n + i, h // 2)),
                  pl.BlockSpec((seq, LANES), lambda b, h, i: (sblk0 + b, n_pairs + h // 2)),
                  pl.BlockSpec((n, HEAD_DIM, tile), lambda b, h, i: (sblk0 + b, h, 0)),
                  pl.BlockSpec((None, n_bias, tile, tile), lambda b, h, i: (h, 0, 0, 0)),
                  pl.BlockSpec(memory_space=pl.ANY)],
        out_specs=pl.BlockSpec((None, HEAD_DIM, tile), lambda b, h, i: (qblk0 + b * n + i, h, 0)),
        out_shape=jax.ShapeDtypeStruct(out.shape, out.dtype),
        input_output_aliases={7: 0},
        scratch_shapes=[pltpu.VMEM((2, HEAD_DIM, tile), F32),
                        pltpu.VMEM((2, 1, tile), F32),
                        pltpu.VMEM((2, 2, macro * tile, tile), BF16),
                        pltpu.VMEM((2, 3, 2 * LANES, tile), BF16),
                        pltpu.VMEM((1, LANES), F32),
                        pltpu.VMEM((2, 1, tile), F32)],
        compiler_params=_cparams("parallel", "parallel", "arbitrary"),
        name="diff_attention",
    )(rel_log2, lamp, dng, dqk, dqk, vt3, bias_t, out)


def _layer_norm(y, g, b):
    mu = jnp.mean(y, axis=-1, keepdims=True)
    d = y - mu
    var = jnp.mean(d * d, axis=-1, keepdims=True)
    return d * lax.rsqrt(var + LN_EPS) * g + b


def _outproj_kernel(yr_ref, yd_ref, w_ref, x_ref, g_ref, b_ref, o_ref, ob_ref, *, alpha):
    mix = jnp.dot(yr_ref[...], w_ref[:GROUP_WIDTH, :], preferred_element_type=F32)
    mix = mix + lax.dot_general(yd_ref[...], w_ref[GROUP_WIDTH:, :], (((0,), (0,)), ((), ())),
                                preferred_element_type=F32)
    y = _layer_norm(alpha * x_ref[...] + mix, g_ref[...], b_ref[...])
    o_ref[...] = y
    ob_ref[...] = y.astype(BF16)


def _out_projection(yr, ydt, w_out, x, g, b, alpha, tm):
    t = x.shape[0]
    assert ydt.shape == (t // tm, GROUP_WIDTH, tm)
    kernel = functools.partial(_outproj_kernel, alpha=alpha)
    row = lambda i: (i, 0)
    fixed = lambda i: (0, 0)
    return pl.pallas_call(
        kernel,
        grid=(t // tm,),
        in_specs=[pl.BlockSpec((tm, GROUP_WIDTH), row),
                  pl.BlockSpec((None, GROUP_WIDTH, tm), lambda i: (i, 0, 0)),
                  pl.BlockSpec((2 * GROUP_WIDTH, D_MODEL), fixed),
                  pl.BlockSpec((tm, D_MODEL), row),
                  pl.BlockSpec((1, D_MODEL), fixed),
                  pl.BlockSpec((1, D_MODEL), fixed)],
        out_specs=[pl.BlockSpec((tm, D_MODEL), row), pl.BlockSpec((tm, D_MODEL), row)],
        out_shape=[jax.ShapeDtypeStruct((t, D_MODEL), F32), jax.ShapeDtypeStruct((t, D_MODEL), BF16)],
        compiler_params=_cparams("parallel"),
        name="out_proj_ln",
    )(yr, ydt, w_out, x, g, b)


def _ffn_kernel(x_ref, xb_ref, xp_ref, xn_ref, wa_ref, wv_ref, wd_ref, cp_ref, g_ref, b_ref,
                o_ref, ob_ref, acc_ref, *, alpha, tm, n_chunks, starts, ends):
    i = pl.program_id(0)
    t0 = i * tm
    is_start = functools.reduce(jnp.logical_or, [t0 == s for s in starts])
    is_end = functools.reduce(jnp.logical_or, [t0 + tm == e for e in ends])
    keep_prev = jnp.where(is_start, 0.0, 1.0)
    keep_next = jnp.where(is_end, 0.0, 1.0)

    xb = xb_ref[...]
    halo = jnp.concatenate([xp_ref[...], xn_ref[...]], axis=0)
    hrows = xp_ref.shape[0]
    acc_ref[...] = jnp.zeros_like(acc_ref)

    def chunk_body(c, carry):
        wa = wa_ref[c]
        a = jnp.dot(xb, wa, preferred_element_type=F32)
        val = jnp.dot(xb, wv_ref[c], preferred_element_type=F32)
        ah = jnp.dot(halo, wa, preferred_element_type=F32)
        prev_row = ah[hrows - 1:hrows] * keep_prev
        next_row = ah[hrows:hrows + 1] * keep_next
        row = lax.broadcasted_iota(jnp.int32, a.shape, 0)
        a_m1 = jnp.where(row == 0, prev_row, pltpu.roll(a, 1, 0))
        a_p1 = jnp.where(row == tm - 1, next_row, pltpu.roll(a, tm - 1, 0))
        cp = cp_ref[c]
        conv = cp[3:4] + a_m1 * cp[0:1]
        conv = conv + a * cp[1:2]
        conv = conv + a_p1 * cp[2:3]
        gelu = 0.5 * conv * (1.0 + lax.erf(conv * (1.0 / math.sqrt(2.0))))
        hidden = (gelu * val).astype(BF16)
        acc_ref[...] += jnp.dot(hidden, wd_ref[c], preferred_element_type=F32)
        return carry

    for c in range(n_chunks):
        chunk_body(c, 0)
    y = _layer_norm(alpha * x_ref[...] + acc_ref[...], g_ref[...], b_ref[...])
    o_ref[...] = y
    ob_ref[...] = y.astype(BF16)


def _conv_glu(x, xb, wa, wv, wd, cp, g, b, alpha, tm, groups):
    t = x.shape[0]
    n_chunks, _, ck = wa.shape
    hrows = 16
    starts = tuple(r0 + bi * s for (r0, nb, s) in groups for bi in range(nb))
    ends = tuple(r0 + (bi + 1) * s for (r0, nb, s) in groups for bi in range(nb))
    kernel = functools.partial(_ffn_kernel, alpha=alpha, tm=tm, n_chunks=n_chunks,
                               starts=starts, ends=ends)
    row = lambda i: (i, 0)
    fixed2 = lambda i: (0, 0)
    fixed3 = lambda i: (0, 0, 0)
    per = tm // hrows
    last = t // hrows - 1
    return pl.pallas_call(
        kernel,
        grid=(t // tm,),
        in_specs=[pl.BlockSpec((tm, D_MODEL), row),
                  pl.BlockSpec((tm, D_MODEL), row),
                  pl.BlockSpec((hrows, D_MODEL), lambda i: (jnp.maximum(i * per - 1, 0), 0)),
                  pl.BlockSpec((hrows, D_MODEL), lambda i: (jnp.minimum((i + 1) * per, last), 0)),
                  pl.BlockSpec((n_chunks, D_MODEL, ck), fixed3, pipeline_mode=pl.Buffered(1)),
                  pl.BlockSpec((n_chunks, D_MODEL, ck), fixed3, pipeline_mode=pl.Buffered(1)),
                  pl.BlockSpec((n_chunks, ck, D_MODEL), fixed3, pipeline_mode=pl.Buffered(1)),
                  pl.BlockSpec((n_chunks, 8, ck), fixed3),
                  pl.BlockSpec((1, D_MODEL), fixed2),
                  pl.BlockSpec((1, D_MODEL), fixed2)],
        out_specs=[pl.BlockSpec((tm, D_MODEL), row), pl.BlockSpec((tm, D_MODEL), row)],
        out_shape=[jax.ShapeDtypeStruct((t, D_MODEL), F32), jax.ShapeDtypeStruct((t, D_MODEL), BF16)],
        scratch_shapes=[pltpu.VMEM((tm, D_MODEL), F32)],
        compiler_params=_cparams("parallel"),
        name="conv_glu_ln",
    )(x, xb, xb, xb, wa, wv, wd, cp, g, b)


def _rotary_tables(seq):
    d = HEAD_DIM
    inv = 1.0 / (ROPE_BASE ** (jnp.arange(0, d, 2, dtype=F32) / d))
    ang = jnp.arange(seq, dtype=F32)[:, None] * inv[None, :]
    cos, sin = jnp.cos(ang), jnp.sin(ang)
    cos_t = jnp.concatenate([cos, cos, cos, cos], axis=-1)
    sin_t = jnp.concatenate([-sin, sin, -sin, sin], axis=-1)
    return cos_t, sin_t


def _tiles(groups):
    smin = min(s for (_, _, s) in groups)
    attn_tile = min(512, smin // 2)
    chunk = min(256, smin // 2)
    tm = min(512, smin // 2)
    ffn_tm = min(1024, smin // 2)
    return attn_tile, chunk, tm, ffn_tm


def _forward(x, groups, w_in, ret_decay_logit, rel_bias, lambda_q1, lambda_k1, lambda_q2,
             lambda_k2, diff_norm_g, w_out, ln_g, ln_b, w_up, conv_w, conv_b, w_down):
    depth = w_in.shape[0]
    alpha = (2 * depth) ** 0.25
    t = x.shape[0]
    attn_tile, chunk, tm, ffn_tm = _tiles(groups)
    smax = max(s for (_, _, s) in groups)
    gw = GROUP_WIDTH
    ck = 256
    n_chunks = D_FF // ck

    cos_t, sin_t = _rotary_tables(smax)
    bias_t = _bias_tiles(rel_bias, attn_tile)
    rel_log2 = rel_bias.astype(F32) * LOG2E
    rel_log2 = jnp.concatenate([rel_log2, jnp.max(rel_log2, axis=0, keepdims=True)], axis=0)
    in_scale = jnp.concatenate([
        jnp.ones((gw,), F32), jnp.full((gw,), HEAD_DIM ** -0.5, F32), jnp.ones((2 * gw,), F32),
        jnp.full((gw,), DIFF_QK_DIM ** -0.5 * LOG2E, F32), jnp.ones((gw,), F32)])[None, :]
    assert tm == attn_tile

    xb = x.astype(BF16)
    for l in range(depth):
        lam_init = 0.8 - 0.6 * math.exp(-0.3 * l)
        w_in_b = w_in[l].astype(BF16)
        ret, gate, dqk, vt3 = _project(xb, w_in_b[:, :6 * gw], w_in_b[:, 6 * gw:].T, in_scale,
                                       cos_t, sin_t, groups, tm)

        log_g = jax.nn.log_sigmoid(ret_decay_logit[l].astype(F32))
        lamp = jnp.stack([lambda_q1[l], lambda_k1[l], lambda_q2[l], lambda_k2[l],
                          jnp.full((DIFF_QK_DIM,), lam_init)]).astype(F32)
        dng = diff_norm_g[l].astype(F32)[:, None]

        yr = jnp.zeros((t, gw), BF16)
        ydt = jnp.zeros((t // attn_tile, gw, attn_tile), BF16)
        for (row0, batch, seq) in groups:
            yr = _retention(ret, gate, log_g, yr, row0, batch, seq, chunk)
            ydt = _diff_attention(dqk, vt3, bias_t, rel_log2, lamp, dng, ydt, row0, batch, seq,
                                  attn_tile)

        x, xb = _out_projection(yr, ydt, w_out[l].astype(BF16), x, ln_g[l, 0][None, :].astype(F32),
                                ln_b[l, 0][None, :].astype(F32), alpha, tm)

        wa = w_up[l][:, :D_FF].astype(BF16).reshape(D_MODEL, n_chunks, ck).transpose(1, 0, 2)
        wv = w_up[l][:, D_FF:].astype(BF16).reshape(D_MODEL, n_chunks, ck).transpose(1, 0, 2)
        wd = w_down[l].astype(BF16).reshape(n_chunks, ck, D_MODEL)
        cp = jnp.concatenate([conv_w[l].astype(F32), conv_b[l].astype(F32)[None, :],
                              jnp.zeros((4, D_FF), F32)], axis=0)
        cp = cp.reshape(8, n_chunks, ck).transpose(1, 0, 2)
        x, xb = _conv_glu(x, xb, wa, wv, wd, cp, ln_g[l, 1][None, :].astype(F32),
                          ln_b[l, 1][None, :].astype(F32), alpha, ffn_tm, groups)
    return x


def kernel(x_prompt, x_sample, w_in, ret_decay_logit, rel_bias, lambda_q1, lambda_k1, lambda_q2,
           lambda_k2, diff_norm_g, w_out, ln_g, ln_b, w_up, conv_w, conv_b, w_down):
    bp, sp, d = x_prompt.shape
    bs, ss, _ = x_sample.shape
    groups = ((0, bp, sp), (bp * sp, bs, ss))
    x = jnp.concatenate([x_prompt.reshape(bp * sp, d), x_sample.reshape(bs * ss, d)], axis=0)
    y = _forward(x.astype(F32), groups, w_in, ret_decay_logit, rel_bias, lambda_q1, lambda_k1,
                 lambda_q2, lambda_k2, diff_norm_g, w_out, ln_g, ln_b, w_up, conv_w, conv_b, w_down)
    y_prompt = y[:bp * sp].reshape(bp, sp, d).astype(x_prompt.dtype)
    y_sample = y[bp * sp:].reshape(bs, ss, d).astype(x_sample.dtype)
    return y_prompt, y_sample
```

```python
import functools
import math

import jax
import jax.numpy as jnp
from jax import lax
from jax.experimental import pallas as pl
from jax.experimental.pallas import tpu as pltpu

D_MODEL = 1024
HEAD_DIM = 64
N_HEADS = 8
GROUP_WIDTH = N_HEADS * HEAD_DIM
DIFF_QK_DIM = HEAD_DIM // 2
D_FF = 2816
N_BUCKETS = 32
MAX_DISTANCE = 128
ROPE_BASE = 10000.0
LN_EPS = 1e-5
HEAD_NORM_EPS = 1e-6
LANES = 128
BF16_SUBLANES = 16
MXU_WIDTH = 256
FAR_DISTANCE = 91
BIAS_REACH = 2
MAX_UNROLLED_MACRO_TILES = 8
LOG2E = math.log2(math.e)
NEG_BIG = -1e30
UNDERFLOW_GUARD = 2.0 ** -90
VMEM_LIMIT = 56 * 1024 * 1024

F32 = jnp.float32
BF16 = jnp.bfloat16


def _cparams(*sem):
    return pltpu.CompilerParams(dimension_semantics=sem, vmem_limit_bytes=VMEM_LIMIT)


def _proj_kernel(x_ref, w_ref, wvt_ref, s_ref, cos_ref, sin_ref, ret_ref, gate_ref, dqk_ref, vt_ref):
    gw = GROUP_WIDTH
    x = x_ref[...]
    tm = x.shape[0]
    lane = lax.broadcasted_iota(jnp.int32, (tm, LANES), 1)
    low_half = (lane % HEAD_DIM) < (HEAD_DIM // 2)
    cos = cos_ref[...]
    sin = sin_ref[...]
    qk = jnp.dot(x, w_ref[:, :2 * gw], preferred_element_type=F32) * s_ref[:, :2 * gw]
    for p in range(2 * gw // LANES):
        sl = slice(p * LANES, (p + 1) * LANES)
        xx = qk[:, sl]
        swapped = jnp.where(low_half, pltpu.roll(xx, LANES - HEAD_DIM // 2, 1),
                            pltpu.roll(xx, HEAD_DIM // 2, 1))
        ret_ref[:, sl] = (xx * cos + swapped * sin).astype(ret_ref.dtype)
    ret_ref[:, 2 * gw:] = jnp.dot(x, w_ref[:, 2 * gw:3 * gw],
                                  preferred_element_type=F32).astype(ret_ref.dtype)
    gate_ref[...] = jnp.dot(x, w_ref[:, 3 * gw:4 * gw], preferred_element_type=F32)
    dqk_ref[...] = (jnp.dot(x, w_ref[:, 4 * gw:], preferred_element_type=F32)
                    * s_ref[:, 4 * gw:]).astype(dqk_ref.dtype)
    vt_ref[...] = lax.dot_general(wvt_ref[...], x, (((1,), (1,)), ((), ())),
                                  preferred_element_type=F32).astype(vt_ref.dtype)


def _position_block(i, tm, groups):
    t0 = i * tm
    blk = t0 // tm
    for (row0, _, seq) in groups:
        blk = jnp.where(t0 >= row0, ((t0 - row0) % seq) // tm, blk)
    return blk


def _project(xb, w, w_vt, scale, cos_t, sin_t, groups, tm):
    t, k = xb.shape
    gw = GROUP_WIDTH
    row = lambda i: (i, 0)
    fixed = lambda i: (0, 0)
    pos = lambda i: (_position_block(i, tm, groups), 0)
    return pl.pallas_call(
        _proj_kernel,
        grid=(t // tm,),
        in_specs=[pl.BlockSpec((tm, k), row),
                  pl.BlockSpec((k, 6 * gw), fixed, pipeline_mode=pl.Buffered(1)),
                  pl.BlockSpec((gw, k), fixed, pipeline_mode=pl.Buffered(1)),
                  pl.BlockSpec((1, 6 * gw), fixed),
                  pl.BlockSpec((tm, LANES), pos),
                  pl.BlockSpec((tm, LANES), pos)],
        out_specs=[pl.BlockSpec((tm, 3 * gw), row), pl.BlockSpec((tm, gw), row),
                   pl.BlockSpec((tm, 2 * gw), row),
                   pl.BlockSpec((None, gw, tm), lambda i: (i, 0, 0))],
        out_shape=[jax.ShapeDtypeStruct((t, 3 * gw), BF16), jax.ShapeDtypeStruct((t, gw), F32),
                   jax.ShapeDtypeStruct((t, 2 * gw), BF16),
                   jax.ShapeDtypeStruct((t // tm, gw, tm), BF16)],
        compiler_params=_cparams("parallel"),
        name="in_proj",
    )(xb, w, w_vt, scale, cos_t, sin_t)


def _ret_kernel(lg_ref, q_ref, k_ref, v_ref, g_ref, _, o_ref,
                rf_ref, rb_ref, rnext_ref, mask_ref, tab_ref, *, nc, chunk):
    t = pl.program_id(1)
    c = chunk
    n_pairs = GROUP_WIDTH // LANES
    lane = lax.broadcasted_iota(jnp.int32, (c, LANES), 1)
    head0 = lane < HEAD_DIM
    r_i = lax.broadcasted_iota(jnp.int32, (LANES, LANES), 0) // HEAD_DIM
    c_i = lax.broadcasted_iota(jnp.int32, (LANES, LANES), 1) // HEAD_DIM
    same_head = r_i == c_i

    @pl.when(t == 0)
    def _init():
        rb_ref[...] = jnp.zeros_like(rb_ref)
        qi = lax.broadcasted_iota(jnp.int32, (c, c), 0)
        ki = lax.broadcasted_iota(jnp.int32, (c, c), 1)
        diff = (qi - ki).astype(F32)
        pos = lax.broadcasted_iota(jnp.int32, (c, LANES), 0).astype(F32)
        for p in range(n_pairs):
            for hh in range(2):
                lf = lg_ref[0, 2 * p + hh]
                lb = lg_ref[1, 2 * p + hh]
                mask_ref[p, hh * c:(hh + 1) * c, :] = jnp.where(
                    diff >= 0, jnp.exp(lf * jnp.maximum(diff, 0.0)),
                    jnp.exp(lb * jnp.maximum(-diff, 0.0)))
            lfl = jnp.where(head0, lg_ref[0, 2 * p], lg_ref[0, 2 * p + 1])
            lbl = jnp.where(head0, lg_ref[1, 2 * p], lg_ref[1, 2 * p + 1])
            tab_ref[p, 0] = jnp.exp(lfl * (c - 1 - pos))
            tab_ref[p, 1] = jnp.exp(lbl * pos)
            tab_ref[p, 2] = jnp.exp(lfl * (pos + 1.0))
            tab_ref[p, 3] = jnp.exp(lbl * (c - pos))
            tab_ref[p, 4] = jnp.exp(lfl * c)
            tab_ref[p, 5] = jnp.exp(lbl * c)

    def summary(k, w, v):
        kw = (k.astype(F32) * w).astype(BF16)
        kv = lax.dot_general(kw, v, (((0,), (0,)), ((), ())), preferred_element_type=F32)
        return jnp.where(same_head, kv, 0.0)

    @pl.when(t < nc)
    def _backward():
        for p in range(n_pairs):
            sl = slice(p * LANES, (p + 1) * LANES)
            rnext_ref[nc - 1 - t, p] = rb_ref[p].astype(BF16)
            rb_ref[p] = (tab_ref[p, 5][:LANES] * rb_ref[p]
                         + summary(k_ref[:, sl], tab_ref[p, 1], v_ref[:, sl]))

    @pl.when(t >= nc)
    def _forward():
        @pl.when(t == nc)
        def _():
            rf_ref[...] = jnp.zeros_like(rf_ref)

        for p in range(n_pairs):
            sl = slice(p * LANES, (p + 1) * LANES)
            q = q_ref[:, sl]
            k = k_ref[:, sl]
            v = v_ref[:, sl]
            zero = jnp.zeros_like(q)
            q2 = jnp.concatenate([jnp.where(head0, q, zero), jnp.where(head0, zero, q)], axis=0)
            s = lax.dot_general(q2, k, (((1,), (1,)), ((), ())), preferred_element_type=F32)
            o2 = jnp.dot((s * mask_ref[p]).astype(BF16), v, preferred_element_type=F32)
            y = jnp.where(head0, o2[:c], o2[c:])
            y = y + jnp.dot(q, rf_ref[p].astype(BF16), preferred_element_type=F32) * tab_ref[p, 2]
            y = y + jnp.dot(q, rnext_ref[t - nc, p], preferred_element_type=F32) * tab_ref[p, 3]

            sq = y * y
            s0 = jnp.sum(jnp.where(head0, sq, 0.0), axis=1, keepdims=True)
            s1 = jnp.sum(jnp.where(head0, 0.0, sq), axis=1, keepdims=True)
            ms = jnp.where(head0, s0, s1) * (1.0 / HEAD_DIM)
            y = y * lax.rsqrt(ms + HEAD_NORM_EPS)
            g = g_ref[:, sl]
            o_ref[:, sl] = (g / (1.0 + jnp.exp(-g)) * y).astype(o_ref.dtype)

            rf_ref[p] = tab_ref[p, 4][:LANES] * rf_ref[p] + summary(k, tab_ref[p, 0], v)


def _retention(ret, gate, log_g, out, row0, batch, seq, chunk):
    nc = seq // chunk
    blk0 = row0 // chunk
    n_pairs = GROUP_WIDTH // LANES
    gw = GROUP_WIDTH

    def kc(t):
        return jnp.where(t < nc, nc - 1 - t, t - nc)

    def qc(t):
        return jnp.maximum(t - nc, 0)

    def rows(b, cc):
        return blk0 + b * nc + cc

    kernel = functools.partial(_ret_kernel, nc=nc, chunk=chunk)
    return pl.pallas_call(
        kernel,
        grid=(batch, 2 * nc),
        in_specs=[pl.BlockSpec(memory_space=pltpu.SMEM),
                  pl.BlockSpec((chunk, gw), lambda b, t: (rows(b, qc(t)), 0)),
                  pl.BlockSpec((chunk, gw), lambda b, t: (rows(b, kc(t)), 1)),
                  pl.BlockSpec((chunk, gw), lambda b, t: (rows(b, kc(t)), 2)),
                  pl.BlockSpec((chunk, gw), lambda b, t: (rows(b, qc(t)), 0)),
                  pl.BlockSpec(memory_space=pl.ANY)],
        out_specs=pl.BlockSpec((chunk, gw), lambda b, t: (rows(b, qc(t)), 0)),
        out_shape=jax.ShapeDtypeStruct(out.shape, out.dtype),
        input_output_aliases={5: 0},
        scratch_shapes=[pltpu.VMEM((n_pairs, LANES, LANES), F32),
                        pltpu.VMEM((n_pairs, LANES, LANES), F32),
                        pltpu.VMEM((nc, n_pairs, LANES, LANES), BF16),
                        pltpu.VMEM((n_pairs, 2 * chunk, chunk), F32),
                        pltpu.VMEM((n_pairs, 6, chunk, LANES), F32)],
        compiler_params=_cparams("parallel", "arbitrary"),
        name="retention",
    )(log_g, ret, ret, ret, gate, out)


def _bias_kernel(rb_ref, bucket_ref, o_ref):
    h = pl.program_id(0)
    bk = bucket_ref[...]
    out = jnp.zeros(bk.shape, F32)
    for n in range(N_BUCKETS):
        out = jnp.where(bk == n, rb_ref[n, h], out)
    o_ref[...] = out * LOG2E


def _bias_tiles(rel_bias, tile):
    assert tile > FAR_DISTANCE
    a = jnp.arange(tile, dtype=jnp.int32)[:, None]
    b = jnp.arange(tile, dtype=jnp.int32)[None, :]
    rel = jnp.stack([(d * tile + a - b) for d in range(-BIAS_REACH, BIAS_REACH + 1)])
    nb = N_BUCKETS // 2
    max_exact = nb // 2
    n = jnp.abs(rel)
    nf = jnp.maximum(n, 1).astype(F32)
    large = max_exact + (jnp.log(nf / max_exact) / math.log(MAX_DISTANCE / max_exact)
                         * (nb - max_exact)).astype(jnp.int32)
    large = jnp.minimum(large, nb - 1)
    bucket = jnp.where(rel > 0, nb, 0) + jnp.where(n < max_exact, n, large)
    return pl.pallas_call(
        _bias_kernel,
        grid=(N_HEADS, 2 * BIAS_REACH + 1),
        in_specs=[pl.BlockSpec(memory_space=pltpu.SMEM),
                  pl.BlockSpec((None, tile, tile), lambda h, d: (d, 0, 0))],
        out_specs=pl.BlockSpec((None, None, tile, tile), lambda h, d: (h, d, 0, 0)),
        out_shape=jax.ShapeDtypeStruct((N_HEADS, 2 * BIAS_REACH + 1, tile, tile), F32),
        compiler_params=_cparams("parallel", "arbitrary"),
        name="t5_bias_tiles",
    )(rel_bias.astype(F32), bucket.astype(jnp.int32))


def _attn_kernel(rb_ref, lamp_ref, dng_ref, q_ref, k_ref, vt_ref, bt_ref, _, o_ref,
                 a_ref, l_ref, p_ref, qa_ref, kmax_ref, m_ref, *, n_tiles, tile, macro):
    h = pl.program_id(1)
    qi = pl.program_id(2)
    hh = h % 2
    n = n_tiles
    half = DIFF_QK_DIM
    group_of_lane = lax.broadcasted_iota(jnp.int32, (LANES, LANES), 0) // half
    group_sum = (group_of_lane == lax.broadcasted_iota(jnp.int32, (LANES, LANES), 1)).astype(BF16)

    def half_norms(x):
        xf = x.astype(F32)
        return jnp.dot((xf * xf).astype(BF16), group_sum, preferred_element_type=F32)

    @pl.when(qi == 0)
    def _key_norms():
        def body(j, mx):
            kt = k_ref[pl.ds(pl.multiple_of(j * tile, tile), tile), :]
            return jnp.maximum(mx, jnp.max(half_norms(kt), axis=0, keepdims=True))
        kmax_ref[...] = lax.fori_loop(0, n, body, jnp.zeros((1, LANES), F32))

    qf = q_ref[...].astype(F32)
    lane = lax.broadcasted_iota(jnp.int32, qf.shape, 1)
    lane_row = lax.broadcasted_iota(jnp.int32, (1, LANES), 1)
    row = lax.broadcasted_iota(jnp.int32, (LANES, tile), 0)
    group_sum_t = (lax.broadcasted_iota(jnp.int32, (LANES, LANES), 0)
                   == lax.broadcasted_iota(jnp.int32, (LANES, LANES), 1) // half).astype(BF16)
    qn_rows = lax.dot_general(group_sum_t, (qf * qf).astype(BF16), (((1,), (1,)), ((), ())),
                              preferred_element_type=F32)
    for t in range(2):
        g = 2 * hh + t
        lo_lane = g * half
        qzt = jnp.where((lane >= lo_lane) & (lane < lo_lane + half), qf, 0.0).T.astype(BF16)
        qn2 = jnp.sum(jnp.where(row == g, qn_rows, 0.0), axis=0, keepdims=True)
        kn2 = jnp.max(jnp.where(lane_row == g, kmax_ref[...], 0.0), axis=1, keepdims=True)
        v = -(jnp.sqrt(qn2 * kn2) + rb_ref[N_BUCKETS, h])
        hi = v.astype(BF16).astype(F32)
        lo = (v - hi).astype(BF16).astype(F32)
        qa_ref[t, :LANES, :] = qzt
        qa_ref[t, LANES:, :] = jnp.where(row == 0, hi, jnp.where(row == 1, lo, 0.0)).astype(BF16)

    ones = jnp.ones((macro * tile, LANES), BF16)

    def probabilities(m, slot):
        j0 = m * macro
        kaug = jnp.concatenate([k_ref[j0 * tile:(j0 + macro) * tile, :], ones], axis=1)
        for t in range(2):
            s = jnp.dot(kaug, qa_ref[t], preferred_element_type=F32)
            for r in range(macro):
                bias = bt_ref[jnp.clip(j0 + r - qi, -BIAS_REACH, BIAS_REACH) + BIAS_REACH]
                sr = s[r * tile:(r + 1) * tile] + bias
                e = jnp.exp2(sr)
                l_ref[t] += jnp.sum(e, axis=0, keepdims=True)
                p_ref[slot, t, r * tile:(r + 1) * tile, :] = e.astype(BF16)

    def accumulate(m, slot):
        vt = jnp.concatenate([vt_ref[m * macro + r] for r in range(macro)], axis=1)
        for t in range(2):
            a_ref[t] += jnp.dot(vt, p_ref[slot, t], preferred_element_type=F32)

    a_ref[...] = jnp.zeros_like(a_ref)
    l_ref[...] = jnp.zeros_like(l_ref)

    n_macro = n // macro
    probabilities(0, 0)
    for m in range(n_macro):
        if m + 1 < n_macro:
            probabilities(m + 1, (m + 1) % 2)
        accumulate(m, m % 2)

    trusted = jnp.min(jnp.minimum(l_ref[0], l_ref[1])) >= UNDERFLOW_GUARD

    @pl.when(jnp.logical_not(trusted))
    def _exact_running_max():
        m_ref[...] = jnp.full(m_ref.shape, NEG_BIG, F32)
        a_ref[...] = jnp.zeros_like(a_ref)
        l_ref[...] = jnp.zeros_like(l_ref)

        def body(j, carry):
            kt = k_ref[pl.ds(pl.multiple_of(j * tile, tile), tile), :]
            vt = vt_ref[j]
            bias = bt_ref[jnp.clip(j - qi, -BIAS_REACH, BIAS_REACH) + BIAS_REACH]
            for t in range(2):
                s = jnp.dot(kt, qa_ref[t, :LANES, :], preferred_element_type=F32) + bias
                m_prev = m_ref[t]
                m_new = jnp.maximum(m_prev, jnp.max(s, axis=0, keepdims=True))
                alpha = jnp.exp2(m_prev - m_new)
                e = jnp.exp2(s - m_new)
                l_ref[t] = alpha * l_ref[t] + jnp.sum(e, axis=0, keepdims=True)
                a_ref[t] = alpha * a_ref[t] + jnp.dot(vt, e.astype(BF16),
                                                      preferred_element_type=F32)
                m_ref[t] = m_new
            return carry

        lax.fori_loop(0, n, body, 0)

    lp = lamp_ref[...]
    lam_init = lp[4:5, 0:1]
    lam = (jnp.exp(jnp.sum(lp[0:1] * lp[1:2], axis=1, keepdims=True))
           - jnp.exp(jnp.sum(lp[2:3] * lp[3:4], axis=1, keepdims=True)) + lam_init)
    o = a_ref[0] / l_ref[0] - lam * (a_ref[1] / l_ref[1])
    ms = jnp.mean(o * o, axis=0, keepdims=True)
    y = o * lax.rsqrt(ms + HEAD_NORM_EPS) * dng_ref[...] * (1.0 - lam_init)
    o_ref[...] = y.astype(o_ref.dtype)


def _diff_attention(dqk, vt3, bias_t, rel_log2, lamp, dng, out, row0, batch, seq, tile):
    n = seq // tile
    macro = max(g for g in (1, 2, 4) if n % g == 0 and n // g >= min(n, 2))
    assert n // macro <= MAX_UNROLLED_MACRO_TILES
    n_pairs = GROUP_WIDTH // LANES
    qblk0 = row0 // tile
    sblk0 = row0 // seq
    n_bias = 2 * BIAS_REACH + 1
    kernel = functools.partial(_attn_kernel, n_tiles=n, tile=tile, macro=macro)
    return pl.pallas_call(
        kernel,
        grid=(batch, N_HEADS, n),
        in_specs=[pl.BlockSpec(memory_space=pltpu.SMEM),
                  pl.BlockSpec((5, DIFF_QK_DIM), lambda b, h, i: (0, 0)),
                  pl.BlockSpec((HEAD_DIM, 1), lambda b, h, i: (0, 0)),
                  pl.BlockSpec((tile, LANES), lambda b, h, i: (qblk0 + b * n + i, h // 2)),
                  pl.BlockSpec((seq, LANES), lambda b, h, i: (sblk0 + b, n_pairs + h // 2)),
                  pl.BlockSpec((n, HEAD_DIM, tile), lambda b, h, i: (sblk0 + b, h, 0)),
                  pl.BlockSpec((None, n_bias, tile, tile), lambda b, h, i: (h, 0, 0, 0)),
                  pl.BlockSpec(memory_space=pl.ANY)],
        out_specs=pl.BlockSpec((None, HEAD_DIM, tile), lambda b, h, i: (qblk0 + b * n + i, h, 0)),
        out_shape=jax.ShapeDtypeStruct(out.shape, out.dtype),
        input_output_aliases={7: 0},
        scratch_shapes=[pltpu.VMEM((2, HEAD_DIM, tile), F32),
                        pltpu.VMEM((2, 1, tile), F32),
                        pltpu.VMEM((2, 2, macro * tile, tile), BF16),
                        pltpu.VMEM((2, 2 * LANES, tile), BF16),
                        pltpu.VMEM((1, LANES), F32),
                        pltpu.VMEM((2, 1, tile), F32)],
        compiler_params=_cparams("parallel", "parallel", "arbitrary"),
        name="diff_attention",
    )(rel_log2, lamp, dng, dqk, dqk, vt3, bias_t, out)


def _layer_norm(y, g, b):
    mu = jnp.mean(y, axis=-1, keepdims=True)
    d = y - mu
    var = jnp.mean(d * d, axis=-1, keepdims=True)
    return d * lax.rsqrt(var + LN_EPS) * g + b


def _outproj_kernel(yr_ref, yd_ref, w_ref, x_ref, g_ref, b_ref, o_ref, ob_ref, *, alpha):
    mix = jnp.dot(yr_ref[...], w_ref[:GROUP_WIDTH, :], preferred_element_type=F32)
    mix = mix + lax.dot_general(yd_ref[...], w_ref[GROUP_WIDTH:, :], (((0,), (0,)), ((), ())),
                                preferred_element_type=F32)
    y = _layer_norm(alpha * x_ref[...] + mix, g_ref[...], b_ref[...])
    o_ref[...] = y
    ob_ref[...] = y.astype(BF16)


def _out_projection(yr, ydt, w_out, x, g, b, alpha, tm):
    t = x.shape[0]
    assert ydt.shape == (t // tm, GROUP_WIDTH, tm)
    kernel = functools.partial(_outproj_kernel, alpha=alpha)
    row = lambda i: (i, 0)
    fixed = lambda i: (0, 0)
    return pl.pallas_call(
        kernel,
        grid=(t // tm,),
        in_specs=[pl.BlockSpec((tm, GROUP_WIDTH), row),
                  pl.BlockSpec((None, GROUP_WIDTH, tm), lambda i: (i, 0, 0)),
                  pl.BlockSpec((2 * GROUP_WIDTH, D_MODEL), fixed),
                  pl.BlockSpec((tm, D_MODEL), row),
                  pl.BlockSpec((1, D_MODEL), fixed),
                  pl.BlockSpec((1, D_MODEL), fixed)],
        out_specs=[pl.BlockSpec((tm, D_MODEL), row), pl.BlockSpec((tm, D_MODEL), row)],
        out_shape=[jax.ShapeDtypeStruct((t, D_MODEL), F32), jax.ShapeDtypeStruct((t, D_MODEL), BF16)],
        compiler_params=_cparams("parallel"),
        name="out_proj_ln",
    )(yr, ydt, w_out, x, g, b)


def _ffn_kernel(x_ref, xb_ref, xp_ref, xn_ref, wa_ref, wv_ref, wd_ref, cp_ref, g_ref, b_ref,
                o_ref, ob_ref, acc_ref, *, alpha, tm, n_chunks, starts, ends):
    i = pl.program_id(0)
    t0 = i * tm
    is_start = functools.reduce(jnp.logical_or, [t0 == s for s in starts])
    is_end = functools.reduce(jnp.logical_or, [t0 + tm == e for e in ends])
    keep_prev = jnp.where(is_start, 0.0, 1.0)
    keep_next = jnp.where(is_end, 0.0, 1.0)

    xb = xb_ref[...]
    halo = jnp.concatenate([xp_ref[...], xn_ref[...]], axis=0)
    hrows = xp_ref.shape[0]
    acc_ref[...] = jnp.zeros_like(acc_ref)

    def chunk_body(c, carry):
        wa = wa_ref[c]
        a = jnp.dot(xb, wa, preferred_element_type=F32)
        val = jnp.dot(xb, wv_ref[c], preferred_element_type=F32)
        ah = jnp.dot(halo, wa, preferred_element_type=F32)
        prev_row = ah[hrows - 1:hrows] * keep_prev
        next_row = ah[hrows:hrows + 1] * keep_next
        row = lax.broadcasted_iota(jnp.int32, a.shape, 0)
        a_m1 = jnp.where(row == 0, prev_row, pltpu.roll(a, 1, 0))
        a_p1 = jnp.where(row == tm - 1, next_row, pltpu.roll(a, tm - 1, 0))
        cp = cp_ref[c]
        conv = cp[3:4] + a_m1 * cp[0:1]
        conv = conv + a * cp[1:2]
        conv = conv + a_p1 * cp[2:3]
        gelu = 0.5 * conv * (1.0 + lax.erf(conv * (1.0 / math.sqrt(2.0))))
        hidden = (gelu * val).astype(BF16)
        acc_ref[...] += jnp.dot(hidden, wd_ref[c], preferred_element_type=F32)
        return carry

    for c in range(n_chunks):
        chunk_body(c, 0)
    y = _layer_norm(alpha * x_ref[...] + acc_ref[...], g_ref[...], b_ref[...])
    o_ref[...] = y
    ob_ref[...] = y.astype(BF16)


def _conv_glu(x, xb, wa, wv, wd, cp, g, b, alpha, tm, groups):
    t = x.shape[0]
    n_chunks, _, ck = wa.shape
    hrows = BF16_SUBLANES
    starts = tuple(r0 + bi * s for (r0, nb, s) in groups for bi in range(nb))
    ends = tuple(r0 + (bi + 1) * s for (r0, nb, s) in groups for bi in range(nb))
    kernel = functools.partial(_ffn_kernel, alpha=alpha, tm=tm, n_chunks=n_chunks,
                               starts=starts, ends=ends)
    row = lambda i: (i, 0)
    fixed2 = lambda i: (0, 0)
    fixed3 = lambda i: (0, 0, 0)
    per = tm // hrows
    last = t // hrows - 1
    return pl.pallas_call(
        kernel,
        grid=(t // tm,),
        in_specs=[pl.BlockSpec((tm, D_MODEL), row),
                  pl.BlockSpec((tm, D_MODEL), row),
                  pl.BlockSpec((hrows, D_MODEL), lambda i: (jnp.maximum(i * per - 1, 0), 0)),
                  pl.BlockSpec((hrows, D_MODEL), lambda i: (jnp.minimum((i + 1) * per, last), 0)),
                  pl.BlockSpec((n_chunks, D_MODEL, ck), fixed3, pipeline_mode=pl.Buffered(1)),
                  pl.BlockSpec((n_chunks, D_MODEL, ck), fixed3, pipeline_mode=pl.Buffered(1)),
                  pl.BlockSpec((n_chunks, ck, D_MODEL), fixed3, pipeline_mode=pl.Buffered(1)),
                  pl.BlockSpec((n_chunks, 8, ck), fixed3),
                  pl.BlockSpec((1, D_MODEL), fixed2),
                  pl.BlockSpec((1, D_MODEL), fixed2)],
        out_specs=[pl.BlockSpec((tm, D_MODEL), row), pl.BlockSpec((tm, D_MODEL), row)],
        out_shape=[jax.ShapeDtypeStruct((t, D_MODEL), F32), jax.ShapeDtypeStruct((t, D_MODEL), BF16)],
        scratch_shapes=[pltpu.VMEM((tm, D_MODEL), F32)],
        compiler_params=_cparams("parallel"),
        name="conv_glu_ln",
    )(x, xb, xb, xb, wa, wv, wd, cp, g, b)


def _rotary_tables(seq):
    d = HEAD_DIM
    inv = 1.0 / (ROPE_BASE ** (jnp.arange(0, d, 2, dtype=F32) / d))
    ang = jnp.arange(seq, dtype=F32)[:, None] * inv[None, :]
    cos, sin = jnp.cos(ang), jnp.sin(ang)
    cos_t = jnp.concatenate([cos, cos, cos, cos], axis=-1)
    sin_t = jnp.concatenate([-sin, sin, -sin, sin], axis=-1)
    return cos_t, sin_t


def _tiles(groups):
    smin = min(s for (_, _, s) in groups)
    attn_tile = min(512, smin // 2)
    chunk = min(256, smin // 2)
    tm = min(512, smin // 2)
    ffn_tm = min(1024, smin // 2)
    return attn_tile, chunk, tm, ffn_tm


def _forward(x, groups, w_in, ret_decay_logit, rel_bias, lambda_q1, lambda_k1, lambda_q2,
             lambda_k2, diff_norm_g, w_out, ln_g, ln_b, w_up, conv_w, conv_b, w_down):
    depth = w_in.shape[0]
    alpha = (2 * depth) ** 0.25
    t = x.shape[0]
    attn_tile, chunk, tm, ffn_tm = _tiles(groups)
    smax = max(s for (_, _, s) in groups)
    gw = GROUP_WIDTH
    ck = MXU_WIDTH
    n_chunks = D_FF // ck

    cos_t, sin_t = _rotary_tables(smax)
    bias_t = _bias_tiles(rel_bias, attn_tile)
    rel_log2 = rel_bias.astype(F32) * LOG2E
    rel_log2 = jnp.concatenate([rel_log2, jnp.max(rel_log2, axis=0, keepdims=True)], axis=0)
    in_scale = jnp.concatenate([
        jnp.ones((gw,), F32), jnp.full((gw,), HEAD_DIM ** -0.5, F32), jnp.ones((2 * gw,), F32),
        jnp.full((gw,), DIFF_QK_DIM ** -0.5 * LOG2E, F32), jnp.ones((gw,), F32)])[None, :]
    assert tm == attn_tile

    xb = x.astype(BF16)
    for l in range(depth):
        lam_init = 0.8 - 0.6 * math.exp(-0.3 * l)
        w_in_b = w_in[l].astype(BF16)
        ret, gate, dqk, vt3 = _project(xb, w_in_b[:, :6 * gw], w_in_b[:, 6 * gw:].T, in_scale,
                                       cos_t, sin_t, groups, tm)

        log_g = jax.nn.log_sigmoid(ret_decay_logit[l].astype(F32))
        lamp = jnp.stack([lambda_q1[l], lambda_k1[l], lambda_q2[l], lambda_k2[l],
                          jnp.full((DIFF_QK_DIM,), lam_init)]).astype(F32)
        dng = diff_norm_g[l].astype(F32)[:, None]

        yr = jnp.zeros((t, gw), BF16)
        ydt = jnp.zeros((t // attn_tile, gw, attn_tile), BF16)
        for (row0, batch, seq) in groups:
            yr = _retention(ret, gate, log_g, yr, row0, batch, seq, chunk)
            ydt = _diff_attention(dqk, vt3, bias_t, rel_log2, lamp, dng, ydt, row0, batch, seq,
                                  attn_tile)

        x, xb = _out_projection(yr, ydt, w_out[l].astype(BF16), x, ln_g[l, 0][None, :].astype(F32),
                                ln_b[l, 0][None, :].astype(F32), alpha, tm)

        wa = w_up[l][:, :D_FF].astype(BF16).reshape(D_MODEL, n_chunks, ck).transpose(1, 0, 2)
        wv = w_up[l][:, D_FF:].astype(BF16).reshape(D_MODEL, n_chunks, ck).transpose(1, 0, 2)
        wd = w_down[l].astype(BF16).reshape(n_chunks, ck, D_MODEL)
        cp = jnp.concatenate([conv_w[l].astype(F32), conv_b[l].astype(F32)[None, :],
                              jnp.zeros((4, D_FF), F32)], axis=0)
        cp = cp.reshape(8, n_chunks, ck).transpose(1, 0, 2)
        x, xb = _conv_glu(x, xb, wa, wv, wd, cp, ln_g[l, 1][None, :].astype(F32),
                          ln_b[l, 1][None, :].astype(F32), alpha, ffn_tm, groups)
    return x


def kernel(x_prompt, x_sample, w_in, ret_decay_logit, rel_bias, lambda_q1, lambda_k1, lambda_q2,
           lambda_k2, diff_norm_g, w_out, ln_g, ln_b, w_up, conv_w, conv_b, w_down):
    bp, sp, d = x_prompt.shape
    bs, ss, _ = x_sample.shape
    groups = ((0, bp, sp), (bp * sp, bs, ss))
    x = jnp.concatenate([x_prompt.reshape(bp * sp, d), x_sample.reshape(bs * ss, d)], axis=0)
    y = _forward(x.astype(F32), groups, w_in, ret_decay_logit, rel_bias, lambda_q1, lambda_k1,
                 lambda_q2, lambda_k2, diff_norm_g, w_out, ln_g, ln_b, w_up, conv_w, conv_b, w_down)
    y_prompt = y[:bp * sp].reshape(bp, sp, d).astype(x_prompt.dtype)
    y_sample = y[bp * sp:].reshape(bs, ss, d).astype(x_sample.dtype)
    return y_prompt, y_sample
```

```python
import functools
import math

import jax
import jax.numpy as jnp
from jax import lax
from jax.experimental import pallas as pl
from jax.experimental.pallas import tpu as pltpu

D_MODEL = 1024
HEAD_DIM = 64
N_HEADS = 8
GROUP_WIDTH = N_HEADS * HEAD_DIM
DIFF_QK_DIM = HEAD_DIM // 2
D_FF = 2816
N_BUCKETS = 32
MAX_DISTANCE = 128
ROPE_BASE = 10000.0
LN_EPS = 1e-5
HEAD_NORM_EPS = 1e-6
LANES = 128
BF16_SUBLANES = 16
MXU_WIDTH = 256
FAR_DISTANCE = 91
BIAS_REACH = 2
MAX_UNROLLED_MACRO_TILES = 8
LOG2E = math.log2(math.e)
NEG_BIG = -1e30
UNDERFLOW_GUARD = 2.0 ** -90
VMEM_LIMIT = 56 * 1024 * 1024

F32 = jnp.float32
BF16 = jnp.bfloat16


def _cparams(*sem):
    return pltpu.CompilerParams(dimension_semantics=sem, vmem_limit_bytes=VMEM_LIMIT)


def _proj_kernel(x_ref, w_ref, wvt_ref, s_ref, cos_ref, sin_ref, ret_ref, gate_ref, dqk_ref, vt_ref):
    gw = GROUP_WIDTH
    x = x_ref[...]
    tm = x.shape[0]
    lane = lax.broadcasted_iota(jnp.int32, (tm, LANES), 1)
    low_half = (lane % HEAD_DIM) < (HEAD_DIM // 2)
    cos = cos_ref[...]
    sin = sin_ref[...]
    qk = jnp.dot(x, w_ref[:, :2 * gw], preferred_element_type=F32) * s_ref[:, :2 * gw]
    for p in range(2 * gw // LANES):
        sl = slice(p * LANES, (p + 1) * LANES)
        xx = qk[:, sl]
        swapped = jnp.where(low_half, pltpu.roll(xx, LANES - HEAD_DIM // 2, 1),
                            pltpu.roll(xx, HEAD_DIM // 2, 1))
        ret_ref[:, sl] = (xx * cos + swapped * sin).astype(ret_ref.dtype)
    ret_ref[:, 2 * gw:] = jnp.dot(x, w_ref[:, 2 * gw:3 * gw],
                                  preferred_element_type=F32).astype(ret_ref.dtype)
    gate_ref[...] = jnp.dot(x, w_ref[:, 3 * gw:4 * gw], preferred_element_type=F32)
    dqk_ref[...] = (jnp.dot(x, w_ref[:, 4 * gw:], preferred_element_type=F32)
                    * s_ref[:, 4 * gw:]).astype(dqk_ref.dtype)
    vt_ref[...] = lax.dot_general(wvt_ref[...], x, (((1,), (1,)), ((), ())),
                                  preferred_element_type=F32).astype(vt_ref.dtype)


def _position_block(i, tm, groups):
    t0 = i * tm
    blk = t0 // tm
    for (row0, _, seq) in groups:
        blk = jnp.where(t0 >= row0, ((t0 - row0) % seq) // tm, blk)
    return blk


def _project(xb, w, w_vt, scale, cos_t, sin_t, groups, tm):
    t, k = xb.shape
    gw = GROUP_WIDTH
    row = lambda i: (i, 0)
    fixed = lambda i: (0, 0)
    pos = lambda i: (_position_block(i, tm, groups), 0)
    return pl.pallas_call(
        _proj_kernel,
        grid=(t // tm,),
        in_specs=[pl.BlockSpec((tm, k), row),
                  pl.BlockSpec((k, 6 * gw), fixed, pipeline_mode=pl.Buffered(1)),
                  pl.BlockSpec((gw, k), fixed, pipeline_mode=pl.Buffered(1)),
                  pl.BlockSpec((1, 6 * gw), fixed),
                  pl.BlockSpec((tm, LANES), pos),
                  pl.BlockSpec((tm, LANES), pos)],
        out_specs=[pl.BlockSpec((tm, 3 * gw), row), pl.BlockSpec((tm, gw), row),
                   pl.BlockSpec((tm, 2 * gw), row),
                   pl.BlockSpec((None, gw, tm), lambda i: (i, 0, 0))],
        out_shape=[jax.ShapeDtypeStruct((t, 3 * gw), BF16), jax.ShapeDtypeStruct((t, gw), F32),
                   jax.ShapeDtypeStruct((t, 2 * gw), BF16),
                   jax.ShapeDtypeStruct((t // tm, gw, tm), BF16)],
        compiler_params=_cparams("parallel"),
        name="in_proj",
    )(xb, w, w_vt, scale, cos_t, sin_t)


def _ret_kernel(lg_ref, q_ref, k_ref, v_ref, g_ref, _, o_ref,
                rf_ref, rb_ref, rnext_ref, mask_ref, tab_ref, *, nc, chunk):
    t = pl.program_id(1)
    c = chunk
    n_pairs = GROUP_WIDTH // LANES
    lane = lax.broadcasted_iota(jnp.int32, (c, LANES), 1)
    head0 = lane < HEAD_DIM
    r_i = lax.broadcasted_iota(jnp.int32, (LANES, LANES), 0) // HEAD_DIM
    c_i = lax.broadcasted_iota(jnp.int32, (LANES, LANES), 1) // HEAD_DIM
    same_head = r_i == c_i

    @pl.when(t == 0)
    def _init():
        rb_ref[...] = jnp.zeros_like(rb_ref)
        qi = lax.broadcasted_iota(jnp.int32, (c, c), 0)
        ki = lax.broadcasted_iota(jnp.int32, (c, c), 1)
        diff = (qi - ki).astype(F32)
        pos = lax.broadcasted_iota(jnp.int32, (c, LANES), 0).astype(F32)
        for p in range(n_pairs):
            for hh in range(2):
                lf = lg_ref[0, 2 * p + hh]
                lb = lg_ref[1, 2 * p + hh]
                mask_ref[p, hh * c:(hh + 1) * c, :] = jnp.where(
                    diff >= 0, jnp.exp(lf * jnp.maximum(diff, 0.0)),
                    jnp.exp(lb * jnp.maximum(-diff, 0.0)))
            lfl = jnp.where(head0, lg_ref[0, 2 * p], lg_ref[0, 2 * p + 1])
            lbl = jnp.where(head0, lg_ref[1, 2 * p], lg_ref[1, 2 * p + 1])
            tab_ref[p, 0] = jnp.exp(lfl * (c - 1 - pos))
            tab_ref[p, 1] = jnp.exp(lbl * pos)
            tab_ref[p, 2] = jnp.exp(lfl * (pos + 1.0))
            tab_ref[p, 3] = jnp.exp(lbl * (c - pos))
            tab_ref[p, 4] = jnp.exp(lfl * c)
            tab_ref[p, 5] = jnp.exp(lbl * c)

    def summary(k, w, v):
        kw = (k.astype(F32) * w).astype(BF16)
        kv = lax.dot_general(kw, v, (((0,), (0,)), ((), ())), preferred_element_type=F32)
        return jnp.where(same_head, kv, 0.0)

    @pl.when(t < nc)
    def _backward():
        for p in range(n_pairs):
            sl = slice(p * LANES, (p + 1) * LANES)
            rnext_ref[nc - 1 - t, p] = rb_ref[p].astype(BF16)
            rb_ref[p] = (tab_ref[p, 5][:LANES] * rb_ref[p]
                         + summary(k_ref[:, sl], tab_ref[p, 1], v_ref[:, sl]))

    @pl.when(t >= nc)
    def _forward():
        @pl.when(t == nc)
        def _():
            rf_ref[...] = jnp.zeros_like(rf_ref)

        for p in range(n_pairs):
            sl = slice(p * LANES, (p + 1) * LANES)
            q = q_ref[:, sl]
            k = k_ref[:, sl]
            v = v_ref[:, sl]
            zero = jnp.zeros_like(q)
            q2 = jnp.concatenate([jnp.where(head0, q, zero), jnp.where(head0, zero, q)], axis=0)
            s = lax.dot_general(q2, k, (((1,), (1,)), ((), ())), preferred_element_type=F32)
            o2 = jnp.dot((s * mask_ref[p]).astype(BF16), v, preferred_element_type=F32)
            y = jnp.where(head0, o2[:c], o2[c:])
            y = y + jnp.dot(q, rf_ref[p].astype(BF16), preferred_element_type=F32) * tab_ref[p, 2]
            y = y + jnp.dot(q, rnext_ref[t - nc, p], preferred_element_type=F32) * tab_ref[p, 3]

            sq = y * y
            s0 = jnp.sum(jnp.where(head0, sq, 0.0), axis=1, keepdims=True)
            s1 = jnp.sum(jnp.where(head0, 0.0, sq), axis=1, keepdims=True)
            ms = jnp.where(head0, s0, s1) * (1.0 / HEAD_DIM)
            y = y * lax.rsqrt(ms + HEAD_NORM_EPS)
            g = g_ref[:, sl]
            o_ref[:, sl] = (g / (1.0 + jnp.exp(-g)) * y).astype(o_ref.dtype)

            rf_ref[p] = tab_ref[p, 4][:LANES] * rf_ref[p] + summary(k, tab_ref[p, 0], v)


def _retention(ret, gate, log_g, out, row0, batch, seq, chunk):
    nc = seq // chunk
    blk0 = row0 // chunk
    n_pairs = GROUP_WIDTH // LANES
    gw = GROUP_WIDTH

    def kc(t):
        return jnp.where(t < nc, nc - 1 - t, t - nc)

    def qc(t):
        return jnp.maximum(t - nc, 0)

    def rows(b, cc):
        return blk0 + b * nc + cc

    kernel = functools.partial(_ret_kernel, nc=nc, chunk=chunk)
    return pl.pallas_call(
        kernel,
        grid=(batch, 2 * nc),
        in_specs=[pl.BlockSpec(memory_space=pltpu.SMEM),
                  pl.BlockSpec((chunk, gw), lambda b, t: (rows(b, qc(t)), 0)),
                  pl.BlockSpec((chunk, gw), lambda b, t: (rows(b, kc(t)), 1)),
                  pl.BlockSpec((chunk, gw), lambda b, t: (rows(b, kc(t)), 2)),
                  pl.BlockSpec((chunk, gw), lambda b, t: (rows(b, qc(t)), 0)),
                  pl.BlockSpec(memory_space=pl.ANY)],
        out_specs=pl.BlockSpec((chunk, gw), lambda b, t: (rows(b, qc(t)), 0)),
        out_shape=jax.ShapeDtypeStruct(out.shape, out.dtype),
        input_output_aliases={5: 0},
        scratch_shapes=[pltpu.VMEM((n_pairs, LANES, LANES), F32),
                        pltpu.VMEM((n_pairs, LANES, LANES), F32),
                        pltpu.VMEM((nc, n_pairs, LANES, LANES), BF16),
                        pltpu.VMEM((n_pairs, 2 * chunk, chunk), F32),
                        pltpu.VMEM((n_pairs, 6, chunk, LANES), F32)],
        compiler_params=_cparams("parallel", "arbitrary"),
        name="retention",
    )(log_g, ret, ret, ret, gate, out)


def _bias_kernel(rb_ref, bucket_ref, o_ref):
    h = pl.program_id(0)
    bk = bucket_ref[...]
    out = jnp.zeros(bk.shape, F32)
    for n in range(N_BUCKETS):
        out = jnp.where(bk == n, rb_ref[n, h], out)
    o_ref[...] = out * LOG2E


def _bias_tiles(rel_bias, tile):
    assert tile > FAR_DISTANCE
    a = jnp.arange(tile, dtype=jnp.int32)[:, None]
    b = jnp.arange(tile, dtype=jnp.int32)[None, :]
    rel = jnp.stack([(d * tile + a - b) for d in range(-BIAS_REACH, BIAS_REACH + 1)])
    nb = N_BUCKETS // 2
    max_exact = nb // 2
    n = jnp.abs(rel)
    nf = jnp.maximum(n, 1).astype(F32)
    large = max_exact + (jnp.log(nf / max_exact) / math.log(MAX_DISTANCE / max_exact)
                         * (nb - max_exact)).astype(jnp.int32)
    large = jnp.minimum(large, nb - 1)
    bucket = jnp.where(rel > 0, nb, 0) + jnp.where(n < max_exact, n, large)
    return pl.pallas_call(
        _bias_kernel,
        grid=(N_HEADS, 2 * BIAS_REACH + 1),
        in_specs=[pl.BlockSpec(memory_space=pltpu.SMEM),
                  pl.BlockSpec((None, tile, tile), lambda h, d: (d, 0, 0))],
        out_specs=pl.BlockSpec((None, None, tile, tile), lambda h, d: (h, d, 0, 0)),
        out_shape=jax.ShapeDtypeStruct((N_HEADS, 2 * BIAS_REACH + 1, tile, tile), F32),
        compiler_params=_cparams("parallel", "arbitrary"),
        name="t5_bias_tiles",
    )(rel_bias.astype(F32), bucket.astype(jnp.int32))


def _attn_kernel(rb_ref, lamp_ref, dng_ref, q_ref, k_ref, vt_ref, bt_ref, _, o_ref,
                 a_ref, l_ref, p_ref, qa_ref, kmax_ref, m_ref, *, n_tiles, tile, macro):
    h = pl.program_id(1)
    qi = pl.program_id(2)
    hh = h % 2
    n = n_tiles
    half = DIFF_QK_DIM
    group_of_lane = lax.broadcasted_iota(jnp.int32, (LANES, LANES), 0) // half
    group_sum = (group_of_lane == lax.broadcasted_iota(jnp.int32, (LANES, LANES), 1)).astype(BF16)

    def half_norms(x):
        xf = x.astype(F32)
        return jnp.dot((xf * xf).astype(BF16), group_sum, preferred_element_type=F32)

    @pl.when(qi == 0)
    def _key_norms():
        def body(j, mx):
            kt = k_ref[pl.ds(pl.multiple_of(j * tile, tile), tile), :]
            return jnp.maximum(mx, jnp.max(half_norms(kt), axis=0, keepdims=True))
        kmax_ref[...] = lax.fori_loop(0, n, body, jnp.zeros((1, LANES), F32))

    qf = q_ref[...].astype(F32)
    lane = lax.broadcasted_iota(jnp.int32, qf.shape, 1)
    lane_row = lax.broadcasted_iota(jnp.int32, (1, LANES), 1)
    row = lax.broadcasted_iota(jnp.int32, (LANES, tile), 0)
    group_sum_t = (lax.broadcasted_iota(jnp.int32, (LANES, LANES), 0)
                   == lax.broadcasted_iota(jnp.int32, (LANES, LANES), 1) // half).astype(BF16)
    qn_rows = lax.dot_general(group_sum_t, (qf * qf).astype(BF16), (((1,), (1,)), ((), ())),
                              preferred_element_type=F32)
    for t in range(2):
        g = 2 * hh + t
        lo_lane = g * half
        qzt = jnp.where((lane >= lo_lane) & (lane < lo_lane + half), qf, 0.0).T.astype(BF16)
        qn2 = jnp.sum(jnp.where(row == g, qn_rows, 0.0), axis=0, keepdims=True)
        kn2 = jnp.max(jnp.where(lane_row == g, kmax_ref[...], 0.0), axis=1, keepdims=True)
        v = -(jnp.sqrt(qn2 * kn2) + rb_ref[N_BUCKETS, h])
        hi = v.astype(BF16).astype(F32)
        lo = (v - hi).astype(BF16).astype(F32)
        qa_ref[t, :LANES, :] = qzt
        qa_ref[t, LANES:, :] = jnp.where(row == 0, hi, jnp.where(row == 1, lo, 0.0)).astype(BF16)

    ones = jnp.ones((macro * tile, LANES), BF16)

    def probabilities(m, slot):
        j0 = m * macro
        kaug = jnp.concatenate([k_ref[j0 * tile:(j0 + macro) * tile, :], ones], axis=1)
        for t in range(2):
            s = jnp.dot(kaug, qa_ref[t], preferred_element_type=F32)
            for r in range(macro):
                bias = bt_ref[jnp.clip(j0 + r - qi, -BIAS_REACH, BIAS_REACH) + BIAS_REACH]
                sr = s[r * tile:(r + 1) * tile] + bias
                e = jnp.exp2(sr)
                l_ref[t] += jnp.sum(e, axis=0, keepdims=True)
                p_ref[slot, t, r * tile:(r + 1) * tile, :] = e.astype(BF16)

    def accumulate(m, slot):
        vt = jnp.concatenate([vt_ref[m * macro + r] for r in range(macro)], axis=1)
        for t in range(2):
            a_ref[t] += jnp.dot(vt, p_ref[slot, t], preferred_element_type=F32)

    a_ref[...] = jnp.zeros_like(a_ref)
    l_ref[...] = jnp.zeros_like(l_ref)

    n_macro = n // macro
    probabilities(0, 0)
    for m in range(n_macro):
        if m + 1 < n_macro:
            probabilities(m + 1, (m + 1) % 2)
        accumulate(m, m % 2)

    trusted = jnp.min(jnp.minimum(l_ref[0], l_ref[1])) >= UNDERFLOW_GUARD

    @pl.when(jnp.logical_not(trusted))
    def _exact_running_max():
        m_ref[...] = jnp.full(m_ref.shape, NEG_BIG, F32)
        a_ref[...] = jnp.zeros_like(a_ref)
        l_ref[...] = jnp.zeros_like(l_ref)

        def body(j, carry):
            kt = k_ref[pl.ds(pl.multiple_of(j * tile, tile), tile), :]
            vt = vt_ref[j]
            bias = bt_ref[jnp.clip(j - qi, -BIAS_REACH, BIAS_REACH) + BIAS_REACH]
            for t in range(2):
                s = jnp.dot(kt, qa_ref[t, :LANES, :], preferred_element_type=F32) + bias
                m_prev = m_ref[t]
                m_new = jnp.maximum(m_prev, jnp.max(s, axis=0, keepdims=True))
                alpha = jnp.exp2(m_prev - m_new)
                e = jnp.exp2(s - m_new)
                l_ref[t] = alpha * l_ref[t] + jnp.sum(e, axis=0, keepdims=True)
                a_ref[t] = alpha * a_ref[t] + jnp.dot(vt, e.astype(BF16),
                                                      preferred_element_type=F32)
                m_ref[t] = m_new
            return carry

        lax.fori_loop(0, n, body, 0)

    lp = lamp_ref[...]
    lam_init = lp[4:5, 0:1]
    lam = (jnp.exp(jnp.sum(lp[0:1] * lp[1:2], axis=1, keepdims=True))
           - jnp.exp(jnp.sum(lp[2:3] * lp[3:4], axis=1, keepdims=True)) + lam_init)
    o = a_ref[0] / l_ref[0] - lam * (a_ref[1] / l_ref[1])
    ms = jnp.mean(o * o, axis=0, keepdims=True)
    y = o * lax.rsqrt(ms + HEAD_NORM_EPS) * dng_ref[...] * (1.0 - lam_init)
    o_ref[...] = y.astype(o_ref.dtype)


def _diff_attention(dqk, vt3, bias_t, rel_log2, lamp, dng, out, row0, batch, seq, tile):
    n = seq // tile
    macro = max(g for g in (1, 2, 4, 8) if n % g == 0)
    assert n // macro <= MAX_UNROLLED_MACRO_TILES
    n_pairs = GROUP_WIDTH // LANES
    qblk0 = row0 // tile
    sblk0 = row0 // seq
    n_bias = 2 * BIAS_REACH + 1
    kernel = functools.partial(_attn_kernel, n_tiles=n, tile=tile, macro=macro)
    return pl.pallas_call(
        kernel,
        grid=(batch, N_HEADS, n),
        in_specs=[pl.BlockSpec(memory_space=pltpu.SMEM),
                  pl.BlockSpec((5, DIFF_QK_DIM), lambda b, h, i: (0, 0)),
                  pl.BlockSpec((HEAD_DIM, 1), lambda b, h, i: (0, 0)),
                  pl.BlockSpec((tile, LANES), lambda b, h, i: (qblk0 + b * n + i, h // 2)),
                  pl.BlockSpec((seq, LANES), lambda b, h, i: (sblk0 + b, n_pairs + h // 2)),
                  pl.BlockSpec((n, HEAD_DIM, tile), lambda b, h, i: (sblk0 + b, h, 0)),
                  pl.BlockSpec((None, n_bias, tile, tile), lambda b, h, i: (h, 0, 0, 0)),
                  pl.BlockSpec(memory_space=pl.ANY)],
        out_specs=pl.BlockSpec((None, HEAD_DIM, tile), lambda b, h, i: (qblk0 + b * n + i, h, 0)),
        out_shape=jax.ShapeDtypeStruct(out.shape, out.dtype),
        input_output_aliases={7: 0},
        scratch_shapes=[pltpu.VMEM((2, HEAD_DIM, tile), F32),
                        pltpu.VMEM((2, 1, tile), F32),
                        pltpu.VMEM((2, 2, macro * tile, tile), BF16),
                        pltpu.VMEM((2, 2 * LANES, tile), BF16),
                        pltpu.VMEM((1, LANES), F32),
                        pltpu.VMEM((2, 1, tile), F32)],
        compiler_params=_cparams("parallel", "parallel", "arbitrary"),
        name="diff_attention",
    )(rel_log2, lamp, dng, dqk, dqk, vt3, bias_t, out)


def _layer_norm(y, g, b):
    mu = jnp.mean(y, axis=-1, keepdims=True)
    d = y - mu
    var = jnp.mean(d * d, axis=-1, keepdims=True)
    return d * lax.rsqrt(var + LN_EPS) * g + b


def _outproj_kernel(yr_ref, yd_ref, w_ref, x_ref, g_ref, b_ref, o_ref, ob_ref, *, alpha):
    mix = jnp.dot(yr_ref[...], w_ref[:GROUP_WIDTH, :], preferred_element_type=F32)
    mix = mix + lax.dot_general(yd_ref[...], w_ref[GROUP_WIDTH:, :], (((0,), (0,)), ((), ())),
                                preferred_element_type=F32)
    y = _layer_norm(alpha * x_ref[...] + mix, g_ref[...], b_ref[...])
    o_ref[...] = y
    ob_ref[...] = y.astype(BF16)


def _out_projection(yr, ydt, w_out, x, g, b, alpha, tm):
    t = x.shape[0]
    assert ydt.shape == (t // tm, GROUP_WIDTH, tm)
    kernel = functools.partial(_outproj_kernel, alpha=alpha)
    row = lambda i: (i, 0)
    fixed = lambda i: (0, 0)
    return pl.pallas_call(
        kernel,
        grid=(t // tm,),
        in_specs=[pl.BlockSpec((tm, GROUP_WIDTH), row),
                  pl.BlockSpec((None, GROUP_WIDTH, tm), lambda i: (i, 0, 0)),
                  pl.BlockSpec((2 * GROUP_WIDTH, D_MODEL), fixed),
                  pl.BlockSpec((tm, D_MODEL), row),
                  pl.BlockSpec((1, D_MODEL), fixed),
                  pl.BlockSpec((1, D_MODEL), fixed)],
        out_specs=[pl.BlockSpec((tm, D_MODEL), row), pl.BlockSpec((tm, D_MODEL), row)],
        out_shape=[jax.ShapeDtypeStruct((t, D_MODEL), F32), jax.ShapeDtypeStruct((t, D_MODEL), BF16)],
        compiler_params=_cparams("parallel"),
        name="out_proj_ln",
    )(yr, ydt, w_out, x, g, b)


def _ffn_kernel(x_ref, xb_ref, xp_ref, xn_ref, wa_ref, wv_ref, wd_ref, cp_ref, g_ref, b_ref,
                o_ref, ob_ref, acc_ref, *, alpha, tm, n_chunks, starts, ends):
    i = pl.program_id(0)
    t0 = i * tm
    is_start = functools.reduce(jnp.logical_or, [t0 == s for s in starts])
    is_end = functools.reduce(jnp.logical_or, [t0 + tm == e for e in ends])
    keep_prev = jnp.where(is_start, 0.0, 1.0)
    keep_next = jnp.where(is_end, 0.0, 1.0)

    xb = xb_ref[...]
    halo = jnp.concatenate([xp_ref[...], xn_ref[...]], axis=0)
    hrows = xp_ref.shape[0]
    acc_ref[...] = jnp.zeros_like(acc_ref)

    def chunk_body(c, carry):
        wa = wa_ref[c]
        a = jnp.dot(xb, wa, preferred_element_type=F32)
        val = jnp.dot(xb, wv_ref[c], preferred_element_type=F32)
        ah = jnp.dot(halo, wa, preferred_element_type=F32)
        prev_row = ah[hrows - 1:hrows] * keep_prev
        next_row = ah[hrows:hrows + 1] * keep_next
        row = lax.broadcasted_iota(jnp.int32, a.shape, 0)
        a_m1 = jnp.where(row == 0, prev_row, pltpu.roll(a, 1, 0))
        a_p1 = jnp.where(row == tm - 1, next_row, pltpu.roll(a, tm - 1, 0))
        cp = cp_ref[c]
        conv = cp[3:4] + a_m1 * cp[0:1]
        conv = conv + a * cp[1:2]
        conv = conv + a_p1 * cp[2:3]
        gelu = 0.5 * conv * (1.0 + lax.erf(conv * (1.0 / math.sqrt(2.0))))
        hidden = (gelu * val).astype(BF16)
        acc_ref[...] += jnp.dot(hidden, wd_ref[c], preferred_element_type=F32)
        return carry

    for c in range(n_chunks):
        chunk_body(c, 0)
    y = _layer_norm(alpha * x_ref[...] + acc_ref[...], g_ref[...], b_ref[...])
    o_ref[...] = y
    ob_ref[...] = y.astype(BF16)


def _conv_glu(x, xb, wa, wv, wd, cp, g, b, alpha, tm, groups):
    t = x.shape[0]
    n_chunks, _, ck = wa.shape
    hrows = BF16_SUBLANES
    starts = tuple(r0 + bi * s for (r0, nb, s) in groups for bi in range(nb))
    ends = tuple(r0 + (bi + 1) * s for (r0, nb, s) in groups for bi in range(nb))
    kernel = functools.partial(_ffn_kernel, alpha=alpha, tm=tm, n_chunks=n_chunks,
                               starts=starts, ends=ends)
    row = lambda i: (i, 0)
    fixed2 = lambda i: (0, 0)
    fixed3 = lambda i: (0, 0, 0)
    per = tm // hrows
    last = t // hrows - 1
    return pl.pallas_call(
        kernel,
        grid=(t // tm,),
        in_specs=[pl.BlockSpec((tm, D_MODEL), row),
                  pl.BlockSpec((tm, D_MODEL), row),
                  pl.BlockSpec((hrows, D_MODEL), lambda i: (jnp.maximum(i * per - 1, 0), 0)),
                  pl.BlockSpec((hrows, D_MODEL), lambda i: (jnp.minimum((i + 1) * per, last), 0)),
                  pl.BlockSpec((n_chunks, D_MODEL, ck), fixed3, pipeline_mode=pl.Buffered(1)),
                  pl.BlockSpec((n_chunks, D_MODEL, ck), fixed3, pipeline_mode=pl.Buffered(1)),
                  pl.BlockSpec((n_chunks, ck, D_MODEL), fixed3, pipeline_mode=pl.Buffered(1)),
                  pl.BlockSpec((n_chunks, 8, ck), fixed3),
                  pl.BlockSpec((1, D_MODEL), fixed2),
                  pl.BlockSpec((1, D_MODEL), fixed2)],
        out_specs=[pl.BlockSpec((tm, D_MODEL), row), pl.BlockSpec((tm, D_MODEL), row)],
        out_shape=[jax.ShapeDtypeStruct((t, D_MODEL), F32), jax.ShapeDtypeStruct((t, D_MODEL), BF16)],
        scratch_shapes=[pltpu.VMEM((tm, D_MODEL), F32)],
        compiler_params=_cparams("parallel"),
        name="conv_glu_ln",
    )(x, xb, xb, xb, wa, wv, wd, cp, g, b)


def _rotary_tables(seq):
    d = HEAD_DIM
    inv = 1.0 / (ROPE_BASE ** (jnp.arange(0, d, 2, dtype=F32) / d))
    ang = jnp.arange(seq, dtype=F32)[:, None] * inv[None, :]
    cos, sin = jnp.cos(ang), jnp.sin(ang)
    cos_t = jnp.concatenate([cos, cos, cos, cos], axis=-1)
    sin_t = jnp.concatenate([-sin, sin, -sin, sin], axis=-1)
    return cos_t, sin_t


def _tiles(groups):
    smin = min(s for (_, _, s) in groups)
    attn_tile = min(512, smin // 2)
    chunk = min(256, smin // 2)
    tm = min(512, smin // 2)
    ffn_tm = min(1024, smin // 2)
    return attn_tile, chunk, tm, ffn_tm


def _forward(x, groups, w_in, ret_decay_logit, rel_bias, lambda_q1, lambda_k1, lambda_q2,
             lambda_k2, diff_norm_g, w_out, ln_g, ln_b, w_up, conv_w, conv_b, w_down):
    depth = w_in.shape[0]
    alpha = (2 * depth) ** 0.25
    t = x.shape[0]
    attn_tile, chunk, tm, ffn_tm = _tiles(groups)
    smax = max(s for (_, _, s) in groups)
    gw = GROUP_WIDTH
    ck = MXU_WIDTH
    n_chunks = D_FF // ck

    cos_t, sin_t = _rotary_tables(smax)
    bias_t = _bias_tiles(rel_bias, attn_tile)
    rel_log2 = rel_bias.astype(F32) * LOG2E
    rel_log2 = jnp.concatenate([rel_log2, jnp.max(rel_log2, axis=0, keepdims=True)], axis=0)
    in_scale = jnp.concatenate([
        jnp.ones((gw,), F32), jnp.full((gw,), HEAD_DIM ** -0.5, F32), jnp.ones((2 * gw,), F32),
        jnp.full((gw,), DIFF_QK_DIM ** -0.5 * LOG2E, F32), jnp.ones((gw,), F32)])[None, :]
    assert tm == attn_tile

    xb = x.astype(BF16)
    for l in range(depth):
        lam_init = 0.8 - 0.6 * math.exp(-0.3 * l)
        w_in_b = w_in[l].astype(BF16)
        ret, gate, dqk, vt3 = _project(xb, w_in_b[:, :6 * gw], w_in_b[:, 6 * gw:].T, in_scale,
                                       cos_t, sin_t, groups, tm)

        log_g = jax.nn.log_sigmoid(ret_decay_logit[l].astype(F32))
        lamp = jnp.stack([lambda_q1[l], lambda_k1[l], lambda_q2[l], lambda_k2[l],
                          jnp.full((DIFF_QK_DIM,), lam_init)]).astype(F32)
        dng = diff_norm_g[l].astype(F32)[:, None]

        yr = jnp.zeros((t, gw), BF16)
        ydt = jnp.zeros((t // attn_tile, gw, attn_tile), BF16)
        for (row0, batch, seq) in groups:
            yr = _retention(ret, gate, log_g, yr, row0, batch, seq, chunk)
            ydt = _diff_attention(dqk, vt3, bias_t, rel_log2, lamp, dng, ydt, row0, batch, seq,
                                  attn_tile)

        x, xb = _out_projection(yr, ydt, w_out[l].astype(BF16), x, ln_g[l, 0][None, :].astype(F32),
                                ln_b[l, 0][None, :].astype(F32), alpha, tm)

        wa = w_up[l][:, :D_FF].astype(BF16).reshape(D_MODEL, n_chunks, ck).transpose(1, 0, 2)
        wv = w_up[l][:, D_FF:].astype(BF16).reshape(D_MODEL, n_chunks, ck).transpose(1, 0, 2)
        wd = w_down[l].astype(BF16).reshape(n_chunks, ck, D_MODEL)
        cp = jnp.concatenate([conv_w[l].astype(F32), conv_b[l].astype(F32)[None, :],
                              jnp.zeros((4, D_FF), F32)], axis=0)
        cp = cp.reshape(8, n_chunks, ck).transpose(1, 0, 2)
        x, xb = _conv_glu(x, xb, wa, wv, wd, cp, ln_g[l, 1][None, :].astype(F32),
                          ln_b[l, 1][None, :].astype(F32), alpha, ffn_tm, groups)
    return x


def kernel(x_prompt, x_sample, w_in, ret_decay_logit, rel_bias, lambda_q1, lambda_k1, lambda_q2,
           lambda_k2, diff_norm_g, w_out, ln_g, ln_b, w_up, conv_w, conv_b, w_down):
    bp, sp, d = x_prompt.shape
    bs, ss, _ = x_sample.shape
    groups = ((0, bp, sp), (bp * sp, bs, ss))
    x = jnp.concatenate([x_prompt.reshape(bp * sp, d), x_sample.reshape(bs * ss, d)], axis=0)
    y = _forward(x.astype(F32), groups, w_in, ret_decay_logit, rel_bias, lambda_q1, lambda_k1,
                 lambda_q2, lambda_k2, diff_norm_g, w_out, ln_g, ln_b, w_up, conv_w, conv_b, w_down)
    y_prompt = y[:bp * sp].reshape(bp, sp, d).astype(x_prompt.dtype)
    y_sample = y[bp * sp:].reshape(bs, ss, d).astype(x_sample.dtype)
    return y_prompt, y_sample
```

```python
import functools
import math

import jax
import jax.numpy as jnp
from jax import lax
from jax.experimental import pallas as pl
from jax.experimental.pallas import tpu as pltpu

D_MODEL = 1024
HEAD_DIM = 64
N_HEADS = 8
GROUP_WIDTH = N_HEADS * HEAD_DIM
DIFF_QK_DIM = HEAD_DIM // 2
D_FF = 2816
N_BUCKETS = 32
MAX_DISTANCE = 128
ROPE_BASE = 10000.0
LN_EPS = 1e-5
HEAD_NORM_EPS = 1e-6
LANES = 128
BF16_SUBLANES = 16
MXU_WIDTH = 256
FAR_DISTANCE = 91
BIAS_REACH = 2
MAX_UNROLLED_MACRO_TILES = 8
LOG2E = math.log2(math.e)
NEG_BIG = -1e30
UNDERFLOW_GUARD = 2.0 ** -90
VMEM_LIMIT = 56 * 1024 * 1024

F32 = jnp.float32
BF16 = jnp.bfloat16


def _cparams(*sem):
    return pltpu.CompilerParams(dimension_semantics=sem, vmem_limit_bytes=VMEM_LIMIT)


def _proj_kernel(x_ref, w_ref, wvt_ref, s_ref, cos_ref, sin_ref, ret_ref, gate_ref, dqk_ref, vt_ref):
    gw = GROUP_WIDTH
    x = x_ref[...]
    tm = x.shape[0]
    lane = lax.broadcasted_iota(jnp.int32, (tm, LANES), 1)
    low_half = (lane % HEAD_DIM) < (HEAD_DIM // 2)
    cos = cos_ref[...]
    sin = sin_ref[...]
    qk = jnp.dot(x, w_ref[:, :2 * gw], preferred_element_type=F32) * s_ref[:, :2 * gw]
    for p in range(2 * gw // LANES):
        sl = slice(p * LANES, (p + 1) * LANES)
        xx = qk[:, sl]
        swapped = jnp.where(low_half, pltpu.roll(xx, LANES - HEAD_DIM // 2, 1),
                            pltpu.roll(xx, HEAD_DIM // 2, 1))
        ret_ref[:, sl] = (xx * cos + swapped * sin).astype(ret_ref.dtype)
    ret_ref[:, 2 * gw:] = jnp.dot(x, w_ref[:, 2 * gw:3 * gw],
                                  preferred_element_type=F32).astype(ret_ref.dtype)
    gate_ref[...] = jnp.dot(x, w_ref[:, 3 * gw:4 * gw], preferred_element_type=F32)
    dqk_ref[...] = (jnp.dot(x, w_ref[:, 4 * gw:], preferred_element_type=F32)
                    * s_ref[:, 4 * gw:]).astype(dqk_ref.dtype)
    vt_ref[...] = lax.dot_general(wvt_ref[...], x, (((1,), (1,)), ((), ())),
                                  preferred_element_type=F32).astype(vt_ref.dtype)


def _position_block(i, tm, groups):
    t0 = i * tm
    blk = t0 // tm
    for (row0, _, seq) in groups:
        blk = jnp.where(t0 >= row0, ((t0 - row0) % seq) // tm, blk)
    return blk


def _project(xb, w, w_vt, scale, cos_t, sin_t, groups, tm):
    t, k = xb.shape
    gw = GROUP_WIDTH
    row = lambda i: (i, 0)
    fixed = lambda i: (0, 0)
    pos = lambda i: (_position_block(i, tm, groups), 0)
    return pl.pallas_call(
        _proj_kernel,
        grid=(t // tm,),
        in_specs=[pl.BlockSpec((tm, k), row),
                  pl.BlockSpec((k, 6 * gw), fixed, pipeline_mode=pl.Buffered(1)),
                  pl.BlockSpec((gw, k), fixed, pipeline_mode=pl.Buffered(1)),
                  pl.BlockSpec((1, 6 * gw), fixed),
                  pl.BlockSpec((tm, LANES), pos),
                  pl.BlockSpec((tm, LANES), pos)],
        out_specs=[pl.BlockSpec((tm, 3 * gw), row), pl.BlockSpec((tm, gw), row),
                   pl.BlockSpec((tm, 2 * gw), row),
                   pl.BlockSpec((None, gw, tm), lambda i: (i, 0, 0))],
        out_shape=[jax.ShapeDtypeStruct((t, 3 * gw), BF16), jax.ShapeDtypeStruct((t, gw), F32),
                   jax.ShapeDtypeStruct((t, 2 * gw), BF16),
                   jax.ShapeDtypeStruct((t // tm, gw, tm), BF16)],
        compiler_params=_cparams("parallel"),
        name="in_proj",
    )(xb, w, w_vt, scale, cos_t, sin_t)


def _ret_kernel(lg_ref, q_ref, k_ref, v_ref, g_ref, _, o_ref,
                rf_ref, rb_ref, rnext_ref, mask_ref, tab_ref, *, ns, per_step, chunk):
    t = pl.program_id(1)
    c = chunk
    n_pairs = GROUP_WIDTH // LANES
    lane = lax.broadcasted_iota(jnp.int32, (c, LANES), 1)
    head0 = lane < HEAD_DIM
    r_i = lax.broadcasted_iota(jnp.int32, (LANES, LANES), 0) // HEAD_DIM
    c_i = lax.broadcasted_iota(jnp.int32, (LANES, LANES), 1) // HEAD_DIM
    same_head = r_i == c_i

    @pl.when(t == 0)
    def _init():
        rb_ref[...] = jnp.zeros_like(rb_ref)
        qi = lax.broadcasted_iota(jnp.int32, (c, c), 0)
        ki = lax.broadcasted_iota(jnp.int32, (c, c), 1)
        diff = (qi - ki).astype(F32)
        pos = lax.broadcasted_iota(jnp.int32, (c, LANES), 0).astype(F32)
        for p in range(n_pairs):
            for hh in range(2):
                lf = lg_ref[0, 2 * p + hh]
                lb = lg_ref[1, 2 * p + hh]
                mask_ref[p, hh * c:(hh + 1) * c, :] = jnp.where(
                    diff >= 0, jnp.exp(lf * jnp.maximum(diff, 0.0)),
                    jnp.exp(lb * jnp.maximum(-diff, 0.0)))
            lfl = jnp.where(head0, lg_ref[0, 2 * p], lg_ref[0, 2 * p + 1])
            lbl = jnp.where(head0, lg_ref[1, 2 * p], lg_ref[1, 2 * p + 1])
            tab_ref[p, 0] = jnp.exp(lfl * (c - 1 - pos))
            tab_ref[p, 1] = jnp.exp(lbl * pos)
            tab_ref[p, 2] = jnp.exp(lfl * (pos + 1.0))
            tab_ref[p, 3] = jnp.exp(lbl * (c - pos))
            tab_ref[p, 4] = jnp.exp(lfl * c)
            tab_ref[p, 5] = jnp.exp(lbl * c)

    def summary(k, w, v):
        kw = (k.astype(F32) * w).astype(BF16)
        kv = lax.dot_general(kw, v, (((0,), (0,)), ((), ())), preferred_element_type=F32)
        return jnp.where(same_head, kv, 0.0)

    @pl.when(t < ns)
    def _backward():
        for u in reversed(range(per_step)):
            rows = slice(u * c, (u + 1) * c)
            ci = (ns - 1 - t) * per_step + u
            for p in range(n_pairs):
                sl = slice(p * LANES, (p + 1) * LANES)
                rnext_ref[ci, p] = rb_ref[p].astype(BF16)
                rb_ref[p] = (tab_ref[p, 5][:LANES] * rb_ref[p]
                             + summary(k_ref[rows, sl], tab_ref[p, 1], v_ref[rows, sl]))

    @pl.when(t >= ns)
    def _forward():
        @pl.when(t == ns)
        def _():
            rf_ref[...] = jnp.zeros_like(rf_ref)

        for u in range(per_step):
            rows = slice(u * c, (u + 1) * c)
            ci = (t - ns) * per_step + u
            for p in range(n_pairs):
                sl = slice(p * LANES, (p + 1) * LANES)
                q = q_ref[rows, sl]
                k = k_ref[rows, sl]
                v = v_ref[rows, sl]
                zero = jnp.zeros_like(q)
                q2 = jnp.concatenate([jnp.where(head0, q, zero), jnp.where(head0, zero, q)], axis=0)
                s = lax.dot_general(q2, k, (((1,), (1,)), ((), ())), preferred_element_type=F32)
                o2 = jnp.dot((s * mask_ref[p]).astype(BF16), v, preferred_element_type=F32)
                y = jnp.where(head0, o2[:c], o2[c:])
                y = y + jnp.dot(q, rf_ref[p].astype(BF16), preferred_element_type=F32) * tab_ref[p, 2]
                y = y + jnp.dot(q, rnext_ref[ci, p], preferred_element_type=F32) * tab_ref[p, 3]

                sq = y * y
                s0 = jnp.sum(jnp.where(head0, sq, 0.0), axis=1, keepdims=True)
                s1 = jnp.sum(jnp.where(head0, 0.0, sq), axis=1, keepdims=True)
                ms = jnp.where(head0, s0, s1) * (1.0 / HEAD_DIM)
                y = y * lax.rsqrt(ms + HEAD_NORM_EPS)
                g = g_ref[rows, sl]
                o_ref[rows, sl] = (g / (1.0 + jnp.exp(-g)) * y).astype(o_ref.dtype)

                rf_ref[p] = tab_ref[p, 4][:LANES] * rf_ref[p] + summary(k, tab_ref[p, 0], v)


def _retention(ret, gate, log_g, out, row0, batch, seq, chunk):
    nc = seq // chunk
    per_step = max(u for u in (1, 2, 4) if nc % u == 0)
    ns = nc // per_step
    step_rows = per_step * chunk
    blk0 = row0 // step_rows
    n_pairs = GROUP_WIDTH // LANES
    gw = GROUP_WIDTH

    def kc(t):
        return jnp.where(t < ns, ns - 1 - t, t - ns)

    def qc(t):
        return jnp.maximum(t - ns, 0)

    def rows(b, cc):
        return blk0 + b * ns + cc

    kernel = functools.partial(_ret_kernel, ns=ns, per_step=per_step, chunk=chunk)
    return pl.pallas_call(
        kernel,
        grid=(batch, 2 * ns),
        in_specs=[pl.BlockSpec(memory_space=pltpu.SMEM),
                  pl.BlockSpec((step_rows, gw), lambda b, t: (rows(b, qc(t)), 0)),
                  pl.BlockSpec((step_rows, gw), lambda b, t: (rows(b, kc(t)), 1)),
                  pl.BlockSpec((step_rows, gw), lambda b, t: (rows(b, kc(t)), 2)),
                  pl.BlockSpec((step_rows, gw), lambda b, t: (rows(b, qc(t)), 0)),
                  pl.BlockSpec(memory_space=pl.ANY)],
        out_specs=pl.BlockSpec((step_rows, gw), lambda b, t: (rows(b, qc(t)), 0)),
        out_shape=jax.ShapeDtypeStruct(out.shape, out.dtype),
        input_output_aliases={5: 0},
        scratch_shapes=[pltpu.VMEM((n_pairs, LANES, LANES), F32),
                        pltpu.VMEM((n_pairs, LANES, LANES), F32),
                        pltpu.VMEM((nc, n_pairs, LANES, LANES), BF16),
                        pltpu.VMEM((n_pairs, 2 * chunk, chunk), F32),
                        pltpu.VMEM((n_pairs, 6, chunk, LANES), F32)],
        compiler_params=_cparams("parallel", "arbitrary"),
        name="retention",
    )(log_g, ret, ret, ret, gate, out)


def _bias_kernel(rb_ref, bucket_ref, o_ref):
    h = pl.program_id(0)
    bk = bucket_ref[...]
    out = jnp.zeros(bk.shape, F32)
    for n in range(N_BUCKETS):
        out = jnp.where(bk == n, rb_ref[n, h], out)
    o_ref[...] = out * LOG2E


def _bias_tiles(rel_bias, tile):
    assert tile > FAR_DISTANCE
    a = jnp.arange(tile, dtype=jnp.int32)[:, None]
    b = jnp.arange(tile, dtype=jnp.int32)[None, :]
    rel = jnp.stack([(d * tile + a - b) for d in range(-BIAS_REACH, BIAS_REACH + 1)])
    nb = N_BUCKETS // 2
    max_exact = nb // 2
    n = jnp.abs(rel)
    nf = jnp.maximum(n, 1).astype(F32)
    large = max_exact + (jnp.log(nf / max_exact) / math.log(MAX_DISTANCE / max_exact)
                         * (nb - max_exact)).astype(jnp.int32)
    large = jnp.minimum(large, nb - 1)
    bucket = jnp.where(rel > 0, nb, 0) + jnp.where(n < max_exact, n, large)
    return pl.pallas_call(
        _bias_kernel,
        grid=(N_HEADS, 2 * BIAS_REACH + 1),
        in_specs=[pl.BlockSpec(memory_space=pltpu.SMEM),
                  pl.BlockSpec((None, tile, tile), lambda h, d: (d, 0, 0))],
        out_specs=pl.BlockSpec((None, None, tile, tile), lambda h, d: (h, d, 0, 0)),
        out_shape=jax.ShapeDtypeStruct((N_HEADS, 2 * BIAS_REACH + 1, tile, tile), F32),
        compiler_params=_cparams("parallel", "arbitrary"),
        name="t5_bias_tiles",
    )(rel_bias.astype(F32), bucket.astype(jnp.int32))


def _attn_kernel(rb_ref, lamp_ref, dng_ref, q_ref, k_ref, vt_ref, bt_ref, _, o_ref,
                 a_ref, l_ref, p_ref, qa_ref, kmax_ref, m_ref, *, n_tiles, tile, macro):
    h = pl.program_id(1)
    qi = pl.program_id(2)
    hh = h % 2
    n = n_tiles
    half = DIFF_QK_DIM
    group_of_lane = lax.broadcasted_iota(jnp.int32, (LANES, LANES), 0) // half
    group_sum = (group_of_lane == lax.broadcasted_iota(jnp.int32, (LANES, LANES), 1)).astype(BF16)

    def half_norms(x):
        xf = x.astype(F32)
        return jnp.dot((xf * xf).astype(BF16), group_sum, preferred_element_type=F32)

    @pl.when(qi == 0)
    def _key_norms():
        def body(j, mx):
            kt = k_ref[pl.ds(pl.multiple_of(j * tile, tile), tile), :]
            return jnp.maximum(mx, jnp.max(half_norms(kt), axis=0, keepdims=True))
        kmax_ref[...] = lax.fori_loop(0, n, body, jnp.zeros((1, LANES), F32))

    qf = q_ref[...].astype(F32)
    lane = lax.broadcasted_iota(jnp.int32, qf.shape, 1)
    lane_row = lax.broadcasted_iota(jnp.int32, (1, LANES), 1)
    row = lax.broadcasted_iota(jnp.int32, (LANES, tile), 0)
    group_sum_t = (lax.broadcasted_iota(jnp.int32, (LANES, LANES), 0)
                   == lax.broadcasted_iota(jnp.int32, (LANES, LANES), 1) // half).astype(BF16)
    qn_rows = lax.dot_general(group_sum_t, (qf * qf).astype(BF16), (((1,), (1,)), ((), ())),
                              preferred_element_type=F32)
    for t in range(2):
        g = 2 * hh + t
        lo_lane = g * half
        qzt = jnp.where((lane >= lo_lane) & (lane < lo_lane + half), qf, 0.0).T.astype(BF16)
        qn2 = jnp.sum(jnp.where(row == g, qn_rows, 0.0), axis=0, keepdims=True)
        kn2 = jnp.max(jnp.where(lane_row == g, kmax_ref[...], 0.0), axis=1, keepdims=True)
        v = -(jnp.sqrt(qn2 * kn2) + rb_ref[N_BUCKETS, h])
        hi = v.astype(BF16).astype(F32)
        lo = (v - hi).astype(BF16).astype(F32)
        qa_ref[t, :LANES, :] = qzt
        qa_ref[t, LANES:, :] = jnp.where(row == 0, hi, jnp.where(row == 1, lo, 0.0)).astype(BF16)

    ones = jnp.ones((macro * tile, LANES), BF16)

    def probabilities(m, slot):
        j0 = m * macro
        kaug = jnp.concatenate([k_ref[j0 * tile:(j0 + macro) * tile, :], ones], axis=1)
        for t in range(2):
            s = jnp.dot(kaug, qa_ref[t], preferred_element_type=F32)
            for r in range(macro):
                bias = bt_ref[jnp.clip(j0 + r - qi, -BIAS_REACH, BIAS_REACH) + BIAS_REACH]
                sr = s[r * tile:(r + 1) * tile] + bias
                e = jnp.exp2(sr)
                l_ref[t] += jnp.sum(e, axis=0, keepdims=True)
                p_ref[slot, t, r * tile:(r + 1) * tile, :] = e.astype(BF16)

    def accumulate(m, slot):
        vt = jnp.concatenate([vt_ref[m * macro + r] for r in range(macro)], axis=1)
        for t in range(2):
            a_ref[t] += jnp.dot(vt, p_ref[slot, t], preferred_element_type=F32)

    a_ref[...] = jnp.zeros_like(a_ref)
    l_ref[...] = jnp.zeros_like(l_ref)

    n_macro = n // macro
    probabilities(0, 0)
    for m in range(n_macro):
        if m + 1 < n_macro:
            probabilities(m + 1, (m + 1) % 2)
        accumulate(m, m % 2)

    trusted = jnp.min(jnp.minimum(l_ref[0], l_ref[1])) >= UNDERFLOW_GUARD

    @pl.when(jnp.logical_not(trusted))
    def _exact_running_max():
        m_ref[...] = jnp.full(m_ref.shape, NEG_BIG, F32)
        a_ref[...] = jnp.zeros_like(a_ref)
        l_ref[...] = jnp.zeros_like(l_ref)

        def body(j, carry):
            kt = k_ref[pl.ds(pl.multiple_of(j * tile, tile), tile), :]
            vt = vt_ref[j]
            bias = bt_ref[jnp.clip(j - qi, -BIAS_REACH, BIAS_REACH) + BIAS_REACH]
            for t in range(2):
                s = jnp.dot(kt, qa_ref[t, :LANES, :], preferred_element_type=F32) + bias
                m_prev = m_ref[t]
                m_new = jnp.maximum(m_prev, jnp.max(s, axis=0, keepdims=True))
                alpha = jnp.exp2(m_prev - m_new)
                e = jnp.exp2(s - m_new)
                l_ref[t] = alpha * l_ref[t] + jnp.sum(e, axis=0, keepdims=True)
                a_ref[t] = alpha * a_ref[t] + jnp.dot(vt, e.astype(BF16),
                                                      preferred_element_type=F32)
                m_ref[t] = m_new
            return carry

        lax.fori_loop(0, n, body, 0)

    lp = lamp_ref[...]
    lam_init = lp[4:5, 0:1]
    lam = (jnp.exp(jnp.sum(lp[0:1] * lp[1:2], axis=1, keepdims=True))
           - jnp.exp(jnp.sum(lp[2:3] * lp[3:4], axis=1, keepdims=True)) + lam_init)
    o = a_ref[0] / l_ref[0] - lam * (a_ref[1] / l_ref[1])
    ms = jnp.mean(o * o, axis=0, keepdims=True)
    y = o * lax.rsqrt(ms + HEAD_NORM_EPS) * dng_ref[...] * (1.0 - lam_init)
    o_ref[...] = y.astype(o_ref.dtype)


def _diff_attention(dqk, vt3, bias_t, rel_log2, lamp, dng, out, row0, batch, seq, tile):
    n = seq // tile
    macro = max(g for g in (1, 2, 4, 8) if n % g == 0)
    assert n // macro <= MAX_UNROLLED_MACRO_TILES
    n_pairs = GROUP_WIDTH // LANES
    qblk0 = row0 // tile
    sblk0 = row0 // seq
    n_bias = 2 * BIAS_REACH + 1
    kernel = functools.partial(_attn_kernel, n_tiles=n, tile=tile, macro=macro)
    return pl.pallas_call(
        kernel,
        grid=(batch, N_HEADS, n),
        in_specs=[pl.BlockSpec(memory_space=pltpu.SMEM),
                  pl.BlockSpec((5, DIFF_QK_DIM), lambda b, h, i: (0, 0)),
                  pl.BlockSpec((HEAD_DIM, 1), lambda b, h, i: (0, 0)),
                  pl.BlockSpec((tile, LANES), lambda b, h, i: (qblk0 + b * n + i, h // 2)),
                  pl.BlockSpec((seq, LANES), lambda b, h, i: (sblk0 + b, n_pairs + h // 2)),
                  pl.BlockSpec((n, HEAD_DIM, tile), lambda b, h, i: (sblk0 + b, h, 0)),
                  pl.BlockSpec((None, n_bias, tile, tile), lambda b, h, i: (h, 0, 0, 0)),
                  pl.BlockSpec(memory_space=pl.ANY)],
        out_specs=pl.BlockSpec((None, HEAD_DIM, tile), lambda b, h, i: (qblk0 + b * n + i, h, 0)),
        out_shape=jax.ShapeDtypeStruct(out.shape, out.dtype),
        input_output_aliases={7: 0},
        scratch_shapes=[pltpu.VMEM((2, HEAD_DIM, tile), F32),
                        pltpu.VMEM((2, 1, tile), F32),
                        pltpu.VMEM((2, 2, macro * tile, tile), BF16),
                        pltpu.VMEM((2, 2 * LANES, tile), BF16),
                        pltpu.VMEM((1, LANES), F32),
                        pltpu.VMEM((2, 1, tile), F32)],
        compiler_params=_cparams("parallel", "parallel", "arbitrary"),
        name="diff_attention",
    )(rel_log2, lamp, dng, dqk, dqk, vt3, bias_t, out)


def _layer_norm(y, g, b):
    mu = jnp.mean(y, axis=-1, keepdims=True)
    d = y - mu
    var = jnp.mean(d * d, axis=-1, keepdims=True)
    return d * lax.rsqrt(var + LN_EPS) * g + b


def _outproj_kernel(yr_ref, yd_ref, w_ref, x_ref, g_ref, b_ref, o_ref, ob_ref, *, alpha):
    mix = jnp.dot(yr_ref[...], w_ref[:GROUP_WIDTH, :], preferred_element_type=F32)
    mix = mix + lax.dot_general(yd_ref[...], w_ref[GROUP_WIDTH:, :], (((0,), (0,)), ((), ())),
                                preferred_element_type=F32)
    y = _layer_norm(alpha * x_ref[...] + mix, g_ref[...], b_ref[...])
    o_ref[...] = y
    ob_ref[...] = y.astype(BF16)


def _out_projection(yr, ydt, w_out, x, g, b, alpha, tm):
    t = x.shape[0]
    assert ydt.shape == (t // tm, GROUP_WIDTH, tm)
    kernel = functools.partial(_outproj_kernel, alpha=alpha)
    row = lambda i: (i, 0)
    fixed = lambda i: (0, 0)
    return pl.pallas_call(
        kernel,
        grid=(t // tm,),
        in_specs=[pl.BlockSpec((tm, GROUP_WIDTH), row),
                  pl.BlockSpec((None, GROUP_WIDTH, tm), lambda i: (i, 0, 0)),
                  pl.BlockSpec((2 * GROUP_WIDTH, D_MODEL), fixed),
                  pl.BlockSpec((tm, D_MODEL), row),
                  pl.BlockSpec((1, D_MODEL), fixed),
                  pl.BlockSpec((1, D_MODEL), fixed)],
        out_specs=[pl.BlockSpec((tm, D_MODEL), row), pl.BlockSpec((tm, D_MODEL), row)],
        out_shape=[jax.ShapeDtypeStruct((t, D_MODEL), F32), jax.ShapeDtypeStruct((t, D_MODEL), BF16)],
        compiler_params=_cparams("parallel"),
        name="out_proj_ln",
    )(yr, ydt, w_out, x, g, b)


def _ffn_kernel(x_ref, xb_ref, xp_ref, xn_ref, wa_ref, wv_ref, wd_ref, cp_ref, g_ref, b_ref,
                o_ref, ob_ref, acc_ref, *, alpha, tm, n_chunks, starts, ends):
    i = pl.program_id(0)
    t0 = i * tm
    is_start = functools.reduce(jnp.logical_or, [t0 == s for s in starts])
    is_end = functools.reduce(jnp.logical_or, [t0 + tm == e for e in ends])
    keep_prev = jnp.where(is_start, 0.0, 1.0)
    keep_next = jnp.where(is_end, 0.0, 1.0)

    xb = xb_ref[...]
    halo = jnp.concatenate([xp_ref[...], xn_ref[...]], axis=0)
    hrows = xp_ref.shape[0]
    acc_ref[...] = jnp.zeros_like(acc_ref)

    def chunk_body(c, carry):
        wa = wa_ref[c]
        a = jnp.dot(xb, wa, preferred_element_type=F32)
        val = jnp.dot(xb, wv_ref[c], preferred_element_type=F32)
        ah = jnp.dot(halo, wa, preferred_element_type=F32)
        prev_row = ah[hrows - 1:hrows] * keep_prev
        next_row = ah[hrows:hrows + 1] * keep_next
        row = lax.broadcasted_iota(jnp.int32, a.shape, 0)
        a_m1 = jnp.where(row == 0, prev_row, pltpu.roll(a, 1, 0))
        a_p1 = jnp.where(row == tm - 1, next_row, pltpu.roll(a, tm - 1, 0))
        cp = cp_ref[c]
        conv = cp[3:4] + a_m1 * cp[0:1]
        conv = conv + a * cp[1:2]
        conv = conv + a_p1 * cp[2:3]
        gelu = 0.5 * conv * (1.0 + lax.erf(conv * (1.0 / math.sqrt(2.0))))
        hidden = (gelu * val).astype(BF16)
        acc_ref[...] += jnp.dot(hidden, wd_ref[c], preferred_element_type=F32)
        return carry

    for c in range(n_chunks):
        chunk_body(c, 0)
    y = _layer_norm(alpha * x_ref[...] + acc_ref[...], g_ref[...], b_ref[...])
    o_ref[...] = y
    ob_ref[...] = y.astype(BF16)


def _conv_glu(x, xb, wa, wv, wd, cp, g, b, alpha, tm, groups):
    t = x.shape[0]
    n_chunks, _, ck = wa.shape
    hrows = BF16_SUBLANES
    starts = tuple(r0 + bi * s for (r0, nb, s) in groups for bi in range(nb))
    ends = tuple(r0 + (bi + 1) * s for (r0, nb, s) in groups for bi in range(nb))
    kernel = functools.partial(_ffn_kernel, alpha=alpha, tm=tm, n_chunks=n_chunks,
                               starts=starts, ends=ends)
    row = lambda i: (i, 0)
    fixed2 = lambda i: (0, 0)
    fixed3 = lambda i: (0, 0, 0)
    per = tm // hrows
    last = t // hrows - 1
    return pl.pallas_call(
        kernel,
        grid=(t // tm,),
        in_specs=[pl.BlockSpec((tm, D_MODEL), row),
                  pl.BlockSpec((tm, D_MODEL), row),
                  pl.BlockSpec((hrows, D_MODEL), lambda i: (jnp.maximum(i * per - 1, 0), 0)),
                  pl.BlockSpec((hrows, D_MODEL), lambda i: (jnp.minimum((i + 1) * per, last), 0)),
                  pl.BlockSpec((n_chunks, D_MODEL, ck), fixed3, pipeline_mode=pl.Buffered(1)),
                  pl.BlockSpec((n_chunks, D_MODEL, ck), fixed3, pipeline_mode=pl.Buffered(1)),
                  pl.BlockSpec((n_chunks, ck, D_MODEL), fixed3, pipeline_mode=pl.Buffered(1)),
                  pl.BlockSpec((n_chunks, 8, ck), fixed3),
                  pl.BlockSpec((1, D_MODEL), fixed2),
                  pl.BlockSpec((1, D_MODEL), fixed2)],
        out_specs=[pl.BlockSpec((tm, D_MODEL), row), pl.BlockSpec((tm, D_MODEL), row)],
        out_shape=[jax.ShapeDtypeStruct((t, D_MODEL), F32), jax.ShapeDtypeStruct((t, D_MODEL), BF16)],
        scratch_shapes=[pltpu.VMEM((tm, D_MODEL), F32)],
        compiler_params=_cparams("parallel"),
        name="conv_glu_ln",
    )(x, xb, xb, xb, wa, wv, wd, cp, g, b)


def _rotary_tables(seq):
    d = HEAD_DIM
    inv = 1.0 / (ROPE_BASE ** (jnp.arange(0, d, 2, dtype=F32) / d))
    ang = jnp.arange(seq, dtype=F32)[:, None] * inv[None, :]
    cos, sin = jnp.cos(ang), jnp.sin(ang)
    cos_t = jnp.concatenate([cos, cos, cos, cos], axis=-1)
    sin_t = jnp.concatenate([-sin, sin, -sin, sin], axis=-1)
    return cos_t, sin_t


def _tiles(groups):
    smin = min(s for (_, _, s) in groups)
    attn_tile = min(512, smin // 2)
    chunk = min(256, smin // 2)
    tm = min(512, smin // 2)
    ffn_tm = min(1024, smin // 2)
    return attn_tile, chunk, tm, ffn_tm


def _forward(x, groups, w_in, ret_decay_logit, rel_bias, lambda_q1, lambda_k1, lambda_q2,
             lambda_k2, diff_norm_g, w_out, ln_g, ln_b, w_up, conv_w, conv_b, w_down):
    depth = w_in.shape[0]
    alpha = (2 * depth) ** 0.25
    t = x.shape[0]
    attn_tile, chunk, tm, ffn_tm = _tiles(groups)
    smax = max(s for (_, _, s) in groups)
    gw = GROUP_WIDTH
    ck = MXU_WIDTH
    n_chunks = D_FF // ck

    cos_t, sin_t = _rotary_tables(smax)
    bias_t = _bias_tiles(rel_bias, attn_tile)
    rel_log2 = rel_bias.astype(F32) * LOG2E
    rel_log2 = jnp.concatenate([rel_log2, jnp.max(rel_log2, axis=0, keepdims=True)], axis=0)
    in_scale = jnp.concatenate([
        jnp.ones((gw,), F32), jnp.full((gw,), HEAD_DIM ** -0.5, F32), jnp.ones((2 * gw,), F32),
        jnp.full((gw,), DIFF_QK_DIM ** -0.5 * LOG2E, F32), jnp.ones((gw,), F32)])[None, :]
    assert tm == attn_tile

    xb = x.astype(BF16)
    for l in range(depth):
        lam_init = 0.8 - 0.6 * math.exp(-0.3 * l)
        w_in_b = w_in[l].astype(BF16)
        ret, gate, dqk, vt3 = _project(xb, w_in_b[:, :6 * gw], w_in_b[:, 6 * gw:].T, in_scale,
                                       cos_t, sin_t, groups, tm)

        log_g = jax.nn.log_sigmoid(ret_decay_logit[l].astype(F32))
        lamp = jnp.stack([lambda_q1[l], lambda_k1[l], lambda_q2[l], lambda_k2[l],
                          jnp.full((DIFF_QK_DIM,), lam_init)]).astype(F32)
        dng = diff_norm_g[l].astype(F32)[:, None]

        yr = jnp.zeros((t, gw), BF16)
        ydt = jnp.zeros((t // attn_tile, gw, attn_tile), BF16)
        for (row0, batch, seq) in groups:
            yr = _retention(ret, gate, log_g, yr, row0, batch, seq, chunk)
            ydt = _diff_attention(dqk, vt3, bias_t, rel_log2, lamp, dng, ydt, row0, batch, seq,
                                  attn_tile)

        x, xb = _out_projection(yr, ydt, w_out[l].astype(BF16), x, ln_g[l, 0][None, :].astype(F32),
                                ln_b[l, 0][None, :].astype(F32), alpha, tm)

        wa = w_up[l][:, :D_FF].astype(BF16).reshape(D_MODEL, n_chunks, ck).transpose(1, 0, 2)
        wv = w_up[l][:, D_FF:].astype(BF16).reshape(D_MODEL, n_chunks, ck).transpose(1, 0, 2)
        wd = w_down[l].astype(BF16).reshape(n_chunks, ck, D_MODEL)
        cp = jnp.concatenate([conv_w[l].astype(F32), conv_b[l].astype(F32)[None, :],
                              jnp.zeros((4, D_FF), F32)], axis=0)
        cp = cp.reshape(8, n_chunks, ck).transpose(1, 0, 2)
        x, xb = _conv_glu(x, xb, wa, wv, wd, cp, ln_g[l, 1][None, :].astype(F32),
                          ln_b[l, 1][None, :].astype(F32), alpha, ffn_tm, groups)
    return x


def kernel(x_prompt, x_sample, w_in, ret_decay_logit, rel_bias, lambda_q1, lambda_k1, lambda_q2,
           lambda_k2, diff_norm_g, w_out, ln_g, ln_b, w_up, conv_w, conv_b, w_down):
    bp, sp, d = x_prompt.shape
    bs, ss, _ = x_sample.shape
    groups = ((0, bp, sp), (bp * sp, bs, ss))
    x = jnp.concatenate([x_prompt.reshape(bp * sp, d), x_sample.reshape(bs * ss, d)], axis=0)
    y = _forward(x.astype(F32), groups, w_in, ret_decay_logit, rel_bias, lambda_q1, lambda_k1,
                 lambda_q2, lambda_k2, diff_norm_g, w_out, ln_g, ln_b, w_up, conv_w, conv_b, w_down)
    y_prompt = y[:bp * sp].reshape(bp, sp, d).astype(x_prompt.dtype)
    y_sample = y[bp * sp:].reshape(bs, ss, d).astype(x_sample.dtype)
    return y_prompt, y_sample
```

```python
import functools
import math

import jax
import jax.numpy as jnp
from jax import lax
from jax.experimental import pallas as pl
from jax.experimental.pallas import tpu as pltpu

D_MODEL = 1024
HEAD_DIM = 64
N_HEADS = 8
GROUP_WIDTH = N_HEADS * HEAD_DIM
DIFF_QK_DIM = HEAD_DIM // 2
D_FF = 2816
N_BUCKETS = 32
MAX_DISTANCE = 128
ROPE_BASE = 10000.0
LN_EPS = 1e-5
HEAD_NORM_EPS = 1e-6
LANES = 128
BF16_SUBLANES = 16
MXU_WIDTH = 256
FAR_DISTANCE = 91
BIAS_REACH = 2
MAX_UNROLLED_MACRO_TILES = 8
LOG2E = math.log2(math.e)
NEG_BIG = -1e30
UNDERFLOW_GUARD = 2.0 ** -90
VMEM_LIMIT = 56 * 1024 * 1024

F32 = jnp.float32
BF16 = jnp.bfloat16


def _cparams(*sem):
    return pltpu.CompilerParams(dimension_semantics=sem, vmem_limit_bytes=VMEM_LIMIT)


def _proj_kernel(x_ref, w_ref, wvt_ref, s_ref, cos_ref, sin_ref, ret_ref, gate_ref, dqk_ref, vt_ref):
    gw = GROUP_WIDTH
    x = x_ref[...]
    tm = x.shape[0]
    lane = lax.broadcasted_iota(jnp.int32, (tm, LANES), 1)
    low_half = (lane % HEAD_DIM) < (HEAD_DIM // 2)
    cos = cos_ref[...]
    sin = sin_ref[...]
    qk = jnp.dot(x, w_ref[:, :2 * gw], preferred_element_type=F32) * s_ref[:, :2 * gw]
    for p in range(2 * gw // LANES):
        sl = slice(p * LANES, (p + 1) * LANES)
        xx = qk[:, sl]
        swapped = jnp.where(low_half, pltpu.roll(xx, LANES - HEAD_DIM // 2, 1),
                            pltpu.roll(xx, HEAD_DIM // 2, 1))
        ret_ref[:, sl] = (xx * cos + swapped * sin).astype(ret_ref.dtype)
    ret_ref[:, 2 * gw:] = jnp.dot(x, w_ref[:, 2 * gw:3 * gw],
                                  preferred_element_type=F32).astype(ret_ref.dtype)
    gate_ref[...] = jnp.dot(x, w_ref[:, 3 * gw:4 * gw], preferred_element_type=F32)
    dqk_ref[...] = (jnp.dot(x, w_ref[:, 4 * gw:], preferred_element_type=F32)
                    * s_ref[:, 4 * gw:]).astype(dqk_ref.dtype)
    vt_ref[...] = lax.dot_general(wvt_ref[...], x, (((1,), (1,)), ((), ())),
                                  preferred_element_type=F32).astype(vt_ref.dtype)


def _position_block(i, tm, groups):
    t0 = i * tm
    blk = t0 // tm
    for (row0, _, seq) in groups:
        blk = jnp.where(t0 >= row0, ((t0 - row0) % seq) // tm, blk)
    return blk


def _project(xb, w, w_vt, scale, cos_t, sin_t, groups, tm):
    t, k = xb.shape
    gw = GROUP_WIDTH
    row = lambda i: (i, 0)
    fixed = lambda i: (0, 0)
    pos = lambda i: (_position_block(i, tm, groups), 0)
    return pl.pallas_call(
        _proj_kernel,
        grid=(t // tm,),
        in_specs=[pl.BlockSpec((tm, k), row),
                  pl.BlockSpec((k, 6 * gw), fixed, pipeline_mode=pl.Buffered(1)),
                  pl.BlockSpec((gw, k), fixed, pipeline_mode=pl.Buffered(1)),
                  pl.BlockSpec((1, 6 * gw), fixed),
                  pl.BlockSpec((tm, LANES), pos),
                  pl.BlockSpec((tm, LANES), pos)],
        out_specs=[pl.BlockSpec((tm, 3 * gw), row), pl.BlockSpec((tm, gw), row),
                   pl.BlockSpec((tm, 2 * gw), row),
                   pl.BlockSpec((None, gw, tm), lambda i: (i, 0, 0))],
        out_shape=[jax.ShapeDtypeStruct((t, 3 * gw), BF16), jax.ShapeDtypeStruct((t, gw), F32),
                   jax.ShapeDtypeStruct((t, 2 * gw), BF16),
                   jax.ShapeDtypeStruct((t // tm, gw, tm), BF16)],
        compiler_params=_cparams("parallel"),
        name="in_proj",
    )(xb, w, w_vt, scale, cos_t, sin_t)


def _ret_kernel(lg_ref, q_ref, k_ref, v_ref, g_ref, _, o_ref,
                rf_ref, rb_ref, rnext_ref, mask_ref, tab_ref, *, ns, per_step, chunk):
    t = pl.program_id(1)
    c = chunk
    n_pairs = GROUP_WIDTH // LANES
    lane = lax.broadcasted_iota(jnp.int32, (c, LANES), 1)
    head0 = lane < HEAD_DIM
    r_i = lax.broadcasted_iota(jnp.int32, (LANES, LANES), 0) // HEAD_DIM
    c_i = lax.broadcasted_iota(jnp.int32, (LANES, LANES), 1) // HEAD_DIM
    same_head = r_i == c_i

    @pl.when(t == 0)
    def _init():
        rb_ref[...] = jnp.zeros_like(rb_ref)
        qi = lax.broadcasted_iota(jnp.int32, (c, c), 0)
        ki = lax.broadcasted_iota(jnp.int32, (c, c), 1)
        diff = (qi - ki).astype(F32)
        pos = lax.broadcasted_iota(jnp.int32, (c, LANES), 0).astype(F32)
        for p in range(n_pairs):
            for hh in range(2):
                lf = lg_ref[0, 2 * p + hh]
                lb = lg_ref[1, 2 * p + hh]
                mask_ref[p, hh * c:(hh + 1) * c, :] = jnp.where(
                    diff >= 0, jnp.exp(lf * jnp.maximum(diff, 0.0)),
                    jnp.exp(lb * jnp.maximum(-diff, 0.0)))
            lfl = jnp.where(head0, lg_ref[0, 2 * p], lg_ref[0, 2 * p + 1])
            lbl = jnp.where(head0, lg_ref[1, 2 * p], lg_ref[1, 2 * p + 1])
            tab_ref[p, 0] = jnp.exp(lfl * (c - 1 - pos))
            tab_ref[p, 1] = jnp.exp(lbl * pos)
            tab_ref[p, 2] = jnp.exp(lfl * (pos + 1.0))
            tab_ref[p, 3] = jnp.exp(lbl * (c - pos))
            tab_ref[p, 4] = jnp.exp(lfl * c)
            tab_ref[p, 5] = jnp.exp(lbl * c)

    def summary(k, w, v):
        kw = (k.astype(F32) * w).astype(BF16)
        kv = lax.dot_general(kw, v, (((0,), (0,)), ((), ())), preferred_element_type=F32)
        return jnp.where(same_head, kv, 0.0)

    @pl.when(t < ns)
    def _backward():
        for u in reversed(range(per_step)):
            rows = slice(u * c, (u + 1) * c)
            ci = (ns - 1 - t) * per_step + u
            for p in range(n_pairs):
                sl = slice(p * LANES, (p + 1) * LANES)
                rnext_ref[ci, p] = rb_ref[p].astype(BF16)
                rb_ref[p] = (tab_ref[p, 5][:LANES] * rb_ref[p]
                             + summary(k_ref[rows, sl], tab_ref[p, 1], v_ref[rows, sl]))

    @pl.when(t >= ns)
    def _forward():
        @pl.when(t == ns)
        def _():
            rf_ref[...] = jnp.zeros_like(rf_ref)

        for u in range(per_step):
            rows = slice(u * c, (u + 1) * c)
            ci = (t - ns) * per_step + u
            for p in range(n_pairs):
                sl = slice(p * LANES, (p + 1) * LANES)
                q = q_ref[rows, sl]
                k = k_ref[rows, sl]
                v = v_ref[rows, sl]
                zero = jnp.zeros_like(q)
                q2 = jnp.concatenate([jnp.where(head0, q, zero), jnp.where(head0, zero, q)], axis=0)
                s = lax.dot_general(q2, k, (((1,), (1,)), ((), ())), preferred_element_type=F32)
                o2 = jnp.dot((s * mask_ref[p]).astype(BF16), v, preferred_element_type=F32)
                y = jnp.where(head0, o2[:c], o2[c:])
                y = y + jnp.dot(q, rf_ref[p].astype(BF16), preferred_element_type=F32) * tab_ref[p, 2]
                y = y + jnp.dot(q, rnext_ref[ci, p], preferred_element_type=F32) * tab_ref[p, 3]

                sq = y * y
                s0 = jnp.sum(jnp.where(head0, sq, 0.0), axis=1, keepdims=True)
                s1 = jnp.sum(jnp.where(head0, 0.0, sq), axis=1, keepdims=True)
                ms = jnp.where(head0, s0, s1) * (1.0 / HEAD_DIM)
                y = y * lax.rsqrt(ms + HEAD_NORM_EPS)
                g = g_ref[rows, sl]
                o_ref[rows, sl] = (g / (1.0 + jnp.exp(-g)) * y).astype(o_ref.dtype)

                rf_ref[p] = tab_ref[p, 4][:LANES] * rf_ref[p] + summary(k, tab_ref[p, 0], v)


def _retention(ret, gate, log_g, out, row0, batch, seq, chunk):
    nc = seq // chunk
    per_step = max(u for u in (1, 2, 4, 8) if nc % u == 0)
    ns = nc // per_step
    step_rows = per_step * chunk
    blk0 = row0 // step_rows
    n_pairs = GROUP_WIDTH // LANES
    gw = GROUP_WIDTH

    def kc(t):
        return jnp.where(t < ns, ns - 1 - t, t - ns)

    def qc(t):
        return jnp.maximum(t - ns, 0)

    def rows(b, cc):
        return blk0 + b * ns + cc

    kernel = functools.partial(_ret_kernel, ns=ns, per_step=per_step, chunk=chunk)
    return pl.pallas_call(
        kernel,
        grid=(batch, 2 * ns),
        in_specs=[pl.BlockSpec(memory_space=pltpu.SMEM),
                  pl.BlockSpec((step_rows, gw), lambda b, t: (rows(b, qc(t)), 0)),
                  pl.BlockSpec((step_rows, gw), lambda b, t: (rows(b, kc(t)), 1)),
                  pl.BlockSpec((step_rows, gw), lambda b, t: (rows(b, kc(t)), 2)),
                  pl.BlockSpec((step_rows, gw), lambda b, t: (rows(b, qc(t)), 0)),
                  pl.BlockSpec(memory_space=pl.ANY)],
        out_specs=pl.BlockSpec((step_rows, gw), lambda b, t: (rows(b, qc(t)), 0)),
        out_shape=jax.ShapeDtypeStruct(out.shape, out.dtype),
        input_output_aliases={5: 0},
        scratch_shapes=[pltpu.VMEM((n_pairs, LANES, LANES), F32),
                        pltpu.VMEM((n_pairs, LANES, LANES), F32),
                        pltpu.VMEM((nc, n_pairs, LANES, LANES), BF16),
                        pltpu.VMEM((n_pairs, 2 * chunk, chunk), F32),
                        pltpu.VMEM((n_pairs, 6, chunk, LANES), F32)],
        compiler_params=_cparams("parallel", "arbitrary"),
        name="retention",
    )(log_g, ret, ret, ret, gate, out)


def _bias_kernel(rb_ref, bucket_ref, o_ref):
    h = pl.program_id(0)
    bk = bucket_ref[...]
    out = jnp.zeros(bk.shape, F32)
    for n in range(N_BUCKETS):
        out = jnp.where(bk == n, rb_ref[n, h], out)
    o_ref[...] = out * LOG2E


def _bias_tiles(rel_bias, tile):
    assert tile > FAR_DISTANCE
    a = jnp.arange(tile, dtype=jnp.int32)[:, None]
    b = jnp.arange(tile, dtype=jnp.int32)[None, :]
    rel = jnp.stack([(d * tile + a - b) for d in range(-BIAS_REACH, BIAS_REACH + 1)])
    nb = N_BUCKETS // 2
    max_exact = nb // 2
    n = jnp.abs(rel)
    nf = jnp.maximum(n, 1).astype(F32)
    large = max_exact + (jnp.log(nf / max_exact) / math.log(MAX_DISTANCE / max_exact)
                         * (nb - max_exact)).astype(jnp.int32)
    large = jnp.minimum(large, nb - 1)
    bucket = jnp.where(rel > 0, nb, 0) + jnp.where(n < max_exact, n, large)
    return pl.pallas_call(
        _bias_kernel,
        grid=(N_HEADS, 2 * BIAS_REACH + 1),
        in_specs=[pl.BlockSpec(memory_space=pltpu.SMEM),
                  pl.BlockSpec((None, tile, tile), lambda h, d: (d, 0, 0))],
        out_specs=pl.BlockSpec((None, None, tile, tile), lambda h, d: (h, d, 0, 0)),
        out_shape=jax.ShapeDtypeStruct((N_HEADS, 2 * BIAS_REACH + 1, tile, tile), F32),
        compiler_params=_cparams("parallel", "arbitrary"),
        name="t5_bias_tiles",
    )(rel_bias.astype(F32), bucket.astype(jnp.int32))


def _attn_kernel(rb_ref, lamp_ref, dng_ref, q_ref, k_ref, vt_ref, bt_ref, _, o_ref,
                 a_ref, l_ref, p_ref, qa_ref, kmax_ref, m_ref, *, n_tiles, tile, macro):
    h = pl.program_id(1)
    qi = pl.program_id(2)
    hh = h % 2
    n = n_tiles
    half = DIFF_QK_DIM
    group_of_lane = lax.broadcasted_iota(jnp.int32, (LANES, LANES), 0) // half
    group_sum = (group_of_lane == lax.broadcasted_iota(jnp.int32, (LANES, LANES), 1)).astype(BF16)

    def half_norms(x):
        xf = x.astype(F32)
        return jnp.dot((xf * xf).astype(BF16), group_sum, preferred_element_type=F32)

    @pl.when(qi == 0)
    def _key_norms():
        def body(j, mx):
            kt = k_ref[pl.ds(pl.multiple_of(j * tile, tile), tile), :]
            return jnp.maximum(mx, jnp.max(half_norms(kt), axis=0, keepdims=True))
        kmax_ref[...] = lax.fori_loop(0, n, body, jnp.zeros((1, LANES), F32))

    qf = q_ref[...].astype(F32)
    lane = lax.broadcasted_iota(jnp.int32, qf.shape, 1)
    lane_row = lax.broadcasted_iota(jnp.int32, (1, LANES), 1)
    row = lax.broadcasted_iota(jnp.int32, (LANES, tile), 0)
    group_sum_t = (lax.broadcasted_iota(jnp.int32, (LANES, LANES), 0)
                   == lax.broadcasted_iota(jnp.int32, (LANES, LANES), 1) // half).astype(BF16)
    qn_rows = lax.dot_general(group_sum_t, (qf * qf).astype(BF16), (((1,), (1,)), ((), ())),
                              preferred_element_type=F32)
    for t in range(2):
        g = 2 * hh + t
        lo_lane = g * half
        qzt = jnp.where((lane >= lo_lane) & (lane < lo_lane + half), qf, 0.0).T.astype(BF16)
        qn2 = jnp.sum(jnp.where(row == g, qn_rows, 0.0), axis=0, keepdims=True)
        kn2 = jnp.max(jnp.where(lane_row == g, kmax_ref[...], 0.0), axis=1, keepdims=True)
        v = -(jnp.sqrt(qn2 * kn2) + rb_ref[N_BUCKETS, h])
        hi = v.astype(BF16).astype(F32)
        lo = (v - hi).astype(BF16).astype(F32)
        qa_ref[t, :LANES, :] = qzt
        qa_ref[t, LANES:, :] = jnp.where(row == 0, hi, jnp.where(row == 1, lo, 0.0)).astype(BF16)

    ones = jnp.ones((macro * tile, LANES), BF16)

    def probabilities(m, slot):
        j0 = m * macro
        kaug = jnp.concatenate([k_ref[j0 * tile:(j0 + macro) * tile, :], ones], axis=1)
        for t in range(2):
            s = jnp.dot(kaug, qa_ref[t], preferred_element_type=F32)
            for r in range(macro):
                bias = bt_ref[jnp.clip(j0 + r - qi, -BIAS_REACH, BIAS_REACH) + BIAS_REACH]
                sr = s[r * tile:(r + 1) * tile] + bias
                e = jnp.exp2(sr)
                l_ref[t] += jnp.sum(e, axis=0, keepdims=True)
                p_ref[slot, t, r * tile:(r + 1) * tile, :] = e.astype(BF16)

    def accumulate(m, slot):
        vt = jnp.concatenate([vt_ref[m * macro + r] for r in range(macro)], axis=1)
        for t in range(2):
            a_ref[t] += jnp.dot(vt, p_ref[slot, t], preferred_element_type=F32)

    a_ref[...] = jnp.zeros_like(a_ref)
    l_ref[...] = jnp.zeros_like(l_ref)

    n_macro = n // macro
    probabilities(0, 0)
    for m in range(n_macro):
        if m + 1 < n_macro:
            probabilities(m + 1, (m + 1) % 2)
        accumulate(m, m % 2)

    trusted = jnp.min(jnp.minimum(l_ref[0], l_ref[1])) >= UNDERFLOW_GUARD

    @pl.when(jnp.logical_not(trusted))
    def _exact_running_max():
        m_ref[...] = jnp.full(m_ref.shape, NEG_BIG, F32)
        a_ref[...] = jnp.zeros_like(a_ref)
        l_ref[...] = jnp.zeros_like(l_ref)

        def body(j, carry):
            kt = k_ref[pl.ds(pl.multiple_of(j * tile, tile), tile), :]
            vt = vt_ref[j]
            bias = bt_ref[jnp.clip(j - qi, -BIAS_REACH, BIAS_REACH) + BIAS_REACH]
            for t in range(2):
                s = jnp.dot(kt, qa_ref[t, :LANES, :], preferred_element_type=F32) + bias
                m_prev = m_ref[t]
                m_new = jnp.maximum(m_prev, jnp.max(s, axis=0, keepdims=True))
                alpha = jnp.exp2(m_prev - m_new)
                e = jnp.exp2(s - m_new)
                l_ref[t] = alpha * l_ref[t] + jnp.sum(e, axis=0, keepdims=True)
                a_ref[t] = alpha * a_ref[t] + jnp.dot(vt, e.astype(BF16),
                                                      preferred_element_type=F32)
                m_ref[t] = m_new
            return carry

        lax.fori_loop(0, n, body, 0)

    lp = lamp_ref[...]
    lam_init = lp[4:5, 0:1]
    lam = (jnp.exp(jnp.sum(lp[0:1] * lp[1:2], axis=1, keepdims=True))
           - jnp.exp(jnp.sum(lp[2:3] * lp[3:4], axis=1, keepdims=True)) + lam_init)
    o = a_ref[0] / l_ref[0] - lam * (a_ref[1] / l_ref[1])
    ms = jnp.mean(o * o, axis=0, keepdims=True)
    y = o * lax.rsqrt(ms + HEAD_NORM_EPS) * dng_ref[...] * (1.0 - lam_init)
    o_ref[...] = y.astype(o_ref.dtype)


def _diff_attention(dqk, vt3, bias_t, rel_log2, lamp, dng, out, row0, batch, seq, tile):
    n = seq // tile
    macro = max(g for g in (1, 2, 4, 8) if n % g == 0)
    assert n // macro <= MAX_UNROLLED_MACRO_TILES
    n_pairs = GROUP_WIDTH // LANES
    qblk0 = row0 // tile
    sblk0 = row0 // seq
    n_bias = 2 * BIAS_REACH + 1
    kernel = functools.partial(_attn_kernel, n_tiles=n, tile=tile, macro=macro)
    return pl.pallas_call(
        kernel,
        grid=(batch, N_HEADS, n),
        in_specs=[pl.BlockSpec(memory_space=pltpu.SMEM),
                  pl.BlockSpec((5, DIFF_QK_DIM), lambda b, h, i: (0, 0)),
                  pl.BlockSpec((HEAD_DIM, 1), lambda b, h, i: (0, 0)),
                  pl.BlockSpec((tile, LANES), lambda b, h, i: (qblk0 + b * n + i, h // 2)),
                  pl.BlockSpec((seq, LANES), lambda b, h, i: (sblk0 + b, n_pairs + h // 2)),
                  pl.BlockSpec((n, HEAD_DIM, tile), lambda b, h, i: (sblk0 + b, h, 0)),
                  pl.BlockSpec((None, n_bias, tile, tile), lambda b, h, i: (h, 0, 0, 0)),
                  pl.BlockSpec(memory_space=pl.ANY)],
        out_specs=pl.BlockSpec((None, HEAD_DIM, tile), lambda b, h, i: (qblk0 + b * n + i, h, 0)),
        out_shape=jax.ShapeDtypeStruct(out.shape, out.dtype),
        input_output_aliases={7: 0},
        scratch_shapes=[pltpu.VMEM((2, HEAD_DIM, tile), F32),
                        pltpu.VMEM((2, 1, tile), F32),
                        pltpu.VMEM((2, 2, macro * tile, tile), BF16),
                        pltpu.VMEM((2, 2 * LANES, tile), BF16),
                        pltpu.VMEM((1, LANES), F32),
                        pltpu.VMEM((2, 1, tile), F32)],
        compiler_params=_cparams("parallel", "parallel", "arbitrary"),
        name="diff_attention",
    )(rel_log2, lamp, dng, dqk, dqk, vt3, bias_t, out)


def _layer_norm(y, g, b):
    mu = jnp.mean(y, axis=-1, keepdims=True)
    d = y - mu
    var = jnp.mean(d * d, axis=-1, keepdims=True)
    return d * lax.rsqrt(var + LN_EPS) * g + b


def _outproj_kernel(yr_ref, yd_ref, w_ref, x_ref, g_ref, b_ref, o_ref, ob_ref, *, alpha):
    mix = jnp.dot(yr_ref[...], w_ref[:GROUP_WIDTH, :], preferred_element_type=F32)
    mix = mix + lax.dot_general(yd_ref[...], w_ref[GROUP_WIDTH:, :], (((0,), (0,)), ((), ())),
                                preferred_element_type=F32)
    y = _layer_norm(alpha * x_ref[...] + mix, g_ref[...], b_ref[...])
    o_ref[...] = y
    ob_ref[...] = y.astype(BF16)


def _out_projection(yr, ydt, w_out, x, g, b, alpha, tm):
    t = x.shape[0]
    assert ydt.shape == (t // tm, GROUP_WIDTH, tm)
    kernel = functools.partial(_outproj_kernel, alpha=alpha)
    row = lambda i: (i, 0)
    fixed = lambda i: (0, 0)
    return pl.pallas_call(
        kernel,
        grid=(t // tm,),
        in_specs=[pl.BlockSpec((tm, GROUP_WIDTH), row),
                  pl.BlockSpec((None, GROUP_WIDTH, tm), lambda i: (i, 0, 0)),
                  pl.BlockSpec((2 * GROUP_WIDTH, D_MODEL), fixed),
                  pl.BlockSpec((tm, D_MODEL), row),
                  pl.BlockSpec((1, D_MODEL), fixed),
                  pl.BlockSpec((1, D_MODEL), fixed)],
        out_specs=[pl.BlockSpec((tm, D_MODEL), row), pl.BlockSpec((tm, D_MODEL), row)],
        out_shape=[jax.ShapeDtypeStruct((t, D_MODEL), F32), jax.ShapeDtypeStruct((t, D_MODEL), BF16)],
        compiler_params=_cparams("parallel"),
        name="out_proj_ln",
    )(yr, ydt, w_out, x, g, b)


def _ffn_kernel(x_ref, xb_ref, xp_ref, xn_ref, wa_ref, wv_ref, wd_ref, cp_ref, g_ref, b_ref,
                o_ref, ob_ref, acc_ref, *, alpha, tm, n_chunks, starts, ends):
    i = pl.program_id(0)
    t0 = i * tm
    is_start = functools.reduce(jnp.logical_or, [t0 == s for s in starts])
    is_end = functools.reduce(jnp.logical_or, [t0 + tm == e for e in ends])
    keep_prev = jnp.where(is_start, 0.0, 1.0)
    keep_next = jnp.where(is_end, 0.0, 1.0)

    xb = xb_ref[...]
    halo = jnp.concatenate([xp_ref[...], xn_ref[...]], axis=0)
    hrows = xp_ref.shape[0]
    acc_ref[...] = jnp.zeros_like(acc_ref)

    def chunk_body(c, carry):
        wa = wa_ref[c]
        a = jnp.dot(xb, wa, preferred_element_type=F32)
        val = jnp.dot(xb, wv_ref[c], preferred_element_type=F32)
        ah = jnp.dot(halo, wa, preferred_element_type=F32)
        prev_row = ah[hrows - 1:hrows] * keep_prev
        next_row = ah[hrows:hrows + 1] * keep_next
        row = lax.broadcasted_iota(jnp.int32, a.shape, 0)
        a_m1 = jnp.where(row == 0, prev_row, pltpu.roll(a, 1, 0))
        a_p1 = jnp.where(row == tm - 1, next_row, pltpu.roll(a, tm - 1, 0))
        cp = cp_ref[c]
        conv = cp[3:4] + a_m1 * cp[0:1]
        conv = conv + a * cp[1:2]
        conv = conv + a_p1 * cp[2:3]
        gelu = 0.5 * conv * (1.0 + lax.erf(conv * (1.0 / math.sqrt(2.0))))
        hidden = (gelu * val).astype(BF16)
        acc_ref[...] += jnp.dot(hidden, wd_ref[c], preferred_element_type=F32)
        return carry

    for c in range(n_chunks):
        chunk_body(c, 0)
    y = _layer_norm(alpha * x_ref[...] + acc_ref[...], g_ref[...], b_ref[...])
    o_ref[...] = y
    ob_ref[...] = y.astype(BF16)


def _conv_glu(x, xb, wa, wv, wd, cp, g, b, alpha, tm, groups):
    t = x.shape[0]
    n_chunks, _, ck = wa.shape
    hrows = BF16_SUBLANES
    starts = tuple(r0 + bi * s for (r0, nb, s) in groups for bi in range(nb))
    ends = tuple(r0 + (bi + 1) * s for (r0, nb, s) in groups for bi in range(nb))
    kernel = functools.partial(_ffn_kernel, alpha=alpha, tm=tm, n_chunks=n_chunks,
                               starts=starts, ends=ends)
    row = lambda i: (i, 0)
    fixed2 = lambda i: (0, 0)
    fixed3 = lambda i: (0, 0, 0)
    per = tm // hrows
    last = t // hrows - 1
    return pl.pallas_call(
        kernel,
        grid=(t // tm,),
        in_specs=[pl.BlockSpec((tm, D_MODEL), row),
                  pl.BlockSpec((tm, D_MODEL), row),
                  pl.BlockSpec((hrows, D_MODEL), lambda i: (jnp.maximum(i * per - 1, 0), 0)),
                  pl.BlockSpec((hrows, D_MODEL), lambda i: (jnp.minimum((i + 1) * per, last), 0)),
                  pl.BlockSpec((n_chunks, D_MODEL, ck), fixed3, pipeline_mode=pl.Buffered(1)),
                  pl.BlockSpec((n_chunks, D_MODEL, ck), fixed3, pipeline_mode=pl.Buffered(1)),
                  pl.BlockSpec((n_chunks, ck, D_MODEL), fixed3, pipeline_mode=pl.Buffered(1)),
                  pl.BlockSpec((n_chunks, 8, ck), fixed3),
                  pl.BlockSpec((1, D_MODEL), fixed2),
                  pl.BlockSpec((1, D_MODEL), fixed2)],
        out_specs=[pl.BlockSpec((tm, D_MODEL), row), pl.BlockSpec((tm, D_MODEL), row)],
        out_shape=[jax.ShapeDtypeStruct((t, D_MODEL), F32), jax.ShapeDtypeStruct((t, D_MODEL), BF16)],
        scratch_shapes=[pltpu.VMEM((tm, D_MODEL), F32)],
        compiler_params=_cparams("parallel"),
        name="conv_glu_ln",
    )(x, xb, xb, xb, wa, wv, wd, cp, g, b)


def _rotary_tables(seq):
    d = HEAD_DIM
    inv = 1.0 / (ROPE_BASE ** (jnp.arange(0, d, 2, dtype=F32) / d))
    ang = jnp.arange(seq, dtype=F32)[:, None] * inv[None, :]
    cos, sin = jnp.cos(ang), jnp.sin(ang)
    cos_t = jnp.concatenate([cos, cos, cos, cos], axis=-1)
    sin_t = jnp.concatenate([-sin, sin, -sin, sin], axis=-1)
    return cos_t, sin_t


def _tiles(groups):
    smin = min(s for (_, _, s) in groups)
    attn_tile = min(512, smin // 2)
    chunk = min(256, smin // 2)
    tm = min(512, smin // 2)
    ffn_tm = min(1024, smin // 2)
    return attn_tile, chunk, tm, ffn_tm


def _forward(x, groups, w_in, ret_decay_logit, rel_bias, lambda_q1, lambda_k1, lambda_q2,
             lambda_k2, diff_norm_g, w_out, ln_g, ln_b, w_up, conv_w, conv_b, w_down):
    depth = w_in.shape[0]
    alpha = (2 * depth) ** 0.25
    t = x.shape[0]
    attn_tile, chunk, tm, ffn_tm = _tiles(groups)
    smax = max(s for (_, _, s) in groups)
    gw = GROUP_WIDTH
    ck = MXU_WIDTH
    n_chunks = D_FF // ck

    cos_t, sin_t = _rotary_tables(smax)
    bias_t = _bias_tiles(rel_bias, attn_tile)
    rel_log2 = rel_bias.astype(F32) * LOG2E
    rel_log2 = jnp.concatenate([rel_log2, jnp.max(rel_log2, axis=0, keepdims=True)], axis=0)
    in_scale = jnp.concatenate([
        jnp.ones((gw,), F32), jnp.full((gw,), HEAD_DIM ** -0.5, F32), jnp.ones((2 * gw,), F32),
        jnp.full((gw,), DIFF_QK_DIM ** -0.5 * LOG2E, F32), jnp.ones((gw,), F32)])[None, :]
    assert tm == attn_tile

    xb = x.astype(BF16)
    for l in range(depth):
        lam_init = 0.8 - 0.6 * math.exp(-0.3 * l)
        w_in_b = w_in[l].astype(BF16)
        ret, gate, dqk, vt3 = _project(xb, w_in_b[:, :6 * gw], w_in_b[:, 6 * gw:].T, in_scale,
                                       cos_t, sin_t, groups, tm)

        log_g = jax.nn.log_sigmoid(ret_decay_logit[l].astype(F32))
        lamp = jnp.stack([lambda_q1[l], lambda_k1[l], lambda_q2[l], lambda_k2[l],
                          jnp.full((DIFF_QK_DIM,), lam_init)]).astype(F32)
        dng = diff_norm_g[l].astype(F32)[:, None]

        yr = jnp.zeros((t, gw), BF16)
        ydt = jnp.zeros((t // attn_tile, gw, attn_tile), BF16)
        for (row0, batch, seq) in groups:
            yr = _retention(ret, gate, log_g, yr, row0, batch, seq, chunk)
            ydt = _diff_attention(dqk, vt3, bias_t, rel_log2, lamp, dng, ydt, row0, batch, seq,
                                  attn_tile)

        x, xb = _out_projection(yr, ydt, w_out[l].astype(BF16), x, ln_g[l, 0][None, :].astype(F32),
                                ln_b[l, 0][None, :].astype(F32), alpha, tm)

        wa = w_up[l][:, :D_FF].astype(BF16).reshape(D_MODEL, n_chunks, ck).transpose(1, 0, 2)
        wv = w_up[l][:, D_FF:].astype(BF16).reshape(D_MODEL, n_chunks, ck).transpose(1, 0, 2)
        wd = w_down[l].astype(BF16).reshape(n_chunks, ck, D_MODEL)
        cp = jnp.concatenate([conv_w[l].astype(F32), conv_b[l].astype(F32)[None, :],
                              jnp.zeros((4, D_FF), F32)], axis=0)
        cp = cp.reshape(8, n_chunks, ck).transpose(1, 0, 2)
        x, xb = _conv_glu(x, xb, wa, wv, wd, cp, ln_g[l, 1][None, :].astype(F32),
                          ln_b[l, 1][None, :].astype(F32), alpha, ffn_tm, groups)
    return x


def kernel(x_prompt, x_sample, w_in, ret_decay_logit, rel_bias, lambda_q1, lambda_k1, lambda_q2,
           lambda_k2, diff_norm_g, w_out, ln_g, ln_b, w_up, conv_w, conv_b, w_down):
    bp, sp, d = x_prompt.shape
    bs, ss, _ = x_sample.shape
    groups = ((0, bp, sp), (bp * sp, bs, ss))
    x = jnp.concatenate([x_prompt.reshape(bp * sp, d), x_sample.reshape(bs * ss, d)], axis=0)
    y = _forward(x.astype(F32), groups, w_in, ret_decay_logit, rel_bias, lambda_q1, lambda_k1,
                 lambda_q2, lambda_k2, diff_norm_g, w_out, ln_g, ln_b, w_up, conv_w, conv_b, w_down)
    y_prompt = y[:bp * sp].reshape(bp, sp, d).astype(x_prompt.dtype)
    y_sample = y[bp * sp:].reshape(bs, ss, d).astype(x_sample.dtype)
    return y_prompt, y_sample
```

```python
import functools
import math

import jax
import jax.numpy as jnp
from jax import lax
from jax.experimental import pallas as pl
from jax.experimental.pallas import tpu as pltpu

D_MODEL = 1024
HEAD_DIM = 64
N_HEADS = 8
GROUP_WIDTH = N_HEADS * HEAD_DIM
DIFF_QK_DIM = HEAD_DIM // 2
D_FF = 2816
N_BUCKETS = 32
MAX_DISTANCE = 128
ROPE_BASE = 10000.0
LN_EPS = 1e-5
HEAD_NORM_EPS = 1e-6
LANES = 128
BF16_SUBLANES = 16
MXU_WIDTH = 256
FAR_DISTANCE = 91
BIAS_REACH = 2
MAX_UNROLLED_MACRO_TILES = 8
LOG2E = math.log2(math.e)
NEG_BIG = -1e30
UNDERFLOW_GUARD = 2.0 ** -90
VMEM_LIMIT = 56 * 1024 * 1024

F32 = jnp.float32
BF16 = jnp.bfloat16


def _cparams(*sem):
    return pltpu.CompilerParams(dimension_semantics=sem, vmem_limit_bytes=VMEM_LIMIT)


def _proj_kernel(x_ref, w_ref, wvt_ref, s_ref, cos_ref, sin_ref, ret_ref, gate_ref, dqk_ref, vt_ref):
    gw = GROUP_WIDTH
    x = x_ref[...]
    tm = x.shape[0]
    lane = lax.broadcasted_iota(jnp.int32, (tm, LANES), 1)
    low_half = (lane % HEAD_DIM) < (HEAD_DIM // 2)
    cos = cos_ref[...]
    sin = sin_ref[...]
    qk = jnp.dot(x, w_ref[:, :2 * gw], preferred_element_type=F32) * s_ref[:, :2 * gw]
    for p in range(2 * gw // LANES):
        sl = slice(p * LANES, (p + 1) * LANES)
        xx = qk[:, sl]
        swapped = jnp.where(low_half, pltpu.roll(xx, LANES - HEAD_DIM // 2, 1),
                            pltpu.roll(xx, HEAD_DIM // 2, 1))
        ret_ref[:, sl] = (xx * cos + swapped * sin).astype(ret_ref.dtype)
    ret_ref[:, 2 * gw:] = jnp.dot(x, w_ref[:, 2 * gw:3 * gw],
                                  preferred_element_type=F32).astype(ret_ref.dtype)
    gate_ref[...] = jnp.dot(x, w_ref[:, 3 * gw:4 * gw], preferred_element_type=F32)
    dqk_ref[...] = (jnp.dot(x, w_ref[:, 4 * gw:], preferred_element_type=F32)
                    * s_ref[:, 4 * gw:]).astype(dqk_ref.dtype)
    vt_ref[...] = lax.dot_general(wvt_ref[...], x, (((1,), (1,)), ((), ())),
                                  preferred_element_type=F32).astype(vt_ref.dtype)


def _position_block(i, tm, groups):
    t0 = i * tm
    blk = t0 // tm
    for (row0, _, seq) in groups:
        blk = jnp.where(t0 >= row0, ((t0 - row0) % seq) // tm, blk)
    return blk


def _project(xb, w, w_vt, scale, cos_t, sin_t, groups, tm):
    t, k = xb.shape
    gw = GROUP_WIDTH
    row = lambda i: (i, 0)
    fixed = lambda i: (0, 0)
    pos = lambda i: (_position_block(i, tm, groups), 0)
    return pl.pallas_call(
        _proj_kernel,
        grid=(t // tm,),
        in_specs=[pl.BlockSpec((tm, k), row),
                  pl.BlockSpec((k, 6 * gw), fixed, pipeline_mode=pl.Buffered(1)),
                  pl.BlockSpec((gw, k), fixed, pipeline_mode=pl.Buffered(1)),
                  pl.BlockSpec((1, 6 * gw), fixed),
                  pl.BlockSpec((tm, LANES), pos),
                  pl.BlockSpec((tm, LANES), pos)],
        out_specs=[pl.BlockSpec((tm, 3 * gw), row), pl.BlockSpec((tm, gw), row),
                   pl.BlockSpec((tm, 2 * gw), row),
                   pl.BlockSpec((None, gw, tm), lambda i: (i, 0, 0))],
        out_shape=[jax.ShapeDtypeStruct((t, 3 * gw), BF16), jax.ShapeDtypeStruct((t, gw), F32),
                   jax.ShapeDtypeStruct((t, 2 * gw), BF16),
                   jax.ShapeDtypeStruct((t // tm, gw, tm), BF16)],
        compiler_params=_cparams("parallel"),
        name="in_proj",
    )(xb, w, w_vt, scale, cos_t, sin_t)


def _ret_kernel(lg_ref, q_ref, k_ref, v_ref, g_ref, _, o_ref,
                rf_ref, rb_ref, rnext_ref, mask_ref, tab_ref, *, ns, per_step, chunk):
    t = pl.program_id(1)
    c = chunk
    n_pairs = GROUP_WIDTH // LANES
    lane = lax.broadcasted_iota(jnp.int32, (c, LANES), 1)
    head0 = lane < HEAD_DIM
    r_i = lax.broadcasted_iota(jnp.int32, (LANES, LANES), 0) // HEAD_DIM
    c_i = lax.broadcasted_iota(jnp.int32, (LANES, LANES), 1) // HEAD_DIM
    same_head = r_i == c_i

    @pl.when(t == 0)
    def _init():
        rb_ref[...] = jnp.zeros_like(rb_ref)
        qi = lax.broadcasted_iota(jnp.int32, (c, c), 0)
        ki = lax.broadcasted_iota(jnp.int32, (c, c), 1)
        diff = (qi - ki).astype(F32)
        pos = lax.broadcasted_iota(jnp.int32, (c, LANES), 0).astype(F32)
        for p in range(n_pairs):
            for hh in range(2):
                lf = lg_ref[0, 2 * p + hh]
                lb = lg_ref[1, 2 * p + hh]
                mask_ref[p, hh * c:(hh + 1) * c, :] = jnp.where(
                    diff >= 0, jnp.exp(lf * jnp.maximum(diff, 0.0)),
                    jnp.exp(lb * jnp.maximum(-diff, 0.0)))
            lfl = jnp.where(head0, lg_ref[0, 2 * p], lg_ref[0, 2 * p + 1])
            lbl = jnp.where(head0, lg_ref[1, 2 * p], lg_ref[1, 2 * p + 1])
            tab_ref[p, 0] = jnp.exp(lfl * (c - 1 - pos))
            tab_ref[p, 1] = jnp.exp(lbl * pos)
            tab_ref[p, 2] = jnp.exp(lfl * (pos + 1.0))
            tab_ref[p, 3] = jnp.exp(lbl * (c - pos))
            tab_ref[p, 4] = jnp.exp(lfl * c)
            tab_ref[p, 5] = jnp.exp(lbl * c)

    def summary(k, w, v):
        kw = (k.astype(F32) * w).astype(BF16)
        kv = lax.dot_general(kw, v, (((0,), (0,)), ((), ())), preferred_element_type=F32)
        return jnp.where(same_head, kv, 0.0)

    @pl.when(t < ns)
    def _backward():
        for u in reversed(range(per_step)):
            rows = slice(u * c, (u + 1) * c)
            ci = (ns - 1 - t) * per_step + u
            for p in range(n_pairs):
                sl = slice(p * LANES, (p + 1) * LANES)
                rnext_ref[ci, p] = rb_ref[p].astype(BF16)
                rb_ref[p] = (tab_ref[p, 5][:LANES] * rb_ref[p]
                             + summary(k_ref[rows, sl], tab_ref[p, 1], v_ref[rows, sl]))

    @pl.when(t >= ns)
    def _forward():
        @pl.when(t == ns)
        def _():
            rf_ref[...] = jnp.zeros_like(rf_ref)

        for u in range(per_step):
            rows = slice(u * c, (u + 1) * c)
            ci = (t - ns) * per_step + u
            for p in range(n_pairs):
                sl = slice(p * LANES, (p + 1) * LANES)
                q = q_ref[rows, sl]
                k = k_ref[rows, sl]
                v = v_ref[rows, sl]
                zero = jnp.zeros_like(q)
                q2 = jnp.concatenate([jnp.where(head0, q, zero), jnp.where(head0, zero, q)], axis=0)
                s = lax.dot_general(q2, k, (((1,), (1,)), ((), ())), preferred_element_type=F32)
                o2 = jnp.dot((s * mask_ref[p]).astype(BF16), v, preferred_element_type=F32)
                y = jnp.where(head0, o2[:c], o2[c:])
                y = y + jnp.dot(q, rf_ref[p].astype(BF16), preferred_element_type=F32) * tab_ref[p, 2]
                y = y + jnp.dot(q, rnext_ref[ci, p], preferred_element_type=F32) * tab_ref[p, 3]

                sq = y * y
                s0 = jnp.sum(jnp.where(head0, sq, 0.0), axis=1, keepdims=True)
                s1 = jnp.sum(jnp.where(head0, 0.0, sq), axis=1, keepdims=True)
                ms = jnp.where(head0, s0, s1) * (1.0 / HEAD_DIM)
                y = y * lax.rsqrt(ms + HEAD_NORM_EPS)
                g = g_ref[rows, sl]
                o_ref[rows, sl] = (g / (1.0 + jnp.exp(-g)) * y).astype(o_ref.dtype)

                rf_ref[p] = tab_ref[p, 4][:LANES] * rf_ref[p] + summary(k, tab_ref[p, 0], v)


def _retention(ret, gate, log_g, out, row0, batch, seq, chunk):
    nc = seq // chunk
    per_step = max(u for u in (1, 2, 4) if nc % u == 0)
    ns = nc // per_step
    step_rows = per_step * chunk
    blk0 = row0 // step_rows
    n_pairs = GROUP_WIDTH // LANES
    gw = GROUP_WIDTH

    def kc(t):
        return jnp.where(t < ns, ns - 1 - t, t - ns)

    def qc(t):
        return jnp.maximum(t - ns, 0)

    def rows(b, cc):
        return blk0 + b * ns + cc

    kernel = functools.partial(_ret_kernel, ns=ns, per_step=per_step, chunk=chunk)
    return pl.pallas_call(
        kernel,
        grid=(batch, 2 * ns),
        in_specs=[pl.BlockSpec(memory_space=pltpu.SMEM),
                  pl.BlockSpec((step_rows, gw), lambda b, t: (rows(b, qc(t)), 0)),
                  pl.BlockSpec((step_rows, gw), lambda b, t: (rows(b, kc(t)), 1)),
                  pl.BlockSpec((step_rows, gw), lambda b, t: (rows(b, kc(t)), 2)),
                  pl.BlockSpec((step_rows, gw), lambda b, t: (rows(b, qc(t)), 0)),
                  pl.BlockSpec(memory_space=pl.ANY)],
        out_specs=pl.BlockSpec((step_rows, gw), lambda b, t: (rows(b, qc(t)), 0)),
        out_shape=jax.ShapeDtypeStruct(out.shape, out.dtype),
        input_output_aliases={5: 0},
        scratch_shapes=[pltpu.VMEM((n_pairs, LANES, LANES), F32),
                        pltpu.VMEM((n_pairs, LANES, LANES), F32),
                        pltpu.VMEM((nc, n_pairs, LANES, LANES), BF16),
                        pltpu.VMEM((n_pairs, 2 * chunk, chunk), F32),
                        pltpu.VMEM((n_pairs, 6, chunk, LANES), F32)],
        compiler_params=_cparams("parallel", "arbitrary"),
        name="retention",
    )(log_g, ret, ret, ret, gate, out)


def _bias_kernel(rb_ref, bucket_ref, o_ref):
    h = pl.program_id(0)
    bk = bucket_ref[...]
    out = jnp.zeros(bk.shape, F32)
    for n in range(N_BUCKETS):
        out = jnp.where(bk == n, rb_ref[n, h], out)
    o_ref[...] = out * LOG2E


def _bias_tiles(rel_bias, tile):
    assert tile > FAR_DISTANCE
    a = jnp.arange(tile, dtype=jnp.int32)[:, None]
    b = jnp.arange(tile, dtype=jnp.int32)[None, :]
    rel = jnp.stack([(d * tile + a - b) for d in range(-BIAS_REACH, BIAS_REACH + 1)])
    nb = N_BUCKETS // 2
    max_exact = nb // 2
    n = jnp.abs(rel)
    nf = jnp.maximum(n, 1).astype(F32)
    large = max_exact + (jnp.log(nf / max_exact) / math.log(MAX_DISTANCE / max_exact)
                         * (nb - max_exact)).astype(jnp.int32)
    large = jnp.minimum(large, nb - 1)
    bucket = jnp.where(rel > 0, nb, 0) + jnp.where(n < max_exact, n, large)
    return pl.pallas_call(
        _bias_kernel,
        grid=(N_HEADS, 2 * BIAS_REACH + 1),
        in_specs=[pl.BlockSpec(memory_space=pltpu.SMEM),
                  pl.BlockSpec((None, tile, tile), lambda h, d: (d, 0, 0))],
        out_specs=pl.BlockSpec((None, None, tile, tile), lambda h, d: (h, d, 0, 0)),
        out_shape=jax.ShapeDtypeStruct((N_HEADS, 2 * BIAS_REACH + 1, tile, tile), F32),
        compiler_params=_cparams("parallel", "arbitrary"),
        name="t5_bias_tiles",
    )(rel_bias.astype(F32), bucket.astype(jnp.int32))


def _attn_kernel(rb_ref, lamp_ref, dng_ref, q_ref, qn_ref, k_ref, vt_ref, bt_ref, _, o_ref,
                 a_ref, l_ref, p_ref, qa_ref, kmax_ref, m_ref, *, n_tiles, tile, macro):
    h = pl.program_id(1)
    qi = pl.program_id(2)
    hh = h % 2
    n = n_tiles
    half = DIFF_QK_DIM
    group_of_lane = lax.broadcasted_iota(jnp.int32, (LANES, LANES), 0) // half
    group_sum = (group_of_lane == lax.broadcasted_iota(jnp.int32, (LANES, LANES), 1)).astype(BF16)

    def half_norms(x):
        xf = x.astype(F32)
        return jnp.dot((xf * xf).astype(BF16), group_sum, preferred_element_type=F32)

    def augmented_queries(src_ref, dst):
        qf = src_ref[...].astype(F32)
        lane = lax.broadcasted_iota(jnp.int32, qf.shape, 1)
        lane_row = lax.broadcasted_iota(jnp.int32, (1, LANES), 1)
        row = lax.broadcasted_iota(jnp.int32, (LANES, tile), 0)
        group_sum_t = (lax.broadcasted_iota(jnp.int32, (LANES, LANES), 0)
                       == lax.broadcasted_iota(jnp.int32, (LANES, LANES), 1) // half).astype(BF16)
        qn_rows = lax.dot_general(group_sum_t, (qf * qf).astype(BF16), (((1,), (1,)), ((), ())),
                                  preferred_element_type=F32)
        for t in range(2):
            g = 2 * hh + t
            lo_lane = g * half
            qzt = jnp.where((lane >= lo_lane) & (lane < lo_lane + half), qf, 0.0).T.astype(BF16)
            qn2 = jnp.sum(jnp.where(row == g, qn_rows, 0.0), axis=0, keepdims=True)
            kn2 = jnp.max(jnp.where(lane_row == g, kmax_ref[...], 0.0), axis=1, keepdims=True)
            v = -(jnp.sqrt(qn2 * kn2) + rb_ref[N_BUCKETS, h])
            hi = v.astype(BF16).astype(F32)
            lo = (v - hi).astype(BF16).astype(F32)
            qa_ref[dst, t, :LANES, :] = qzt
            qa_ref[dst, t, LANES:, :] = jnp.where(row == 0, hi,
                                                  jnp.where(row == 1, lo, 0.0)).astype(BF16)

    @pl.when(qi == 0)
    def _first_tile():
        def body(j, mx):
            kt = k_ref[pl.ds(pl.multiple_of(j * tile, tile), tile), :]
            return jnp.maximum(mx, jnp.max(half_norms(kt), axis=0, keepdims=True))
        kmax_ref[...] = lax.fori_loop(0, n, body, jnp.zeros((1, LANES), F32))
        augmented_queries(q_ref, 0)

    cur = qi % 2
    ones = jnp.ones((macro * tile, LANES), BF16)

    def probabilities(m, slot):
        j0 = m * macro
        kaug = jnp.concatenate([k_ref[j0 * tile:(j0 + macro) * tile, :], ones], axis=1)
        for t in range(2):
            s = jnp.dot(kaug, qa_ref[cur, t], preferred_element_type=F32)
            for r in range(macro):
                bias = bt_ref[jnp.clip(j0 + r - qi, -BIAS_REACH, BIAS_REACH) + BIAS_REACH]
                sr = s[r * tile:(r + 1) * tile] + bias
                e = jnp.exp2(sr)
                l_ref[t] += jnp.sum(e, axis=0, keepdims=True)
                p_ref[slot, t, r * tile:(r + 1) * tile, :] = e.astype(BF16)

    def accumulate(m, slot):
        vt = jnp.concatenate([vt_ref[m * macro + r] for r in range(macro)], axis=1)
        for t in range(2):
            a_ref[t] += jnp.dot(vt, p_ref[slot, t], preferred_element_type=F32)

    a_ref[...] = jnp.zeros_like(a_ref)
    l_ref[...] = jnp.zeros_like(l_ref)

    n_macro = n // macro
    probabilities(0, 0)
    augmented_queries(qn_ref, 1 - cur)
    for m in range(n_macro):
        if m + 1 < n_macro:
            probabilities(m + 1, (m + 1) % 2)
        accumulate(m, m % 2)

    trusted = jnp.min(jnp.minimum(l_ref[0], l_ref[1])) >= UNDERFLOW_GUARD

    @pl.when(jnp.logical_not(trusted))
    def _exact_running_max():
        m_ref[...] = jnp.full(m_ref.shape, NEG_BIG, F32)
        a_ref[...] = jnp.zeros_like(a_ref)
        l_ref[...] = jnp.zeros_like(l_ref)

        def body(j, carry):
            kt = k_ref[pl.ds(pl.multiple_of(j * tile, tile), tile), :]
            vt = vt_ref[j]
            bias = bt_ref[jnp.clip(j - qi, -BIAS_REACH, BIAS_REACH) + BIAS_REACH]
            for t in range(2):
                s = jnp.dot(kt, qa_ref[cur, t, :LANES, :], preferred_element_type=F32) + bias
                m_prev = m_ref[t]
                m_new = jnp.maximum(m_prev, jnp.max(s, axis=0, keepdims=True))
                alpha = jnp.exp2(m_prev - m_new)
                e = jnp.exp2(s - m_new)
                l_ref[t] = alpha * l_ref[t] + jnp.sum(e, axis=0, keepdims=True)
                a_ref[t] = alpha * a_ref[t] + jnp.dot(vt, e.astype(BF16),
                                                      preferred_element_type=F32)
                m_ref[t] = m_new
            return carry

        lax.fori_loop(0, n, body, 0)

    lp = lamp_ref[...]
    lam_init = lp[4:5, 0:1]
    lam = (jnp.exp(jnp.sum(lp[0:1] * lp[1:2], axis=1, keepdims=True))
           - jnp.exp(jnp.sum(lp[2:3] * lp[3:4], axis=1, keepdims=True)) + lam_init)
    o = a_ref[0] / l_ref[0] - lam * (a_ref[1] / l_ref[1])
    ms = jnp.mean(o * o, axis=0, keepdims=True)
    y = o * lax.rsqrt(ms + HEAD_NORM_EPS) * dng_ref[...] * (1.0 - lam_init)
    o_ref[...] = y.astype(o_ref.dtype)


def _diff_attention(dqk, vt3, bias_t, rel_log2, lamp, dng, out, row0, batch, seq, tile):
    n = seq // tile
    macro = max(g for g in (1, 2, 4, 8) if n % g == 0)
    assert n // macro <= MAX_UNROLLED_MACRO_TILES
    n_pairs = GROUP_WIDTH // LANES
    qblk0 = row0 // tile
    sblk0 = row0 // seq
    n_bias = 2 * BIAS_REACH + 1
    kernel = functools.partial(_attn_kernel, n_tiles=n, tile=tile, macro=macro)
    return pl.pallas_call(
        kernel,
        grid=(batch, N_HEADS, n),
        in_specs=[pl.BlockSpec(memory_space=pltpu.SMEM),
                  pl.BlockSpec((5, DIFF_QK_DIM), lambda b, h, i: (0, 0)),
                  pl.BlockSpec((HEAD_DIM, 1), lambda b, h, i: (0, 0)),
                  pl.BlockSpec((tile, LANES), lambda b, h, i: (qblk0 + b * n + i, h // 2)),
                  pl.BlockSpec((tile, LANES),
                               lambda b, h, i: (qblk0 + b * n + jnp.minimum(i + 1, n - 1), h // 2)),
                  pl.BlockSpec((seq, LANES), lambda b, h, i: (sblk0 + b, n_pairs + h // 2)),
                  pl.BlockSpec((n, HEAD_DIM, tile), lambda b, h, i: (sblk0 + b, h, 0)),
                  pl.BlockSpec((None, n_bias, tile, tile), lambda b, h, i: (h, 0, 0, 0),
                               pipeline_mode=pl.Buffered(1)),
                  pl.BlockSpec(memory_space=pl.ANY)],
        out_specs=pl.BlockSpec((None, HEAD_DIM, tile), lambda b, h, i: (qblk0 + b * n + i, h, 0)),
        out_shape=jax.ShapeDtypeStruct(out.shape, out.dtype),
        input_output_aliases={8: 0},
        scratch_shapes=[pltpu.VMEM((2, HEAD_DIM, tile), F32),
                        pltpu.VMEM((2, 1, tile), F32),
                        pltpu.VMEM((2, 2, macro * tile, tile), BF16),
                        pltpu.VMEM((2, 2, 2 * LANES, tile), BF16),
                        pltpu.VMEM((1, LANES), F32),
                        pltpu.VMEM((2, 1, tile), F32)],
        compiler_params=_cparams("parallel", "parallel", "arbitrary"),
        name="diff_attention",
    )(rel_log2, lamp, dng, dqk, dqk, dqk, vt3, bias_t, out)


def _layer_norm(y, g, b):
    mu = jnp.mean(y, axis=-1, keepdims=True)
    d = y - mu
    var = jnp.mean(d * d, axis=-1, keepdims=True)
    return d * lax.rsqrt(var + LN_EPS) * g + b


def _outproj_kernel(yr_ref, yd_ref, w_ref, x_ref, g_ref, b_ref, o_ref, ob_ref, *, alpha):
    mix = jnp.dot(yr_ref[...], w_ref[:GROUP_WIDTH, :], preferred_element_type=F32)
    mix = mix + lax.dot_general(yd_ref[...], w_ref[GROUP_WIDTH:, :], (((0,), (0,)), ((), ())),
                                preferred_element_type=F32)
    y = _layer_norm(alpha * x_ref[...] + mix, g_ref[...], b_ref[...])
    o_ref[...] = y
    ob_ref[...] = y.astype(BF16)


def _out_projection(yr, ydt, w_out, x, g, b, alpha, tm):
    t = x.shape[0]
    assert ydt.shape == (t // tm, GROUP_WIDTH, tm)
    kernel = functools.partial(_outproj_kernel, alpha=alpha)
    row = lambda i: (i, 0)
    fixed = lambda i: (0, 0)
    return pl.pallas_call(
        kernel,
        grid=(t // tm,),
        in_specs=[pl.BlockSpec((tm, GROUP_WIDTH), row),
                  pl.BlockSpec((None, GROUP_WIDTH, tm), lambda i: (i, 0, 0)),
                  pl.BlockSpec((2 * GROUP_WIDTH, D_MODEL), fixed),
                  pl.BlockSpec((tm, D_MODEL), row),
                  pl.BlockSpec((1, D_MODEL), fixed),
                  pl.BlockSpec((1, D_MODEL), fixed)],
        out_specs=[pl.BlockSpec((tm, D_MODEL), row), pl.BlockSpec((tm, D_MODEL), row)],
        out_shape=[jax.ShapeDtypeStruct((t, D_MODEL), F32), jax.ShapeDtypeStruct((t, D_MODEL), BF16)],
        compiler_params=_cparams("parallel"),
        name="out_proj_ln",
    )(yr, ydt, w_out, x, g, b)


def _ffn_kernel(x_ref, xb_ref, xp_ref, xn_ref, wa_ref, wv_ref, wd_ref, cp_ref, g_ref, b_ref,
                o_ref, ob_ref, acc_ref, *, alpha, tm, n_chunks, starts, ends):
    i = pl.program_id(0)
    t0 = i * tm
    is_start = functools.reduce(jnp.logical_or, [t0 == s for s in starts])
    is_end = functools.reduce(jnp.logical_or, [t0 + tm == e for e in ends])
    keep_prev = jnp.where(is_start, 0.0, 1.0)
    keep_next = jnp.where(is_end, 0.0, 1.0)

    xb = xb_ref[...]
    halo = jnp.concatenate([xp_ref[...], xn_ref[...]], axis=0)
    hrows = xp_ref.shape[0]
    acc_ref[...] = jnp.zeros_like(acc_ref)

    def chunk_body(c, carry):
        wa = wa_ref[c]
        a = jnp.dot(xb, wa, preferred_element_type=F32)
        val = jnp.dot(xb, wv_ref[c], preferred_element_type=F32)
        ah = jnp.dot(halo, wa, preferred_element_type=F32)
        prev_row = ah[hrows - 1:hrows] * keep_prev
        next_row = ah[hrows:hrows + 1] * keep_next
        row = lax.broadcasted_iota(jnp.int32, a.shape, 0)
        a_m1 = jnp.where(row == 0, prev_row, pltpu.roll(a, 1, 0))
        a_p1 = jnp.where(row == tm - 1, next_row, pltpu.roll(a, tm - 1, 0))
        cp = cp_ref[c]
        conv = cp[3:4] + a_m1 * cp[0:1]
        conv = conv + a * cp[1:2]
        conv = conv + a_p1 * cp[2:3]
        gelu = 0.5 * conv * (1.0 + lax.erf(conv * (1.0 / math.sqrt(2.0))))
        hidden = (gelu * val).astype(BF16)
        acc_ref[...] += jnp.dot(hidden, wd_ref[c], preferred_element_type=F32)
        return carry

    for c in range(n_chunks):
        chunk_body(c, 0)
    y = _layer_norm(alpha * x_ref[...] + acc_ref[...], g_ref[...], b_ref[...])
    o_ref[...] = y
    ob_ref[...] = y.astype(BF16)


def _conv_glu(x, xb, wa, wv, wd, cp, g, b, alpha, tm, groups):
    t = x.shape[0]
    n_chunks, _, ck = wa.shape
    hrows = BF16_SUBLANES
    starts = tuple(r0 + bi * s for (r0, nb, s) in groups for bi in range(nb))
    ends = tuple(r0 + (bi + 1) * s for (r0, nb, s) in groups for bi in range(nb))
    kernel = functools.partial(_ffn_kernel, alpha=alpha, tm=tm, n_chunks=n_chunks,
                               starts=starts, ends=ends)
    row = lambda i: (i, 0)
    fixed2 = lambda i: (0, 0)
    fixed3 = lambda i: (0, 0, 0)
    per = tm // hrows
    last = t // hrows - 1
    return pl.pallas_call(
        kernel,
        grid=(t // tm,),
        in_specs=[pl.BlockSpec((tm, D_MODEL), row),
                  pl.BlockSpec((tm, D_MODEL), row),
                  pl.BlockSpec((hrows, D_MODEL), lambda i: (jnp.maximum(i * per - 1, 0), 0)),
                  pl.BlockSpec((hrows, D_MODEL), lambda i: (jnp.minimum((i + 1) * per, last), 0)),
                  pl.BlockSpec((n_chunks, D_MODEL, ck), fixed3, pipeline_mode=pl.Buffered(1)),
                  pl.BlockSpec((n_chunks, D_MODEL, ck), fixed3, pipeline_mode=pl.Buffered(1)),
                  pl.BlockSpec((n_chunks, ck, D_MODEL), fixed3, pipeline_mode=pl.Buffered(1)),
                  pl.BlockSpec((n_chunks, 8, ck), fixed3),
                  pl.BlockSpec((1, D_MODEL), fixed2),
                  pl.BlockSpec((1, D_MODEL), fixed2)],
        out_specs=[pl.BlockSpec((tm, D_MODEL), row), pl.BlockSpec((tm, D_MODEL), row)],
        out_shape=[jax.ShapeDtypeStruct((t, D_MODEL), F32), jax.ShapeDtypeStruct((t, D_MODEL), BF16)],
        scratch_shapes=[pltpu.VMEM((tm, D_MODEL), F32)],
        compiler_params=_cparams("parallel"),
        name="conv_glu_ln",
    )(x, xb, xb, xb, wa, wv, wd, cp, g, b)


def _rotary_tables(seq):
    d = HEAD_DIM
    inv = 1.0 / (ROPE_BASE ** (jnp.arange(0, d, 2, dtype=F32) / d))
    ang = jnp.arange(seq, dtype=F32)[:, None] * inv[None, :]
    cos, sin = jnp.cos(ang), jnp.sin(ang)
    cos_t = jnp.concatenate([cos, cos, cos, cos], axis=-1)
    sin_t = jnp.concatenate([-sin, sin, -sin, sin], axis=-1)
    return cos_t, sin_t


def _tiles(groups):
    smin = min(s for (_, _, s) in groups)
    attn_tile = min(512, smin // 2)
    chunk = min(256, smin // 2)
    tm = min(512, smin // 2)
    ffn_tm = min(1024, smin // 2)
    return attn_tile, chunk, tm, ffn_tm


def _forward(x, groups, w_in, ret_decay_logit, rel_bias, lambda_q1, lambda_k1, lambda_q2,
             lambda_k2, diff_norm_g, w_out, ln_g, ln_b, w_up, conv_w, conv_b, w_down):
    depth = w_in.shape[0]
    alpha = (2 * depth) ** 0.25
    t = x.shape[0]
    attn_tile, chunk, tm, ffn_tm = _tiles(groups)
    smax = max(s for (_, _, s) in groups)
    gw = GROUP_WIDTH
    ck = MXU_WIDTH
    n_chunks = D_FF // ck

    cos_t, sin_t = _rotary_tables(smax)
    bias_t = _bias_tiles(rel_bias, attn_tile)
    rel_log2 = rel_bias.astype(F32) * LOG2E
    rel_log2 = jnp.concatenate([rel_log2, jnp.max(rel_log2, axis=0, keepdims=True)], axis=0)
    in_scale = jnp.concatenate([
        jnp.ones((gw,), F32), jnp.full((gw,), HEAD_DIM ** -0.5, F32), jnp.ones((2 * gw,), F32),
        jnp.full((gw,), DIFF_QK_DIM ** -0.5 * LOG2E, F32), jnp.ones((gw,), F32)])[None, :]
    assert tm == attn_tile

    xb = x.astype(BF16)
    for l in range(depth):
        lam_init = 0.8 - 0.6 * math.exp(-0.3 * l)
        w_in_b = w_in[l].astype(BF16)
        ret, gate, dqk, vt3 = _project(xb, w_in_b[:, :6 * gw], w_in_b[:, 6 * gw:].T, in_scale,
                                       cos_t, sin_t, groups, tm)

        log_g = jax.nn.log_sigmoid(ret_decay_logit[l].astype(F32))
        lamp = jnp.stack([lambda_q1[l], lambda_k1[l], lambda_q2[l], lambda_k2[l],
                          jnp.full((DIFF_QK_DIM,), lam_init)]).astype(F32)
        dng = diff_norm_g[l].astype(F32)[:, None]

        yr = jnp.zeros((t, gw), BF16)
        ydt = jnp.zeros((t // attn_tile, gw, attn_tile), BF16)
        for (row0, batch, seq) in groups:
            yr = _retention(ret, gate, log_g, yr, row0, batch, seq, chunk)
            ydt = _diff_attention(dqk, vt3, bias_t, rel_log2, lamp, dng, ydt, row0, batch, seq,
                                  attn_tile)

        x, xb = _out_projection(yr, ydt, w_out[l].astype(BF16), x, ln_g[l, 0][None, :].astype(F32),
                                ln_b[l, 0][None, :].astype(F32), alpha, tm)

        wa = w_up[l][:, :D_FF].astype(BF16).reshape(D_MODEL, n_chunks, ck).transpose(1, 0, 2)
        wv = w_up[l][:, D_FF:].astype(BF16).reshape(D_MODEL, n_chunks, ck).transpose(1, 0, 2)
        wd = w_down[l].astype(BF16).reshape(n_chunks, ck, D_MODEL)
        cp = jnp.concatenate([conv_w[l].astype(F32), conv_b[l].astype(F32)[None, :],
                              jnp.zeros((4, D_FF), F32)], axis=0)
        cp = cp.reshape(8, n_chunks, ck).transpose(1, 0, 2)
        x, xb = _conv_glu(x, xb, wa, wv, wd, cp, ln_g[l, 1][None, :].astype(F32),
                          ln_b[l, 1][None, :].astype(F32), alpha, ffn_tm, groups)
    return x


def kernel(x_prompt, x_sample, w_in, ret_decay_logit, rel_bias, lambda_q1, lambda_k1, lambda_q2,
           lambda_k2, diff_norm_g, w_out, ln_g, ln_b, w_up, conv_w, conv_b, w_down):
    bp, sp, d = x_prompt.shape
    bs, ss, _ = x_sample.shape
    groups = ((0, bp, sp), (bp * sp, bs, ss))
    x = jnp.concatenate([x_prompt.reshape(bp * sp, d), x_sample.reshape(bs * ss, d)], axis=0)
    y = _forward(x.astype(F32), groups, w_in, ret_decay_logit, rel_bias, lambda_q1, lambda_k1,
                 lambda_q2, lambda_k2, diff_norm_g, w_out, ln_g, ln_b, w_up, conv_w, conv_b, w_down)
    y_prompt = y[:bp * sp].reshape(bp, sp, d).astype(x_prompt.dtype)
    y_sample = y[bp * sp:].reshape(bs, ss, d).astype(x_sample.dtype)
    return y_prompt, y_sample
```

```python
import functools
import math

import jax
import jax.numpy as jnp
from jax import lax
from jax.experimental import pallas as pl
from jax.experimental.pallas import tpu as pltpu

D_MODEL = 1024
HEAD_DIM = 64
N_HEADS = 8
GROUP_WIDTH = N_HEADS * HEAD_DIM
DIFF_QK_DIM = HEAD_DIM // 2
D_FF = 2816
N_BUCKETS = 32
MAX_DISTANCE = 128
ROPE_BASE = 10000.0
LN_EPS = 1e-5
HEAD_NORM_EPS = 1e-6
LANES = 128
BF16_SUBLANES = 16
MXU_WIDTH = 256
FAR_DISTANCE = 91
BIAS_REACH = 2
MAX_UNROLLED_MACRO_TILES = 8
LOG2E = math.log2(math.e)
NEG_BIG = -1e30
UNDERFLOW_GUARD = 2.0 ** -90
VMEM_LIMIT = 56 * 1024 * 1024

F32 = jnp.float32
BF16 = jnp.bfloat16


def _cparams(*sem):
    return pltpu.CompilerParams(dimension_semantics=sem, vmem_limit_bytes=VMEM_LIMIT)


def _proj_kernel(x_ref, w_ref, wvt_ref, s_ref, cos_ref, sin_ref, ret_ref, gate_ref, dqk_ref, vt_ref):
    gw = GROUP_WIDTH
    x = x_ref[...]
    tm = x.shape[0]
    lane = lax.broadcasted_iota(jnp.int32, (tm, LANES), 1)
    low_half = (lane % HEAD_DIM) < (HEAD_DIM // 2)
    cos = cos_ref[...]
    sin = sin_ref[...]
    qk = jnp.dot(x, w_ref[:, :2 * gw], preferred_element_type=F32) * s_ref[:, :2 * gw]
    for p in range(2 * gw // LANES):
        sl = slice(p * LANES, (p + 1) * LANES)
        xx = qk[:, sl]
        swapped = jnp.where(low_half, pltpu.roll(xx, LANES - HEAD_DIM // 2, 1),
                            pltpu.roll(xx, HEAD_DIM // 2, 1))
        ret_ref[:, sl] = (xx * cos + swapped * sin).astype(ret_ref.dtype)
    ret_ref[:, 2 * gw:] = jnp.dot(x, w_ref[:, 2 * gw:3 * gw],
                                  preferred_element_type=F32).astype(ret_ref.dtype)
    gate_ref[...] = jnp.dot(x, w_ref[:, 3 * gw:4 * gw], preferred_element_type=F32)
    dqk_ref[...] = (jnp.dot(x, w_ref[:, 4 * gw:], preferred_element_type=F32)
                    * s_ref[:, 4 * gw:]).astype(dqk_ref.dtype)
    vt_ref[...] = lax.dot_general(wvt_ref[...], x, (((1,), (1,)), ((), ())),
                                  preferred_element_type=F32).astype(vt_ref.dtype)


def _position_block(i, tm, groups):
    t0 = i * tm
    blk = t0 // tm
    for (row0, _, seq) in groups:
        blk = jnp.where(t0 >= row0, ((t0 - row0) % seq) // tm, blk)
    return blk


def _project(xb, w, w_vt, scale, cos_t, sin_t, groups, tm):
    t, k = xb.shape
    gw = GROUP_WIDTH
    row = lambda i: (i, 0)
    fixed = lambda i: (0, 0)
    pos = lambda i: (_position_block(i, tm, groups), 0)
    return pl.pallas_call(
        _proj_kernel,
        grid=(t // tm,),
        in_specs=[pl.BlockSpec((tm, k), row),
                  pl.BlockSpec((k, 6 * gw), fixed, pipeline_mode=pl.Buffered(1)),
                  pl.BlockSpec((gw, k), fixed, pipeline_mode=pl.Buffered(1)),
                  pl.BlockSpec((1, 6 * gw), fixed),
                  pl.BlockSpec((tm, LANES), pos),
                  pl.BlockSpec((tm, LANES), pos)],
        out_specs=[pl.BlockSpec((tm, 3 * gw), row), pl.BlockSpec((tm, gw), row),
                   pl.BlockSpec((tm, 2 * gw), row),
                   pl.BlockSpec((None, gw, tm), lambda i: (i, 0, 0))],
        out_shape=[jax.ShapeDtypeStruct((t, 3 * gw), BF16), jax.ShapeDtypeStruct((t, gw), F32),
                   jax.ShapeDtypeStruct((t, 2 * gw), BF16),
                   jax.ShapeDtypeStruct((t // tm, gw, tm), BF16)],
        compiler_params=_cparams("parallel"),
        name="in_proj",
    )(xb, w, w_vt, scale, cos_t, sin_t)


def _ret_kernel(lg_ref, q_ref, k_ref, v_ref, g_ref, _, o_ref,
                rf_ref, rb_ref, rnext_ref, mask_ref, tab_ref, *, ns, per_step, chunk):
    t = pl.program_id(1)
    c = chunk
    n_pairs = GROUP_WIDTH // LANES
    lane = lax.broadcasted_iota(jnp.int32, (c, LANES), 1)
    head0 = lane < HEAD_DIM
    r_i = lax.broadcasted_iota(jnp.int32, (LANES, LANES), 0) // HEAD_DIM
    c_i = lax.broadcasted_iota(jnp.int32, (LANES, LANES), 1) // HEAD_DIM
    same_head = r_i == c_i

    @pl.when(t == 0)
    def _init():
        rb_ref[...] = jnp.zeros_like(rb_ref)
        qi = lax.broadcasted_iota(jnp.int32, (c, c), 0)
        ki = lax.broadcasted_iota(jnp.int32, (c, c), 1)
        diff = (qi - ki).astype(F32)
        pos = lax.broadcasted_iota(jnp.int32, (c, LANES), 0).astype(F32)
        for p in range(n_pairs):
            for hh in range(2):
                lf = lg_ref[0, 2 * p + hh]
                lb = lg_ref[1, 2 * p + hh]
                mask_ref[p, hh * c:(hh + 1) * c, :] = jnp.where(
                    diff >= 0, jnp.exp(lf * jnp.maximum(diff, 0.0)),
                    jnp.exp(lb * jnp.maximum(-diff, 0.0)))
            lfl = jnp.where(head0, lg_ref[0, 2 * p], lg_ref[0, 2 * p + 1])
            lbl = jnp.where(head0, lg_ref[1, 2 * p], lg_ref[1, 2 * p + 1])
            tab_ref[p, 0] = jnp.exp(lfl * (c - 1 - pos))
            tab_ref[p, 1] = jnp.exp(lbl * pos)
            tab_ref[p, 2] = jnp.exp(lfl * (pos + 1.0))
            tab_ref[p, 3] = jnp.exp(lbl * (c - pos))
            tab_ref[p, 4] = jnp.exp(lfl * c)
            tab_ref[p, 5] = jnp.exp(lbl * c)

    def summary(k, w, v):
        kw = (k.astype(F32) * w).astype(BF16)
        kv = lax.dot_general(kw, v, (((0,), (0,)), ((), ())), preferred_element_type=F32)
        return jnp.where(same_head, kv, 0.0)

    @pl.when(t < ns)
    def _backward():
        for u in reversed(range(per_step)):
            rows = slice(u * c, (u + 1) * c)
            ci = (ns - 1 - t) * per_step + u
            for p in range(n_pairs):
                sl = slice(p * LANES, (p + 1) * LANES)
                rnext_ref[ci, p] = rb_ref[p].astype(BF16)
                rb_ref[p] = (tab_ref[p, 5][:LANES] * rb_ref[p]
                             + summary(k_ref[rows, sl], tab_ref[p, 1], v_ref[rows, sl]))

    @pl.when(t >= ns)
    def _forward():
        @pl.when(t == ns)
        def _():
            rf_ref[...] = jnp.zeros_like(rf_ref)

        for u in range(per_step):
            rows = slice(u * c, (u + 1) * c)
            ci = (t - ns) * per_step + u
            for p in range(n_pairs):
                sl = slice(p * LANES, (p + 1) * LANES)
                q = q_ref[rows, sl]
                k = k_ref[rows, sl]
                v = v_ref[rows, sl]
                zero = jnp.zeros_like(q)
                q2 = jnp.concatenate([jnp.where(head0, q, zero), jnp.where(head0, zero, q)], axis=0)
                s = lax.dot_general(q2, k, (((1,), (1,)), ((), ())), preferred_element_type=F32)
                o2 = jnp.dot((s * mask_ref[p]).astype(BF16), v, preferred_element_type=F32)
                y = jnp.where(head0, o2[:c], o2[c:])
                y = y + jnp.dot(q, rf_ref[p].astype(BF16), preferred_element_type=F32) * tab_ref[p, 2]
                y = y + jnp.dot(q, rnext_ref[ci, p], preferred_element_type=F32) * tab_ref[p, 3]

                sq = y * y
                s0 = jnp.sum(jnp.where(head0, sq, 0.0), axis=1, keepdims=True)
                s1 = jnp.sum(jnp.where(head0, 0.0, sq), axis=1, keepdims=True)
                ms = jnp.where(head0, s0, s1) * (1.0 / HEAD_DIM)
                y = y * lax.rsqrt(ms + HEAD_NORM_EPS)
                g = g_ref[rows, sl]
                o_ref[rows, sl] = (g / (1.0 + jnp.exp(-g)) * y).astype(o_ref.dtype)

                rf_ref[p] = tab_ref[p, 4][:LANES] * rf_ref[p] + summary(k, tab_ref[p, 0], v)


def _retention(ret, gate, log_g, out, row0, batch, seq, chunk):
    nc = seq // chunk
    per_step = max(u for u in (1, 2, 4) if nc % u == 0)
    ns = nc // per_step
    step_rows = per_step * chunk
    blk0 = row0 // step_rows
    n_pairs = GROUP_WIDTH // LANES
    gw = GROUP_WIDTH

    def kc(t):
        return jnp.where(t < ns, ns - 1 - t, t - ns)

    def qc(t):
        return jnp.maximum(t - ns, 0)

    def rows(b, cc):
        return blk0 + b * ns + cc

    kernel = functools.partial(_ret_kernel, ns=ns, per_step=per_step, chunk=chunk)
    return pl.pallas_call(
        kernel,
        grid=(batch, 2 * ns),
        in_specs=[pl.BlockSpec(memory_space=pltpu.SMEM),
                  pl.BlockSpec((step_rows, gw), lambda b, t: (rows(b, qc(t)), 0)),
                  pl.BlockSpec((step_rows, gw), lambda b, t: (rows(b, kc(t)), 1)),
                  pl.BlockSpec((step_rows, gw), lambda b, t: (rows(b, kc(t)), 2)),
                  pl.BlockSpec((step_rows, gw), lambda b, t: (rows(b, qc(t)), 0)),
                  pl.BlockSpec(memory_space=pl.ANY)],
        out_specs=pl.BlockSpec((step_rows, gw), lambda b, t: (rows(b, qc(t)), 0)),
        out_shape=jax.ShapeDtypeStruct(out.shape, out.dtype),
        input_output_aliases={5: 0},
        scratch_shapes=[pltpu.VMEM((n_pairs, LANES, LANES), F32),
                        pltpu.VMEM((n_pairs, LANES, LANES), F32),
                        pltpu.VMEM((nc, n_pairs, LANES, LANES), BF16),
                        pltpu.VMEM((n_pairs, 2 * chunk, chunk), F32),
                        pltpu.VMEM((n_pairs, 6, chunk, LANES), F32)],
        compiler_params=_cparams("parallel", "arbitrary"),
        name="retention",
    )(log_g, ret, ret, ret, gate, out)


def _bias_kernel(rb_ref, bucket_ref, o_ref):
    h = pl.program_id(0)
    bk = bucket_ref[...]
    out = jnp.zeros(bk.shape, F32)
    for n in range(N_BUCKETS):
        out = jnp.where(bk == n, rb_ref[n, h], out)
    o_ref[...] = out * LOG2E


def _bias_tiles(rel_bias, tile):
    assert tile > FAR_DISTANCE
    a = jnp.arange(tile, dtype=jnp.int32)[:, None]
    b = jnp.arange(tile, dtype=jnp.int32)[None, :]
    rel = jnp.stack([(d * tile + a - b) for d in range(-BIAS_REACH, BIAS_REACH + 1)])
    nb = N_BUCKETS // 2
    max_exact = nb // 2
    n = jnp.abs(rel)
    nf = jnp.maximum(n, 1).astype(F32)
    large = max_exact + (jnp.log(nf / max_exact) / math.log(MAX_DISTANCE / max_exact)
                         * (nb - max_exact)).astype(jnp.int32)
    large = jnp.minimum(large, nb - 1)
    bucket = jnp.where(rel > 0, nb, 0) + jnp.where(n < max_exact, n, large)
    return pl.pallas_call(
        _bias_kernel,
        grid=(N_HEADS, 2 * BIAS_REACH + 1),
        in_specs=[pl.BlockSpec(memory_space=pltpu.SMEM),
                  pl.BlockSpec((None, tile, tile), lambda h, d: (d, 0, 0))],
        out_specs=pl.BlockSpec((None, None, tile, tile), lambda h, d: (h, d, 0, 0)),
        out_shape=jax.ShapeDtypeStruct((N_HEADS, 2 * BIAS_REACH + 1, tile, tile), F32),
        compiler_params=_cparams("parallel", "arbitrary"),
        name="t5_bias_tiles",
    )(rel_bias.astype(F32), bucket.astype(jnp.int32))


def _attn_kernel(rb_ref, lamp_ref, dng_ref, q_ref, k_ref, vt_ref, bt_ref, _, o_ref,
                 a_ref, l_ref, p_ref, qa_ref, kmax_ref, m_ref, *, n_tiles, tile, macro):
    h = pl.program_id(1)
    qi = pl.program_id(2)
    hh = h % 2
    n = n_tiles
    half = DIFF_QK_DIM
    group_of_lane = lax.broadcasted_iota(jnp.int32, (LANES, LANES), 0) // half
    group_sum = (group_of_lane == lax.broadcasted_iota(jnp.int32, (LANES, LANES), 1)).astype(BF16)

    def half_norms(x):
        xf = x.astype(F32)
        return jnp.dot((xf * xf).astype(BF16), group_sum, preferred_element_type=F32)

    @pl.when(qi == 0)
    def _key_norms():
        def body(j, mx):
            kt = k_ref[pl.ds(pl.multiple_of(j * tile, tile), tile), :]
            return jnp.maximum(mx, jnp.max(half_norms(kt), axis=0, keepdims=True))
        kmax_ref[...] = lax.fori_loop(0, n, body, jnp.zeros((1, LANES), F32))

    qf = q_ref[...].astype(F32)
    lane_row = lax.broadcasted_iota(jnp.int32, (1, LANES), 1)
    row = lax.broadcasted_iota(jnp.int32, (LANES, tile), 0)
    group_sum_t = (lax.broadcasted_iota(jnp.int32, (LANES, LANES), 0)
                   == lax.broadcasted_iota(jnp.int32, (LANES, LANES), 1) // half).astype(BF16)
    qn_rows = lax.dot_general(group_sum_t, (qf * qf).astype(BF16), (((1,), (1,)), ((), ())),
                              preferred_element_type=F32)
    qft = qf.T
    for t in range(2):
        g = 2 * hh + t
        lo_lane = g * half
        qzt = jnp.where((row >= lo_lane) & (row < lo_lane + half), qft, 0.0).astype(BF16)
        qn2 = jnp.sum(jnp.where(row == g, qn_rows, 0.0), axis=0, keepdims=True)
        kn2 = jnp.max(jnp.where(lane_row == g, kmax_ref[...], 0.0), axis=1, keepdims=True)
        v = -(jnp.sqrt(qn2 * kn2) + rb_ref[N_BUCKETS, h])
        hi = v.astype(BF16).astype(F32)
        lo = (v - hi).astype(BF16).astype(F32)
        qa_ref[t, :LANES, :] = qzt
        qa_ref[t, LANES:, :] = jnp.where(row == 0, hi, jnp.where(row == 1, lo, 0.0)).astype(BF16)

    ones = jnp.ones((macro * tile, LANES), BF16)

    def probabilities(m, slot):
        j0 = m * macro
        kaug = jnp.concatenate([k_ref[j0 * tile:(j0 + macro) * tile, :], ones], axis=1)
        for t in range(2):
            s = jnp.dot(kaug, qa_ref[t], preferred_element_type=F32)
            for r in range(macro):
                bias = bt_ref[jnp.clip(j0 + r - qi, -BIAS_REACH, BIAS_REACH) + BIAS_REACH]
                sr = s[r * tile:(r + 1) * tile] + bias
                e = jnp.exp2(sr)
                l_ref[t] += jnp.sum(e, axis=0, keepdims=True)
                p_ref[slot, t, r * tile:(r + 1) * tile, :] = e.astype(BF16)

    def accumulate(m, slot):
        vt = jnp.concatenate([vt_ref[m * macro + r] for r in range(macro)], axis=1)
        for t in range(2):
            a_ref[t] += jnp.dot(vt, p_ref[slot, t], preferred_element_type=F32)

    a_ref[...] = jnp.zeros_like(a_ref)
    l_ref[...] = jnp.zeros_like(l_ref)

    n_macro = n // macro
    probabilities(0, 0)
    for m in range(n_macro):
        if m + 1 < n_macro:
            probabilities(m + 1, (m + 1) % 2)
        accumulate(m, m % 2)

    trusted = jnp.min(jnp.minimum(l_ref[0], l_ref[1])) >= UNDERFLOW_GUARD

    @pl.when(jnp.logical_not(trusted))
    def _exact_running_max():
        m_ref[...] = jnp.full(m_ref.shape, NEG_BIG, F32)
        a_ref[...] = jnp.zeros_like(a_ref)
        l_ref[...] = jnp.zeros_like(l_ref)

        def body(j, carry):
            kt = k_ref[pl.ds(pl.multiple_of(j * tile, tile), tile), :]
            vt = vt_ref[j]
            bias = bt_ref[jnp.clip(j - qi, -BIAS_REACH, BIAS_REACH) + BIAS_REACH]
            for t in range(2):
                s = jnp.dot(kt, qa_ref[t, :LANES, :], preferred_element_type=F32) + bias
                m_prev = m_ref[t]
                m_new = jnp.maximum(m_prev, jnp.max(s, axis=0, keepdims=True))
                alpha = jnp.exp2(m_prev - m_new)
                e = jnp.exp2(s - m_new)
                l_ref[t] = alpha * l_ref[t] + jnp.sum(e, axis=0, keepdims=True)
                a_ref[t] = alpha * a_ref[t] + jnp.dot(vt, e.astype(BF16),
                                                      preferred_element_type=F32)
                m_ref[t] = m_new
            return carry

        lax.fori_loop(0, n, body, 0)

    lp = lamp_ref[...]
    lam_init = lp[4:5, 0:1]
    lam = (jnp.exp(jnp.sum(lp[0:1] * lp[1:2], axis=1, keepdims=True))
           - jnp.exp(jnp.sum(lp[2:3] * lp[3:4], axis=1, keepdims=True)) + lam_init)
    o = a_ref[0] / l_ref[0] - lam * (a_ref[1] / l_ref[1])
    ms = jnp.mean(o * o, axis=0, keepdims=True)
    y = o * lax.rsqrt(ms + HEAD_NORM_EPS) * dng_ref[...] * (1.0 - lam_init)
    o_ref[...] = y.astype(o_ref.dtype)


def _diff_attention(dqk, vt3, bias_t, rel_log2, lamp, dng, out, row0, batch, seq, tile):
    n = seq // tile
    macro = max(g for g in (1, 2, 4, 8) if n % g == 0)
    assert n // macro <= MAX_UNROLLED_MACRO_TILES
    n_pairs = GROUP_WIDTH // LANES
    qblk0 = row0 // tile
    sblk0 = row0 // seq
    n_bias = 2 * BIAS_REACH + 1
    kernel = functools.partial(_attn_kernel, n_tiles=n, tile=tile, macro=macro)
    return pl.pallas_call(
        kernel,
        grid=(batch, N_HEADS, n),
        in_specs=[pl.BlockSpec(memory_space=pltpu.SMEM),
                  pl.BlockSpec((5, DIFF_QK_DIM), lambda b, h, i: (0, 0)),
                  pl.BlockSpec((HEAD_DIM, 1), lambda b, h, i: (0, 0)),
                  pl.BlockSpec((tile, LANES), lambda b, h, i: (qblk0 + b * n + i, h // 2)),
                  pl.BlockSpec((seq, LANES), lambda b, h, i: (sblk0 + b, n_pairs + h // 2)),
                  pl.BlockSpec((n, HEAD_DIM, tile), lambda b, h, i: (sblk0 + b, h, 0)),
                  pl.BlockSpec((None, n_bias, tile, tile), lambda b, h, i: (h, 0, 0, 0)),
                  pl.BlockSpec(memory_space=pl.ANY)],
        out_specs=pl.BlockSpec((None, HEAD_DIM, tile), lambda b, h, i: (qblk0 + b * n + i, h, 0)),
        out_shape=jax.ShapeDtypeStruct(out.shape, out.dtype),
        input_output_aliases={7: 0},
        scratch_shapes=[pltpu.VMEM((2, HEAD_DIM, tile), F32),
                        pltpu.VMEM((2, 1, tile), F32),
                        pltpu.VMEM((2, 2, macro * tile, tile), BF16),
                        pltpu.VMEM((2, 2 * LANES, tile), BF16),
                        pltpu.VMEM((1, LANES), F32),
                        pltpu.VMEM((2, 1, tile), F32)],
        compiler_params=_cparams("parallel", "parallel", "arbitrary"),
        name="diff_attention",
    )(rel_log2, lamp, dng, dqk, dqk, vt3, bias_t, out)


def _layer_norm(y, g, b):
    mu = jnp.mean(y, axis=-1, keepdims=True)
    d = y - mu
    var = jnp.mean(d * d, axis=-1, keepdims=True)
    return d * lax.rsqrt(var + LN_EPS) * g + b


def _outproj_kernel(yr_ref, yd_ref, w_ref, x_ref, g_ref, b_ref, o_ref, ob_ref, *, alpha):
    mix = jnp.dot(yr_ref[...], w_ref[:GROUP_WIDTH, :], preferred_element_type=F32)
    mix = mix + lax.dot_general(yd_ref[...], w_ref[GROUP_WIDTH:, :], (((0,), (0,)), ((), ())),
                                preferred_element_type=F32)
    y = _layer_norm(alpha * x_ref[...] + mix, g_ref[...], b_ref[...])
    o_ref[...] = y
    ob_ref[...] = y.astype(BF16)


def _out_projection(yr, ydt, w_out, x, g, b, alpha, tm):
    t = x.shape[0]
    assert ydt.shape == (t // tm, GROUP_WIDTH, tm)
    kernel = functools.partial(_outproj_kernel, alpha=alpha)
    row = lambda i: (i, 0)
    fixed = lambda i: (0, 0)
    return pl.pallas_call(
        kernel,
        grid=(t // tm,),
        in_specs=[pl.BlockSpec((tm, GROUP_WIDTH), row),
                  pl.BlockSpec((None, GROUP_WIDTH, tm), lambda i: (i, 0, 0)),
                  pl.BlockSpec((2 * GROUP_WIDTH, D_MODEL), fixed),
                  pl.BlockSpec((tm, D_MODEL), row),
                  pl.BlockSpec((1, D_MODEL), fixed),
                  pl.BlockSpec((1, D_MODEL), fixed)],
        out_specs=[pl.BlockSpec((tm, D_MODEL), row), pl.BlockSpec((tm, D_MODEL), row)],
        out_shape=[jax.ShapeDtypeStruct((t, D_MODEL), F32), jax.ShapeDtypeStruct((t, D_MODEL), BF16)],
        compiler_params=_cparams("parallel"),
        name="out_proj_ln",
    )(yr, ydt, w_out, x, g, b)


def _ffn_kernel(x_ref, xb_ref, xp_ref, xn_ref, wa_ref, wv_ref, wd_ref, cp_ref, g_ref, b_ref,
                o_ref, ob_ref, acc_ref, *, alpha, tm, n_chunks, starts, ends):
    i = pl.program_id(0)
    t0 = i * tm
    is_start = functools.reduce(jnp.logical_or, [t0 == s for s in starts])
    is_end = functools.reduce(jnp.logical_or, [t0 + tm == e for e in ends])
    keep_prev = jnp.where(is_start, 0.0, 1.0)
    keep_next = jnp.where(is_end, 0.0, 1.0)

    xb = xb_ref[...]
    halo = jnp.concatenate([xp_ref[...], xn_ref[...]], axis=0)
    hrows = xp_ref.shape[0]
    acc_ref[...] = jnp.zeros_like(acc_ref)

    def chunk_body(c, carry):
        wa = wa_ref[c]
        a = jnp.dot(xb, wa, preferred_element_type=F32)
        val = jnp.dot(xb, wv_ref[c], preferred_element_type=F32)
        ah = jnp.dot(halo, wa, preferred_element_type=F32)
        prev_row = ah[hrows - 1:hrows] * keep_prev
        next_row = ah[hrows:hrows + 1] * keep_next
        row = lax.broadcasted_iota(jnp.int32, a.shape, 0)
        a_m1 = jnp.where(row == 0, prev_row, pltpu.roll(a, 1, 0))
        a_p1 = jnp.where(row == tm - 1, next_row, pltpu.roll(a, tm - 1, 0))
        cp = cp_ref[c]
        conv = cp[3:4] + a_m1 * cp[0:1]
        conv = conv + a * cp[1:2]
        conv = conv + a_p1 * cp[2:3]
        gelu = 0.5 * conv * (1.0 + lax.erf(conv * (1.0 / math.sqrt(2.0))))
        hidden = (gelu * val).astype(BF16)
        acc_ref[...] += jnp.dot(hidden, wd_ref[c], preferred_element_type=F32)
        return carry

    for c in range(n_chunks):
        chunk_body(c, 0)
    y = _layer_norm(alpha * x_ref[...] + acc_ref[...], g_ref[...], b_ref[...])
    o_ref[...] = y
    ob_ref[...] = y.astype(BF16)


def _conv_glu(x, xb, wa, wv, wd, cp, g, b, alpha, tm, groups):
    t = x.shape[0]
    n_chunks, _, ck = wa.shape
    hrows = BF16_SUBLANES
    starts = tuple(r0 + bi * s for (r0, nb, s) in groups for bi in range(nb))
    ends = tuple(r0 + (bi + 1) * s for (r0, nb, s) in groups for bi in range(nb))
    kernel = functools.partial(_ffn_kernel, alpha=alpha, tm=tm, n_chunks=n_chunks,
                               starts=starts, ends=ends)
    row = lambda i: (i, 0)
    fixed2 = lambda i: (0, 0)
    fixed3 = lambda i: (0, 0, 0)
    per = tm // hrows
    last = t // hrows - 1
    return pl.pallas_call(
        kernel,
        grid=(t // tm,),
        in_specs=[pl.BlockSpec((tm, D_MODEL), row),
                  pl.BlockSpec((tm, D_MODEL), row),
                  pl.BlockSpec((hrows, D_MODEL), lambda i: (jnp.maximum(i * per - 1, 0), 0)),
                  pl.BlockSpec((hrows, D_MODEL), lambda i: (jnp.minimum((i + 1) * per, last), 0)),
                  pl.BlockSpec((n_chunks, D_MODEL, ck), fixed3, pipeline_mode=pl.Buffered(1)),
                  pl.BlockSpec((n_chunks, D_MODEL, ck), fixed3, pipeline_mode=pl.Buffered(1)),
                  pl.BlockSpec((n_chunks, ck, D_MODEL), fixed3, pipeline_mode=pl.Buffered(1)),
                  pl.BlockSpec((n_chunks, 8, ck), fixed3),
                  pl.BlockSpec((1, D_MODEL), fixed2),
                  pl.BlockSpec((1, D_MODEL), fixed2)],
        out_specs=[pl.BlockSpec((tm, D_MODEL), row), pl.BlockSpec((tm, D_MODEL), row)],
        out_shape=[jax.ShapeDtypeStruct((t, D_MODEL), F32), jax.ShapeDtypeStruct((t, D_MODEL), BF16)],
        scratch_shapes=[pltpu.VMEM((tm, D_MODEL), F32)],
        compiler_params=_cparams("parallel"),
        name="conv_glu_ln",
    )(x, xb, xb, xb, wa, wv, wd, cp, g, b)


def _rotary_tables(seq):
    d = HEAD_DIM
    inv = 1.0 / (ROPE_BASE ** (jnp.arange(0, d, 2, dtype=F32) / d))
    ang = jnp.arange(seq, dtype=F32)[:, None] * inv[None, :]
    cos, sin = jnp.cos(ang), jnp.sin(ang)
    cos_t = jnp.concatenate([cos, cos, cos, cos], axis=-1)
    sin_t = jnp.concatenate([-sin, sin, -sin, sin], axis=-1)
    return cos_t, sin_t


def _tiles(groups):
    smin = min(s for (_, _, s) in groups)
    attn_tile = min(512, smin // 2)
    chunk = min(256, smin // 2)
    tm = min(512, smin // 2)
    ffn_tm = min(1024, smin // 2)
    return attn_tile, chunk, tm, ffn_tm


def _forward(x, groups, w_in, ret_decay_logit, rel_bias, lambda_q1, lambda_k1, lambda_q2,
             lambda_k2, diff_norm_g, w_out, ln_g, ln_b, w_up, conv_w, conv_b, w_down):
    depth = w_in.shape[0]
    alpha = (2 * depth) ** 0.25
    t = x.shape[0]
    attn_tile, chunk, tm, ffn_tm = _tiles(groups)
    smax = max(s for (_, _, s) in groups)
    gw = GROUP_WIDTH
    ck = MXU_WIDTH
    n_chunks = D_FF // ck

    cos_t, sin_t = _rotary_tables(smax)
    bias_t = _bias_tiles(rel_bias, attn_tile)
    rel_log2 = rel_bias.astype(F32) * LOG2E
    rel_log2 = jnp.concatenate([rel_log2, jnp.max(rel_log2, axis=0, keepdims=True)], axis=0)
    in_scale = jnp.concatenate([
        jnp.ones((gw,), F32), jnp.full((gw,), HEAD_DIM ** -0.5, F32), jnp.ones((2 * gw,), F32),
        jnp.full((gw,), DIFF_QK_DIM ** -0.5 * LOG2E, F32), jnp.ones((gw,), F32)])[None, :]
    assert tm == attn_tile

    xb = x.astype(BF16)
    for l in range(depth):
        lam_init = 0.8 - 0.6 * math.exp(-0.3 * l)
        w_in_b = w_in[l].astype(BF16)
        ret, gate, dqk, vt3 = _project(xb, w_in_b[:, :6 * gw], w_in_b[:, 6 * gw:].T, in_scale,
                                       cos_t, sin_t, groups, tm)

        log_g = jax.nn.log_sigmoid(ret_decay_logit[l].astype(F32))
        lamp = jnp.stack([lambda_q1[l], lambda_k1[l], lambda_q2[l], lambda_k2[l],
                          jnp.full((DIFF_QK_DIM,), lam_init)]).astype(F32)
        dng = diff_norm_g[l].astype(F32)[:, None]

        yr = jnp.zeros((t, gw), BF16)
        ydt = jnp.zeros((t // attn_tile, gw, attn_tile), BF16)
        for (row0, batch, seq) in groups:
            yr = _retention(ret, gate, log_g, yr, row0, batch, seq, chunk)
            ydt = _diff_attention(dqk, vt3, bias_t, rel_log2, lamp, dng, ydt, row0, batch, seq,
                                  attn_tile)

        x, xb = _out_projection(yr, ydt, w_out[l].astype(BF16), x, ln_g[l, 0][None, :].astype(F32),
                                ln_b[l, 0][None, :].astype(F32), alpha, tm)

        wa = w_up[l][:, :D_FF].astype(BF16).reshape(D_MODEL, n_chunks, ck).transpose(1, 0, 2)
        wv = w_up[l][:, D_FF:].astype(BF16).reshape(D_MODEL, n_chunks, ck).transpose(1, 0, 2)
        wd = w_down[l].astype(BF16).reshape(n_chunks, ck, D_MODEL)
        cp = jnp.concatenate([conv_w[l].astype(F32), conv_b[l].astype(F32)[None, :],
                              jnp.zeros((4, D_FF), F32)], axis=0)
        cp = cp.reshape(8, n_chunks, ck).transpose(1, 0, 2)
        x, xb = _conv_glu(x, xb, wa, wv, wd, cp, ln_g[l, 1][None, :].astype(F32),
                          ln_b[l, 1][None, :].astype(F32), alpha, ffn_tm, groups)
    return x


def kernel(x_prompt, x_sample, w_in, ret_decay_logit, rel_bias, lambda_q1, lambda_k1, lambda_q2,
           lambda_k2, diff_norm_g, w_out, ln_g, ln_b, w_up, conv_w, conv_b, w_down):
    bp, sp, d = x_prompt.shape
    bs, ss, _ = x_sample.shape
    groups = ((0, bp, sp), (bp * sp, bs, ss))
    x = jnp.concatenate([x_prompt.reshape(bp * sp, d), x_sample.reshape(bs * ss, d)], axis=0)
    y = _forward(x.astype(F32), groups, w_in, ret_decay_logit, rel_bias, lambda_q1, lambda_k1,
                 lambda_q2, lambda_k2, diff_norm_g, w_out, ln_g, ln_b, w_up, conv_w, conv_b, w_down)
    y_prompt = y[:bp * sp].reshape(bp, sp, d).astype(x_prompt.dtype)
    y_sample = y[bp * sp:].reshape(bs, ss, d).astype(x_sample.dtype)
    return y_prompt, y_sample
```

```python
import functools
import math

import jax
import jax.numpy as jnp
from jax import lax
from jax.experimental import pallas as pl
from jax.experimental.pallas import tpu as pltpu

D_MODEL = 1024
HEAD_DIM = 64
N_HEADS = 8
GROUP_WIDTH = N_HEADS * HEAD_DIM
DIFF_QK_DIM = HEAD_DIM // 2
D_FF = 2816
N_BUCKETS = 32
MAX_DISTANCE = 128
ROPE_BASE = 10000.0
LN_EPS = 1e-5
HEAD_NORM_EPS = 1e-6
LANES = 128
BF16_SUBLANES = 16
MXU_WIDTH = 256
FAR_DISTANCE = 91
BIAS_REACH = 2
MAX_UNROLLED_MACRO_TILES = 8
LOG2E = math.log2(math.e)
NEG_BIG = -1e30
UNDERFLOW_GUARD = 2.0 ** -90
VMEM_LIMIT = 56 * 1024 * 1024

F32 = jnp.float32
BF16 = jnp.bfloat16


def _cparams(*sem):
    return pltpu.CompilerParams(dimension_semantics=sem, vmem_limit_bytes=VMEM_LIMIT)


def _proj_kernel(x_ref, w_ref, wvt_ref, s_ref, cos_ref, sin_ref, ret_ref, gate_ref, dqk_ref, vt_ref):
    gw = GROUP_WIDTH
    x = x_ref[...]
    tm = x.shape[0]
    lane = lax.broadcasted_iota(jnp.int32, (tm, LANES), 1)
    low_half = (lane % HEAD_DIM) < (HEAD_DIM // 2)
    cos = cos_ref[...]
    sin = sin_ref[...]
    qk = jnp.dot(x, w_ref[:, :2 * gw], preferred_element_type=F32) * s_ref[:, :2 * gw]
    for p in range(2 * gw // LANES):
        sl = slice(p * LANES, (p + 1) * LANES)
        xx = qk[:, sl]
        swapped = jnp.where(low_half, pltpu.roll(xx, LANES - HEAD_DIM // 2, 1),
                            pltpu.roll(xx, HEAD_DIM // 2, 1))
        ret_ref[:, sl] = (xx * cos + swapped * sin).astype(ret_ref.dtype)
    ret_ref[:, 2 * gw:] = jnp.dot(x, w_ref[:, 2 * gw:3 * gw],
                                  preferred_element_type=F32).astype(ret_ref.dtype)
    gate_ref[...] = jnp.dot(x, w_ref[:, 3 * gw:4 * gw], preferred_element_type=F32)
    dqk_ref[...] = (jnp.dot(x, w_ref[:, 4 * gw:], preferred_element_type=F32)
                    * s_ref[:, 4 * gw:]).astype(dqk_ref.dtype)
    vt_ref[...] = lax.dot_general(wvt_ref[...], x, (((1,), (1,)), ((), ())),
                                  preferred_element_type=F32).astype(vt_ref.dtype)


def _position_block(i, tm, groups):
    t0 = i * tm
    blk = t0 // tm
    for (row0, _, seq) in groups:
        blk = jnp.where(t0 >= row0, ((t0 - row0) % seq) // tm, blk)
    return blk


def _project(xb, w, w_vt, scale, cos_t, sin_t, groups, tm):
    t, k = xb.shape
    gw = GROUP_WIDTH
    row = lambda i: (i, 0)
    fixed = lambda i: (0, 0)
    pos = lambda i: (_position_block(i, tm, groups), 0)
    return pl.pallas_call(
        _proj_kernel,
        grid=(t // tm,),
        in_specs=[pl.BlockSpec((tm, k), row),
                  pl.BlockSpec((k, 6 * gw), fixed, pipeline_mode=pl.Buffered(1)),
                  pl.BlockSpec((gw, k), fixed, pipeline_mode=pl.Buffered(1)),
                  pl.BlockSpec((1, 6 * gw), fixed),
                  pl.BlockSpec((tm, LANES), pos),
                  pl.BlockSpec((tm, LANES), pos)],
        out_specs=[pl.BlockSpec((tm, 3 * gw), row), pl.BlockSpec((tm, gw), row),
                   pl.BlockSpec((tm, 2 * gw), row),
                   pl.BlockSpec((None, gw, tm), lambda i: (i, 0, 0))],
        out_shape=[jax.ShapeDtypeStruct((t, 3 * gw), BF16), jax.ShapeDtypeStruct((t, gw), F32),
                   jax.ShapeDtypeStruct((t, 2 * gw), BF16),
                   jax.ShapeDtypeStruct((t // tm, gw, tm), BF16)],
        compiler_params=_cparams("parallel"),
        name="in_proj",
    )(xb, w, w_vt, scale, cos_t, sin_t)


def _ret_kernel(lg_ref, q_ref, k_ref, v_ref, g_ref, _, o_ref,
                rf_ref, rb_ref, rnext_ref, mask_ref, tab_ref, *, ns, per_step, chunk):
    t = pl.program_id(1)
    c = chunk
    n_pairs = GROUP_WIDTH // LANES
    lane = lax.broadcasted_iota(jnp.int32, (c, LANES), 1)
    head0 = lane < HEAD_DIM
    r_i = lax.broadcasted_iota(jnp.int32, (LANES, LANES), 0) // HEAD_DIM
    c_i = lax.broadcasted_iota(jnp.int32, (LANES, LANES), 1) // HEAD_DIM
    same_head = r_i == c_i

    @pl.when(t == 0)
    def _init():
        rb_ref[...] = jnp.zeros_like(rb_ref)
        qi = lax.broadcasted_iota(jnp.int32, (c, c), 0)
        ki = lax.broadcasted_iota(jnp.int32, (c, c), 1)
        diff = (qi - ki).astype(F32)
        pos = lax.broadcasted_iota(jnp.int32, (c, LANES), 0).astype(F32)
        for p in range(n_pairs):
            for hh in range(2):
                lf = lg_ref[0, 2 * p + hh]
                lb = lg_ref[1, 2 * p + hh]
                mask_ref[p, hh * c:(hh + 1) * c, :] = jnp.where(
                    diff >= 0, jnp.exp(lf * jnp.maximum(diff, 0.0)),
                    jnp.exp(lb * jnp.maximum(-diff, 0.0)))
            lfl = jnp.where(head0, lg_ref[0, 2 * p], lg_ref[0, 2 * p + 1])
            lbl = jnp.where(head0, lg_ref[1, 2 * p], lg_ref[1, 2 * p + 1])
            tab_ref[p, 0] = jnp.exp(lfl * (c - 1 - pos))
            tab_ref[p, 1] = jnp.exp(lbl * pos)
            tab_ref[p, 2] = jnp.exp(lfl * (pos + 1.0))
            tab_ref[p, 3] = jnp.exp(lbl * (c - pos))
            tab_ref[p, 4] = jnp.exp(lfl * c)
            tab_ref[p, 5] = jnp.exp(lbl * c)

    def summary(k, w, v):
        kw = (k.astype(F32) * w).astype(BF16)
        kv = lax.dot_general(kw, v, (((0,), (0,)), ((), ())), preferred_element_type=F32)
        return jnp.where(same_head, kv, 0.0)

    @pl.when(t < ns)
    def _backward():
        for u in reversed(range(per_step)):
            rows = slice(u * c, (u + 1) * c)
            ci = (ns - 1 - t) * per_step + u
            for p in range(n_pairs):
                sl = slice(p * LANES, (p + 1) * LANES)
                rnext_ref[ci, p] = rb_ref[p].astype(BF16)
                rb_ref[p] = (tab_ref[p, 5][:LANES] * rb_ref[p]
                             + summary(k_ref[rows, sl], tab_ref[p, 1], v_ref[rows, sl]))

    @pl.when(t >= ns)
    def _forward():
        @pl.when(t == ns)
        def _():
            rf_ref[...] = jnp.zeros_like(rf_ref)

        for u in range(per_step):
            rows = slice(u * c, (u + 1) * c)
            ci = (t - ns) * per_step + u
            for p in range(n_pairs):
                sl = slice(p * LANES, (p + 1) * LANES)
                q = q_ref[rows, sl]
                k = k_ref[rows, sl]
                v = v_ref[rows, sl]
                zero = jnp.zeros_like(q)
                q2 = jnp.concatenate([jnp.where(head0, q, zero), jnp.where(head0, zero, q)], axis=0)
                s = lax.dot_general(q2, k, (((1,), (1,)), ((), ())), preferred_element_type=F32)
                o2 = jnp.dot((s * mask_ref[p]).astype(BF16), v, preferred_element_type=F32)
                y = jnp.where(head0, o2[:c], o2[c:])
                y = y + jnp.dot(q, rf_ref[p].astype(BF16), preferred_element_type=F32) * tab_ref[p, 2]
                y = y + jnp.dot(q, rnext_ref[ci, p], preferred_element_type=F32) * tab_ref[p, 3]

                sq = y * y
                s0 = jnp.sum(jnp.where(head0, sq, 0.0), axis=1, keepdims=True)
                s1 = jnp.sum(jnp.where(head0, 0.0, sq), axis=1, keepdims=True)
                ms = jnp.where(head0, s0, s1) * (1.0 / HEAD_DIM)
                y = y * lax.rsqrt(ms + HEAD_NORM_EPS)
                g = g_ref[rows, sl]
                o_ref[rows, sl] = (g / (1.0 + jnp.exp(-g)) * y).astype(o_ref.dtype)

                rf_ref[p] = tab_ref[p, 4][:LANES] * rf_ref[p] + summary(k, tab_ref[p, 0], v)


def _retention(ret, gate, log_g, out, row0, batch, seq, chunk):
    nc = seq // chunk
    per_step = max(u for u in (1, 2, 4) if nc % u == 0)
    ns = nc // per_step
    step_rows = per_step * chunk
    blk0 = row0 // step_rows
    n_pairs = GROUP_WIDTH // LANES
    gw = GROUP_WIDTH

    def kc(t):
        return jnp.where(t < ns, ns - 1 - t, t - ns)

    def qc(t):
        return jnp.maximum(t - ns, 0)

    def rows(b, cc):
        return blk0 + b * ns + cc

    kernel = functools.partial(_ret_kernel, ns=ns, per_step=per_step, chunk=chunk)
    return pl.pallas_call(
        kernel,
        grid=(batch, 2 * ns),
        in_specs=[pl.BlockSpec(memory_space=pltpu.SMEM),
                  pl.BlockSpec((step_rows, gw), lambda b, t: (rows(b, qc(t)), 0)),
                  pl.BlockSpec((step_rows, gw), lambda b, t: (rows(b, kc(t)), 1)),
                  pl.BlockSpec((step_rows, gw), lambda b, t: (rows(b, kc(t)), 2)),
                  pl.BlockSpec((step_rows, gw), lambda b, t: (rows(b, qc(t)), 0)),
                  pl.BlockSpec(memory_space=pl.ANY)],
        out_specs=pl.BlockSpec((step_rows, gw), lambda b, t: (rows(b, qc(t)), 0)),
        out_shape=jax.ShapeDtypeStruct(out.shape, out.dtype),
        input_output_aliases={5: 0},
        scratch_shapes=[pltpu.VMEM((n_pairs, LANES, LANES), F32),
                        pltpu.VMEM((n_pairs, LANES, LANES), F32),
                        pltpu.VMEM((nc, n_pairs, LANES, LANES), BF16),
                        pltpu.VMEM((n_pairs, 2 * chunk, chunk), F32),
                        pltpu.VMEM((n_pairs, 6, chunk, LANES), F32)],
        compiler_params=_cparams("parallel", "arbitrary"),
        name="retention",
    )(log_g, ret, ret, ret, gate, out)


def _bias_kernel(rb_ref, bucket_ref, o_ref):
    h = pl.program_id(0)
    bk = bucket_ref[...]
    out = jnp.zeros(bk.shape, F32)
    for n in range(N_BUCKETS):
        out = jnp.where(bk == n, rb_ref[n, h], out)
    o_ref[...] = out * LOG2E


def _bias_tiles(rel_bias, tile):
    assert tile > FAR_DISTANCE
    a = jnp.arange(tile, dtype=jnp.int32)[:, None]
    b = jnp.arange(tile, dtype=jnp.int32)[None, :]
    rel = jnp.stack([(d * tile + a - b) for d in range(-BIAS_REACH, BIAS_REACH + 1)])
    nb = N_BUCKETS // 2
    max_exact = nb // 2
    n = jnp.abs(rel)
    nf = jnp.maximum(n, 1).astype(F32)
    large = max_exact + (jnp.log(nf / max_exact) / math.log(MAX_DISTANCE / max_exact)
                         * (nb - max_exact)).astype(jnp.int32)
    large = jnp.minimum(large, nb - 1)
    bucket = jnp.where(rel > 0, nb, 0) + jnp.where(n < max_exact, n, large)
    return pl.pallas_call(
        _bias_kernel,
        grid=(N_HEADS, 2 * BIAS_REACH + 1),
        in_specs=[pl.BlockSpec(memory_space=pltpu.SMEM),
                  pl.BlockSpec((None, tile, tile), lambda h, d: (d, 0, 0))],
        out_specs=pl.BlockSpec((None, None, tile, tile), lambda h, d: (h, d, 0, 0)),
        out_shape=jax.ShapeDtypeStruct((N_HEADS, 2 * BIAS_REACH + 1, tile, tile), F32),
        compiler_params=_cparams("parallel", "arbitrary"),
        name="t5_bias_tiles",
    )(rel_bias.astype(F32), bucket.astype(jnp.int32))


def _attn_kernel(rb_ref, lamp_ref, dng_ref, q_ref, k_ref, vt_ref, bt_ref, _, o_ref,
                 a_ref, l_ref, p_ref, qa_ref, kmax_ref, m_ref, *, n_tiles, tile, macro):
    h = pl.program_id(1)
    qi = pl.program_id(2)
    hh = h % 2
    n = n_tiles
    half = DIFF_QK_DIM
    group_of_lane = lax.broadcasted_iota(jnp.int32, (LANES, LANES), 0) // half
    group_sum = (group_of_lane == lax.broadcasted_iota(jnp.int32, (LANES, LANES), 1)).astype(BF16)

    def half_norms(x):
        xf = x.astype(F32)
        return jnp.dot((xf * xf).astype(BF16), group_sum, preferred_element_type=F32)

    @pl.when(qi == 0)
    def _key_norms():
        def body(j, mx):
            kt = k_ref[pl.ds(pl.multiple_of(j * tile, tile), tile), :]
            return jnp.maximum(mx, jnp.max(half_norms(kt), axis=0, keepdims=True))
        kmax_ref[...] = lax.fori_loop(0, n, body, jnp.zeros((1, LANES), F32))

    qf = q_ref[...].astype(F32)
    lane = lax.broadcasted_iota(jnp.int32, qf.shape, 1)
    lane_row = lax.broadcasted_iota(jnp.int32, (1, LANES), 1)
    row = lax.broadcasted_iota(jnp.int32, (LANES, tile), 0)
    group_sum_t = (lax.broadcasted_iota(jnp.int32, (LANES, LANES), 0)
                   == lax.broadcasted_iota(jnp.int32, (LANES, LANES), 1) // half).astype(BF16)
    qn_rows = lax.dot_general(group_sum_t, (qf * qf).astype(BF16), (((1,), (1,)), ((), ())),
                              preferred_element_type=F32)
    for t in range(2):
        g = 2 * hh + t
        lo_lane = g * half
        qzt = jnp.where((lane >= lo_lane) & (lane < lo_lane + half), qf, 0.0).T.astype(BF16)
        qn2 = jnp.sum(jnp.where(row == g, qn_rows, 0.0), axis=0, keepdims=True)
        kn2 = jnp.max(jnp.where(lane_row == g, kmax_ref[...], 0.0), axis=1, keepdims=True)
        v = -(jnp.sqrt(qn2 * kn2) + rb_ref[N_BUCKETS, h])
        hi = v.astype(BF16).astype(F32)
        lo = (v - hi).astype(BF16).astype(F32)
        qa_ref[t, :LANES, :] = qzt
        qa_ref[t, LANES:, :] = jnp.where(row == 0, hi, jnp.where(row == 1, lo, 0.0)).astype(BF16)

    ones = jnp.ones((macro * tile, LANES), BF16)

    def probabilities(m, slot):
        j0 = m * macro
        kaug = jnp.concatenate([k_ref[j0 * tile:(j0 + macro) * tile, :], ones], axis=1)
        for t in range(2):
            s = jnp.dot(kaug, qa_ref[t], preferred_element_type=F32)
            for r in range(macro):
                bias = bt_ref[jnp.clip(j0 + r - qi, -BIAS_REACH, BIAS_REACH) + BIAS_REACH]
                sr = s[r * tile:(r + 1) * tile] + bias
                e = jnp.exp2(sr)
                l_ref[t] += jnp.sum(e, axis=0, keepdims=True)
                p_ref[slot, t, r * tile:(r + 1) * tile, :] = e.astype(BF16)

    def accumulate(m, slot):
        vt = jnp.concatenate([vt_ref[m * macro + r] for r in range(macro)], axis=1)
        for t in range(2):
            a_ref[t] += jnp.dot(vt, p_ref[slot, t], preferred_element_type=F32)

    a_ref[...] = jnp.zeros_like(a_ref)
    l_ref[...] = jnp.zeros_like(l_ref)

    n_macro = n // macro
    probabilities(0, 0)
    for m in range(n_macro):
        if m + 1 < n_macro:
            probabilities(m + 1, (m + 1) % 2)
        accumulate(m, m % 2)

    trusted = jnp.min(jnp.minimum(l_ref[0], l_ref[1])) >= UNDERFLOW_GUARD

    @pl.when(jnp.logical_not(trusted))
    def _exact_running_max():
        m_ref[...] = jnp.full(m_ref.shape, NEG_BIG, F32)
        a_ref[...] = jnp.zeros_like(a_ref)
        l_ref[...] = jnp.zeros_like(l_ref)

        def body(j, carry):
            kt = k_ref[pl.ds(pl.multiple_of(j * tile, tile), tile), :]
            vt = vt_ref[j]
            bias = bt_ref[jnp.clip(j - qi, -BIAS_REACH, BIAS_REACH) + BIAS_REACH]
            for t in range(2):
                s = jnp.dot(kt, qa_ref[t, :LANES, :], preferred_element_type=F32) + bias
                m_prev = m_ref[t]
                m_new = jnp.maximum(m_prev, jnp.max(s, axis=0, keepdims=True))
                alpha = jnp.exp2(m_prev - m_new)
                e = jnp.exp2(s - m_new)
                l_ref[t] = alpha * l_ref[t] + jnp.sum(e, axis=0, keepdims=True)
                a_ref[t] = alpha * a_ref[t] + jnp.dot(vt, e.astype(BF16),
                                                      preferred_element_type=F32)
                m_ref[t] = m_new
            return carry

        lax.fori_loop(0, n, body, 0)

    lp = lamp_ref[...]
    lam_init = lp[4:5, 0:1]
    lam = (jnp.exp(jnp.sum(lp[0:1] * lp[1:2], axis=1, keepdims=True))
           - jnp.exp(jnp.sum(lp[2:3] * lp[3:4], axis=1, keepdims=True)) + lam_init)
    o = a_ref[0] / l_ref[0] - lam * (a_ref[1] / l_ref[1])
    ms = jnp.mean(o * o, axis=0, keepdims=True)
    y = o * lax.rsqrt(ms + HEAD_NORM_EPS) * dng_ref[...] * (1.0 - lam_init)
    o_ref[...] = y.astype(o_ref.dtype)


def _diff_attention(dqk, vt3, bias_t, rel_log2, lamp, dng, out, row0, batch, seq, tile):
    n = seq // tile
    macro = max(g for g in (1, 2, 4, 8) if n % g == 0)
    assert n // macro <= MAX_UNROLLED_MACRO_TILES
    n_pairs = GROUP_WIDTH // LANES
    qblk0 = row0 // tile
    sblk0 = row0 // seq
    n_bias = 2 * BIAS_REACH + 1
    kernel = functools.partial(_attn_kernel, n_tiles=n, tile=tile, macro=macro)
    return pl.pallas_call(
        kernel,
        grid=(batch, N_HEADS, n),
        in_specs=[pl.BlockSpec(memory_space=pltpu.SMEM),
                  pl.BlockSpec((5, DIFF_QK_DIM), lambda b, h, i: (0, 0)),
                  pl.BlockSpec((HEAD_DIM, 1), lambda b, h, i: (0, 0)),
                  pl.BlockSpec((tile, LANES), lambda b, h, i: (qblk0 + b * n + i, h // 2)),
                  pl.BlockSpec((seq, LANES), lambda b, h, i: (sblk0 + b, n_pairs + h // 2)),
                  pl.BlockSpec((n, HEAD_DIM, tile), lambda b, h, i: (sblk0 + b, h, 0)),
                  pl.BlockSpec((None, n_bias, tile, tile), lambda b, h, i: (h, 0, 0, 0)),
                  pl.BlockSpec(memory_space=pl.ANY)],
        out_specs=pl.BlockSpec((None, HEAD_DIM, tile), lambda b, h, i: (qblk0 + b * n + i, h, 0)),
        out_shape=jax.ShapeDtypeStruct(out.shape, out.dtype),
        input_output_aliases={7: 0},
        scratch_shapes=[pltpu.VMEM((2, HEAD_DIM, tile), F32),
                        pltpu.VMEM((2, 1, tile), F32),
                        pltpu.VMEM((2, 2, macro * tile, tile), BF16),
                        pltpu.VMEM((2, 2 * LANES, tile), BF16),
                        pltpu.VMEM((1, LANES), F32),
                        pltpu.VMEM((2, 1, tile), F32)],
        compiler_params=_cparams("parallel", "parallel", "arbitrary"),
        name="diff_attention",
    )(rel_log2, lamp, dng, dqk, dqk, vt3, bias_t, out)


def _layer_norm(y, g, b):
    mu = jnp.mean(y, axis=-1, keepdims=True)
    d = y - mu
    var = jnp.mean(d * d, axis=-1, keepdims=True)
    return d * lax.rsqrt(var + LN_EPS) * g + b


def _outproj_kernel(yr_ref, yd_ref, w_ref, x_ref, g_ref, b_ref, o_ref, ob_ref, *, alpha):
    mix = jnp.dot(yr_ref[...], w_ref[:GROUP_WIDTH, :], preferred_element_type=F32)
    w_d = w_ref[GROUP_WIDTH:, :]
    mix = mix + jnp.concatenate(
        [lax.dot_general(yd_ref[u], w_d, (((0,), (0,)), ((), ())), preferred_element_type=F32)
         for u in range(yd_ref.shape[0])], axis=0)
    y = _layer_norm(alpha * x_ref[...] + mix, g_ref[...], b_ref[...])
    o_ref[...] = y
    ob_ref[...] = y.astype(BF16)


def _out_projection(yr, ydt, w_out, x, g, b, alpha, tm):
    t = x.shape[0]
    tile = ydt.shape[2]
    assert ydt.shape == (t // tile, GROUP_WIDTH, tile) and tm % tile == 0
    kernel = functools.partial(_outproj_kernel, alpha=alpha)
    row = lambda i: (i, 0)
    fixed = lambda i: (0, 0)
    return pl.pallas_call(
        kernel,
        grid=(t // tm,),
        in_specs=[pl.BlockSpec((tm, GROUP_WIDTH), row),
                  pl.BlockSpec((tm // tile, GROUP_WIDTH, tile), lambda i: (i, 0, 0)),
                  pl.BlockSpec((2 * GROUP_WIDTH, D_MODEL), fixed, pipeline_mode=pl.Buffered(1)),
                  pl.BlockSpec((tm, D_MODEL), row),
                  pl.BlockSpec((1, D_MODEL), fixed),
                  pl.BlockSpec((1, D_MODEL), fixed)],
        out_specs=[pl.BlockSpec((tm, D_MODEL), row), pl.BlockSpec((tm, D_MODEL), row)],
        out_shape=[jax.ShapeDtypeStruct((t, D_MODEL), F32), jax.ShapeDtypeStruct((t, D_MODEL), BF16)],
        compiler_params=_cparams("parallel"),
        name="out_proj_ln",
    )(yr, ydt, w_out, x, g, b)


def _ffn_kernel(x_ref, xb_ref, xp_ref, xn_ref, wa_ref, wv_ref, wd_ref, cp_ref, g_ref, b_ref,
                o_ref, ob_ref, acc_ref, *, alpha, tm, n_chunks, starts, ends):
    i = pl.program_id(0)
    t0 = i * tm
    is_start = functools.reduce(jnp.logical_or, [t0 == s for s in starts])
    is_end = functools.reduce(jnp.logical_or, [t0 + tm == e for e in ends])
    keep_prev = jnp.where(is_start, 0.0, 1.0)
    keep_next = jnp.where(is_end, 0.0, 1.0)

    xb = xb_ref[...]
    halo = jnp.concatenate([xp_ref[...], xn_ref[...]], axis=0)
    hrows = xp_ref.shape[0]
    acc_ref[...] = jnp.zeros_like(acc_ref)

    def chunk_body(c, carry):
        wa = wa_ref[c]
        a = jnp.dot(xb, wa, preferred_element_type=F32)
        val = jnp.dot(xb, wv_ref[c], preferred_element_type=F32)
        ah = jnp.dot(halo, wa, preferred_element_type=F32)
        prev_row = ah[hrows - 1:hrows] * keep_prev
        next_row = ah[hrows:hrows + 1] * keep_next
        row = lax.broadcasted_iota(jnp.int32, a.shape, 0)
        a_m1 = jnp.where(row == 0, prev_row, pltpu.roll(a, 1, 0))
        a_p1 = jnp.where(row == tm - 1, next_row, pltpu.roll(a, tm - 1, 0))
        cp = cp_ref[c]
        conv = cp[3:4] + a_m1 * cp[0:1]
        conv = conv + a * cp[1:2]
        conv = conv + a_p1 * cp[2:3]
        gelu = 0.5 * conv * (1.0 + lax.erf(conv * (1.0 / math.sqrt(2.0))))
        hidden = (gelu * val).astype(BF16)
        acc_ref[...] += jnp.dot(hidden, wd_ref[c], preferred_element_type=F32)
        return carry

    for c in range(n_chunks):
        chunk_body(c, 0)
    y = _layer_norm(alpha * x_ref[...] + acc_ref[...], g_ref[...], b_ref[...])
    o_ref[...] = y
    ob_ref[...] = y.astype(BF16)


def _conv_glu(x, xb, wa, wv, wd, cp, g, b, alpha, tm, groups):
    t = x.shape[0]
    n_chunks, _, ck = wa.shape
    hrows = BF16_SUBLANES
    starts = tuple(r0 + bi * s for (r0, nb, s) in groups for bi in range(nb))
    ends = tuple(r0 + (bi + 1) * s for (r0, nb, s) in groups for bi in range(nb))
    kernel = functools.partial(_ffn_kernel, alpha=alpha, tm=tm, n_chunks=n_chunks,
                               starts=starts, ends=ends)
    row = lambda i: (i, 0)
    fixed2 = lambda i: (0, 0)
    fixed3 = lambda i: (0, 0, 0)
    per = tm // hrows
    last = t // hrows - 1
    return pl.pallas_call(
        kernel,
        grid=(t // tm,),
        in_specs=[pl.BlockSpec((tm, D_MODEL), row),
                  pl.BlockSpec((tm, D_MODEL), row),
                  pl.BlockSpec((hrows, D_MODEL), lambda i: (jnp.maximum(i * per - 1, 0), 0)),
                  pl.BlockSpec((hrows, D_MODEL), lambda i: (jnp.minimum((i + 1) * per, last), 0)),
                  pl.BlockSpec((n_chunks, D_MODEL, ck), fixed3, pipeline_mode=pl.Buffered(1)),
                  pl.BlockSpec((n_chunks, D_MODEL, ck), fixed3, pipeline_mode=pl.Buffered(1)),
                  pl.BlockSpec((n_chunks, ck, D_MODEL), fixed3, pipeline_mode=pl.Buffered(1)),
                  pl.BlockSpec((n_chunks, 8, ck), fixed3),
                  pl.BlockSpec((1, D_MODEL), fixed2),
                  pl.BlockSpec((1, D_MODEL), fixed2)],
        out_specs=[pl.BlockSpec((tm, D_MODEL), row), pl.BlockSpec((tm, D_MODEL), row)],
        out_shape=[jax.ShapeDtypeStruct((t, D_MODEL), F32), jax.ShapeDtypeStruct((t, D_MODEL), BF16)],
        scratch_shapes=[pltpu.VMEM((tm, D_MODEL), F32)],
        compiler_params=_cparams("parallel"),
        name="conv_glu_ln",
    )(x, xb, xb, xb, wa, wv, wd, cp, g, b)


def _rotary_tables(seq):
    d = HEAD_DIM
    inv = 1.0 / (ROPE_BASE ** (jnp.arange(0, d, 2, dtype=F32) / d))
    ang = jnp.arange(seq, dtype=F32)[:, None] * inv[None, :]
    cos, sin = jnp.cos(ang), jnp.sin(ang)
    cos_t = jnp.concatenate([cos, cos, cos, cos], axis=-1)
    sin_t = jnp.concatenate([-sin, sin, -sin, sin], axis=-1)
    return cos_t, sin_t


def _tiles(groups):
    smin = min(s for (_, _, s) in groups)
    attn_tile = min(512, smin // 2)
    chunk = min(256, smin // 2)
    tm = min(512, smin // 2)
    ffn_tm = min(1024, smin // 2)
    return attn_tile, chunk, tm, ffn_tm


def _forward(x, groups, w_in, ret_decay_logit, rel_bias, lambda_q1, lambda_k1, lambda_q2,
             lambda_k2, diff_norm_g, w_out, ln_g, ln_b, w_up, conv_w, conv_b, w_down):
    depth = w_in.shape[0]
    alpha = (2 * depth) ** 0.25
    t = x.shape[0]
    attn_tile, chunk, tm, ffn_tm = _tiles(groups)
    smax = max(s for (_, _, s) in groups)
    gw = GROUP_WIDTH
    ck = MXU_WIDTH
    n_chunks = D_FF // ck

    cos_t, sin_t = _rotary_tables(smax)
    bias_t = _bias_tiles(rel_bias, attn_tile)
    rel_log2 = rel_bias.astype(F32) * LOG2E
    rel_log2 = jnp.concatenate([rel_log2, jnp.max(rel_log2, axis=0, keepdims=True)], axis=0)
    in_scale = jnp.concatenate([
        jnp.ones((gw,), F32), jnp.full((gw,), HEAD_DIM ** -0.5, F32), jnp.ones((2 * gw,), F32),
        jnp.full((gw,), DIFF_QK_DIM ** -0.5 * LOG2E, F32), jnp.ones((gw,), F32)])[None, :]
    assert tm == attn_tile

    xb = x.astype(BF16)
    for l in range(depth):
        lam_init = 0.8 - 0.6 * math.exp(-0.3 * l)
        w_in_b = w_in[l].astype(BF16)
        ret, gate, dqk, vt3 = _project(xb, w_in_b[:, :6 * gw], w_in_b[:, 6 * gw:].T, in_scale,
                                       cos_t, sin_t, groups, tm)

        log_g = jax.nn.log_sigmoid(ret_decay_logit[l].astype(F32))
        lamp = jnp.stack([lambda_q1[l], lambda_k1[l], lambda_q2[l], lambda_k2[l],
                          jnp.full((DIFF_QK_DIM,), lam_init)]).astype(F32)
        dng = diff_norm_g[l].astype(F32)[:, None]

        yr = jnp.zeros((t, gw), BF16)
        ydt = jnp.zeros((t // attn_tile, gw, attn_tile), BF16)
        for (row0, batch, seq) in groups:
            yr = _retention(ret, gate, log_g, yr, row0, batch, seq, chunk)
            ydt = _diff_attention(dqk, vt3, bias_t, rel_log2, lamp, dng, ydt, row0, batch, seq,
                                  attn_tile)

        x, xb = _out_projection(yr, ydt, w_out[l].astype(BF16), x, ln_g[l, 0][None, :].astype(F32),
                                ln_b[l, 0][None, :].astype(F32), alpha, ffn_tm)

        wa = w_up[l][:, :D_FF].astype(BF16).reshape(D_MODEL, n_chunks, ck).transpose(1, 0, 2)
        wv = w_up[l][:, D_FF:].astype(BF16).reshape(D_MODEL, n_chunks, ck).transpose(1, 0, 2)
        wd = w_down[l].astype(BF16).reshape(n_chunks, ck, D_MODEL)
        cp = jnp.concatenate([conv_w[l].astype(F32), conv_b[l].astype(F32)[None, :],
                              jnp.zeros((4, D_FF), F32)], axis=0)
        cp = cp.reshape(8, n_chunks, ck).transpose(1, 0, 2)
        x, xb = _conv_glu(x, xb, wa, wv, wd, cp, ln_g[l, 1][None, :].astype(F32),
                          ln_b[l, 1][None, :].astype(F32), alpha, ffn_tm, groups)
    return x


def kernel(x_prompt, x_sample, w_in, ret_decay_logit, rel_bias, lambda_q1, lambda_k1, lambda_q2,
           lambda_k2, diff_norm_g, w_out, ln_g, ln_b, w_up, conv_w, conv_b, w_down):
    bp, sp, d = x_prompt.shape
    bs, ss, _ = x_sample.shape
    groups = ((0, bp, sp), (bp * sp, bs, ss))
    x = jnp.concatenate([x_prompt.reshape(bp * sp, d), x_sample.reshape(bs * ss, d)], axis=0)
    y = _forward(x.astype(F32), groups, w_in, ret_decay_logit, rel_bias, lambda_q1, lambda_k1,
                 lambda_q2, lambda_k2, diff_norm_g, w_out, ln_g, ln_b, w_up, conv_w, conv_b, w_down)
    y_prompt = y[:bp * sp].reshape(bp, sp, d).astype(x_prompt.dtype)
    y_sample = y[bp * sp:].reshape(bs, ss, d).astype(x_sample.dtype)
    return y_prompt, y_sample
```

```python
import functools
import math

import jax
import jax.numpy as jnp
from jax import lax
from jax.experimental import pallas as pl
from jax.experimental.pallas import tpu as pltpu

D_MODEL = 1024
HEAD_DIM = 64
N_HEADS = 8
GROUP_WIDTH = N_HEADS * HEAD_DIM
DIFF_QK_DIM = HEAD_DIM // 2
D_FF = 2816
N_BUCKETS = 32
MAX_DISTANCE = 128
ROPE_BASE = 10000.0
LN_EPS = 1e-5
HEAD_NORM_EPS = 1e-6
LANES = 128
BF16_SUBLANES = 16
MXU_WIDTH = 256
FAR_DISTANCE = 91
BIAS_REACH = 2
MAX_UNROLLED_MACRO_TILES = 8
LOG2E = math.log2(math.e)
NEG_BIG = -1e30
UNDERFLOW_GUARD = 2.0 ** -90
VMEM_LIMIT = 56 * 1024 * 1024

F32 = jnp.float32
BF16 = jnp.bfloat16


def _cparams(*sem):
    return pltpu.CompilerParams(dimension_semantics=sem, vmem_limit_bytes=VMEM_LIMIT)


def _proj_kernel(x_ref, w_ref, wvt_ref, s_ref, cos_ref, sin_ref, ret_ref, gate_ref, dqk_ref, vt_ref):
    gw = GROUP_WIDTH
    x = x_ref[...]
    tm = x.shape[0]
    lane = lax.broadcasted_iota(jnp.int32, (tm, LANES), 1)
    low_half = (lane % HEAD_DIM) < (HEAD_DIM // 2)
    cos = cos_ref[...]
    sin = sin_ref[...]
    qk = jnp.dot(x, w_ref[:, :2 * gw], preferred_element_type=F32) * s_ref[:, :2 * gw]
    for p in range(2 * gw // LANES):
        sl = slice(p * LANES, (p + 1) * LANES)
        xx = qk[:, sl]
        swapped = jnp.where(low_half, pltpu.roll(xx, LANES - HEAD_DIM // 2, 1),
                            pltpu.roll(xx, HEAD_DIM // 2, 1))
        ret_ref[:, sl] = (xx * cos + swapped * sin).astype(ret_ref.dtype)
    ret_ref[:, 2 * gw:] = jnp.dot(x, w_ref[:, 2 * gw:3 * gw],
                                  preferred_element_type=F32).astype(ret_ref.dtype)
    gate_ref[...] = jnp.dot(x, w_ref[:, 3 * gw:4 * gw], preferred_element_type=F32)
    dqk_ref[...] = (jnp.dot(x, w_ref[:, 4 * gw:], preferred_element_type=F32)
                    * s_ref[:, 4 * gw:]).astype(dqk_ref.dtype)
    tile = vt_ref.shape[2]
    for u in range(vt_ref.shape[0]):
        vt_ref[u] = lax.dot_general(wvt_ref[...], x[u * tile:(u + 1) * tile], (((1,), (1,)), ((), ())),
                                    preferred_element_type=F32).astype(vt_ref.dtype)


def _position_block(i, tm, groups):
    t0 = i * tm
    blk = t0 // tm
    for (row0, _, seq) in groups:
        blk = jnp.where(t0 >= row0, ((t0 - row0) % seq) // tm, blk)
    return blk


def _project(xb, w, w_vt, scale, cos_t, sin_t, groups, tm, tile):
    t, k = xb.shape
    gw = GROUP_WIDTH
    row = lambda i: (i, 0)
    fixed = lambda i: (0, 0)
    pos = lambda i: (_position_block(i, tm, groups), 0)
    return pl.pallas_call(
        _proj_kernel,
        grid=(t // tm,),
        in_specs=[pl.BlockSpec((tm, k), row),
                  pl.BlockSpec((k, 6 * gw), fixed, pipeline_mode=pl.Buffered(1)),
                  pl.BlockSpec((gw, k), fixed, pipeline_mode=pl.Buffered(1)),
                  pl.BlockSpec((1, 6 * gw), fixed),
                  pl.BlockSpec((tm, LANES), pos),
                  pl.BlockSpec((tm, LANES), pos)],
        out_specs=[pl.BlockSpec((tm, 3 * gw), row), pl.BlockSpec((tm, gw), row),
                   pl.BlockSpec((tm, 2 * gw), row),
                   pl.BlockSpec((tm // tile, gw, tile), lambda i: (i, 0, 0))],
        out_shape=[jax.ShapeDtypeStruct((t, 3 * gw), BF16), jax.ShapeDtypeStruct((t, gw), F32),
                   jax.ShapeDtypeStruct((t, 2 * gw), BF16),
                   jax.ShapeDtypeStruct((t // tile, gw, tile), BF16)],
        compiler_params=_cparams("parallel"),
        name="in_proj",
    )(xb, w, w_vt, scale, cos_t, sin_t)


def _ret_kernel(lg_ref, q_ref, k_ref, v_ref, g_ref, _, o_ref,
                rf_ref, rb_ref, rnext_ref, mask_ref, tab_ref, *, ns, per_step, chunk):
    t = pl.program_id(1)
    c = chunk
    n_pairs = GROUP_WIDTH // LANES
    lane = lax.broadcasted_iota(jnp.int32, (c, LANES), 1)
    head0 = lane < HEAD_DIM
    r_i = lax.broadcasted_iota(jnp.int32, (LANES, LANES), 0) // HEAD_DIM
    c_i = lax.broadcasted_iota(jnp.int32, (LANES, LANES), 1) // HEAD_DIM
    same_head = r_i == c_i

    @pl.when(t == 0)
    def _init():
        rb_ref[...] = jnp.zeros_like(rb_ref)
        qi = lax.broadcasted_iota(jnp.int32, (c, c), 0)
        ki = lax.broadcasted_iota(jnp.int32, (c, c), 1)
        diff = (qi - ki).astype(F32)
        pos = lax.broadcasted_iota(jnp.int32, (c, LANES), 0).astype(F32)
        for p in range(n_pairs):
            for hh in range(2):
                lf = lg_ref[0, 2 * p + hh]
                lb = lg_ref[1, 2 * p + hh]
                mask_ref[p, hh * c:(hh + 1) * c, :] = jnp.where(
                    diff >= 0, jnp.exp(lf * jnp.maximum(diff, 0.0)),
                    jnp.exp(lb * jnp.maximum(-diff, 0.0)))
            lfl = jnp.where(head0, lg_ref[0, 2 * p], lg_ref[0, 2 * p + 1])
            lbl = jnp.where(head0, lg_ref[1, 2 * p], lg_ref[1, 2 * p + 1])
            tab_ref[p, 0] = jnp.exp(lfl * (c - 1 - pos))
            tab_ref[p, 1] = jnp.exp(lbl * pos)
            tab_ref[p, 2] = jnp.exp(lfl * (pos + 1.0))
            tab_ref[p, 3] = jnp.exp(lbl * (c - pos))
            tab_ref[p, 4] = jnp.exp(lfl * c)
            tab_ref[p, 5] = jnp.exp(lbl * c)

    def summary(k, w, v):
        kw = (k.astype(F32) * w).astype(BF16)
        kv = lax.dot_general(kw, v, (((0,), (0,)), ((), ())), preferred_element_type=F32)
        return jnp.where(same_head, kv, 0.0)

    @pl.when(t < ns)
    def _backward():
        for u in reversed(range(per_step)):
            rows = slice(u * c, (u + 1) * c)
            ci = (ns - 1 - t) * per_step + u
            for p in range(n_pairs):
                sl = slice(p * LANES, (p + 1) * LANES)
                rnext_ref[ci, p] = rb_ref[p].astype(BF16)
                rb_ref[p] = (tab_ref[p, 5][:LANES] * rb_ref[p]
                             + summary(k_ref[rows, sl], tab_ref[p, 1], v_ref[rows, sl]))

    @pl.when(t >= ns)
    def _forward():
        @pl.when(t == ns)
        def _():
            rf_ref[...] = jnp.zeros_like(rf_ref)

        for u in range(per_step):
            rows = slice(u * c, (u + 1) * c)
            ci = (t - ns) * per_step + u
            for p in range(n_pairs):
                sl = slice(p * LANES, (p + 1) * LANES)
                q = q_ref[rows, sl]
                k = k_ref[rows, sl]
                v = v_ref[rows, sl]
                zero = jnp.zeros_like(q)
                q2 = jnp.concatenate([jnp.where(head0, q, zero), jnp.where(head0, zero, q)], axis=0)
                s = lax.dot_general(q2, k, (((1,), (1,)), ((), ())), preferred_element_type=F32)
                o2 = jnp.dot((s * mask_ref[p]).astype(BF16), v, preferred_element_type=F32)
                y = jnp.where(head0, o2[:c], o2[c:])
                y = y + jnp.dot(q, rf_ref[p].astype(BF16), preferred_element_type=F32) * tab_ref[p, 2]
                y = y + jnp.dot(q, rnext_ref[ci, p], preferred_element_type=F32) * tab_ref[p, 3]

                sq = y * y
                s0 = jnp.sum(jnp.where(head0, sq, 0.0), axis=1, keepdims=True)
                s1 = jnp.sum(jnp.where(head0, 0.0, sq), axis=1, keepdims=True)
                ms = jnp.where(head0, s0, s1) * (1.0 / HEAD_DIM)
                y = y * lax.rsqrt(ms + HEAD_NORM_EPS)
                g = g_ref[rows, sl]
                o_ref[rows, sl] = (g / (1.0 + jnp.exp(-g)) * y).astype(o_ref.dtype)

                rf_ref[p] = tab_ref[p, 4][:LANES] * rf_ref[p] + summary(k, tab_ref[p, 0], v)


def _retention(ret, gate, log_g, out, row0, batch, seq, chunk):
    nc = seq // chunk
    per_step = max(u for u in (1, 2, 4) if nc % u == 0)
    ns = nc // per_step
    step_rows = per_step * chunk
    blk0 = row0 // step_rows
    n_pairs = GROUP_WIDTH // LANES
    gw = GROUP_WIDTH

    def kc(t):
        return jnp.where(t < ns, ns - 1 - t, t - ns)

    def qc(t):
        return jnp.maximum(t - ns, 0)

    def rows(b, cc):
        return blk0 + b * ns + cc

    kernel = functools.partial(_ret_kernel, ns=ns, per_step=per_step, chunk=chunk)
    return pl.pallas_call(
        kernel,
        grid=(batch, 2 * ns),
        in_specs=[pl.BlockSpec(memory_space=pltpu.SMEM),
                  pl.BlockSpec((step_rows, gw), lambda b, t: (rows(b, qc(t)), 0)),
                  pl.BlockSpec((step_rows, gw), lambda b, t: (rows(b, kc(t)), 1)),
                  pl.BlockSpec((step_rows, gw), lambda b, t: (rows(b, kc(t)), 2)),
                  pl.BlockSpec((step_rows, gw), lambda b, t: (rows(b, qc(t)), 0)),
                  pl.BlockSpec(memory_space=pl.ANY)],
        out_specs=pl.BlockSpec((step_rows, gw), lambda b, t: (rows(b, qc(t)), 0)),
        out_shape=jax.ShapeDtypeStruct(out.shape, out.dtype),
        input_output_aliases={5: 0},
        scratch_shapes=[pltpu.VMEM((n_pairs, LANES, LANES), F32),
                        pltpu.VMEM((n_pairs, LANES, LANES), F32),
                        pltpu.VMEM((nc, n_pairs, LANES, LANES), BF16),
                        pltpu.VMEM((n_pairs, 2 * chunk, chunk), F32),
                        pltpu.VMEM((n_pairs, 6, chunk, LANES), F32)],
        compiler_params=_cparams("parallel", "arbitrary"),
        name="retention",
    )(log_g, ret, ret, ret, gate, out)


def _bias_kernel(rb_ref, bucket_ref, o_ref):
    h = pl.program_id(0)
    bk = bucket_ref[...]
    out = jnp.zeros(bk.shape, F32)
    for n in range(N_BUCKETS):
        out = jnp.where(bk == n, rb_ref[n, h], out)
    o_ref[...] = out * LOG2E


def _bias_tiles(rel_bias, tile):
    assert tile > FAR_DISTANCE
    a = jnp.arange(tile, dtype=jnp.int32)[:, None]
    b = jnp.arange(tile, dtype=jnp.int32)[None, :]
    rel = jnp.stack([(d * tile + a - b) for d in range(-BIAS_REACH, BIAS_REACH + 1)])
    nb = N_BUCKETS // 2
    max_exact = nb // 2
    n = jnp.abs(rel)
    nf = jnp.maximum(n, 1).astype(F32)
    large = max_exact + (jnp.log(nf / max_exact) / math.log(MAX_DISTANCE / max_exact)
                         * (nb - max_exact)).astype(jnp.int32)
    large = jnp.minimum(large, nb - 1)
    bucket = jnp.where(rel > 0, nb, 0) + jnp.where(n < max_exact, n, large)
    return pl.pallas_call(
        _bias_kernel,
        grid=(N_HEADS, 2 * BIAS_REACH + 1),
        in_specs=[pl.BlockSpec(memory_space=pltpu.SMEM),
                  pl.BlockSpec((None, tile, tile), lambda h, d: (d, 0, 0))],
        out_specs=pl.BlockSpec((None, None, tile, tile), lambda h, d: (h, d, 0, 0)),
        out_shape=jax.ShapeDtypeStruct((N_HEADS, 2 * BIAS_REACH + 1, tile, tile), F32),
        compiler_params=_cparams("parallel", "arbitrary"),
        name="t5_bias_tiles",
    )(rel_bias.astype(F32), bucket.astype(jnp.int32))


def _attn_kernel(rb_ref, lamp_ref, dng_ref, q_ref, k_ref, vt_ref, bt_ref, _, o_ref,
                 a_ref, l_ref, p_ref, qa_ref, kmax_ref, m_ref, *, n_tiles, tile, macro):
    h = pl.program_id(1)
    qi = pl.program_id(2)
    hh = h % 2
    n = n_tiles
    half = DIFF_QK_DIM
    group_of_lane = lax.broadcasted_iota(jnp.int32, (LANES, LANES), 0) // half
    group_sum = (group_of_lane == lax.broadcasted_iota(jnp.int32, (LANES, LANES), 1)).astype(BF16)

    def half_norms(x):
        xf = x.astype(F32)
        return jnp.dot((xf * xf).astype(BF16), group_sum, preferred_element_type=F32)

    @pl.when(qi == 0)
    def _key_norms():
        def body(j, mx):
            kt = k_ref[pl.ds(pl.multiple_of(j * tile, tile), tile), :]
            return jnp.maximum(mx, jnp.max(half_norms(kt), axis=0, keepdims=True))
        kmax_ref[...] = lax.fori_loop(0, n, body, jnp.zeros((1, LANES), F32))

    qf = q_ref[...].astype(F32)
    lane = lax.broadcasted_iota(jnp.int32, qf.shape, 1)
    lane_row = lax.broadcasted_iota(jnp.int32, (1, LANES), 1)
    row = lax.broadcasted_iota(jnp.int32, (LANES, tile), 0)
    group_sum_t = (lax.broadcasted_iota(jnp.int32, (LANES, LANES), 0)
                   == lax.broadcasted_iota(jnp.int32, (LANES, LANES), 1) // half).astype(BF16)
    qn_rows = lax.dot_general(group_sum_t, (qf * qf).astype(BF16), (((1,), (1,)), ((), ())),
                              preferred_element_type=F32)
    for t in range(2):
        g = 2 * hh + t
        lo_lane = g * half
        qzt = jnp.where((lane >= lo_lane) & (lane < lo_lane + half), qf, 0.0).T.astype(BF16)
        qn2 = jnp.sum(jnp.where(row == g, qn_rows, 0.0), axis=0, keepdims=True)
        kn2 = jnp.max(jnp.where(lane_row == g, kmax_ref[...], 0.0), axis=1, keepdims=True)
        v = -(jnp.sqrt(qn2 * kn2) + rb_ref[N_BUCKETS, h])
        hi = v.astype(BF16).astype(F32)
        lo = (v - hi).astype(BF16).astype(F32)
        qa_ref[t, :LANES, :] = qzt
        qa_ref[t, LANES:, :] = jnp.where(row == 0, hi, jnp.where(row == 1, lo, 0.0)).astype(BF16)

    ones = jnp.ones((macro * tile, LANES), BF16)

    def probabilities(m, slot):
        j0 = m * macro
        kaug = jnp.concatenate([k_ref[j0 * tile:(j0 + macro) * tile, :], ones], axis=1)
        for t in range(2):
            s = jnp.dot(kaug, qa_ref[t], preferred_element_type=F32)
            for r in range(macro):
                bias = bt_ref[jnp.clip(j0 + r - qi, -BIAS_REACH, BIAS_REACH) + BIAS_REACH]
                sr = s[r * tile:(r + 1) * tile] + bias
                e = jnp.exp2(sr)
                l_ref[t] += jnp.sum(e, axis=0, keepdims=True)
                p_ref[slot, t, r * tile:(r + 1) * tile, :] = e.astype(BF16)

    def accumulate(m, slot):
        vt = jnp.concatenate([vt_ref[m * macro + r] for r in range(macro)], axis=1)
        for t in range(2):
            a_ref[t] += jnp.dot(vt, p_ref[slot, t], preferred_element_type=F32)

    a_ref[...] = jnp.zeros_like(a_ref)
    l_ref[...] = jnp.zeros_like(l_ref)

    n_macro = n // macro
    probabilities(0, 0)
    for m in range(n_macro):
        if m + 1 < n_macro:
            probabilities(m + 1, (m + 1) % 2)
        accumulate(m, m % 2)

    trusted = jnp.min(jnp.minimum(l_ref[0], l_ref[1])) >= UNDERFLOW_GUARD

    @pl.when(jnp.logical_not(trusted))
    def _exact_running_max():
        m_ref[...] = jnp.full(m_ref.shape, NEG_BIG, F32)
        a_ref[...] = jnp.zeros_like(a_ref)
        l_ref[...] = jnp.zeros_like(l_ref)

        def body(j, carry):
            kt = k_ref[pl.ds(pl.multiple_of(j * tile, tile), tile), :]
            vt = vt_ref[j]
            bias = bt_ref[jnp.clip(j - qi, -BIAS_REACH, BIAS_REACH) + BIAS_REACH]
            for t in range(2):
                s = jnp.dot(kt, qa_ref[t, :LANES, :], preferred_element_type=F32) + bias
                m_prev = m_ref[t]
                m_new = jnp.maximum(m_prev, jnp.max(s, axis=0, keepdims=True))
                alpha = jnp.exp2(m_prev - m_new)
                e = jnp.exp2(s - m_new)
                l_ref[t] = alpha * l_ref[t] + jnp.sum(e, axis=0, keepdims=True)
                a_ref[t] = alpha * a_ref[t] + jnp.dot(vt, e.astype(BF16),
                                                      preferred_element_type=F32)
                m_ref[t] = m_new
            return carry

        lax.fori_loop(0, n, body, 0)

    lp = lamp_ref[...]
    lam_init = lp[4:5, 0:1]
    lam = (jnp.exp(jnp.sum(lp[0:1] * lp[1:2], axis=1, keepdims=True))
           - jnp.exp(jnp.sum(lp[2:3] * lp[3:4], axis=1, keepdims=True)) + lam_init)
    o = a_ref[0] / l_ref[0] - lam * (a_ref[1] / l_ref[1])
    ms = jnp.mean(o * o, axis=0, keepdims=True)
    y = o * lax.rsqrt(ms + HEAD_NORM_EPS) * dng_ref[...] * (1.0 - lam_init)
    o_ref[...] = y.astype(o_ref.dtype)


def _diff_attention(dqk, vt3, bias_t, rel_log2, lamp, dng, out, row0, batch, seq, tile):
    n = seq // tile
    macro = max(g for g in (1, 2, 4, 8) if n % g == 0)
    assert n // macro <= MAX_UNROLLED_MACRO_TILES
    n_pairs = GROUP_WIDTH // LANES
    qblk0 = row0 // tile
    sblk0 = row0 // seq
    n_bias = 2 * BIAS_REACH + 1
    kernel = functools.partial(_attn_kernel, n_tiles=n, tile=tile, macro=macro)
    return pl.pallas_call(
        kernel,
        grid=(batch, N_HEADS, n),
        in_specs=[pl.BlockSpec(memory_space=pltpu.SMEM),
                  pl.BlockSpec((5, DIFF_QK_DIM), lambda b, h, i: (0, 0)),
                  pl.BlockSpec((HEAD_DIM, 1), lambda b, h, i: (0, 0)),
                  pl.BlockSpec((tile, LANES), lambda b, h, i: (qblk0 + b * n + i, h // 2)),
                  pl.BlockSpec((seq, LANES), lambda b, h, i: (sblk0 + b, n_pairs + h // 2)),
                  pl.BlockSpec((n, HEAD_DIM, tile), lambda b, h, i: (sblk0 + b, h, 0)),
                  pl.BlockSpec((None, n_bias, tile, tile), lambda b, h, i: (h, 0, 0, 0)),
                  pl.BlockSpec(memory_space=pl.ANY)],
        out_specs=pl.BlockSpec((None, HEAD_DIM, tile), lambda b, h, i: (qblk0 + b * n + i, h, 0)),
        out_shape=jax.ShapeDtypeStruct(out.shape, out.dtype),
        input_output_aliases={7: 0},
        scratch_shapes=[pltpu.VMEM((2, HEAD_DIM, tile), F32),
                        pltpu.VMEM((2, 1, tile), F32),
                        pltpu.VMEM((2, 2, macro * tile, tile), BF16),
                        pltpu.VMEM((2, 2 * LANES, tile), BF16),
                        pltpu.VMEM((1, LANES), F32),
                        pltpu.VMEM((2, 1, tile), F32)],
        compiler_params=_cparams("parallel", "parallel", "arbitrary"),
        name="diff_attention",
    )(rel_log2, lamp, dng, dqk, dqk, vt3, bias_t, out)


def _layer_norm(y, g, b):
    mu = jnp.mean(y, axis=-1, keepdims=True)
    d = y - mu
    var = jnp.mean(d * d, axis=-1, keepdims=True)
    return d * lax.rsqrt(var + LN_EPS) * g + b


def _outproj_kernel(yr_ref, yd_ref, w_ref, x_ref, g_ref, b_ref, o_ref, ob_ref, *, alpha):
    mix = jnp.dot(yr_ref[...], w_ref[:GROUP_WIDTH, :], preferred_element_type=F32)
    w_d = w_ref[GROUP_WIDTH:, :]
    mix = mix + jnp.concatenate(
        [lax.dot_general(yd_ref[u], w_d, (((0,), (0,)), ((), ())), preferred_element_type=F32)
         for u in range(yd_ref.shape[0])], axis=0)
    y = _layer_norm(alpha * x_ref[...] + mix, g_ref[...], b_ref[...])
    o_ref[...] = y
    ob_ref[...] = y.astype(BF16)


def _out_projection(yr, ydt, w_out, x, g, b, alpha, tm):
    t = x.shape[0]
    tile = ydt.shape[2]
    assert ydt.shape == (t // tile, GROUP_WIDTH, tile) and tm % tile == 0
    kernel = functools.partial(_outproj_kernel, alpha=alpha)
    row = lambda i: (i, 0)
    fixed = lambda i: (0, 0)
    return pl.pallas_call(
        kernel,
        grid=(t // tm,),
        in_specs=[pl.BlockSpec((tm, GROUP_WIDTH), row),
                  pl.BlockSpec((tm // tile, GROUP_WIDTH, tile), lambda i: (i, 0, 0)),
                  pl.BlockSpec((2 * GROUP_WIDTH, D_MODEL), fixed, pipeline_mode=pl.Buffered(1)),
                  pl.BlockSpec((tm, D_MODEL), row),
                  pl.BlockSpec((1, D_MODEL), fixed),
                  pl.BlockSpec((1, D_MODEL), fixed)],
        out_specs=[pl.BlockSpec((tm, D_MODEL), row), pl.BlockSpec((tm, D_MODEL), row)],
        out_shape=[jax.ShapeDtypeStruct((t, D_MODEL), F32), jax.ShapeDtypeStruct((t, D_MODEL), BF16)],
        compiler_params=_cparams("parallel"),
        name="out_proj_ln",
    )(yr, ydt, w_out, x, g, b)


def _ffn_kernel(x_ref, xb_ref, xp_ref, xn_ref, wa_ref, wv_ref, wd_ref, cp_ref, g_ref, b_ref,
                o_ref, ob_ref, acc_ref, *, alpha, tm, n_chunks, starts, ends):
    i = pl.program_id(0)
    t0 = i * tm
    is_start = functools.reduce(jnp.logical_or, [t0 == s for s in starts])
    is_end = functools.reduce(jnp.logical_or, [t0 + tm == e for e in ends])
    keep_prev = jnp.where(is_start, 0.0, 1.0)
    keep_next = jnp.where(is_end, 0.0, 1.0)

    xb = xb_ref[...]
    halo = jnp.concatenate([xp_ref[...], xn_ref[...]], axis=0)
    hrows = xp_ref.shape[0]
    acc_ref[...] = jnp.zeros_like(acc_ref)

    def chunk_body(c, carry):
        wa = wa_ref[c]
        a = jnp.dot(xb, wa, preferred_element_type=F32)
        val = jnp.dot(xb, wv_ref[c], preferred_element_type=F32)
        ah = jnp.dot(halo, wa, preferred_element_type=F32)
        prev_row = ah[hrows - 1:hrows] * keep_prev
        next_row = ah[hrows:hrows + 1] * keep_next
        row = lax.broadcasted_iota(jnp.int32, a.shape, 0)
        a_m1 = jnp.where(row == 0, prev_row, pltpu.roll(a, 1, 0))
        a_p1 = jnp.where(row == tm - 1, next_row, pltpu.roll(a, tm - 1, 0))
        cp = cp_ref[c]
        conv = cp[3:4] + a_m1 * cp[0:1]
        conv = conv + a * cp[1:2]
        conv = conv + a_p1 * cp[2:3]
        gelu = 0.5 * conv * (1.0 + lax.erf(conv * (1.0 / math.sqrt(2.0))))
        hidden = (gelu * val).astype(BF16)
        acc_ref[...] += jnp.dot(hidden, wd_ref[c], preferred_element_type=F32)
        return carry

    for c in range(n_chunks):
        chunk_body(c, 0)
    y = _layer_norm(alpha * x_ref[...] + acc_ref[...], g_ref[...], b_ref[...])
    o_ref[...] = y
    ob_ref[...] = y.astype(BF16)


def _conv_glu(x, xb, wa, wv, wd, cp, g, b, alpha, tm, groups):
    t = x.shape[0]
    n_chunks, _, ck = wa.shape
    hrows = BF16_SUBLANES
    starts = tuple(r0 + bi * s for (r0, nb, s) in groups for bi in range(nb))
    ends = tuple(r0 + (bi + 1) * s for (r0, nb, s) in groups for bi in range(nb))
    kernel = functools.partial(_ffn_kernel, alpha=alpha, tm=tm, n_chunks=n_chunks,
                               starts=starts, ends=ends)
    row = lambda i: (i, 0)
    fixed2 = lambda i: (0, 0)
    fixed3 = lambda i: (0, 0, 0)
    per = tm // hrows
    last = t // hrows - 1
    return pl.pallas_call(
        kernel,
        grid=(t // tm,),
        in_specs=[pl.BlockSpec((tm, D_MODEL), row),
                  pl.BlockSpec((tm, D_MODEL), row),
                  pl.BlockSpec((hrows, D_MODEL), lambda i: (jnp.maximum(i * per - 1, 0), 0)),
                  pl.BlockSpec((hrows, D_MODEL), lambda i: (jnp.minimum((i + 1) * per, last), 0)),
                  pl.BlockSpec((n_chunks, D_MODEL, ck), fixed3, pipeline_mode=pl.Buffered(1)),
                  pl.BlockSpec((n_chunks, D_MODEL, ck), fixed3, pipeline_mode=pl.Buffered(1)),
                  pl.BlockSpec((n_chunks, ck, D_MODEL), fixed3, pipeline_mode=pl.Buffered(1)),
                  pl.BlockSpec((n_chunks, 8, ck), fixed3),
                  pl.BlockSpec((1, D_MODEL), fixed2),
                  pl.BlockSpec((1, D_MODEL), fixed2)],
        out_specs=[pl.BlockSpec((tm, D_MODEL), row), pl.BlockSpec((tm, D_MODEL), row)],
        out_shape=[jax.ShapeDtypeStruct((t, D_MODEL), F32), jax.ShapeDtypeStruct((t, D_MODEL), BF16)],
        scratch_shapes=[pltpu.VMEM((tm, D_MODEL), F32)],
        compiler_params=_cparams("parallel"),
        name="conv_glu_ln",
    )(x, xb, xb, xb, wa, wv, wd, cp, g, b)


def _rotary_tables(seq):
    d = HEAD_DIM
    inv = 1.0 / (ROPE_BASE ** (jnp.arange(0, d, 2, dtype=F32) / d))
    ang = jnp.arange(seq, dtype=F32)[:, None] * inv[None, :]
    cos, sin = jnp.cos(ang), jnp.sin(ang)
    cos_t = jnp.concatenate([cos, cos, cos, cos], axis=-1)
    sin_t = jnp.concatenate([-sin, sin, -sin, sin], axis=-1)
    return cos_t, sin_t


def _tiles(groups):
    smin = min(s for (_, _, s) in groups)
    attn_tile = min(512, smin // 2)
    chunk = min(256, smin // 2)
    tm = min(512, smin // 2)
    ffn_tm = min(1024, smin // 2)
    return attn_tile, chunk, tm, ffn_tm


def _forward(x, groups, w_in, ret_decay_logit, rel_bias, lambda_q1, lambda_k1, lambda_q2,
             lambda_k2, diff_norm_g, w_out, ln_g, ln_b, w_up, conv_w, conv_b, w_down):
    depth = w_in.shape[0]
    alpha = (2 * depth) ** 0.25
    t = x.shape[0]
    attn_tile, chunk, tm, ffn_tm = _tiles(groups)
    smax = max(s for (_, _, s) in groups)
    gw = GROUP_WIDTH
    ck = MXU_WIDTH
    n_chunks = D_FF // ck

    cos_t, sin_t = _rotary_tables(smax)
    bias_t = _bias_tiles(rel_bias, attn_tile)
    rel_log2 = rel_bias.astype(F32) * LOG2E
    rel_log2 = jnp.concatenate([rel_log2, jnp.max(rel_log2, axis=0, keepdims=True)], axis=0)
    in_scale = jnp.concatenate([
        jnp.ones((gw,), F32), jnp.full((gw,), HEAD_DIM ** -0.5, F32), jnp.ones((2 * gw,), F32),
        jnp.full((gw,), DIFF_QK_DIM ** -0.5 * LOG2E, F32), jnp.ones((gw,), F32)])[None, :]
    assert ffn_tm % attn_tile == 0

    xb = x.astype(BF16)
    for l in range(depth):
        lam_init = 0.8 - 0.6 * math.exp(-0.3 * l)
        w_in_b = w_in[l].astype(BF16)
        ret, gate, dqk, vt3 = _project(xb, w_in_b[:, :6 * gw], w_in_b[:, 6 * gw:].T, in_scale,
                                       cos_t, sin_t, groups, ffn_tm, attn_tile)

        log_g = jax.nn.log_sigmoid(ret_decay_logit[l].astype(F32))
        lamp = jnp.stack([lambda_q1[l], lambda_k1[l], lambda_q2[l], lambda_k2[l],
                          jnp.full((DIFF_QK_DIM,), lam_init)]).astype(F32)
        dng = diff_norm_g[l].astype(F32)[:, None]

        yr = jnp.zeros((t, gw), BF16)
        ydt = jnp.zeros((t // attn_tile, gw, attn_tile), BF16)
        for (row0, batch, seq) in groups:
            yr = _retention(ret, gate, log_g, yr, row0, batch, seq, chunk)
            ydt = _diff_attention(dqk, vt3, bias_t, rel_log2, lamp, dng, ydt, row0, batch, seq,
                                  attn_tile)

        x, xb = _out_projection(yr, ydt, w_out[l].astype(BF16), x, ln_g[l, 0][None, :].astype(F32),
                                ln_b[l, 0][None, :].astype(F32), alpha, ffn_tm)

        wa = w_up[l][:, :D_FF].astype(BF16).reshape(D_MODEL, n_chunks, ck).transpose(1, 0, 2)
        wv = w_up[l][:, D_FF:].astype(BF16).reshape(D_MODEL, n_chunks, ck).transpose(1, 0, 2)
        wd = w_down[l].astype(BF16).reshape(n_chunks, ck, D_MODEL)
        cp = jnp.concatenate([conv_w[l].astype(F32), conv_b[l].astype(F32)[None, :],
                              jnp.zeros((4, D_FF), F32)], axis=0)
        cp = cp.reshape(8, n_chunks, ck).transpose(1, 0, 2)
        x, xb = _conv_glu(x, xb, wa, wv, wd, cp, ln_g[l, 1][None, :].astype(F32),
                          ln_b[l, 1][None, :].astype(F32), alpha, ffn_tm, groups)
    return x


def kernel(x_prompt, x_sample, w_in, ret_decay_logit, rel_bias, lambda_q1, lambda_k1, lambda_q2,
           lambda_k2, diff_norm_g, w_out, ln_g, ln_b, w_up, conv_w, conv_b, w_down):
    bp, sp, d = x_prompt.shape
    bs, ss, _ = x_sample.shape
    groups = ((0, bp, sp), (bp * sp, bs, ss))
    x = jnp.concatenate([x_prompt.reshape(bp * sp, d), x_sample.reshape(bs * ss, d)], axis=0)
    y = _forward(x.astype(F32), groups, w_in, ret_decay_logit, rel_bias, lambda_q1, lambda_k1,
                 lambda_q2, lambda_k2, diff_norm_g, w_out, ln_g, ln_b, w_up, conv_w, conv_b, w_down)
    y_prompt = y[:bp * sp].reshape(bp, sp, d).astype(x_prompt.dtype)
    y_sample = y[bp * sp:].reshape(bs, ss, d).astype(x_sample.dtype)
    return y_prompt, y_sample
```
